```python
import jax, jax.numpy as jnp
from jax import lax
import numpy as np

D_MODEL = 1024
BATCH = 16
SEQ = 2048
DEPTH = 1

HEAD_DIM = 64
N_HEADS = D_MODEL // HEAD_DIM
N_HEADS_NSA = N_HEADS // 2
N_KV_NSA = 2
N_HEADS_SWA = N_HEADS - N_HEADS_NSA
N_KV_SWA = 2
MIX_WIDTH = N_HEADS * HEAD_DIM
CMP_BLOCK = 32
CMP_STRIDE = 16
CMP_HIDDEN = 4 * HEAD_DIM
SEL_BLOCK = 64
SEL_TOPN = 8
NSA_WINDOW = 512
SWA_WINDOW = 128
Q_BLOCK = 128
D_FF = 4 * D_MODEL
NORM_EPS = 1e-6
N_NSA_BRANCHES = 3

kernel_name = "hymba_nsa_swa_sink_alibi_sandwich_block"


def rms_norm(x, g):
    xf = x.astype(jnp.float32)
    y = xf * lax.rsqrt(jnp.mean(xf * xf, axis=-1, keepdims=True) + NORM_EPS)
    return (y * g.astype(jnp.float32)).astype(x.dtype)


def alibi_slopes():
    s = 2.0 ** (-8.0 * (np.arange(N_HEADS) + 1) / N_HEADS)
    s = jnp.asarray(s, dtype=jnp.float32)
    nsa = s[0::2].reshape(N_KV_NSA, N_HEADS_NSA // N_KV_NSA)
    swa = s[1::2].reshape(N_KV_SWA, N_HEADS_SWA // N_KV_SWA)
    return nsa, swa


def masked_softmax(s, mask, sink=None):
    s = jnp.where(mask, s, -jnp.inf)
    m = jnp.max(s, axis=-1, keepdims=True)
    if sink is not None:
        m = jnp.maximum(m, sink)
    m = jnp.where(jnp.isfinite(m), m, 0.0)
    e = jnp.exp(s - m)
    denom = jnp.sum(e, axis=-1, keepdims=True)
    if sink is not None:
        denom = denom + jnp.exp(sink - m)
    return e / jnp.where(denom > 0, denom, 1.0)


def compress_blocks(kv, pe, w1, w2):
    B, G, T, D = kv.shape
    n_cmp = (T - CMP_BLOCK) // CMP_STRIDE + 1
    idx = np.arange(n_cmp)[:, None] * CMP_STRIDE + np.arange(CMP_BLOCK)[None, :]
    blocks = kv[:, :, idx, :] + pe
    flat = blocks.reshape(B, G, n_cmp, CMP_BLOCK * D)
    return jax.nn.gelu(flat @ w1) @ w2


def banded_attention(q, k, v, slopes, window, sinks=None):
    B, G, R, T, D = q.shape
    scale = D ** -0.5
    n_blk = T // Q_BLOCK
    L = window + Q_BLOCK
    kp = jnp.pad(k, ((0, 0), (0, 0), (window, 0), (0, 0)))
    vp = jnp.pad(v, ((0, 0), (0, 0), (window, 0), (0, 0)))
    sink = None if sinks is None else sinks.astype(jnp.float32)[None, :, :, None, None]

    def body(c):
        start = c * Q_BLOCK
        q_b = lax.dynamic_slice_in_dim(q, start, Q_BLOCK, axis=3)
        ks = lax.dynamic_slice_in_dim(kp, start, L, axis=2)
        vs = lax.dynamic_slice_in_dim(vp, start, L, axis=2)
        pos_q = start + jnp.arange(Q_BLOCK)
        pos_k = start - window + jnp.arange(L)
        dist = pos_q[:, None] - pos_k[None, :]
        mask = (pos_k >= 0)[None, :] & (dist >= 0) & (dist < window)
        s = jnp.einsum('bgrqd,bgkd->bgrqk', q_b, ks).astype(jnp.float32) * scale
        s = s - slopes[None, :, :, None, None] * dist.astype(jnp.float32)
        p = masked_softmax(s, mask, sink)
        return jnp.einsum('bgrqk,bgkd->bgrqd', p.astype(vs.dtype), vs)

    out = lax.map(body, jnp.arange(n_blk))
    return out.transpose(1, 2, 3, 0, 4, 5).reshape(B, G, R, T, D)


def selected_attention(q, k, v, tok, slopes):
    B, G, R, T, D = q.shape
    scale = D ** -0.5
    n_blk = T // Q_BLOCK
    qc = q.reshape(B, G, R, n_blk, Q_BLOCK, D).transpose(3, 0, 1, 2, 4, 5)
    tc = tok.reshape(B, G, n_blk, Q_BLOCK, -1).transpose(2, 0, 1, 3, 4)
    pc = jnp.arange(T).reshape(n_blk, Q_BLOCK)
    gather = jax.vmap(jax.vmap(lambda kk, ii: kk[ii]))

    def body(args):
        q_b, tok_b, pos_b = args
        ks = gather(k, tok_b)
        vs = gather(v, tok_b)
        dist = pos_b[None, None, :, None] - tok_b
        s = jnp.einsum('bgrqd,bgqkd->bgrqk', q_b, ks).astype(jnp.float32) * scale
        s = s - slopes[None, :, :, None, None] * dist[:, :, None].astype(jnp.float32)
        p = masked_softmax(s, (dist >= 0)[:, :, None])
        return jnp.einsum('bgrqk,bgqkd->bgrqd', p.astype(vs.dtype), vs)

    out = lax.map(body, (qc, tc, pc))
    return out.transpose(1, 2, 3, 0, 4, 5).reshape(B, G, R, T, D)


def nsa_attention(q, k_cmp, v_cmp, k_slc, v_slc, k_win, v_win, gates, slopes,
                  pe_k, w1_k, w2_k, pe_v, w1_v, w2_v):
    B, G, R, T, D = q.shape
    scale = D ** -0.5
    pos = jnp.arange(T)
    kc = compress_blocks(k_cmp, pe_k, w1_k, w2_k)
    vc = compress_blocks(v_cmp, pe_v, w1_v, w2_v)
    n_cmp = kc.shape[2]
    cmp_start = jnp.arange(n_cmp) * CMP_STRIDE
    cmp_end = cmp_start + CMP_BLOCK - 1
    dist_c = pos[:, None] - cmp_end[None, :]
    s = jnp.einsum('bgrtd,bgnd->bgrtn', q, kc).astype(jnp.float32) * scale
    s = s - slopes[None, :, :, None, None] * dist_c.astype(jnp.float32)
    p_cmp = masked_softmax(s, dist_c >= 0)
    o_cmp = jnp.einsum('bgrtn,bgnd->bgrtd', p_cmp.astype(vc.dtype), vc)
    n_sel = T // SEL_BLOCK
    sel_start = jnp.arange(n_sel) * SEL_BLOCK
    overlap = ((cmp_start[:, None] < sel_start[None, :] + SEL_BLOCK) &
               (cmp_start[:, None] + CMP_BLOCK > sel_start[None, :])).astype(jnp.float32)
    imp = jnp.einsum('bgrtn,nj->bgtj', p_cmp, overlap)
    blk_t = pos // SEL_BLOCK
    j = jnp.arange(n_sel)
    valid = j[None, :] <= blk_t[:, None]
    forced = (j[None, :] == 0) | (j[None, :] == blk_t[:, None]) | (j[None, :] == blk_t[:, None] - 1)
    score = jnp.where(valid, imp, -jnp.inf)
    score = jnp.where(forced & valid, jnp.inf, score)
    top_n = min(SEL_TOPN, n_sel)
    _, sel_idx = lax.top_k(score, top_n)
    tok = (sel_idx[..., None] * SEL_BLOCK + jnp.arange(SEL_BLOCK)).reshape(B, G, T, top_n * SEL_BLOCK)
    o_slc = selected_attention(q, k_slc, v_slc, tok, slopes)
    o_win = banded_attention(q, k_win, v_win, slopes, NSA_WINDOW)
    g = gates.astype(q.dtype)
    return g[..., 0:1] * o_cmp + g[..., 1:2] * o_slc + g[..., 2:3] * o_win


def setup_inputs(seed: int = 0) -> dict:
    key = jax.random.key(seed)
    ks = jax.random.split(key, 16)
    n_cols = (N_HEADS_NSA * HEAD_DIM + 6 * N_KV_NSA * HEAD_DIM + N_HEADS_NSA * N_NSA_BRANCHES
              + N_HEADS_SWA * HEAD_DIM + 2 * N_KV_SWA * HEAD_DIM)
    f = jnp.float32
    nrm = lambda k, shape, s: jax.random.normal(k, shape, f) * s
    gain = lambda k: 1.0 + 0.05 * jax.random.normal(k, (DEPTH, D_MODEL), f)
    flat = CMP_BLOCK * HEAD_DIM
    return {
        "x": jax.random.normal(ks[0], (BATCH, SEQ, D_MODEL), f),
        "norm_mix_pre": gain(ks[1]),
        "w_in": nrm(ks[2], (DEPTH, D_MODEL, n_cols), D_MODEL ** -0.5),
        "cmp_pe_k": nrm(ks[3], (DEPTH, CMP_BLOCK, HEAD_DIM), 0.02),
        "cmp_w1_k": nrm(ks[4], (DEPTH, flat, CMP_HIDDEN), flat ** -0.5),
        "cmp_w2_k": nrm(ks[5], (DEPTH, CMP_HIDDEN, HEAD_DIM), CMP_HIDDEN ** -0.5),
        "cmp_pe_v": nrm(ks[6], (DEPTH, CMP_BLOCK, HEAD_DIM), 0.02),
        "cmp_w1_v": nrm(ks[7], (DEPTH, flat, CMP_HIDDEN), flat ** -0.5),
        "cmp_w2_v": nrm(ks[8], (DEPTH, CMP_HIDDEN, HEAD_DIM), CMP_HIDDEN ** -0.5),
        "sinks": nrm(ks[9], (DEPTH, N_KV_SWA, N_HEADS_SWA // N_KV_SWA), 0.5),
        "w_out": nrm(ks[10], (DEPTH, MIX_WIDTH, D_MODEL), MIX_WIDTH ** -0.5),
        "norm_mix_post": gain(ks[11]),
        "norm_mlp_pre": gain(ks[12]),
        "w_up": nrm(ks[13], (DEPTH, D_MODEL, D_FF), D_MODEL ** -0.5),
        "w_down": nrm(ks[14], (DEPTH, D_FF, D_MODEL), D_FF ** -0.5),
        "norm_mlp_post": gain(ks[15]),
    }


def reference(x, norm_mix_pre, w_in, cmp_pe_k, cmp_w1_k, cmp_w2_k, cmp_pe_v, cmp_w1_v, cmp_w2_v,
              sinks, w_out, norm_mix_post, norm_mlp_pre, w_up, w_down, norm_mlp_post):
    B, T, _ = x.shape
    D = HEAD_DIM
    Rn = N_HEADS_NSA // N_KV_NSA
    Rs = N_HEADS_SWA // N_KV_SWA
    slopes_nsa, slopes_swa = alibi_slopes()
    sizes = [N_HEADS_NSA * D] + [N_KV_NSA * D] * 6 + [N_HEADS_NSA * N_NSA_BRANCHES,
             N_HEADS_SWA * D, N_KV_SWA * D, N_KV_SWA * D]
    split_at = list(np.cumsum(sizes)[:-1])

    def to_q(t, G, R):
        return t.reshape(B, T, G, R, D).transpose(0, 2, 3, 1, 4)

    def to_kv(t, G):
        return t.reshape(B, T, G, D).transpose(0, 2, 1, 3)

    h = x
    for i in range(DEPTH):
        a = rms_norm(h, norm_mix_pre[i])
        proj = a @ w_in[i]
        (q_n, kc_n, vc_n, ks_n, vs_n, kw_n, vw_n, g_n, q_s, k_s, v_s) = jnp.split(proj, split_at, axis=-1)
        gates = jax.nn.sigmoid(g_n.astype(jnp.float32)).reshape(B, T, N_KV_NSA, Rn, N_NSA_BRANCHES)
        gates = gates.transpose(0, 2, 3, 1, 4)
        o_nsa = nsa_attention(to_q(q_n, N_KV_NSA, Rn), to_kv(kc_n, N_KV_NSA), to_kv(vc_n, N_KV_NSA),
                              to_kv(ks_n, N_KV_NSA), to_kv(vs_n, N_KV_NSA),
                              to_kv(kw_n, N_KV_NSA), to_kv(vw_n, N_KV_NSA), gates, slopes_nsa,
                              cmp_pe_k[i], cmp_w1_k[i], cmp_w2_k[i], cmp_pe_v[i], cmp_w1_v[i], cmp_w2_v[i])
        o_swa = banded_attention(to_q(q_s, N_KV_SWA, Rs), to_kv(k_s, N_KV_SWA), to_kv(v_s, N_KV_SWA),
                                 slopes_swa, SWA_WINDOW, sinks[i])
        o_nsa = o_nsa.transpose(0, 3, 1, 2, 4).reshape(B, T, N_HEADS_NSA * D)
        o_swa = o_swa.transpose(0, 3, 1, 2, 4).reshape(B, T, N_HEADS_SWA * D)
        mix = jnp.concatenate([o_nsa, o_swa], axis=-1) @ w_out[i]
        h = h + rms_norm(mix, norm_mix_post[i])
        m = rms_norm(h, norm_mlp_pre[i])
        u = jnp.square(jax.nn.relu(m @ w_up[i]))
        h = h + rms_norm(u @ w_down[i], norm_mlp_post[i])
    return h
```

```python
import functools

import jax
import jax.numpy as jnp
import numpy as np
from jax import lax
from jax.experimental import pallas as pl
from jax.experimental.pallas import tpu as pltpu

F32 = jnp.float32
BF16 = jnp.bfloat16

D_MODEL = 1024
HEAD_DIM = 64
N_HEADS = 16
N_GROUPS = 2
N_REP = 4
CMP_BLOCK = 32
CMP_STRIDE = 16
CMP_HIDDEN = 4 * HEAD_DIM
SEL_BLOCK = 64
SEL_TOPN = 8
NSA_WINDOW = 512
SWA_WINDOW = 128
D_FF = 4 * D_MODEL
NORM_EPS = 1e-6
N_BRANCH = 3

LANE = 128
NEG = -1e30
SEL_NEG = -(2.0 ** 100)
VMEM_LIMIT = 56 * 1024 * 1024

QN_COL, QS_COL = 0, 4
KS_COL, VS_COL, KW_COL, VW_COL, KSW_COL, VSW_COL = 8, 10, 12, 14, 16, 18
MAIN_COLS = 20 * LANE
PROJ_COLS = MAIN_COLS + 3 * LANE


def _slopes():
    s = 2.0 ** (-8.0 * (np.arange(N_HEADS) + 1) / N_HEADS)
    nsa = s[0::2].reshape(N_GROUPS, N_REP)
    swa = s[1::2].reshape(N_GROUPS, N_REP)
    return nsa, swa


SLOPES_NSA, SLOPES_SWA = _slopes()

_NT = (((1,), (1,)), ((), ()))


def _dot_nt(a, b, **kw):
    return lax.dot_general(a, b, _NT, preferred_element_type=F32, **kw)


def _dot(a, b):
    return jnp.dot(a, b, preferred_element_type=F32)


def _rms(v, g):
    return v * lax.rsqrt(jnp.mean(v * v, axis=-1, keepdims=True) + NORM_EPS) * g


def _head_rows(vals, tq):
    rb = lax.broadcasted_iota(jnp.int32, (N_REP * tq, 1), 0) // tq
    out = jnp.full((N_REP * tq, 1), float(vals[N_REP - 1]), F32)
    for r in range(N_REP - 1):
        out = jnp.where(rb == r, float(vals[r]), out)
    return out


def _split_heads(qblk, g, lo):
    outs = []
    for r in range(N_REP):
        c = 2 * g + r // 2
        h = qblk[:, c * LANE:(c + 1) * LANE].astype(F32)
        if r % 2:
            h = pltpu.roll(h, HEAD_DIM, 1)
        outs.append(jnp.where(lo, h, 0.0))
    return outs


def _merge_heads(os_, lo):
    return [jnp.where(lo, os_[2 * k], pltpu.roll(os_[2 * k + 1], HEAD_DIM, 1)) for k in range(2)]


def _inproj_kernel(x_ref, g_ref, w_ref, main_ref, kc_ref, vc_ref, gate_ref, *, tm, seq):
    x = x_ref[...]
    a = _rms(x, g_ref[...]).astype(BF16)
    res = _dot(a, w_ref[...])
    nq = 8 * LANE
    main_ref[:, 0:nq] = (res[:, 0:nq] * (HEAD_DIM ** -0.5)).astype(BF16)
    t0 = (pl.program_id(0) * tm) % seq
    row = lax.broadcasted_iota(jnp.int32, (tm, LANE), 0)
    lane = lax.broadcasted_iota(jnp.int32, (tm, LANE), 1)
    onehot = jnp.where(lane == (t0 + row) // SEL_BLOCK + HEAD_DIM, 1.0, 0.0)
    for g in range(N_GROUPS):
        c0 = (KS_COL + g) * LANE
        main_ref[:, c0:c0 + LANE] = (res[:, c0:c0 + LANE] + onehot).astype(BF16)
    c1 = VS_COL * LANE
    main_ref[:, c1:MAIN_COLS] = res[:, c1:MAIN_COLS].astype(BF16)
    kc_ref[...] = res[:, MAIN_COLS:MAIN_COLS + LANE].astype(kc_ref.dtype)
    vc_ref[...] = res[:, MAIN_COLS + LANE:MAIN_COLS + 2 * LANE].astype(vc_ref.dtype)
    gate_ref[...] = jax.nn.sigmoid(res[:, MAIN_COLS + 2 * LANE:PROJ_COLS])


def _inproj(x2, gain, w_p, seq, tm=512):
    n = x2.shape[0]
    return pl.pallas_call(
        functools.partial(_inproj_kernel, tm=tm, seq=seq),
        grid=(n // tm,),
        in_specs=[
            pl.BlockSpec((tm, D_MODEL), lambda i: (i, 0)),
            pl.BlockSpec((1, D_MODEL), lambda i: (0, 0)),
            pl.BlockSpec((D_MODEL, PROJ_COLS), lambda i: (0, 0)),
        ],
        out_specs=[
            pl.BlockSpec((tm, MAIN_COLS), lambda i: (i, 0)),
            pl.BlockSpec((tm, LANE), lambda i: (i, 0)),
            pl.BlockSpec((tm, LANE), lambda i: (i, 0)),
            pl.BlockSpec((tm, LANE), lambda i: (i, 0)),
        ],
        out_shape=[
            jax.ShapeDtypeStruct((n, MAIN_COLS), BF16),
            jax.ShapeDtypeStruct((n, LANE), BF16),
            jax.ShapeDtypeStruct((n, LANE), BF16),
            jax.ShapeDtypeStruct((n, LANE), F32),
        ],
        compiler_params=pltpu.CompilerParams(
            dimension_semantics=("arbitrary",), vmem_limit_bytes=VMEM_LIMIT),
        name="inproj",
    )(x2, gain, w_p)


def _compress_kernel(z_ref, pe_ref, w1_ref, w2_ref, o_ref):
    z = z_ref[0, 0].astype(F32)
    zt = (z + pe_ref[0, 0]).astype(BF16)
    zb = (z + pe_ref[0, 1]).astype(BF16)
    a = _dot(zt, w1_ref[0, 0])
    bm = _dot(zb, w1_ref[0, 1])
    n_rows = a.shape[0]
    h = a + pltpu.roll(bm, n_rows - 1, 0)
    hg = jax.nn.gelu(h).astype(BF16)
    o_ref[0, 0] = _dot(hg, w2_ref[0]).astype(o_ref.dtype)


def _compress(z, pe2, w1e, w2e):
    _, b, n, k = z.shape
    return pl.pallas_call(
        _compress_kernel,
        grid=(2, b),
        in_specs=[
            pl.BlockSpec((1, 1, n, k), lambda t, i: (t, i, 0, 0)),
            pl.BlockSpec((1, 2, 1, k), lambda t, i: (t, 0, 0, 0)),
            pl.BlockSpec((1, 2, k, 2 * CMP_HIDDEN), lambda t, i: (t, 0, 0, 0)),
            pl.BlockSpec((1, 2 * CMP_HIDDEN, 2 * LANE), lambda t, i: (t, 0, 0)),
        ],
        out_specs=pl.BlockSpec((1, 1, n, 2 * LANE), lambda t, i: (t, i, 0, 0)),
        out_shape=jax.ShapeDtypeStruct((2, b, n, 2 * LANE), BF16),
        compiler_params=pltpu.CompilerParams(
            dimension_semantics=("arbitrary", "arbitrary"), vmem_limit_bytes=VMEM_LIMIT),
        name="compress",
    )(z, pe2, w1e, w2e)


def _cmp_kernel(q_ref, k_ref, v_ref, ocmp_ref, qaug_ref, *, tq, n_cmp):
    i = pl.program_id(1)
    t0 = i * tq
    qblk = q_ref[...]
    lane = lax.broadcasted_iota(jnp.int32, (tq, LANE), 1)
    lo = lane < HEAD_DIM
    row_t = t0 + lax.broadcasted_iota(jnp.int32, (tq, LANE), 0)
    dist = (row_t - (lane * CMP_STRIDE + CMP_BLOCK - 1)).astype(F32)
    maskc = (dist >= 0) & (lane < n_cmp)

    jr = lax.broadcasted_iota(jnp.int32, (LANE, LANE), 0) - HEAD_DIM
    nc = lax.broadcasted_iota(jnp.int32, (LANE, LANE), 1)
    ov = ((nc * CMP_STRIDE < jr * SEL_BLOCK + SEL_BLOCK) & (nc * CMP_STRIDE + CMP_BLOCK > jr * SEL_BLOCK)
          & (jr >= 0) & (jr < 32) & (nc < n_cmp))
    ovT = jnp.where(ov, 1.0, 0.0).astype(F32)

    n_sel = 32
    jj = lax.broadcasted_iota(jnp.int32, (n_sel, tq), 0)
    blk_t = (t0 + lax.broadcasted_iota(jnp.int32, (n_sel, tq), 1)) // SEL_BLOCK
    valid = jj <= blk_t
    forced = (jj == 0) | (jj == blk_t) | (jj == blk_t - 1)

    for g in range(N_GROUPS):
        kc = k_ref[0, 0][:, g * LANE:(g + 1) * LANE]
        vc = v_ref[0, 0][:, g * LANE:(g + 1) * LANE]
        qs = _split_heads(qblk, g, lo)
        psum = jnp.zeros((tq, LANE), F32)
        outs = []
        for r in range(N_REP):
            s = _dot_nt(qs[r].astype(BF16), kc) - float(SLOPES_NSA[g, r]) * dist
            sm = jnp.where(maskc, s, NEG)
            m = jnp.max(sm, axis=1, keepdims=True)
            e = jnp.where(maskc, jnp.exp(sm - m), 0.0)
            den = jnp.sum(e, axis=1, keepdims=True)
            p = e / jnp.where(den > 0, den, 1.0)
            outs.append(_dot(p.astype(BF16), vc))
            psum = psum + p
        merged = _merge_heads(outs, lo)
        for k in range(2):
            c = 2 * g + k
            ocmp_ref[:, c * LANE:(c + 1) * LANE] = merged[k]

        impT = _dot_nt(ovT, psum, precision=lax.Precision.HIGHEST)
        imp = impT[HEAD_DIM:HEAD_DIM + n_sel, :]
        score = jnp.where(valid, imp, -jnp.inf)
        score = jnp.where(forced & valid, jnp.inf, score)
        rank = jnp.zeros((n_sel, tq), jnp.int32)
        for k in range(n_sel):
            sk = score[k:k + 1, :]
            ahead = (sk > score) | ((sk == score) & (jj > k))
            rank = rank + ahead.astype(jnp.int32)
        negm = jnp.where(rank < SEL_TOPN, 0.0, SEL_NEG)
        z64 = jnp.zeros((HEAD_DIM, tq), F32)
        z32 = jnp.zeros((LANE - HEAD_DIM - n_sel, tq), F32)
        negT = jnp.concatenate([z64, negm, z32], axis=0)
        neg = negT.T
        for r in range(N_REP):
            qaug_ref[0, N_REP * g + r] = (qs[r] + neg).astype(BF16)


def _cmp(main, cmp_kv, b, seq, tq=256):
    nq = seq // tq
    n_cmp = (seq - CMP_BLOCK) // CMP_STRIDE + 1
    n = b * seq
    return pl.pallas_call(
        functools.partial(_cmp_kernel, tq=tq, n_cmp=n_cmp),
        grid=(b, nq),
        in_specs=[
            pl.BlockSpec((tq, 4 * LANE), lambda bi, i: (bi * nq + i, QN_COL // 4)),
            pl.BlockSpec((1, 1, LANE, 2 * LANE), lambda bi, i: (0, bi, 0, 0)),
            pl.BlockSpec((1, 1, LANE, 2 * LANE), lambda bi, i: (1, bi, 0, 0)),
        ],
        out_specs=[
            pl.BlockSpec((tq, 4 * LANE), lambda bi, i: (bi * nq + i, 0)),
            pl.BlockSpec((1, N_GROUPS * N_REP, tq, LANE), lambda bi, i: (bi, 0, i, 0)),
        ],
        out_shape=[
            jax.ShapeDtypeStruct((n, 4 * LANE), F32),
            jax.ShapeDtypeStruct((b, N_GROUPS * N_REP, seq, LANE), BF16),
        ],
        compiler_params=pltpu.CompilerParams(
            dimension_semantics=("arbitrary", "arbitrary"), vmem_limit_bytes=VMEM_LIMIT),
        name="cmp",
    )(main, cmp_kv, cmp_kv)


def _slcwin_kernel(qaug_ref, ks0, ks1, vs0, vs1, kw0, kw1, vw0, vw1, gate_ref, ocmp_ref,
                   o_ref, sbias, wbias, *, tq):
    tk = tq
    bi = pl.program_id(0)
    i = pl.program_id(1)
    rows = N_REP * tq
    n_wt = NSA_WINDOW // tk

    @pl.when((bi == 0) & (i == 0))
    def _init():
        col = lax.broadcasted_iota(jnp.int32, (rows, tk), 1)
        row = lax.broadcasted_iota(jnp.int32, (rows, tk), 0) % tq
        colf = col.astype(F32)
        rel = (col - row).astype(F32)
        for g in range(N_GROUPS):
            sl = _head_rows(SLOPES_NSA[g], tq)
            sbias[0, g] = sl * colf
            sbias[1, g] = jnp.where(col <= row, sl * colf, NEG)
            for d in range(n_wt + 1):
                b = sl * (rel - float(d * tk))
                if d == 0:
                    b = jnp.where(col <= row, b, NEG)
                elif d == n_wt:
                    b = jnp.where(col > row, b, NEG)
                wbias[d, g] = b
            wbias[n_wt + 1, g] = jnp.full((rows, tk), NEG, F32)

    lane = lax.broadcasted_iota(jnp.int32, (tq, LANE), 1)
    lo = lane < HEAD_DIM
    gates = gate_ref[...]
    ks = (ks0, ks1)
    vs = (vs0, vs1)
    kw = (kw0, kw1)
    vw = (vw0, vw1)

    for g in range(N_GROUPS):
        q = qaug_ref[0, N_REP * g:N_REP * (g + 1)].reshape(rows, LANE)
        sl = _head_rows(SLOPES_NSA[g], tq)

        def slc_step(j, carry, bias):
            m, l, acc = carry
            off = pl.multiple_of(j * tk, tk)
            kt = ks[g][pl.ds(off, tk), :]
            vt = vs[g][pl.ds(off, tk), :]
            s = _dot_nt(q, kt) + bias
            cvec = sl * (j * tk).astype(F32)
            m_new = jnp.maximum(m, jnp.max(s, axis=1, keepdims=True) + cvec)
            alpha = jnp.exp(m - m_new)
            p = jnp.exp(s - (m_new - cvec))
            l = alpha * l + jnp.sum(p, axis=1, keepdims=True)
            acc = alpha * acc + _dot(p.astype(BF16), vt)
            return m_new, l, acc

        init = (jnp.full((rows, 1), -3e38, F32), jnp.zeros((rows, 1), F32), jnp.zeros((rows, LANE), F32))
        carry = lax.fori_loop(0, i, lambda j, c: slc_step(j, c, sbias[0, g]), init)
        _, l, acc = slc_step(i, carry, sbias[1, g])
        o_slc = acc / l

        s_list, v_list = [], []
        for d in range(n_wt, -1, -1):
            idx = i - d
            off = pl.multiple_of(jnp.maximum(idx, 0) * tk, tk)
            slot = jnp.where(idx >= 0, d, n_wt + 1)
            s_list.append(_dot_nt(q, kw[g][pl.ds(off, tk), :]) + wbias[slot, g])
            v_list.append(vw[g][pl.ds(off, tk), :])
        m = s_list[0].max(axis=1, keepdims=True)
        for s in s_list[1:]:
            m = jnp.maximum(m, s.max(axis=1, keepdims=True))
        lw = jnp.zeros((rows, 1), F32)
        ow = jnp.zeros((rows, LANE), F32)
        for s, vt in zip(s_list, v_list):
            p = jnp.exp(s - m)
            lw = lw + jnp.sum(p, axis=1, keepdims=True)
            ow = ow + _dot(p.astype(BF16), vt)
        o_win = ow / lw

        outs = []
        for r in range(N_REP):
            gc = (g * N_REP + r) * N_BRANCH
            o_c = ocmp_ref[:, (2 * g + r // 2) * LANE:(2 * g + r // 2 + 1) * LANE]
            if r % 2:
                o_c = pltpu.roll(o_c, HEAD_DIM, 1)
            o = (gates[:, gc:gc + 1] * o_c
                 + gates[:, gc + 1:gc + 2] * o_slc[r * tq:(r + 1) * tq]
                 + gates[:, gc + 2:gc + 3] * o_win[r * tq:(r + 1) * tq])
            outs.append(o)
        merged = _merge_heads(outs, lo)
        for k in range(2):
            c = 2 * g + k
            o_ref[:, c * LANE:(c + 1) * LANE] = merged[k].astype(o_ref.dtype)


def _slcwin(qaug, main, gates, ocmp, b, seq, tq=256):
    nq = seq // tq
    n = b * seq
    rows = N_REP * tq

    def col(c):
        return pl.BlockSpec((seq, LANE), lambda bi, i, c=c: (bi, c))

    return pl.pallas_call(
        functools.partial(_slcwin_kernel, tq=tq),
        grid=(b, nq),
        in_specs=[
            pl.BlockSpec((1, N_GROUPS * N_REP, tq, LANE), lambda bi, i: (bi, 0, i, 0)),
            col(KS_COL), col(KS_COL + 1), col(VS_COL), col(VS_COL + 1),
            col(KW_COL), col(KW_COL + 1), col(VW_COL), col(VW_COL + 1),
            pl.BlockSpec((tq, LANE), lambda bi, i: (bi * nq + i, 0)),
            pl.BlockSpec((tq, 4 * LANE), lambda bi, i: (bi * nq + i, 0)),
        ],
        out_specs=pl.BlockSpec((tq, 4 * LANE), lambda bi, i: (bi * nq + i, 0)),
        out_shape=jax.ShapeDtypeStruct((n, 4 * LANE), BF16),
        scratch_shapes=[
            pltpu.VMEM((2, N_GROUPS, rows, tq), F32),
            pltpu.VMEM((NSA_WINDOW // tq + 2, N_GROUPS, rows, tq), F32),
        ],
        compiler_params=pltpu.CompilerParams(
            dimension_semantics=("arbitrary", "arbitrary"), vmem_limit_bytes=VMEM_LIMIT),
        name="slcwin",
    )(qaug, main, main, main, main, main, main, main, main, gates, ocmp)


def _swa_kernel(sink_ref, q_ref, k0, k1, v0, v1, o_ref, bias, *, tq):
    bi = pl.program_id(0)
    i = pl.program_id(1)
    rows = N_REP * tq
    span = SWA_WINDOW + tq

    @pl.when((bi == 0) & (i == 0))
    def _init():
        col = lax.broadcasted_iota(jnp.int32, (rows, span), 1)
        row = lax.broadcasted_iota(jnp.int32, (rows, span), 0) % tq
        for g in range(N_GROUPS):
            sl = _head_rows(SLOPES_SWA[g], tq)
            dist = row + SWA_WINDOW - col
            bias[0, g] = jnp.where((dist >= 0) & (dist < SWA_WINDOW), -sl * dist.astype(F32), NEG)
            dist = row - col
            bias[1, g] = jnp.where((dist >= 0) & (dist < SWA_WINDOW), -sl * dist.astype(F32), NEG)

    lane = lax.broadcasted_iota(jnp.int32, (tq, LANE), 1)
    lo = lane < HEAD_DIM
    qblk = q_ref[...]
    kk = (k0, k1)
    vv = (v0, v1)
    start = pl.multiple_of(jnp.maximum(i * tq - SWA_WINDOW, 0), LANE)
    slot = jnp.where(i == 0, 1, 0)
    rb = lax.broadcasted_iota(jnp.int32, (rows, 1), 0) // tq
    for g in range(N_GROUPS):
        q = jnp.concatenate([h.astype(BF16) for h in _split_heads(qblk, g, lo)], axis=0)
        kt = kk[g][pl.ds(start, span), :]
        vt = vv[g][pl.ds(start, span), :]
        s = _dot_nt(q, kt) + bias[slot, g]
        sink = jnp.full((rows, 1), sink_ref[g, N_REP - 1], F32)
        for r in range(N_REP - 1):
            sink = jnp.where(rb == r, sink_ref[g, r], sink)
        m = jnp.maximum(jnp.max(s, axis=1, keepdims=True), sink)
        p = jnp.exp(s - m)
        den = jnp.sum(p, axis=1, keepdims=True) + jnp.exp(sink - m)
        o = _dot(p.astype(BF16), vt) / den
        merged = _merge_heads([o[r * tq:(r + 1) * tq] for r in range(N_REP)], lo)
        for k in range(2):
            c = 2 * g + k
            o_ref[:, c * LANE:(c + 1) * LANE] = merged[k].astype(o_ref.dtype)


def _swa(sinks, main, b, seq, tq=128):
    nq = seq // tq
    n = b * seq
    rows = N_REP * tq

    def col(c):
        return pl.BlockSpec((seq, LANE), lambda bi, i, c=c: (bi, c))

    return pl.pallas_call(
        functools.partial(_swa_kernel, tq=tq),
        grid=(b, nq),
        in_specs=[
            pl.BlockSpec(memory_space=pltpu.SMEM),
            pl.BlockSpec((tq, 4 * LANE), lambda bi, i: (bi * nq + i, QS_COL // 4)),
            col(KSW_COL), col(KSW_COL + 1), col(VSW_COL), col(VSW_COL + 1),
        ],
        out_specs=pl.BlockSpec((tq, 4 * LANE), lambda bi, i: (bi * nq + i, 0)),
        out_shape=jax.ShapeDtypeStruct((n, 4 * LANE), BF16),
        scratch_shapes=[pltpu.VMEM((2, N_GROUPS, rows, SWA_WINDOW + tq), F32)],
        compiler_params=pltpu.CompilerParams(
            dimension_semantics=("arbitrary", "arbitrary"), vmem_limit_bytes=VMEM_LIMIT),
        name="swa",
    )(sinks, main, main, main, main, main)


def _outmlp_kernel(on_ref, os_ref, x_ref, wo_ref, g2_ref, g3_ref, wu_ref, wd_ref, g4_ref, o_ref, *, ff_chunk):
    half = N_HEADS * HEAD_DIM // 2
    mix = _dot(on_ref[...], wo_ref[0:half, :]) + _dot(os_ref[...], wo_ref[half:2 * half, :])
    h1 = x_ref[...] + _rms(mix, g2_ref[...])
    m = _rms(h1, g3_ref[...]).astype(BF16)
    acc = jnp.zeros(h1.shape, F32)
    for c in range(D_FF // ff_chunk):
        u = _dot(m, wu_ref[:, c * ff_chunk:(c + 1) * ff_chunk])
        u = jnp.square(jnp.maximum(u, 0.0)).astype(BF16)
        acc = acc + _dot(u, wd_ref[c * ff_chunk:(c + 1) * ff_chunk, :])
    o_ref[...] = h1 + _rms(acc, g4_ref[...])


def _outmlp(onsa, oswa, x2, wo, g2, g3, wu, wd, g4, tm=512, ff_chunk=1024):
    n = x2.shape[0]

    def const(shape):
        return pl.BlockSpec(shape, lambda i: (0, 0), pipeline_mode=pl.Buffered(1))

    return pl.pallas_call(
        functools.partial(_outmlp_kernel, ff_chunk=ff_chunk),
        grid=(n // tm,),
        in_specs=[
            pl.BlockSpec((tm, 4 * LANE), lambda i: (i, 0)),
            pl.BlockSpec((tm, 4 * LANE), lambda i: (i, 0)),
            pl.BlockSpec((tm, D_MODEL), lambda i: (i, 0)),
            const((D_MODEL, D_MODEL)),
            const((1, D_MODEL)),
            const((1, D_MODEL)),
            const((D_MODEL, D_FF)),
            const((D_FF, D_MODEL)),
            const((1, D_MODEL)),
        ],
        out_specs=pl.BlockSpec((tm, D_MODEL), lambda i: (i, 0)),
        out_shape=jax.ShapeDtypeStruct((n, D_MODEL), F32),
        compiler_params=pltpu.CompilerParams(
            dimension_semantics=("arbitrary",), vmem_limit_bytes=VMEM_LIMIT),
        name="outmlp",
    )(onsa, oswa, x2, wo, g2, g3, wu, wd, g4)


def _pad_groups(w):
    z = jnp.zeros((w.shape[0], HEAD_DIM), w.dtype)
    return jnp.concatenate([w[:, :HEAD_DIM], z, w[:, HEAD_DIM:], z], axis=1)


def _layout_w_in(w):
    sizes = [512, 128, 128, 128, 128, 128, 128, N_HEADS // 2 * N_BRANCH, 512, 128, 128]
    offs = np.concatenate([[0], np.cumsum(sizes)])
    q_n, kc, vc, ks, vs, kw, vw, gt, q_s, k_s, v_s = [w[:, offs[k]:offs[k + 1]] for k in range(len(sizes))]
    gt = jnp.concatenate([gt, jnp.zeros((w.shape[0], LANE - gt.shape[1]), w.dtype)], axis=1)
    cols = [q_n, q_s] + [_pad_groups(t) for t in (ks, vs, kw, vw, k_s, v_s)] + [kc, vc, gt]
    return jnp.concatenate(cols, axis=1).astype(BF16)


def _layout_w1(w1):
    w = w1.reshape(2, CMP_STRIDE, HEAD_DIM, CMP_HIDDEN)
    z = jnp.zeros_like(w)
    top = jnp.concatenate([w, z], axis=-1)
    bot = jnp.concatenate([z, w], axis=-1)
    e = jnp.stack([top, bot], axis=2)
    return e.reshape(2, CMP_STRIDE * 2 * HEAD_DIM, 2 * CMP_HIDDEN).astype(BF16)


def _layout_w2(w2):
    z = jnp.zeros((CMP_HIDDEN, HEAD_DIM), w2.dtype)
    top = jnp.concatenate([w2, z, z, z], axis=1)
    bot = jnp.concatenate([z, z, w2, z], axis=1)
    return jnp.concatenate([top, bot], axis=0).astype(BF16)


def _layout_pe(pe):
    p = pe.reshape(2, CMP_STRIDE, 1, HEAD_DIM)
    return jnp.broadcast_to(p, (2, CMP_STRIDE, 2, HEAD_DIM)).reshape(2, 1, CMP_STRIDE * 2 * HEAD_DIM)


def kernel(x, norm_mix_pre, w_in, cmp_pe_k, cmp_w1_k, cmp_w2_k, cmp_pe_v, cmp_w1_v, cmp_w2_v,
           sinks, w_out, norm_mix_post, norm_mlp_pre, w_up, w_down, norm_mlp_post):
    b, seq, _ = x.shape
    depth = w_in.shape[0]
    h = x.reshape(b * seq, D_MODEL)
    for li in range(depth):
        main, kc, vc, gates = _inproj(h, norm_mix_pre[li][None], _layout_w_in(w_in[li]), seq)
        z = jnp.stack([kc, vc]).reshape(2, b, seq // CMP_STRIDE, CMP_STRIDE * LANE)
        pe2 = jnp.stack([_layout_pe(cmp_pe_k[li]), _layout_pe(cmp_pe_v[li])])
        w1e = jnp.stack([_layout_w1(cmp_w1_k[li]), _layout_w1(cmp_w1_v[li])])
        w2e = jnp.stack([_layout_w2(cmp_w2_k[li]), _layout_w2(cmp_w2_v[li])])
        cmp_kv = _compress(z, pe2, w1e, w2e)
        ocmp, qaug = _cmp(main, cmp_kv, b, seq)
        onsa = _slcwin(qaug, main, gates, ocmp, b, seq)
        oswa = _swa(sinks[li], main, b, seq)
        h = _outmlp(onsa, oswa, h, w_out[li].astype(BF16), norm_mix_post[li][None], norm_mlp_pre[li][None],
                    w_up[li].astype(BF16), w_down[li].astype(BF16), norm_mlp_post[li][None])
    return h.reshape(b, seq, D_MODEL)
```

```python
import functools
import math

import jax
import jax.numpy as jnp
import numpy as np
from jax import lax
from jax.experimental import pallas as pl
from jax.experimental.pallas import tpu as pltpu

F32 = jnp.float32
BF16 = jnp.bfloat16

D_MODEL = 1024
HEAD_DIM = 64
N_HEADS = 16
N_GROUPS = 2
N_REP = 4
N_QH = N_GROUPS * N_REP
CMP_BLOCK = 32
CMP_STRIDE = 16
CMP_HIDDEN = 4 * HEAD_DIM
SEL_BLOCK = 64
SEL_TOPN = 8
N_SEL = 32
NSA_WINDOW = 512
SWA_WINDOW = 128
D_FF = 4 * D_MODEL
NORM_EPS = 1e-6
N_BRANCH = 3

LANE = 128
LOG2E = math.log2(math.e)
NEG = -1e30
SEL_NEG = -(2.0 ** 100)
VMEM_LIMIT = 56 * 1024 * 1024

TK = 256
VT_ROWS = 80
SEL_ROW = HEAD_DIM
POS_ROW = HEAD_DIM + N_SEL
N_POS = 3

QS_COL = 0
KS_COL, KW_COL, KSW_COL, VSW_COL = 4, 6, 8, 10
MAIN_COLS = 12 * LANE
STD_COLS = MAIN_COLS + 2 * LANE
QT_ROWS = N_QH * HEAD_DIM
VST_ROW, VWT_ROW = QT_ROWS, QT_ROWS + N_GROUPS * HEAD_DIM
GT_ROW = VWT_ROW + N_GROUPS * HEAD_DIM
GT_ROWS = 32
T_ROWS = GT_ROW + GT_ROWS


def _slopes():
    s = 2.0 ** (-8.0 * (np.arange(N_HEADS) + 1) / N_HEADS)
    nsa = s[0::2].reshape(N_GROUPS, N_REP)
    swa = s[1::2].reshape(N_GROUPS, N_REP)
    return nsa, swa


SLOPES_NSA, SLOPES_SWA = _slopes()


def _bf16_round(x):
    u = np.float32(x).reshape(1).view(np.uint32)
    u = (u + (((u >> 16) & 1) + 0x7FFF)) & np.uint32(0xFFFF0000)
    return float(u.view(np.float32)[0])


def _bf16_pieces(x, n=N_POS):
    out, rem = [], float(np.float32(x))
    for _ in range(n):
        p = _bf16_round(rem)
        out.append(p)
        rem = float(np.float32(rem - p))
    return out


_NT = (((1,), (1,)), ((), ()))


def _dot_nt(a, b, **kw):
    return lax.dot_general(a, b, _NT, preferred_element_type=F32, **kw)


def _dot(a, b, **kw):
    return jnp.dot(a, b, preferred_element_type=F32, **kw)


def _rms(v, g):
    return v * lax.rsqrt(jnp.mean(v * v, axis=-1, keepdims=True) + NORM_EPS) * g


def _head_rows(vals, tq):
    rb = lax.broadcasted_iota(jnp.int32, (N_REP * tq, 1), 0) // tq
    out = jnp.full((N_REP * tq, 1), float(vals[N_REP - 1]), F32)
    for r in range(N_REP - 1):
        out = jnp.where(rb == r, float(vals[r]), out)
    return out


def _split_heads(qblk, g, lo):
    outs = []
    for r in range(N_REP):
        c = 2 * g + r // 2
        h = qblk[:, c * LANE:(c + 1) * LANE].astype(F32)
        if r % 2:
            h = pltpu.roll(h, HEAD_DIM, 1)
        outs.append(jnp.where(lo, h, 0.0))
    return outs


def _merge_heads(os_, lo):
    return [jnp.where(lo, os_[2 * k], pltpu.roll(os_[2 * k + 1], HEAD_DIM, 1)) for k in range(2)]


def _inproj_kernel(x_ref, g_ref, w_ref, wT_ref, main_ref, kc_ref, vc_ref, qT_ref, vsT_ref, vwT_ref, gT_ref,
                   *, tm, seq):
    x = x_ref[...]
    a = _rms(x, g_ref[...]).astype(BF16)
    qscale = LOG2E * HEAD_DIM ** -0.5

    res = _dot(a, w_ref[...])
    nq = 4 * LANE
    main_ref[:, 0:nq] = (res[:, 0:nq] * qscale).astype(BF16)
    t0 = (pl.program_id(0) * tm) % seq
    row = lax.broadcasted_iota(jnp.int32, (tm, LANE), 0)
    lane = lax.broadcasted_iota(jnp.int32, (tm, LANE), 1)
    pos = t0 + row
    posf = jnp.where((lane >= POS_ROW) & (lane < POS_ROW + N_POS), (pos % TK).astype(F32), 0.0)
    onehot = jnp.where(lane == pos // SEL_BLOCK + SEL_ROW, 1.0, 0.0)
    for g in range(N_GROUPS):
        c0 = (KS_COL + g) * LANE
        main_ref[:, c0:c0 + LANE] = (res[:, c0:c0 + LANE] + (posf + onehot)).astype(BF16)
        c0 = (KW_COL + g) * LANE
        main_ref[:, c0:c0 + LANE] = (res[:, c0:c0 + LANE] + posf).astype(BF16)
    c1 = KSW_COL * LANE
    main_ref[:, c1:MAIN_COLS] = res[:, c1:MAIN_COLS].astype(BF16)
    kc_ref[...] = res[:, MAIN_COLS:MAIN_COLS + LANE].astype(kc_ref.dtype)
    vc_ref[...] = res[:, MAIN_COLS + LANE:STD_COLS].astype(vc_ref.dtype)

    resT = _dot_nt(wT_ref[...], a)
    qT_ref[0] = (resT[0:QT_ROWS] * qscale).astype(BF16)
    ones_blk = jnp.where(lax.broadcasted_iota(jnp.int32, (VT_ROWS - HEAD_DIM, TK), 0) == 0, 1.0, 0.0).astype(BF16)
    for out_ref, r0 in ((vsT_ref, VST_ROW), (vwT_ref, VWT_ROW)):
        for g in range(N_GROUPS):
            for kt in range(tm // TK):
                blk = resT[r0 + g * HEAD_DIM:r0 + (g + 1) * HEAD_DIM, kt * TK:(kt + 1) * TK]
                out_ref[0, g, kt, 0:HEAD_DIM, :] = blk.astype(BF16)
                out_ref[0, g, kt, HEAD_DIM:VT_ROWS, :] = ones_blk
    gT_ref[0] = jax.nn.sigmoid(resT[GT_ROW:T_ROWS])


def _inproj(x2, gain, w_p, wT_p, b, seq, tm=512):
    n = x2.shape[0]
    nt = seq // tm
    return pl.pallas_call(
        functools.partial(_inproj_kernel, tm=tm, seq=seq),
        grid=(n // tm,),
        in_specs=[
            pl.BlockSpec((tm, D_MODEL), lambda i: (i, 0)),
            pl.BlockSpec((1, D_MODEL), lambda i: (0, 0)),
            pl.BlockSpec((D_MODEL, STD_COLS), lambda i: (0, 0)),
            pl.BlockSpec((T_ROWS, D_MODEL), lambda i: (0, 0)),
        ],
        out_specs=[
            pl.BlockSpec((tm, MAIN_COLS), lambda i: (i, 0)),
            pl.BlockSpec((tm, LANE), lambda i: (i, 0)),
            pl.BlockSpec((tm, LANE), lambda i: (i, 0)),
            pl.BlockSpec((1, QT_ROWS, tm), lambda i: (i // nt, 0, i % nt)),
            pl.BlockSpec((1, N_GROUPS, tm // TK, VT_ROWS, TK), lambda i: (i // nt, 0, i % nt, 0, 0)),
            pl.BlockSpec((1, N_GROUPS, tm // TK, VT_ROWS, TK), lambda i: (i // nt, 0, i % nt, 0, 0)),
            pl.BlockSpec((1, GT_ROWS, tm), lambda i: (i // nt, 0, i % nt)),
        ],
        out_shape=[
            jax.ShapeDtypeStruct((n, MAIN_COLS), BF16),
            jax.ShapeDtypeStruct((n, LANE), BF16),
            jax.ShapeDtypeStruct((n, LANE), BF16),
            jax.ShapeDtypeStruct((b, QT_ROWS, seq), BF16),
            jax.ShapeDtypeStruct((b, N_GROUPS, seq // TK, VT_ROWS, TK), BF16),
            jax.ShapeDtypeStruct((b, N_GROUPS, seq // TK, VT_ROWS, TK), BF16),
            jax.ShapeDtypeStruct((b, GT_ROWS, seq), F32),
        ],
        compiler_params=pltpu.CompilerParams(
            dimension_semantics=("arbitrary",), vmem_limit_bytes=VMEM_LIMIT),
        name="inproj",
    )(x2, gain, w_p, wT_p)


def _compress_kernel(z_ref, pe_ref, w1_ref, w2_ref, w2T_ref, o_ref, oT_ref):
    z = z_ref[0, 0].astype(F32)
    zt = (z + pe_ref[0, 0]).astype(BF16)
    zb = (z + pe_ref[0, 1]).astype(BF16)
    a = _dot(zt, w1_ref[0, 0])
    bm = _dot(zb, w1_ref[0, 1])
    n_rows = a.shape[0]
    h = a + pltpu.roll(bm, n_rows - 1, 0)
    hg = jax.nn.gelu(h).astype(BF16)
    o_ref[0, 0] = _dot(hg, w2_ref[0]).astype(o_ref.dtype)
    oT_ref[0, 0] = _dot_nt(w2T_ref[0], hg).astype(oT_ref.dtype)


def _compress(z, pe2, w1e, w2e, w2eT):
    _, b, n, k = z.shape
    return pl.pallas_call(
        _compress_kernel,
        grid=(2, b),
        in_specs=[
            pl.BlockSpec((1, 1, n, k), lambda t, i: (t, i, 0, 0)),
            pl.BlockSpec((1, 2, 1, k), lambda t, i: (t, 0, 0, 0)),
            pl.BlockSpec((1, 2, k, 2 * CMP_HIDDEN), lambda t, i: (t, 0, 0, 0)),
            pl.BlockSpec((1, 2 * CMP_HIDDEN, 2 * LANE), lambda t, i: (t, 0, 0)),
            pl.BlockSpec((1, 2 * LANE, 2 * CMP_HIDDEN), lambda t, i: (t, 0, 0)),
        ],
        out_specs=[
            pl.BlockSpec((1, 1, n, 2 * LANE), lambda t, i: (t, i, 0, 0)),
            pl.BlockSpec((1, 1, 2 * LANE, n), lambda t, i: (t, i, 0, 0)),
        ],
        out_shape=[
            jax.ShapeDtypeStruct((2, b, n, 2 * LANE), BF16),
            jax.ShapeDtypeStruct((2, b, 2 * LANE, n), BF16),
        ],
        compiler_params=pltpu.CompilerParams(
            dimension_semantics=("arbitrary", "arbitrary"), vmem_limit_bytes=VMEM_LIMIT),
        name="compress",
    )(z, pe2, w1e, w2e, w2eT)


def _cmp_kernel(qT_ref, k_ref, vT_ref, ocT_ref, qaT_ref, *, tq, n_cmp):
    i = pl.program_id(1)
    t0 = i * tq
    n_pad = k_ref.shape[2]
    nn = lax.broadcasted_iota(jnp.int32, (n_pad, tq), 0)
    tt = t0 + lax.broadcasted_iota(jnp.int32, (n_pad, tq), 1)
    okc = (tt >= nn * CMP_STRIDE + CMP_BLOCK - 1) & (nn < n_cmp)
    cend = (lax.broadcasted_iota(jnp.int32, (n_pad, LANE), 0) * CMP_STRIDE + CMP_BLOCK - 1).astype(F32)

    jr = lax.broadcasted_iota(jnp.int32, (N_SEL, n_pad), 0)
    nc = lax.broadcasted_iota(jnp.int32, (N_SEL, n_pad), 1)
    ov = ((nc * CMP_STRIDE < jr * SEL_BLOCK + SEL_BLOCK) & (nc * CMP_STRIDE + CMP_BLOCK > jr * SEL_BLOCK)
          & (nc < n_cmp))
    ovT = jnp.where(ov, 1.0, 0.0).astype(F32)

    jj = lax.broadcasted_iota(jnp.int32, (N_SEL, tq), 0)
    blk_t = (t0 + lax.broadcasted_iota(jnp.int32, (N_SEL, tq), 1)) // SEL_BLOCK
    valid = jj <= blk_t
    forced = (jj == 0) | (jj == blk_t) | (jj == blk_t - 1)
    prow = lax.broadcasted_iota(jnp.int32, (LANE - POS_ROW, tq), 0)

    for g in range(N_GROUPS):
        kc = k_ref[0, 0][:, g * LANE:g * LANE + HEAD_DIM]
        vcT = vT_ref[0, 0][g * LANE:g * LANE + HEAD_DIM, :]
        psum = jnp.zeros((n_pad, tq), F32)
        qTs = []
        for r in range(N_REP):
            h = N_REP * g + r
            qT = qT_ref[0, h * HEAD_DIM:(h + 1) * HEAD_DIM, :]
            qTs.append(qT)
            bias = float(SLOPES_NSA[g, r] * LOG2E) * cend
            s = _dot(kc, qT) + jnp.concatenate([bias] * (tq // LANE), axis=1)
            sm = jnp.where(okc, s, NEG)
            m = jnp.max(sm, axis=0, keepdims=True)
            e = jnp.where(okc, jnp.exp2(sm - m), 0.0)
            den = jnp.sum(e, axis=0, keepdims=True)
            p = e / jnp.where(den > 0, den, 1.0)
            ocT_ref[0, h * HEAD_DIM:(h + 1) * HEAD_DIM, :] = _dot(vcT, p.astype(BF16))
            psum = psum + p

        imp = _dot(ovT, psum, precision=lax.Precision.HIGHEST)
        score = jnp.where(valid, imp, -jnp.inf)
        score = jnp.where(forced & valid, jnp.inf, score)
        rank = jnp.zeros((N_SEL, tq), jnp.int32)
        for k in range(N_SEL):
            sk = score[k:k + 1, :]
            ahead = (sk > score) | ((sk == score) & (jj > k))
            rank = rank + ahead.astype(jnp.int32)
        negm = jnp.where(rank < SEL_TOPN, 0.0, SEL_NEG).astype(BF16)
        for r in range(N_REP):
            h = N_REP * g + r
            pieces = _bf16_pieces(SLOPES_NSA[g, r] * LOG2E)
            feat = jnp.zeros((LANE - POS_ROW, tq), F32)
            for k, pc in enumerate(pieces):
                feat = jnp.where(prow == k, pc, feat)
            qaT_ref[0, h] = jnp.concatenate([qTs[r], negm, feat.astype(BF16)], axis=0)


def _cmp(qT, cmp_k, cmp_vT, b, seq, tq=256):
    nq = seq // tq
    n_cmp = (seq - CMP_BLOCK) // CMP_STRIDE + 1
    n_pad = cmp_k.shape[2]
    return pl.pallas_call(
        functools.partial(_cmp_kernel, tq=tq, n_cmp=n_cmp),
        grid=(b, nq),
        in_specs=[
            pl.BlockSpec((1, QT_ROWS, tq), lambda bi, i: (bi, 0, i)),
            pl.BlockSpec((1, 1, n_pad, 2 * LANE), lambda bi, i: (0, bi, 0, 0)),
            pl.BlockSpec((1, 1, 2 * LANE, n_pad), lambda bi, i: (1, bi, 0, 0)),
        ],
        out_specs=[
            pl.BlockSpec((1, QT_ROWS, tq), lambda bi, i: (bi, 0, i)),
            pl.BlockSpec((1, N_QH, LANE, tq), lambda bi, i: (bi, 0, 0, i)),
        ],
        out_shape=[
            jax.ShapeDtypeStruct((b, QT_ROWS, seq), F32),
            jax.ShapeDtypeStruct((b, N_QH, LANE, seq), BF16),
        ],
        compiler_params=pltpu.CompilerParams(
            dimension_semantics=("arbitrary", "arbitrary"), vmem_limit_bytes=VMEM_LIMIT),
        name="cmp",
    )(qT, cmp_k, cmp_vT)


def _score_phase(qT_ref, k_tiles, mask, s_buf, mt_buf):
    for g in range(N_GROUPS):
        for r in range(N_REP):
            h = N_REP * g + r
            s = _dot(k_tiles[g], qT_ref[0, h])
            if mask is not None:
                s = s + mask
            s_buf[h] = s
            mt_buf[h] = jnp.max(s, axis=0, keepdims=True)


def _value_phase(vT_tiles, key_base, slopes, s_buf, mt_buf, m_scr, acc_scr):
    for g in range(N_GROUPS):
        for r in range(N_REP):
            h = N_REP * g + r
            sh = float(slopes[g, r] * LOG2E) * key_base
            m_old = m_scr[h]
            m_new = jnp.maximum(m_old, mt_buf[h] + sh)
            alpha = jnp.exp2(m_old - m_new)
            pT = jnp.exp2(s_buf[h] - (m_new - sh)).astype(BF16)
            acc_scr[h] = alpha * acc_scr[h] + _dot(vT_tiles[g], pT)
            m_scr[h] = m_new


def _sweep_reset(m_scr, acc_scr):
    m_scr[...] = jnp.full(m_scr.shape, -3e38, F32)
    acc_scr[...] = jnp.zeros(acc_scr.shape, F32)


def _slcwin_kernel(qT_ref, ks0, ks1, vsT0, vsT1, kw0, kw1, vwT0, vwT1, gT_ref, ocT_ref,
                   o_ref, caus, s_scr, mt_scr, m_s, acc_s, m_w, acc_w, *, tq):
    tk = TK
    bi = pl.program_id(0)
    i = pl.program_id(1)
    assert NSA_WINDOW // tk == 2 and tq == tk

    @pl.when((bi == 0) & (i == 0))
    def _init():
        kk = lax.broadcasted_iota(jnp.int32, (tk, tq), 0)
        qq = lax.broadcasted_iota(jnp.int32, (tk, tq), 1)
        caus[0] = jnp.where(kk <= qq, 0.0, NEG)
        caus[1] = jnp.where(kk > qq, 0.0, NEG)
        caus[2] = jnp.full((tk, tq), NEG, F32)
        caus[3] = jnp.zeros((tk, tq), F32)

    ks = (ks0, ks1)
    vsT = (vsT0, vsT1)
    kw = (kw0, kw1)
    vwT = (vwT0, vwT1)

    def k_at(refs, idx):
        off = pl.multiple_of(idx * tk, tk)
        return [refs[g][pl.ds(off, tk), :] for g in range(N_GROUPS)]

    def score(refs, idx, mask, buf):
        _score_phase(qT_ref, k_at(refs, idx), mask, s_scr.at[buf], mt_scr.at[buf])

    def value(refsT, idx, buf, m_scr, acc_scr):
        _value_phase([refsT[g][idx] for g in range(N_GROUPS)], (idx * tk).astype(F32), SLOPES_NSA,
                     s_scr.at[buf], mt_scr.at[buf], m_scr, acc_scr)

    _sweep_reset(m_s, acc_s)
    _sweep_reset(m_w, acc_w)

    w2 = jnp.maximum(i - 2, 0)
    w1 = jnp.maximum(i - 1, 0)
    score(kw, w2, caus[jnp.where(i >= 2, 1, 2)], 0)
    score(kw, w1, caus[jnp.where(i >= 1, 3, 2)], 1)
    value(vwT, w2, 0, m_w, acc_w)
    score(kw, i, caus[0], 0)
    value(vwT, w1, 1, m_w, acc_w)
    score(ks, 0, caus[jnp.where(i == 0, 0, 3)], 1)
    value(vwT, i, 0, m_w, acc_w)

    def step(j, parity, mask=None):
        score(ks, j + 1, mask, parity)
        value(vsT, j, 1 - parity, m_s, acc_s)

    def body(jj, carry):
        step(2 * jj, 0)
        step(2 * jj + 1, 1)
        return carry

    n_mid = jnp.maximum(i - 1, 0)
    lax.fori_loop(0, n_mid // 2, body, 0)

    @pl.when(n_mid % 2 == 1)
    def _odd():
        step(i - 2, 0)

    @pl.when(i % 2 == 1)
    def _diag_odd():
        step(i - 1, 0, caus[0])
        value(vsT, i, 0, m_s, acc_s)

    @pl.when((i % 2 == 0) & (i > 0))
    def _diag_even():
        step(i - 1, 1, caus[0])

    @pl.when(i % 2 == 0)
    def _last_even():
        value(vsT, i, 1, m_s, acc_s)

    gT = gT_ref[0]
    for g in range(N_GROUPS):
        outs = []
        for r in range(N_REP):
            h = N_REP * g + r
            gc = h * N_BRANCH
            o_slc = acc_s[h, 0:HEAD_DIM, :] / acc_s[h, HEAD_DIM:HEAD_DIM + 1, :]
            o_win = acc_w[h, 0:HEAD_DIM, :] / acc_w[h, HEAD_DIM:HEAD_DIM + 1, :]
            outs.append(gT[gc:gc + 1] * ocT_ref[0, h * HEAD_DIM:(h + 1) * HEAD_DIM, :]
                        + gT[gc + 1:gc + 2] * o_slc
                        + gT[gc + 2:gc + 3] * o_win)
        oT = jnp.concatenate(outs, axis=0)
        o_ref[:, g * 2 * LANE:(g + 1) * 2 * LANE] = oT.T.astype(o_ref.dtype)


def _slcwin(qaT, main, vsT, vwT, gT, ocT, b, seq, tq=TK):
    nq = seq // tq
    n = b * seq

    def col(c):
        return pl.BlockSpec((seq, LANE), lambda bi, i, c=c: (bi, c))

    def vt(g):
        return pl.BlockSpec((None, None, seq // TK, VT_ROWS, TK), lambda bi, i, g=g: (bi, g, 0, 0, 0))

    return pl.pallas_call(
        functools.partial(_slcwin_kernel, tq=tq),
        grid=(b, nq),
        in_specs=[
            pl.BlockSpec((1, N_QH, LANE, tq), lambda bi, i: (bi, 0, 0, i)),
            col(KS_COL), col(KS_COL + 1), vt(0), vt(1),
            col(KW_COL), col(KW_COL + 1), vt(0), vt(1),
            pl.BlockSpec((1, GT_ROWS, tq), lambda bi, i: (bi, 0, i)),
            pl.BlockSpec((1, QT_ROWS, tq), lambda bi, i: (bi, 0, i)),
        ],
        out_specs=pl.BlockSpec((tq, 4 * LANE), lambda bi, i: (bi * nq + i, 0)),
        out_shape=jax.ShapeDtypeStruct((n, 4 * LANE), BF16),
        scratch_shapes=[
            pltpu.VMEM((4, TK, tq), F32),
            pltpu.VMEM((2, N_QH, TK, tq), F32),
            pltpu.VMEM((2, N_QH, 1, tq), F32),
            pltpu.VMEM((N_QH, 1, tq), F32),
            pltpu.VMEM((N_QH, VT_ROWS, tq), F32),
            pltpu.VMEM((N_QH, 1, tq), F32),
            pltpu.VMEM((N_QH, VT_ROWS, tq), F32),
        ],
        compiler_params=pltpu.CompilerParams(
            dimension_semantics=("arbitrary", "arbitrary"), vmem_limit_bytes=VMEM_LIMIT),
        name="slcwin",
    )(qaT, main, main, vsT, vsT, main, main, vwT, vwT, gT, ocT)


def _swa_kernel(sink_ref, q_ref, k0, k1, v0, v1, o_ref, bias, *, tq):
    bi = pl.program_id(0)
    i = pl.program_id(1)
    rows = N_REP * tq
    span = SWA_WINDOW + tq

    @pl.when((bi == 0) & (i == 0))
    def _init():
        col = lax.broadcasted_iota(jnp.int32, (rows, span), 1)
        row = lax.broadcasted_iota(jnp.int32, (rows, span), 0) % tq
        for g in range(N_GROUPS):
            sl = _head_rows(SLOPES_SWA[g] * LOG2E, tq)
            dist = row + SWA_WINDOW - col
            bias[0, g] = jnp.where((dist >= 0) & (dist < SWA_WINDOW), -sl * dist.astype(F32), NEG)
            dist = row - col
            bias[1, g] = jnp.where((dist >= 0) & (dist < SWA_WINDOW), -sl * dist.astype(F32), NEG)

    lane = lax.broadcasted_iota(jnp.int32, (tq, LANE), 1)
    lo = lane < HEAD_DIM
    qblk = q_ref[...]
    kk = (k0, k1)
    vv = (v0, v1)
    start = pl.multiple_of(jnp.maximum(i * tq - SWA_WINDOW, 0), LANE)
    slot = jnp.where(i == 0, 1, 0)
    rb = lax.broadcasted_iota(jnp.int32, (rows, 1), 0) // tq
    for g in range(N_GROUPS):
        q = jnp.concatenate([h.astype(BF16) for h in _split_heads(qblk, g, lo)], axis=0)
        kt = kk[g][pl.ds(start, span), :]
        vt = vv[g][pl.ds(start, span), :]
        s = _dot_nt(q, kt) + bias[slot, g]
        sink = jnp.full((rows, 1), sink_ref[g, N_REP - 1], F32)
        for r in range(N_REP - 1):
            sink = jnp.where(rb == r, sink_ref[g, r], sink)
        sink = sink * LOG2E
        m = jnp.maximum(jnp.max(s, axis=1, keepdims=True), sink)
        p = jnp.exp2(s - m)
        den = jnp.sum(p, axis=1, keepdims=True) + jnp.exp2(sink - m)
        o = _dot(p.astype(BF16), vt) / den
        merged = _merge_heads([o[r * tq:(r + 1) * tq] for r in range(N_REP)], lo)
        for k in range(2):
            c = 2 * g + k
            o_ref[:, c * LANE:(c + 1) * LANE] = merged[k].astype(o_ref.dtype)


def _swa(sinks, main, b, seq, tq=128):
    nq = seq // tq
    n = b * seq
    rows = N_REP * tq

    def col(c):
        return pl.BlockSpec((seq, LANE), lambda bi, i, c=c: (bi, c))

    return pl.pallas_call(
        functools.partial(_swa_kernel, tq=tq),
        grid=(b, nq),
        in_specs=[
            pl.BlockSpec(memory_space=pltpu.SMEM),
            pl.BlockSpec((tq, 4 * LANE), lambda bi, i: (bi * nq + i, QS_COL // 4)),
            col(KSW_COL), col(KSW_COL + 1), col(VSW_COL), col(VSW_COL + 1),
        ],
        out_specs=pl.BlockSpec((tq, 4 * LANE), lambda bi, i: (bi * nq + i, 0)),
        out_shape=jax.ShapeDtypeStruct((n, 4 * LANE), BF16),
        scratch_shapes=[pltpu.VMEM((2, N_GROUPS, rows, SWA_WINDOW + tq), F32)],
        compiler_params=pltpu.CompilerParams(
            dimension_semantics=("arbitrary", "arbitrary"), vmem_limit_bytes=VMEM_LIMIT),
        name="swa",
    )(sinks, main, main, main, main, main)


def _outmlp_kernel(on_ref, os_ref, x_ref, wo_ref, g2_ref, g3_ref, wu_ref, wd_ref, g4_ref, o_ref, *, ff_chunk):
    half = N_HEADS * HEAD_DIM // 2
    mix = _dot(on_ref[...], wo_ref[0:half, :]) + _dot(os_ref[...], wo_ref[half:2 * half, :])
    h1 = x_ref[...] + _rms(mix, g2_ref[...])
    m = _rms(h1, g3_ref[...]).astype(BF16)
    acc = jnp.zeros(h1.shape, F32)
    for c in range(D_FF // ff_chunk):
        u = _dot(m, wu_ref[:, c * ff_chunk:(c + 1) * ff_chunk])
        u = jnp.square(jnp.maximum(u, 0.0)).astype(BF16)
        acc = acc + _dot(u, wd_ref[c * ff_chunk:(c + 1) * ff_chunk, :])
    o_ref[...] = h1 + _rms(acc, g4_ref[...])


def _outmlp(onsa, oswa, x2, wo, g2, g3, wu, wd, g4, tm=512, ff_chunk=1024):
    n = x2.shape[0]

    def const(shape):
        return pl.BlockSpec(shape, lambda i: (0, 0), pipeline_mode=pl.Buffered(1))

    return pl.pallas_call(
        functools.partial(_outmlp_kernel, ff_chunk=ff_chunk),
        grid=(n // tm,),
        in_specs=[
            pl.BlockSpec((tm, 4 * LANE), lambda i: (i, 0)),
            pl.BlockSpec((tm, 4 * LANE), lambda i: (i, 0)),
            pl.BlockSpec((tm, D_MODEL), lambda i: (i, 0)),
            const((D_MODEL, D_MODEL)),
            const((1, D_MODEL)),
            const((1, D_MODEL)),
            const((D_MODEL, D_FF)),
            const((D_FF, D_MODEL)),
            const((1, D_MODEL)),
        ],
        out_specs=pl.BlockSpec((tm, D_MODEL), lambda i: (i, 0)),
        out_shape=jax.ShapeDtypeStruct((n, D_MODEL), F32),
        compiler_params=pltpu.CompilerParams(
            dimension_semantics=("arbitrary",), vmem_limit_bytes=VMEM_LIMIT),
        name="outmlp",
    )(onsa, oswa, x2, wo, g2, g3, wu, wd, g4)


def _pad_groups(w):
    z = jnp.zeros((w.shape[0], HEAD_DIM), w.dtype)
    return jnp.concatenate([w[:, :HEAD_DIM], z, w[:, HEAD_DIM:], z], axis=1)


def _layout_w_in(w):
    sizes = [512, 128, 128, 128, 128, 128, 128, N_HEADS // 2 * N_BRANCH, 512, 128, 128]
    offs = np.concatenate([[0], np.cumsum(sizes)])
    q_n, kc, vc, ks, vs, kw, vw, gt, q_s, k_s, v_s = [w[:, offs[k]:offs[k + 1]] for k in range(len(sizes))]
    std = [q_s] + [_pad_groups(t) for t in (ks, kw, k_s, v_s)] + [kc, vc]
    gt = jnp.concatenate([gt, jnp.zeros((w.shape[0], GT_ROWS - gt.shape[1]), w.dtype)], axis=1)
    tr = jnp.concatenate([q_n, vs, vw, gt], axis=1).T
    return jnp.concatenate(std, axis=1).astype(BF16), tr.astype(BF16)


def _layout_w1(w1):
    w = w1.reshape(2, CMP_STRIDE, HEAD_DIM, CMP_HIDDEN)
    z = jnp.zeros_like(w)
    top = jnp.concatenate([w, z], axis=-1)
    bot = jnp.concatenate([z, w], axis=-1)
    e = jnp.stack([top, bot], axis=2)
    return e.reshape(2, CMP_STRIDE * 2 * HEAD_DIM, 2 * CMP_HIDDEN).astype(BF16)


def _layout_w2(w2):
    z = jnp.zeros((CMP_HIDDEN, HEAD_DIM), w2.dtype)
    top = jnp.concatenate([w2, z, z, z], axis=1)
    bot = jnp.concatenate([z, z, w2, z], axis=1)
    return jnp.concatenate([top, bot], axis=0).astype(BF16)


def _layout_pe(pe):
    p = pe.reshape(2, CMP_STRIDE, 1, HEAD_DIM)
    return jnp.broadcast_to(p, (2, CMP_STRIDE, 2, HEAD_DIM)).reshape(2, 1, CMP_STRIDE * 2 * HEAD_DIM)


def kernel(x, norm_mix_pre, w_in, cmp_pe_k, cmp_w1_k, cmp_w2_k, cmp_pe_v, cmp_w1_v, cmp_w2_v,
           sinks, w_out, norm_mix_post, norm_mlp_pre, w_up, w_down, norm_mlp_post):
    b, seq, _ = x.shape
    assert seq // SEL_BLOCK == N_SEL
    depth = w_in.shape[0]
    h = x.reshape(b * seq, D_MODEL)
    for li in range(depth):
        w_p, wT_p = _layout_w_in(w_in[li])
        main, kc, vc, qT, vsT, vwT, gT = _inproj(h, norm_mix_pre[li][None], w_p, wT_p, b, seq)
        z = jnp.stack([kc, vc]).reshape(2, b, seq // CMP_STRIDE, CMP_STRIDE * LANE)
        pe2 = jnp.stack([_layout_pe(cmp_pe_k[li]), _layout_pe(cmp_pe_v[li])])
        w1e = jnp.stack([_layout_w1(cmp_w1_k[li]), _layout_w1(cmp_w1_v[li])])
        w2e = jnp.stack([_layout_w2(cmp_w2_k[li]), _layout_w2(cmp_w2_v[li])])
        cmp_kv, cmp_kvT = _compress(z, pe2, w1e, w2e, jnp.swapaxes(w2e, 1, 2))
        ocT, qaT = _cmp(qT, cmp_kv, cmp_kvT, b, seq)
        onsa = _slcwin(qaT, main, vsT, vwT, gT, ocT, b, seq)
        oswa = _swa(sinks[li], main, b, seq)
        h = _outmlp(onsa, oswa, h, w_out[li].astype(BF16), norm_mix_post[li][None], norm_mlp_pre[li][None],
                    w_up[li].astype(BF16), w_down[li].astype(BF16), norm_mlp_post[li][None])
    return h.reshape(b, seq, D_MODEL)
```

```python
import functools
import math

import jax
import jax.numpy as jnp
import numpy as np
from jax import lax
from jax.experimental import pallas as pl
from jax.experimental.pallas import tpu as pltpu

F32 = jnp.float32
BF16 = jnp.bfloat16

D_MODEL = 1024
HEAD_DIM = 64
N_HEADS = 16
N_GROUPS = 2
N_REP = 4
N_QH = N_GROUPS * N_REP
CMP_BLOCK = 32
CMP_STRIDE = 16
CMP_HIDDEN = 4 * HEAD_DIM
SEL_BLOCK = 64
SEL_TOPN = 8
N_SEL = 32
NSA_WINDOW = 512
SWA_WINDOW = 128
D_FF = 4 * D_MODEL
NORM_EPS = 1e-6
N_BRANCH = 3

LANE = 128
LOG2E = math.log2(math.e)
NEG = -1e30
SEL_NEG = -(2.0 ** 100)
VMEM_LIMIT = 56 * 1024 * 1024

TK = 256
VT_ROWS = 80
SEL_ROW = HEAD_DIM
POS_ROW = HEAD_DIM + N_SEL
N_PIECE = 3
POS_SPLIT = 256

KS_COL, KW_COL, KSW_COL = 0, 2, 4
MAIN_COLS = 6 * LANE
STD_COLS = MAIN_COLS + 2 * LANE
QT_ROWS = N_QH * HEAD_DIM
VT_ROW0 = 2 * QT_ROWS
GT_ROW = VT_ROW0 + 3 * N_GROUPS * HEAD_DIM
GT_ROWS = 32
T_ROWS = GT_ROW + GT_ROWS


def _slopes():
    s = 2.0 ** (-8.0 * (np.arange(N_HEADS) + 1) / N_HEADS)
    nsa = s[0::2].reshape(N_GROUPS, N_REP)
    swa = s[1::2].reshape(N_GROUPS, N_REP)
    return nsa, swa


SLOPES_NSA, SLOPES_SWA = _slopes()


def _bf16_round(x):
    u = np.float32(x).reshape(1).view(np.uint32)
    u = (u + (((u >> 16) & 1) + 0x7FFF)) & np.uint32(0xFFFF0000)
    return float(u.view(np.float32)[0])


def _bf16_pieces(x, n=N_PIECE):
    out, rem = [], float(np.float32(x))
    for _ in range(n):
        p = _bf16_round(rem)
        out.append(p)
        rem = float(np.float32(rem - p))
    return out


_NT = (((1,), (1,)), ((), ()))


def _dot_nt(a, b, **kw):
    return lax.dot_general(a, b, _NT, preferred_element_type=F32, **kw)


def _dot(a, b, **kw):
    return jnp.dot(a, b, preferred_element_type=F32, **kw)


def _rms(v, g):
    return v * lax.rsqrt(jnp.mean(v * v, axis=-1, keepdims=True) + NORM_EPS) * g


def _slope_rows(slope, tq):
    pieces = _bf16_pieces(slope * LOG2E)
    vals = pieces + [p * POS_SPLIT for p in pieces]
    prow = lax.broadcasted_iota(jnp.int32, (LANE - POS_ROW, tq), 0)
    feat = jnp.zeros((LANE - POS_ROW, tq), F32)
    for k, v in enumerate(vals):
        feat = jnp.where(prow == k, v, feat)
    return feat.astype(BF16)


def _inproj_kernel(x_ref, g_ref, w_ref, wT_ref, main_ref, kc_ref, vc_ref, qT_ref, qsT_ref, vT_ref, gT_ref,
                   *, tm, seq):
    x = x_ref[...]
    a = _rms(x, g_ref[...]).astype(BF16)

    res = _dot(a, w_ref[...])
    t0 = (pl.program_id(0) * tm) % seq
    pos = t0 + lax.broadcasted_iota(jnp.int32, (tm, LANE), 0)
    lane = lax.broadcasted_iota(jnp.int32, (tm, LANE), 1)
    posf = jnp.where((lane >= POS_ROW) & (lane < POS_ROW + N_PIECE), (pos % POS_SPLIT).astype(F32), 0.0)
    posf = jnp.where((lane >= POS_ROW + N_PIECE) & (lane < POS_ROW + 2 * N_PIECE),
                     (pos // POS_SPLIT).astype(F32), posf)
    onehot = jnp.where(lane == pos // SEL_BLOCK + SEL_ROW, 1.0, 0.0)
    for c in range(MAIN_COLS // LANE):
        extra = posf + onehot if c in (KS_COL, KS_COL + 1) else posf
        main_ref[:, c * LANE:(c + 1) * LANE] = (res[:, c * LANE:(c + 1) * LANE] + extra).astype(BF16)
    kc_ref[...] = res[:, MAIN_COLS:MAIN_COLS + LANE].astype(kc_ref.dtype)
    vc_ref[...] = res[:, MAIN_COLS + LANE:STD_COLS].astype(vc_ref.dtype)

    resT = _dot_nt(wT_ref[...], a)
    qscale = LOG2E * HEAD_DIM ** -0.5
    qT_ref[0] = (resT[0:QT_ROWS] * qscale).astype(BF16)
    zmid = jnp.zeros((N_SEL, tm), BF16)
    for h in range(N_QH):
        r0 = QT_ROWS + h * HEAD_DIM
        qsT_ref[0, h] = jnp.concatenate(
            [(resT[r0:r0 + HEAD_DIM] * qscale).astype(BF16), zmid,
             _slope_rows(SLOPES_SWA[h // N_REP, h % N_REP], tm)], axis=0)
    ones_blk = jnp.where(lax.broadcasted_iota(jnp.int32, (VT_ROWS - HEAD_DIM, TK), 0) == 0, 1.0, 0.0).astype(BF16)
    for tg in range(3 * N_GROUPS):
        r0 = VT_ROW0 + tg * HEAD_DIM
        for kt in range(tm // TK):
            vT_ref[0, tg, kt, 0:HEAD_DIM, :] = resT[r0:r0 + HEAD_DIM, kt * TK:(kt + 1) * TK].astype(BF16)
            vT_ref[0, tg, kt, HEAD_DIM:VT_ROWS, :] = ones_blk
    gT_ref[0] = jax.nn.sigmoid(resT[GT_ROW:T_ROWS])


def _inproj(x2, gain, w_p, wT_p, b, seq, tm=512):
    n = x2.shape[0]
    nt = seq // tm
    return pl.pallas_call(
        functools.partial(_inproj_kernel, tm=tm, seq=seq),
        grid=(n // tm,),
        in_specs=[
            pl.BlockSpec((tm, D_MODEL), lambda i: (i, 0)),
            pl.BlockSpec((1, D_MODEL), lambda i: (0, 0)),
            pl.BlockSpec((D_MODEL, STD_COLS), lambda i: (0, 0)),
            pl.BlockSpec((T_ROWS, D_MODEL), lambda i: (0, 0)),
        ],
        out_specs=[
            pl.BlockSpec((tm, MAIN_COLS), lambda i: (i, 0)),
            pl.BlockSpec((tm, LANE), lambda i: (i, 0)),
            pl.BlockSpec((tm, LANE), lambda i: (i, 0)),
            pl.BlockSpec((1, QT_ROWS, tm), lambda i: (i // nt, 0, i % nt)),
            pl.BlockSpec((1, N_QH, LANE, tm), lambda i: (i // nt, 0, 0, i % nt)),
            pl.BlockSpec((1, 3 * N_GROUPS, tm // TK, VT_ROWS, TK), lambda i: (i // nt, 0, i % nt, 0, 0)),
            pl.BlockSpec((1, GT_ROWS, tm), lambda i: (i // nt, 0, i % nt)),
        ],
        out_shape=[
            jax.ShapeDtypeStruct((n, MAIN_COLS), BF16),
            jax.ShapeDtypeStruct((n, LANE), BF16),
            jax.ShapeDtypeStruct((n, LANE), BF16),
            jax.ShapeDtypeStruct((b, QT_ROWS, seq), BF16),
            jax.ShapeDtypeStruct((b, N_QH, LANE, seq), BF16),
            jax.ShapeDtypeStruct((b, 3 * N_GROUPS, seq // TK, VT_ROWS, TK), BF16),
            jax.ShapeDtypeStruct((b, GT_ROWS, seq), F32),
        ],
        compiler_params=pltpu.CompilerParams(
            dimension_semantics=("arbitrary",), vmem_limit_bytes=VMEM_LIMIT),
        name="inproj",
    )(x2, gain, w_p, wT_p)


def _compress_kernel(z_ref, pe_ref, w1_ref, w2_ref, w2T_ref, o_ref, oT_ref):
    z = z_ref[0, 0].astype(F32)
    zt = (z + pe_ref[0, 0]).astype(BF16)
    zb = (z + pe_ref[0, 1]).astype(BF16)
    a = _dot(zt, w1_ref[0, 0])
    bm = _dot(zb, w1_ref[0, 1])
    n_rows = a.shape[0]
    h = a + pltpu.roll(bm, n_rows - 1, 0)
    hg = jax.nn.gelu(h).astype(BF16)
    o_ref[0, 0] = _dot(hg, w2_ref[0]).astype(o_ref.dtype)
    oT_ref[0, 0] = _dot_nt(w2T_ref[0], hg).astype(oT_ref.dtype)


def _compress(z, pe2, w1e, w2e, w2eT):
    _, b, n, k = z.shape
    return pl.pallas_call(
        _compress_kernel,
        grid=(2, b),
        in_specs=[
            pl.BlockSpec((1, 1, n, k), lambda t, i: (t, i, 0, 0)),
            pl.BlockSpec((1, 2, 1, k), lambda t, i: (t, 0, 0, 0)),
            pl.BlockSpec((1, 2, k, 2 * CMP_HIDDEN), lambda t, i: (t, 0, 0, 0)),
            pl.BlockSpec((1, 2 * CMP_HIDDEN, 2 * LANE), lambda t, i: (t, 0, 0)),
            pl.BlockSpec((1, 2 * LANE, 2 * CMP_HIDDEN), lambda t, i: (t, 0, 0)),
        ],
        out_specs=[
            pl.BlockSpec((1, 1, n, 2 * LANE), lambda t, i: (t, i, 0, 0)),
            pl.BlockSpec((1, 1, 2 * LANE, n), lambda t, i: (t, i, 0, 0)),
        ],
        out_shape=[
            jax.ShapeDtypeStruct((2, b, n, 2 * LANE), BF16),
            jax.ShapeDtypeStruct((2, b, 2 * LANE, n), BF16),
        ],
        compiler_params=pltpu.CompilerParams(
            dimension_semantics=("arbitrary", "arbitrary"), vmem_limit_bytes=VMEM_LIMIT),
        name="compress",
    )(z, pe2, w1e, w2e, w2eT)


def _f32_dot_exact_lhs(a_bf16, x):
    out = None
    rem = x
    for _ in range(N_PIECE):
        piece = rem.astype(BF16)
        rem = rem - piece.astype(F32)
        d = _dot(a_bf16, piece)
        out = d if out is None else out + d
    return out


def _select_mask(score, tq):
    sub = 8
    n_chunk = N_SEL // sub
    chunks = [score[sub * c:sub * (c + 1)] for c in range(n_chunk)]
    ranks = [jnp.zeros((sub, tq), jnp.int32) for _ in range(n_chunk)]
    jrow = lax.broadcasted_iota(jnp.int32, (sub, tq), 0)
    for k in range(N_SEL):
        sk = score[k:k + 1, :]
        for c in range(n_chunk):
            if k < sub * c:
                ahead = sk >= chunks[c]
            elif k >= sub * (c + 1):
                ahead = sk > chunks[c]
            else:
                ahead = (sk > chunks[c]) | ((sk == chunks[c]) & (jrow + sub * c > k))
            ranks[c] = ranks[c] + ahead.astype(jnp.int32)
    rank = jnp.concatenate(ranks, axis=0)
    return jnp.where(rank < SEL_TOPN, 0.0, SEL_NEG).astype(BF16)


def _cmp_kernel(qT_ref, k_ref, vT_ref, ocT_ref, qaT_ref, s_scr, *, tq, nsub, n_cmp):
    i = pl.program_id(1)
    n_pad = k_ref.shape[2]
    nn = lax.broadcasted_iota(jnp.int32, (n_pad, tq), 0)
    cend = (lax.broadcasted_iota(jnp.int32, (n_pad, LANE), 0) * CMP_STRIDE + CMP_BLOCK - 1).astype(F32)

    jr = lax.broadcasted_iota(jnp.int32, (N_SEL, n_pad), 0)
    nc = lax.broadcasted_iota(jnp.int32, (N_SEL, n_pad), 1)
    ov = ((nc * CMP_STRIDE < jr * SEL_BLOCK + SEL_BLOCK) & (nc * CMP_STRIDE + CMP_BLOCK > jr * SEL_BLOCK)
          & (nc < n_cmp))
    ovT = jnp.where(ov, 1.0, 0.0).astype(BF16)
    jj = lax.broadcasted_iota(jnp.int32, (N_SEL, tq), 0)

    for sub in range(nsub):
        qs = slice(sub * tq, (sub + 1) * tq)
        for g in range(N_GROUPS):
            kc = k_ref[0, 0][:, g * LANE:g * LANE + HEAD_DIM]
            for r in range(N_REP):
                h = N_REP * g + r
                bias = float(SLOPES_NSA[g, r] * LOG2E) * cend
                s_scr[sub * N_QH + h] = (_dot(kc, qT_ref[0, h * HEAD_DIM:(h + 1) * HEAD_DIM, qs])
                                         + jnp.concatenate([bias] * (tq // LANE), axis=1))

    for sub in range(nsub):
        qs = slice(sub * tq, (sub + 1) * tq)
        t0 = (i * nsub + sub) * tq
        tt = t0 + lax.broadcasted_iota(jnp.int32, (n_pad, tq), 1)
        okc = (tt >= nn * CMP_STRIDE + CMP_BLOCK - 1) & (nn < n_cmp)
        blk_t = (t0 + lax.broadcasted_iota(jnp.int32, (N_SEL, tq), 1)) // SEL_BLOCK
        valid = jj <= blk_t
        forced = (jj == 0) | (jj == blk_t) | (jj == blk_t - 1)

        for g in range(N_GROUPS):
            vcT = vT_ref[0, 0][g * LANE:g * LANE + HEAD_DIM, :]
            psum = jnp.zeros((n_pad, tq), F32)
            qTs = []
            for r in range(N_REP):
                h = N_REP * g + r
                qTs.append(qT_ref[0, h * HEAD_DIM:(h + 1) * HEAD_DIM, qs])
                sm = jnp.where(okc, s_scr[sub * N_QH + h], NEG)
                m = jnp.max(sm, axis=0, keepdims=True)
                e = jnp.where(okc, jnp.exp2(sm - m), 0.0)
                den = jnp.sum(e, axis=0, keepdims=True)
                p = e * (1.0 / jnp.where(den > 0, den, 1.0))
                ocT_ref[0, h * HEAD_DIM:(h + 1) * HEAD_DIM, qs] = _dot(vcT, p.astype(BF16))
                psum = psum + p

            imp = _f32_dot_exact_lhs(ovT, psum)
            score = jnp.where(valid, imp, -jnp.inf)
            score = jnp.where(forced & valid, jnp.inf, score)
            negm = _select_mask(score, tq)
            for r in range(N_REP):
                qaT_ref[0, N_REP * g + r, :, qs] = jnp.concatenate(
                    [qTs[r], negm, _slope_rows(SLOPES_NSA[g, r], tq)], axis=0)


def _cmp(qT, cmp_k, cmp_vT, b, seq, tq=256, nsub=2):
    ts = tq * nsub
    nq = seq // ts
    n_cmp = (seq - CMP_BLOCK) // CMP_STRIDE + 1
    n_pad = cmp_k.shape[2]
    return pl.pallas_call(
        functools.partial(_cmp_kernel, tq=tq, nsub=nsub, n_cmp=n_cmp),
        grid=(b, nq),
        in_specs=[
            pl.BlockSpec((1, QT_ROWS, ts), lambda bi, i: (bi, 0, i)),
            pl.BlockSpec((1, 1, n_pad, 2 * LANE), lambda bi, i: (0, bi, 0, 0)),
            pl.BlockSpec((1, 1, 2 * LANE, n_pad), lambda bi, i: (1, bi, 0, 0)),
        ],
        out_specs=[
            pl.BlockSpec((1, QT_ROWS, ts), lambda bi, i: (bi, 0, i)),
            pl.BlockSpec((1, N_QH, LANE, ts), lambda bi, i: (bi, 0, 0, i)),
        ],
        out_shape=[
            jax.ShapeDtypeStruct((b, QT_ROWS, seq), F32),
            jax.ShapeDtypeStruct((b, N_QH, LANE, seq), BF16),
        ],
        scratch_shapes=[pltpu.VMEM((nsub * N_QH, n_pad, tq), F32)],
        compiler_params=pltpu.CompilerParams(
            dimension_semantics=("arbitrary", "arbitrary"), vmem_limit_bytes=VMEM_LIMIT),
        name="cmp",
    )(qT, cmp_k, cmp_vT)


def _score_phase(qT_ref, k_tiles, mask, s_buf, mt_buf):
    for g in range(N_GROUPS):
        for r in range(N_REP):
            h = N_REP * g + r
            s = _dot(k_tiles[g], qT_ref[0, h])
            if mask is not None:
                s = s + mask
            s_buf[h] = s
            mt_buf[h] = jnp.max(s, axis=0, keepdims=True)


def _value_phase(vT_tiles, s_buf, mt_buf, m_scr, acc_scr):
    for g in range(N_GROUPS):
        for r in range(N_REP):
            h = N_REP * g + r
            m_old = m_scr[h]
            m_new = jnp.maximum(m_old, mt_buf[h])
            alpha = jnp.exp2(m_old - m_new)
            pT = jnp.exp2(s_buf[h] - m_new).astype(BF16)
            acc_scr[h] = alpha * acc_scr[h] + _dot(vT_tiles[g], pT)
            m_scr[h] = m_new


def _sweep_reset(m_scr, acc_scr):
    m_scr[...] = jnp.full(m_scr.shape, -3e38, F32)
    acc_scr[...] = jnp.zeros(acc_scr.shape, F32)


def _heads_out(o_ref, outs, rows=slice(None)):
    for g in range(N_GROUPS):
        oT = jnp.concatenate(outs[N_REP * g:N_REP * (g + 1)], axis=0)
        o_ref[rows, g * 2 * LANE:(g + 1) * 2 * LANE] = oT.T.astype(o_ref.dtype)


def _slcwin_kernel(qT_ref, ks0, ks1, kw0, kw1, vsT_ref, vwT_ref, gT_ref, ocT_ref,
                   o_ref, caus, s_scr, mt_scr, m_s, acc_s, m_w, acc_w, *, tq):
    tk = TK
    bi = pl.program_id(0)
    i = pl.program_id(1)
    assert NSA_WINDOW // tk == 2 and tq == tk

    @pl.when((bi == 0) & (i == 0))
    def _init():
        kk = lax.broadcasted_iota(jnp.int32, (tk, tq), 0)
        qq = lax.broadcasted_iota(jnp.int32, (tk, tq), 1)
        caus[0] = jnp.where(kk <= qq, 0.0, NEG)
        caus[1] = jnp.where(kk > qq, 0.0, NEG)
        caus[2] = jnp.full((tk, tq), NEG, F32)
        caus[3] = jnp.zeros((tk, tq), F32)

    ks = (ks0, ks1)
    kw = (kw0, kw1)
    SLC, WIN = vsT_ref, vwT_ref

    def k_at(refs, idx):
        off = pl.multiple_of(idx * tk, tk)
        return [refs[g][pl.ds(off, tk), :] for g in range(N_GROUPS)]

    def score(refs, idx, mask, buf):
        _score_phase(qT_ref, k_at(refs, idx), mask, s_scr.at[buf], mt_scr.at[buf])

    def value(tensor, idx, buf, m_scr, acc_scr):
        _value_phase([tensor[0, g, idx] for g in range(N_GROUPS)],
                     s_scr.at[buf], mt_scr.at[buf], m_scr, acc_scr)

    _sweep_reset(m_s, acc_s)
    _sweep_reset(m_w, acc_w)

    w2 = jnp.maximum(i - 2, 0)
    w1 = jnp.maximum(i - 1, 0)
    score(kw, w2, caus[jnp.where(i >= 2, 1, 2)], 0)
    score(kw, w1, caus[jnp.where(i >= 1, 3, 2)], 1)
    value(WIN, w2, 0, m_w, acc_w)
    score(kw, i, caus[0], 0)
    value(WIN, w1, 1, m_w, acc_w)
    score(ks, 0, caus[jnp.where(i == 0, 0, 3)], 1)
    value(WIN, i, 0, m_w, acc_w)

    def step(j, parity, mask=None):
        score(ks, j + 1, mask, parity)
        value(SLC, j, 1 - parity, m_s, acc_s)

    def body(jj, carry):
        step(2 * jj, 0)
        step(2 * jj + 1, 1)
        return carry

    n_mid = jnp.maximum(i - 1, 0)
    lax.fori_loop(0, n_mid // 2, body, 0)

    @pl.when(n_mid % 2 == 1)
    def _odd():
        step(i - 2, 0)

    @pl.when(i % 2 == 1)
    def _diag_odd():
        step(i - 1, 0, caus[0])
        value(SLC, i, 0, m_s, acc_s)

    @pl.when((i % 2 == 0) & (i > 0))
    def _diag_even():
        step(i - 1, 1, caus[0])

    @pl.when(i % 2 == 0)
    def _last_even():
        value(SLC, i, 1, m_s, acc_s)

    gT = gT_ref[0]
    outs = []
    for h in range(N_QH):
        gc = h * N_BRANCH
        f_slc = gT[gc + 1:gc + 2] * (1.0 / acc_s[h, HEAD_DIM:HEAD_DIM + 1, :])
        f_win = gT[gc + 2:gc + 3] * (1.0 / acc_w[h, HEAD_DIM:HEAD_DIM + 1, :])
        outs.append(gT[gc:gc + 1] * ocT_ref[0, h * HEAD_DIM:(h + 1) * HEAD_DIM, :]
                    + f_slc * acc_s[h, 0:HEAD_DIM, :]
                    + f_win * acc_w[h, 0:HEAD_DIM, :])
    _heads_out(o_ref, outs)


def _slcwin(qaT, main, vT, gT, ocT, b, seq, tq=TK):
    nq = seq // tq
    n = b * seq

    def col(c):
        return pl.BlockSpec((seq, LANE), lambda bi, i, c=c: (bi, c))

    return pl.pallas_call(
        functools.partial(_slcwin_kernel, tq=tq),
        grid=(b, nq),
        in_specs=[
            pl.BlockSpec((1, N_QH, LANE, tq), lambda bi, i: (bi, 0, 0, i)),
            col(KS_COL), col(KS_COL + 1), col(KW_COL), col(KW_COL + 1),
            pl.BlockSpec((1, N_GROUPS, seq // TK, VT_ROWS, TK), lambda bi, i: (bi, 0, 0, 0, 0)),
            pl.BlockSpec((1, N_GROUPS, seq // TK, VT_ROWS, TK), lambda bi, i: (bi, 1, 0, 0, 0)),
            pl.BlockSpec((1, GT_ROWS, tq), lambda bi, i: (bi, 0, i)),
            pl.BlockSpec((1, QT_ROWS, tq), lambda bi, i: (bi, 0, i)),
        ],
        out_specs=pl.BlockSpec((tq, 4 * LANE), lambda bi, i: (bi * nq + i, 0)),
        out_shape=jax.ShapeDtypeStruct((n, 4 * LANE), BF16),
        scratch_shapes=[
            pltpu.VMEM((4, TK, tq), F32),
            pltpu.VMEM((2, N_QH, TK, tq), F32),
            pltpu.VMEM((2, N_QH, 1, tq), F32),
            pltpu.VMEM((N_QH, 1, tq), F32),
            pltpu.VMEM((N_QH, VT_ROWS, tq), F32),
            pltpu.VMEM((N_QH, 1, tq), F32),
            pltpu.VMEM((N_QH, VT_ROWS, tq), F32),
        ],
        compiler_params=pltpu.CompilerParams(
            dimension_semantics=("arbitrary", "arbitrary"), vmem_limit_bytes=VMEM_LIMIT),
        name="slcwin",
    )(qaT, main, main, main, main, vT, vT, gT, ocT)


SWA_AHEAD = 3
SWA_BUFS = SWA_AHEAD + 1

def _swa_kernel(sink_ref, qT_ref, k0, k1, vT_ref, o_ref, mfar, mmain, s_far, s_main, *, tq, nsub):
    bi = pl.program_id(0)
    i = pl.program_id(1)
    w = SWA_WINDOW
    assert tq == TK and w == LANE

    @pl.when((bi == 0) & (i == 0))
    def _init():
        kk = lax.broadcasted_iota(jnp.int32, (w, tq), 0)
        qq = lax.broadcasted_iota(jnp.int32, (w, tq), 1)
        mfar[0] = jnp.where(kk > qq, 0.0, NEG)
        mfar[1] = jnp.full((w, tq), NEG, F32)
        kk = lax.broadcasted_iota(jnp.int32, (tq, tq), 0)
        qq = lax.broadcasted_iota(jnp.int32, (tq, tq), 1)
        mmain[...] = jnp.where((kk <= qq) & (qq - kk < w), 0.0, NEG)

    kk_ref = (k0, k1)
    mask_main = mmain[...]
    mts, sinks, outs = {}, {}, {}

    def geometry(sub):
        tile = i * nsub + sub
        t0 = tile * tq
        return tile, t0

    def score(n):
        sub, h = divmod(n, N_QH)
        g, r = divmod(h, N_REP)
        tile, t0 = geometry(sub)
        far0 = pl.multiple_of(jnp.maximum(t0 - w, 0), w)
        main0 = pl.multiple_of(t0, tq)
        mask_far = mfar[jnp.where(tile == 0, 1, 0)]
        qaT = qT_ref[0, h, :, sub * tq:(sub + 1) * tq]
        sf = _dot(kk_ref[g][pl.ds(far0, w), :], qaT) + mask_far
        sm = _dot(kk_ref[g][pl.ds(main0, tq), :], qaT) + mask_main
        s_far[n % SWA_BUFS] = sf
        s_main[n % SWA_BUFS] = sm
        tpos = (t0 + lax.broadcasted_iota(jnp.int32, (1, tq), 1)).astype(F32)
        sink = (sink_ref[g, r] * LOG2E) + float(SLOPES_SWA[g, r] * LOG2E) * tpos
        mts[n] = jnp.maximum(jnp.maximum(jnp.max(sf, axis=0, keepdims=True),
                                         jnp.max(sm, axis=0, keepdims=True)), sink)
        sinks[n] = sink

    def value(n):
        sub, h = divmod(n, N_QH)
        g = h // N_REP
        tile, _ = geometry(sub)
        vT_far = vT_ref[0, g, jnp.maximum(tile - 1, 0)][:, w:2 * w]
        vT_main = vT_ref[0, g, tile]
        m = mts.pop(n)
        pf = jnp.exp2(s_far[n % SWA_BUFS] - m).astype(BF16)
        pm = jnp.exp2(s_main[n % SWA_BUFS] - m).astype(BF16)
        acc = _dot(vT_far, pf) + _dot(vT_main, pm)
        den = acc[HEAD_DIM:HEAD_DIM + 1, :] + jnp.exp2(sinks.pop(n) - m)
        outs[n] = acc[0:HEAD_DIM, :] * (1.0 / den)
        if h == N_QH - 1:
            _heads_out(o_ref, [outs.pop(sub * N_QH + hh) for hh in range(N_QH)], rows=slice(sub * tq, (sub + 1) * tq))

    n_items = nsub * N_QH
    for n in range(n_items + SWA_AHEAD):
        if n < n_items:
            score(n)
        if n >= SWA_AHEAD:
            value(n - SWA_AHEAD)


def _swa(sinks, qsT, main, vT, b, seq, tq=TK, nsub=2):
    ts = tq * nsub
    nq = seq // ts
    n = b * seq

    def col(c):
        return pl.BlockSpec((seq, LANE), lambda bi, i, c=c: (bi, c))

    return pl.pallas_call(
        functools.partial(_swa_kernel, tq=tq, nsub=nsub),
        grid=(b, nq),
        in_specs=[
            pl.BlockSpec(memory_space=pltpu.SMEM),
            pl.BlockSpec((1, N_QH, LANE, ts), lambda bi, i: (bi, 0, 0, i)),
            col(KSW_COL), col(KSW_COL + 1),
            pl.BlockSpec((1, N_GROUPS, seq // TK, VT_ROWS, TK), lambda bi, i: (bi, 2, 0, 0, 0)),
        ],
        out_specs=pl.BlockSpec((ts, 4 * LANE), lambda bi, i: (bi * nq + i, 0)),
        out_shape=jax.ShapeDtypeStruct((n, 4 * LANE), BF16),
        scratch_shapes=[
            pltpu.VMEM((2, SWA_WINDOW, tq), F32),
            pltpu.VMEM((tq, tq), F32),
            pltpu.VMEM((SWA_BUFS, SWA_WINDOW, tq), F32),
            pltpu.VMEM((SWA_BUFS, tq, tq), F32),
        ],
        compiler_params=pltpu.CompilerParams(
            dimension_semantics=("arbitrary", "arbitrary"), vmem_limit_bytes=VMEM_LIMIT),
        name="swa",
    )(sinks, qsT, main, main, vT)


def _outmlp_kernel(on_ref, os_ref, x_ref, wo_ref, g2_ref, g3_ref, wu_ref, wd_ref, g4_ref, o_ref, *, ff_chunk):
    half = N_HEADS * HEAD_DIM // 2
    mix = _dot(on_ref[...], wo_ref[0:half, :]) + _dot(os_ref[...], wo_ref[half:2 * half, :])
    h1 = x_ref[...] + _rms(mix, g2_ref[...])
    m = _rms(h1, g3_ref[...]).astype(BF16)
    acc = jnp.zeros(h1.shape, F32)
    for c in range(D_FF // ff_chunk):
        u = _dot(m, wu_ref[:, c * ff_chunk:(c + 1) * ff_chunk])
        u = jnp.square(jnp.maximum(u, 0.0)).astype(BF16)
        acc = acc + _dot(u, wd_ref[c * ff_chunk:(c + 1) * ff_chunk, :])
    o_ref[...] = h1 + _rms(acc, g4_ref[...])


def _outmlp(onsa, oswa, x2, wo, g2, g3, wu, wd, g4, tm=512, ff_chunk=1024):
    n = x2.shape[0]

    def const(shape):
        return pl.BlockSpec(shape, lambda i: (0, 0), pipeline_mode=pl.Buffered(1))

    return pl.pallas_call(
        functools.partial(_outmlp_kernel, ff_chunk=ff_chunk),
        grid=(n // tm,),
        in_specs=[
            pl.BlockSpec((tm, 4 * LANE), lambda i: (i, 0)),
            pl.BlockSpec((tm, 4 * LANE), lambda i: (i, 0)),
            pl.BlockSpec((tm, D_MODEL), lambda i: (i, 0)),
            const((D_MODEL, D_MODEL)),
            const((1, D_MODEL)),
            const((1, D_MODEL)),
            const((D_MODEL, D_FF)),
            const((D_FF, D_MODEL)),
            const((1, D_MODEL)),
        ],
        out_specs=pl.BlockSpec((tm, D_MODEL), lambda i: (i, 0)),
        out_shape=jax.ShapeDtypeStruct((n, D_MODEL), F32),
        compiler_params=pltpu.CompilerParams(
            dimension_semantics=("arbitrary",), vmem_limit_bytes=VMEM_LIMIT),
        name="outmlp",
    )(onsa, oswa, x2, wo, g2, g3, wu, wd, g4)


def _pad_groups(w):
    z = jnp.zeros((w.shape[0], HEAD_DIM), w.dtype)
    return jnp.concatenate([w[:, :HEAD_DIM], z, w[:, HEAD_DIM:], z], axis=1)


def _layout_w_in(w):
    sizes = [512, 128, 128, 128, 128, 128, 128, N_HEADS // 2 * N_BRANCH, 512, 128, 128]
    offs = np.concatenate([[0], np.cumsum(sizes)])
    q_n, kc, vc, ks, vs, kw, vw, gt, q_s, k_s, v_s = [w[:, offs[k]:offs[k + 1]] for k in range(len(sizes))]
    std = [_pad_groups(t) for t in (ks, kw, k_s)] + [kc, vc]
    gt = jnp.concatenate([gt, jnp.zeros((w.shape[0], GT_ROWS - gt.shape[1]), w.dtype)], axis=1)
    tr = jnp.concatenate([q_n, q_s, vs, vw, v_s, gt], axis=1).T
    return jnp.concatenate(std, axis=1).astype(BF16), tr.astype(BF16)


def _layout_w1(w1):
    w = w1.reshape(2, CMP_STRIDE, HEAD_DIM, CMP_HIDDEN)
    z = jnp.zeros_like(w)
    top = jnp.concatenate([w, z], axis=-1)
    bot = jnp.concatenate([z, w], axis=-1)
    e = jnp.stack([top, bot], axis=2)
    return e.reshape(2, CMP_STRIDE * 2 * HEAD_DIM, 2 * CMP_HIDDEN).astype(BF16)


def _layout_w2(w2):
    z = jnp.zeros((CMP_HIDDEN, HEAD_DIM), w2.dtype)
    top = jnp.concatenate([w2, z, z, z], axis=1)
    bot = jnp.concatenate([z, z, w2, z], axis=1)
    return jnp.concatenate([top, bot], axis=0).astype(BF16)


def _layout_pe(pe):
    p = pe.reshape(2, CMP_STRIDE, 1, HEAD_DIM)
    return jnp.broadcast_to(p, (2, CMP_STRIDE, 2, HEAD_DIM)).reshape(2, 1, CMP_STRIDE * 2 * HEAD_DIM)


def kernel(x, norm_mix_pre, w_in, cmp_pe_k, cmp_w1_k, cmp_w2_k, cmp_pe_v, cmp_w1_v, cmp_w2_v,
           sinks, w_out, norm_mix_post, norm_mlp_pre, w_up, w_down, norm_mlp_post):
    b, seq, _ = x.shape
    assert seq // SEL_BLOCK == N_SEL and seq <= POS_SPLIT * 256
    depth = w_in.shape[0]
    h = x.reshape(b * seq, D_MODEL)
    for li in range(depth):
        w_p, wT_p = _layout_w_in(w_in[li])
        main, kc, vc, qT, qsT, vT, gT = _inproj(h, norm_mix_pre[li][None], w_p, wT_p, b, seq)
        z = jnp.stack([kc, vc]).reshape(2, b, seq // CMP_STRIDE, CMP_STRIDE * LANE)
        pe2 = jnp.stack([_layout_pe(cmp_pe_k[li]), _layout_pe(cmp_pe_v[li])])
        w1e = jnp.stack([_layout_w1(cmp_w1_k[li]), _layout_w1(cmp_w1_v[li])])
        w2e = jnp.stack([_layout_w2(cmp_w2_k[li]), _layout_w2(cmp_w2_v[li])])
        cmp_kv, cmp_kvT = _compress(z, pe2, w1e, w2e, jnp.swapaxes(w2e, 1, 2))
        ocT, qaT = _cmp(qT, cmp_kv, cmp_kvT, b, seq)
        onsa = _slcwin(qaT, main, vT, gT, ocT, b, seq)
        oswa = _swa(sinks[li], qsT, main, vT, b, seq)
        h = _outmlp(onsa, oswa, h, w_out[li].astype(BF16), norm_mix_post[li][None], norm_mlp_pre[li][None],
                    w_up[li].astype(BF16), w_down[li].astype(BF16), norm_mlp_post[li][None])
    return h.reshape(b, seq, D_MODEL)
```

```python
import functools
import math

import jax
import jax.numpy as jnp
import numpy as np
from jax import lax
from jax.experimental import pallas as pl
from jax.experimental.pallas import tpu as pltpu

F32 = jnp.float32
BF16 = jnp.bfloat16

D_MODEL = 1024
HEAD_DIM = 64
N_HEADS = 16
N_GROUPS = 2
N_REP = 4
N_QH = N_GROUPS * N_REP
CMP_BLOCK = 32
CMP_STRIDE = 16
CMP_HIDDEN = 4 * HEAD_DIM
SEL_BLOCK = 64
SEL_TOPN = 8
N_SEL = 32
NSA_WINDOW = 512
SWA_WINDOW = 128
D_FF = 4 * D_MODEL
NORM_EPS = 1e-6
N_BRANCH = 3

LANE = 128
LOG2E = math.log2(math.e)
NEG = -1e30
SEL_NEG = -(2.0 ** 100)
VMEM_LIMIT = 56 * 1024 * 1024

TK = 256
VT_ROWS = 80
SEL_ROW = HEAD_DIM
POS_ROW = HEAD_DIM + N_SEL
N_PIECE = 3
POS_SPLIT = 256

KS_COL, KW_COL, KSW_COL = 0, 2, 4
MAIN_COLS = 6 * LANE
STD_COLS = MAIN_COLS + 2 * LANE
QT_ROWS = N_QH * HEAD_DIM
VT_ROW0 = 2 * QT_ROWS
GT_ROW = VT_ROW0 + 3 * N_GROUPS * HEAD_DIM
GT_ROWS = 32
T_ROWS = GT_ROW + GT_ROWS


def _slopes():
    s = 2.0 ** (-8.0 * (np.arange(N_HEADS) + 1) / N_HEADS)
    nsa = s[0::2].reshape(N_GROUPS, N_REP)
    swa = s[1::2].reshape(N_GROUPS, N_REP)
    return nsa, swa


SLOPES_NSA, SLOPES_SWA = _slopes()


def _bf16_round(x):
    u = np.float32(x).reshape(1).view(np.uint32)
    u = (u + (((u >> 16) & 1) + 0x7FFF)) & np.uint32(0xFFFF0000)
    return float(u.view(np.float32)[0])


def _bf16_pieces(x, n=N_PIECE):
    out, rem = [], float(np.float32(x))
    for _ in range(n):
        p = _bf16_round(rem)
        out.append(p)
        rem = float(np.float32(rem - p))
    return out


_NT = (((1,), (1,)), ((), ()))


def _dot_nt(a, b, **kw):
    return lax.dot_general(a, b, _NT, preferred_element_type=F32, **kw)


def _dot(a, b, **kw):
    return jnp.dot(a, b, preferred_element_type=F32, **kw)


def _rms(v, g):
    return v * lax.rsqrt(jnp.mean(v * v, axis=-1, keepdims=True) + NORM_EPS) * g


def _slope_rows(slope, tq):
    pieces = _bf16_pieces(slope * LOG2E)
    vals = pieces + [p * POS_SPLIT for p in pieces]
    prow = lax.broadcasted_iota(jnp.int32, (LANE - POS_ROW, tq), 0)
    feat = jnp.zeros((LANE - POS_ROW, tq), F32)
    for k, v in enumerate(vals):
        feat = jnp.where(prow == k, v, feat)
    return feat.astype(BF16)


def _inproj_kernel(x_ref, g_ref, w_ref, wT_ref, main_ref, kc_ref, vc_ref, qT_ref, qsT_ref, vT_ref, gT_ref,
                   *, tm, seq):
    x = x_ref[...]
    a = _rms(x, g_ref[...]).astype(BF16)

    res = _dot(a, w_ref[...])
    t0 = (pl.program_id(0) * tm) % seq
    pos = t0 + lax.broadcasted_iota(jnp.int32, (tm, LANE), 0)
    lane = lax.broadcasted_iota(jnp.int32, (tm, LANE), 1)
    posf = jnp.where((lane >= POS_ROW) & (lane < POS_ROW + N_PIECE), (pos % POS_SPLIT).astype(F32), 0.0)
    posf = jnp.where((lane >= POS_ROW + N_PIECE) & (lane < POS_ROW + 2 * N_PIECE),
                     (pos // POS_SPLIT).astype(F32), posf)
    onehot = jnp.where(lane == pos // SEL_BLOCK + SEL_ROW, 1.0, 0.0)
    for c in range(MAIN_COLS // LANE):
        extra = posf + onehot if c in (KS_COL, KS_COL + 1) else posf
        main_ref[:, c * LANE:(c + 1) * LANE] = (res[:, c * LANE:(c + 1) * LANE] + extra).astype(BF16)
    kc_ref[...] = res[:, MAIN_COLS:MAIN_COLS + LANE].astype(kc_ref.dtype)
    vc_ref[...] = res[:, MAIN_COLS + LANE:STD_COLS].astype(vc_ref.dtype)

    resT = _dot_nt(wT_ref[...], a)
    qscale = LOG2E * HEAD_DIM ** -0.5
    qT_ref[0] = (resT[0:QT_ROWS] * qscale).astype(BF16)
    zmid = jnp.zeros((N_SEL, tm), BF16)
    for h in range(N_QH):
        r0 = QT_ROWS + h * HEAD_DIM
        qsT_ref[0, h] = jnp.concatenate(
            [(resT[r0:r0 + HEAD_DIM] * qscale).astype(BF16), zmid,
             _slope_rows(SLOPES_SWA[h // N_REP, h % N_REP], tm)], axis=0)
    ones_blk = jnp.where(lax.broadcasted_iota(jnp.int32, (VT_ROWS - HEAD_DIM, TK), 0) == 0, 1.0, 0.0).astype(BF16)
    for tg in range(3 * N_GROUPS):
        r0 = VT_ROW0 + tg * HEAD_DIM
        for kt in range(tm // TK):
            vT_ref[0, tg, kt, 0:HEAD_DIM, :] = resT[r0:r0 + HEAD_DIM, kt * TK:(kt + 1) * TK].astype(BF16)
            vT_ref[0, tg, kt, HEAD_DIM:VT_ROWS, :] = ones_blk
    gT_ref[0] = jax.nn.sigmoid(resT[GT_ROW:T_ROWS])


def _inproj(x2, gain, w_p, wT_p, b, seq, tm=512):
    n = x2.shape[0]
    nt = seq // tm
    return pl.pallas_call(
        functools.partial(_inproj_kernel, tm=tm, seq=seq),
        grid=(n // tm,),
        in_specs=[
            pl.BlockSpec((tm, D_MODEL), lambda i: (i, 0)),
            pl.BlockSpec((1, D_MODEL), lambda i: (0, 0)),
            pl.BlockSpec((D_MODEL, STD_COLS), lambda i: (0, 0)),
            pl.BlockSpec((T_ROWS, D_MODEL), lambda i: (0, 0)),
        ],
        out_specs=[
            pl.BlockSpec((tm, MAIN_COLS), lambda i: (i, 0)),
            pl.BlockSpec((tm, LANE), lambda i: (i, 0)),
            pl.BlockSpec((tm, LANE), lambda i: (i, 0)),
            pl.BlockSpec((1, QT_ROWS, tm), lambda i: (i // nt, 0, i % nt)),
            pl.BlockSpec((1, N_QH, LANE, tm), lambda i: (i // nt, 0, 0, i % nt)),
            pl.BlockSpec((1, 3 * N_GROUPS, tm // TK, VT_ROWS, TK), lambda i: (i // nt, 0, i % nt, 0, 0)),
            pl.BlockSpec((1, GT_ROWS, tm), lambda i: (i // nt, 0, i % nt)),
        ],
        out_shape=[
            jax.ShapeDtypeStruct((n, MAIN_COLS), BF16),
            jax.ShapeDtypeStruct((n, LANE), BF16),
            jax.ShapeDtypeStruct((n, LANE), BF16),
            jax.ShapeDtypeStruct((b, QT_ROWS, seq), BF16),
            jax.ShapeDtypeStruct((b, N_QH, LANE, seq), BF16),
            jax.ShapeDtypeStruct((b, 3 * N_GROUPS, seq // TK, VT_ROWS, TK), BF16),
            jax.ShapeDtypeStruct((b, GT_ROWS, seq), F32),
        ],
        compiler_params=pltpu.CompilerParams(
            dimension_semantics=("arbitrary",), vmem_limit_bytes=VMEM_LIMIT),
        name="inproj",
    )(x2, gain, w_p, wT_p)


def _compress_kernel(z_ref, pe_ref, w1_ref, w2_ref, w2T_ref, o_ref, oT_ref):
    z = z_ref[0, 0].astype(F32)
    zt = (z + pe_ref[0, 0]).astype(BF16)
    zb = (z + pe_ref[0, 1]).astype(BF16)
    a = _dot(zt, w1_ref[0, 0])
    bm = _dot(zb, w1_ref[0, 1])
    n_rows = a.shape[0]
    h = a + pltpu.roll(bm, n_rows - 1, 0)
    hg = jax.nn.gelu(h).astype(BF16)
    o_ref[0, 0] = _dot(hg, w2_ref[0]).astype(o_ref.dtype)
    oT_ref[0, 0] = _dot_nt(w2T_ref[0], hg).astype(oT_ref.dtype)


def _compress(z, pe2, w1e, w2e, w2eT):
    _, b, n, k = z.shape
    return pl.pallas_call(
        _compress_kernel,
        grid=(2, b),
        in_specs=[
            pl.BlockSpec((1, 1, n, k), lambda t, i: (t, i, 0, 0)),
            pl.BlockSpec((1, 2, 1, k), lambda t, i: (t, 0, 0, 0)),
            pl.BlockSpec((1, 2, k, 2 * CMP_HIDDEN), lambda t, i: (t, 0, 0, 0)),
            pl.BlockSpec((1, 2 * CMP_HIDDEN, 2 * LANE), lambda t, i: (t, 0, 0)),
            pl.BlockSpec((1, 2 * LANE, 2 * CMP_HIDDEN), lambda t, i: (t, 0, 0)),
        ],
        out_specs=[
            pl.BlockSpec((1, 1, n, 2 * LANE), lambda t, i: (t, i, 0, 0)),
            pl.BlockSpec((1, 1, 2 * LANE, n), lambda t, i: (t, i, 0, 0)),
        ],
        out_shape=[
            jax.ShapeDtypeStruct((2, b, n, 2 * LANE), BF16),
            jax.ShapeDtypeStruct((2, b, 2 * LANE, n), BF16),
        ],
        compiler_params=pltpu.CompilerParams(
            dimension_semantics=("arbitrary", "arbitrary"), vmem_limit_bytes=VMEM_LIMIT),
        name="compress",
    )(z, pe2, w1e, w2e, w2eT)


def _f32_dot_exact_lhs(a_bf16, x):
    out = None
    rem = x
    for _ in range(N_PIECE):
        piece = rem.astype(BF16)
        rem = rem - piece.astype(F32)
        d = _dot(a_bf16, piece)
        out = d if out is None else out + d
    return out


def _select_mask(score, tq):
    sub = 8
    n_chunk = N_SEL // sub
    chunks = [score[sub * c:sub * (c + 1)] for c in range(n_chunk)]
    ranks = [jnp.zeros((sub, tq), jnp.int32) for _ in range(n_chunk)]
    jrow = lax.broadcasted_iota(jnp.int32, (sub, tq), 0)
    for k in range(N_SEL):
        sk = score[k:k + 1, :]
        for c in range(n_chunk):
            if k < sub * c:
                ahead = sk >= chunks[c]
            elif k >= sub * (c + 1):
                ahead = sk > chunks[c]
            else:
                ahead = (sk > chunks[c]) | ((sk == chunks[c]) & (jrow + sub * c > k))
            ranks[c] = ranks[c] + ahead.astype(jnp.int32)
    rank = jnp.concatenate(ranks, axis=0)
    return jnp.where(rank < SEL_TOPN, 0.0, SEL_NEG).astype(BF16)


def _cmp_kernel(qT_ref, k_ref, vT_ref, ocT_ref, qaT_ref, s_scr, *, tq, nsub, n_cmp):
    i = pl.program_id(1)
    n_pad = k_ref.shape[2]
    nn = lax.broadcasted_iota(jnp.int32, (n_pad, tq), 0)
    cend = (lax.broadcasted_iota(jnp.int32, (n_pad, LANE), 0) * CMP_STRIDE + CMP_BLOCK - 1).astype(F32)

    jr = lax.broadcasted_iota(jnp.int32, (N_SEL, n_pad), 0)
    nc = lax.broadcasted_iota(jnp.int32, (N_SEL, n_pad), 1)
    ov = ((nc * CMP_STRIDE < jr * SEL_BLOCK + SEL_BLOCK) & (nc * CMP_STRIDE + CMP_BLOCK > jr * SEL_BLOCK)
          & (nc < n_cmp))
    ovT = jnp.where(ov, 1.0, 0.0).astype(BF16)
    jj = lax.broadcasted_iota(jnp.int32, (N_SEL, tq), 0)

    for sub in range(nsub):
        qs = slice(sub * tq, (sub + 1) * tq)
        for g in range(N_GROUPS):
            kc = k_ref[0, 0][:, g * LANE:g * LANE + HEAD_DIM]
            for r in range(N_REP):
                h = N_REP * g + r
                bias = float(SLOPES_NSA[g, r] * LOG2E) * cend
                s_scr[sub * N_QH + h] = (_dot(kc, qT_ref[0, h * HEAD_DIM:(h + 1) * HEAD_DIM, qs])
                                         + jnp.concatenate([bias] * (tq // LANE), axis=1))

    for sub in range(nsub):
        qs = slice(sub * tq, (sub + 1) * tq)
        t0 = (i * nsub + sub) * tq
        tt = t0 + lax.broadcasted_iota(jnp.int32, (n_pad, tq), 1)
        okc = (tt >= nn * CMP_STRIDE + CMP_BLOCK - 1) & (nn < n_cmp)
        blk_t = (t0 + lax.broadcasted_iota(jnp.int32, (N_SEL, tq), 1)) // SEL_BLOCK
        valid = jj <= blk_t
        forced = (jj == 0) | (jj == blk_t) | (jj == blk_t - 1)

        for g in range(N_GROUPS):
            vcT = vT_ref[0, 0][g * LANE:g * LANE + HEAD_DIM, :]
            psum = jnp.zeros((n_pad, tq), F32)
            qTs = []
            for r in range(N_REP):
                h = N_REP * g + r
                qTs.append(qT_ref[0, h * HEAD_DIM:(h + 1) * HEAD_DIM, qs])
                sm = jnp.where(okc, s_scr[sub * N_QH + h], NEG)
                m = jnp.max(sm, axis=0, keepdims=True)
                e = jnp.where(okc, jnp.exp2(sm - m), 0.0)
                den = jnp.sum(e, axis=0, keepdims=True)
                p = e * (1.0 / jnp.where(den > 0, den, 1.0))
                ocT_ref[0, h * HEAD_DIM:(h + 1) * HEAD_DIM, qs] = _dot(vcT, p.astype(BF16))
                psum = psum + p

            imp = _f32_dot_exact_lhs(ovT, psum)
            score = jnp.where(valid, imp, -jnp.inf)
            score = jnp.where(forced & valid, jnp.inf, score)
            negm = _select_mask(score, tq)
            for r in range(N_REP):
                qaT_ref[0, N_REP * g + r, :, qs] = jnp.concatenate(
                    [qTs[r], negm, _slope_rows(SLOPES_NSA[g, r], tq)], axis=0)


def _cmp(qT, cmp_k, cmp_vT, b, seq, tq=256, nsub=2):
    ts = tq * nsub
    nq = seq // ts
    n_cmp = (seq - CMP_BLOCK) // CMP_STRIDE + 1
    n_pad = cmp_k.shape[2]
    return pl.pallas_call(
        functools.partial(_cmp_kernel, tq=tq, nsub=nsub, n_cmp=n_cmp),
        grid=(b, nq),
        in_specs=[
            pl.BlockSpec((1, QT_ROWS, ts), lambda bi, i: (bi, 0, i)),
            pl.BlockSpec((1, 1, n_pad, 2 * LANE), lambda bi, i: (0, bi, 0, 0)),
            pl.BlockSpec((1, 1, 2 * LANE, n_pad), lambda bi, i: (1, bi, 0, 0)),
        ],
        out_specs=[
            pl.BlockSpec((1, QT_ROWS, ts), lambda bi, i: (bi, 0, i)),
            pl.BlockSpec((1, N_QH, LANE, ts), lambda bi, i: (bi, 0, 0, i)),
        ],
        out_shape=[
            jax.ShapeDtypeStruct((b, QT_ROWS, seq), F32),
            jax.ShapeDtypeStruct((b, N_QH, LANE, seq), BF16),
        ],
        scratch_shapes=[pltpu.VMEM((nsub * N_QH, n_pad, tq), F32)],
        compiler_params=pltpu.CompilerParams(
            dimension_semantics=("arbitrary", "arbitrary"), vmem_limit_bytes=VMEM_LIMIT),
        name="cmp",
    )(qT, cmp_k, cmp_vT)


SLC_AHEAD = 2
SLC_BUFS = SLC_AHEAD + 1

def _score_phase(qT_ref, k_tiles, mask, s_buf, mt_buf):
    for g in range(N_GROUPS):
        for r in range(N_REP):
            h = N_REP * g + r
            s = _dot(k_tiles[g], qT_ref[0, h])
            if mask is not None:
                s = s + mask
            s_buf[h] = s
            mt_buf[h] = jnp.max(s, axis=0, keepdims=True)


def _value_phase(vT_tiles, s_buf, mt_buf, m_scr, acc_scr):
    for g in range(N_GROUPS):
        for r in range(N_REP):
            h = N_REP * g + r
            m_old = m_scr[h]
            m_new = jnp.maximum(m_old, mt_buf[h])
            alpha = jnp.exp2(m_old - m_new)
            pT = jnp.exp2(s_buf[h] - m_new).astype(BF16)
            acc_scr[h] = alpha * acc_scr[h] + _dot(vT_tiles[g], pT)
            m_scr[h] = m_new


def _sweep_reset(m_scr, acc_scr):
    m_scr[...] = jnp.full(m_scr.shape, -3e38, F32)
    acc_scr[...] = jnp.zeros(acc_scr.shape, F32)


def _heads_out(o_ref, outs, rows=slice(None)):
    for g in range(N_GROUPS):
        oT = jnp.concatenate(outs[N_REP * g:N_REP * (g + 1)], axis=0)
        o_ref[rows, g * 2 * LANE:(g + 1) * 2 * LANE] = oT.T.astype(o_ref.dtype)


def _slcwin_kernel(qT_ref, ks0, ks1, kw0, kw1, vsT_ref, vwT_ref, gT_ref, ocT_ref,
                   o_ref, caus, s_scr, mt_scr, m_s, acc_s, m_w, acc_w, *, tq, nq):
    tk = TK
    bi = pl.program_id(0)
    i = pl.program_id(1)
    assert NSA_WINDOW // tk == 2 and tq == tk

    @pl.when((bi == 0) & (i == 0))
    def _init():
        kk = lax.broadcasted_iota(jnp.int32, (tk, tq), 0)
        qq = lax.broadcasted_iota(jnp.int32, (tk, tq), 1)
        caus[0] = jnp.where(kk <= qq, 0.0, NEG)
        caus[1] = jnp.where(kk > qq, 0.0, NEG)

    ks = (ks0, ks1)
    kw = (kw0, kw1)

    def run(ii):
        _sweep_reset(m_s, acc_s)
        _sweep_reset(m_w, acc_w)
        stream = []
        for d in (2, 1, 0):
            if ii - d >= 0:
                stream.append((kw, vwT_ref, ii - d, {0: 0, 2: 1}.get(d), m_w, acc_w))
        for j in range(ii + 1):
            stream.append((ks, vsT_ref, j, 0 if j == ii else None, m_s, acc_s))

        def score(p):
            refs, _, idx, mask_id, _, _ = stream[p]
            buf = p % SLC_BUFS
            _score_phase(qT_ref, [refs[g][idx * tk:(idx + 1) * tk, :] for g in range(N_GROUPS)],
                         None if mask_id is None else caus[mask_id], s_scr.at[buf], mt_scr.at[buf])

        def value(p):
            _, vref, idx, _, m_scr, acc_scr = stream[p]
            buf = p % SLC_BUFS
            _value_phase([vref[0, g, idx] for g in range(N_GROUPS)], s_scr.at[buf], mt_scr.at[buf], m_scr, acc_scr)

        for p in range(len(stream) + SLC_AHEAD):
            if p < len(stream):
                score(p)
            if p >= SLC_AHEAD:
                value(p - SLC_AHEAD)

        gT = gT_ref[0]
        outs = []
        for h in range(N_QH):
            gc = h * N_BRANCH
            f_slc = gT[gc + 1:gc + 2] * (1.0 / acc_s[h, HEAD_DIM:HEAD_DIM + 1, :])
            f_win = gT[gc + 2:gc + 3] * (1.0 / acc_w[h, HEAD_DIM:HEAD_DIM + 1, :])
            outs.append(gT[gc:gc + 1] * ocT_ref[0, h * HEAD_DIM:(h + 1) * HEAD_DIM, :]
                        + f_slc * acc_s[h, 0:HEAD_DIM, :]
                        + f_win * acc_w[h, 0:HEAD_DIM, :])
        _heads_out(o_ref, outs)

    for ii in range(nq):
        pl.when(i == ii)(functools.partial(run, ii))


def _slcwin(qaT, main, vT, gT, ocT, b, seq, tq=TK):
    nq = seq // tq
    n = b * seq

    def col(c):
        return pl.BlockSpec((seq, LANE), lambda bi, i, c=c: (bi, c))

    return pl.pallas_call(
        functools.partial(_slcwin_kernel, tq=tq, nq=nq),
        grid=(b, nq),
        in_specs=[
            pl.BlockSpec((1, N_QH, LANE, tq), lambda bi, i: (bi, 0, 0, i)),
            col(KS_COL), col(KS_COL + 1), col(KW_COL), col(KW_COL + 1),
            pl.BlockSpec((1, N_GROUPS, seq // TK, VT_ROWS, TK), lambda bi, i: (bi, 0, 0, 0, 0)),
            pl.BlockSpec((1, N_GROUPS, seq // TK, VT_ROWS, TK), lambda bi, i: (bi, 1, 0, 0, 0)),
            pl.BlockSpec((1, GT_ROWS, tq), lambda bi, i: (bi, 0, i)),
            pl.BlockSpec((1, QT_ROWS, tq), lambda bi, i: (bi, 0, i)),
        ],
        out_specs=pl.BlockSpec((tq, 4 * LANE), lambda bi, i: (bi * nq + i, 0)),
        out_shape=jax.ShapeDtypeStruct((n, 4 * LANE), BF16),
        scratch_shapes=[
            pltpu.VMEM((2, TK, tq), F32),
            pltpu.VMEM((SLC_BUFS, N_QH, TK, tq), F32),
            pltpu.VMEM((SLC_BUFS, N_QH, 1, tq), F32),
            pltpu.VMEM((N_QH, 1, tq), F32),
            pltpu.VMEM((N_QH, VT_ROWS, tq), F32),
            pltpu.VMEM((N_QH, 1, tq), F32),
            pltpu.VMEM((N_QH, VT_ROWS, tq), F32),
        ],
        compiler_params=pltpu.CompilerParams(
            dimension_semantics=("arbitrary", "arbitrary"), vmem_limit_bytes=VMEM_LIMIT),
        name="slcwin",
    )(qaT, main, main, main, main, vT, vT, gT, ocT)


SWA_AHEAD = 3
SWA_BUFS = SWA_AHEAD + 1

def _swa_kernel(sink_ref, qT_ref, k0, k1, vT_ref, o_ref, mfar, mmain, s_far, s_main, *, tq, nsub):
    bi = pl.program_id(0)
    i = pl.program_id(1)
    w = SWA_WINDOW
    assert tq == TK and w == LANE

    @pl.when((bi == 0) & (i == 0))
    def _init():
        kk = lax.broadcasted_iota(jnp.int32, (w, tq), 0)
        qq = lax.broadcasted_iota(jnp.int32, (w, tq), 1)
        mfar[0] = jnp.where(kk > qq, 0.0, NEG)
        mfar[1] = jnp.full((w, tq), NEG, F32)
        kk = lax.broadcasted_iota(jnp.int32, (tq, tq), 0)
        qq = lax.broadcasted_iota(jnp.int32, (tq, tq), 1)
        mmain[...] = jnp.where((kk <= qq) & (qq - kk < w), 0.0, NEG)

    kk_ref = (k0, k1)
    mask_main = mmain[...]
    mts, sinks, outs = {}, {}, {}

    def geometry(sub):
        tile = i * nsub + sub
        t0 = tile * tq
        return tile, t0

    def score(n):
        sub, h = divmod(n, N_QH)
        g, r = divmod(h, N_REP)
        tile, t0 = geometry(sub)
        far0 = pl.multiple_of(jnp.maximum(t0 - w, 0), w)
        main0 = pl.multiple_of(t0, tq)
        mask_far = mfar[jnp.where(tile == 0, 1, 0)]
        qaT = qT_ref[0, h, :, sub * tq:(sub + 1) * tq]
        sf = _dot(kk_ref[g][pl.ds(far0, w), :], qaT) + mask_far
        sm = _dot(kk_ref[g][pl.ds(main0, tq), :], qaT) + mask_main
        s_far[n % SWA_BUFS] = sf
        s_main[n % SWA_BUFS] = sm
        tpos = (t0 + lax.broadcasted_iota(jnp.int32, (1, tq), 1)).astype(F32)
        sink = (sink_ref[g, r] * LOG2E) + float(SLOPES_SWA[g, r] * LOG2E) * tpos
        mts[n] = jnp.maximum(jnp.maximum(jnp.max(sf, axis=0, keepdims=True),
                                         jnp.max(sm, axis=0, keepdims=True)), sink)
        sinks[n] = sink

    def value(n):
        sub, h = divmod(n, N_QH)
        g = h // N_REP
        tile, _ = geometry(sub)
        vT_far = vT_ref[0, g, jnp.maximum(tile - 1, 0)][:, w:2 * w]
        vT_main = vT_ref[0, g, tile]
        m = mts.pop(n)
        pf = jnp.exp2(s_far[n % SWA_BUFS] - m).astype(BF16)
        pm = jnp.exp2(s_main[n % SWA_BUFS] - m).astype(BF16)
        acc = _dot(vT_far, pf) + _dot(vT_main, pm)
        den = acc[HEAD_DIM:HEAD_DIM + 1, :] + jnp.exp2(sinks.pop(n) - m)
        outs[n] = acc[0:HEAD_DIM, :] * (1.0 / den)
        if h == N_QH - 1:
            _heads_out(o_ref, [outs.pop(sub * N_QH + hh) for hh in range(N_QH)], rows=slice(sub * tq, (sub + 1) * tq))

    n_items = nsub * N_QH
    for n in range(n_items + SWA_AHEAD):
        if n < n_items:
            score(n)
        if n >= SWA_AHEAD:
            value(n - SWA_AHEAD)


def _swa(sinks, qsT, main, vT, b, seq, tq=TK, nsub=2):
    ts = tq * nsub
    nq = seq // ts
    n = b * seq

    def col(c):
        return pl.BlockSpec((seq, LANE), lambda bi, i, c=c: (bi, c))

    return pl.pallas_call(
        functools.partial(_swa_kernel, tq=tq, nsub=nsub),
        grid=(b, nq),
        in_specs=[
            pl.BlockSpec(memory_space=pltpu.SMEM),
            pl.BlockSpec((1, N_QH, LANE, ts), lambda bi, i: (bi, 0, 0, i)),
            col(KSW_COL), col(KSW_COL + 1),
            pl.BlockSpec((1, N_GROUPS, seq // TK, VT_ROWS, TK), lambda bi, i: (bi, 2, 0, 0, 0)),
        ],
        out_specs=pl.BlockSpec((ts, 4 * LANE), lambda bi, i: (bi * nq + i, 0)),
        out_shape=jax.ShapeDtypeStruct((n, 4 * LANE), BF16),
        scratch_shapes=[
            pltpu.VMEM((2, SWA_WINDOW, tq), F32),
            pltpu.VMEM((tq, tq), F32),
            pltpu.VMEM((SWA_BUFS, SWA_WINDOW, tq), F32),
            pltpu.VMEM((SWA_BUFS, tq, tq), F32),
        ],
        compiler_params=pltpu.CompilerParams(
            dimension_semantics=("arbitrary", "arbitrary"), vmem_limit_bytes=VMEM_LIMIT),
        name="swa",
    )(sinks, qsT, main, main, vT)


def _outmlp_kernel(on_ref, os_ref, x_ref, wo_ref, g2_ref, g3_ref, wu_ref, wd_ref, g4_ref, o_ref, *, ff_chunk):
    half = N_HEADS * HEAD_DIM // 2
    mix = _dot(on_ref[...], wo_ref[0:half, :]) + _dot(os_ref[...], wo_ref[half:2 * half, :])
    h1 = x_ref[...] + _rms(mix, g2_ref[...])
    m = _rms(h1, g3_ref[...]).astype(BF16)
    acc = jnp.zeros(h1.shape, F32)
    for c in range(D_FF // ff_chunk):
        u = _dot(m, wu_ref[:, c * ff_chunk:(c + 1) * ff_chunk])
        u = jnp.square(jnp.maximum(u, 0.0)).astype(BF16)
        acc = acc + _dot(u, wd_ref[c * ff_chunk:(c + 1) * ff_chunk, :])
    o_ref[...] = h1 + _rms(acc, g4_ref[...])


def _outmlp(onsa, oswa, x2, wo, g2, g3, wu, wd, g4, tm=512, ff_chunk=1024):
    n = x2.shape[0]

    def const(shape):
        return pl.BlockSpec(shape, lambda i: (0, 0), pipeline_mode=pl.Buffered(1))

    return pl.pallas_call(
        functools.partial(_outmlp_kernel, ff_chunk=ff_chunk),
        grid=(n // tm,),
        in_specs=[
            pl.BlockSpec((tm, 4 * LANE), lambda i: (i, 0)),
            pl.BlockSpec((tm, 4 * LANE), lambda i: (i, 0)),
            pl.BlockSpec((tm, D_MODEL), lambda i: (i, 0)),
            const((D_MODEL, D_MODEL)),
            const((1, D_MODEL)),
            const((1, D_MODEL)),
            const((D_MODEL, D_FF)),
            const((D_FF, D_MODEL)),
            const((1, D_MODEL)),
        ],
        out_specs=pl.BlockSpec((tm, D_MODEL), lambda i: (i, 0)),
        out_shape=jax.ShapeDtypeStruct((n, D_MODEL), F32),
        compiler_params=pltpu.CompilerParams(
            dimension_semantics=("arbitrary",), vmem_limit_bytes=VMEM_LIMIT),
        name="outmlp",
    )(onsa, oswa, x2, wo, g2, g3, wu, wd, g4)


def _pad_groups(w):
    z = jnp.zeros((w.shape[0], HEAD_DIM), w.dtype)
    return jnp.concatenate([w[:, :HEAD_DIM], z, w[:, HEAD_DIM:], z], axis=1)


def _layout_w_in(w):
    sizes = [512, 128, 128, 128, 128, 128, 128, N_HEADS // 2 * N_BRANCH, 512, 128, 128]
    offs = np.concatenate([[0], np.cumsum(sizes)])
    q_n, kc, vc, ks, vs, kw, vw, gt, q_s, k_s, v_s = [w[:, offs[k]:offs[k + 1]] for k in range(len(sizes))]
    std = [_pad_groups(t) for t in (ks, kw, k_s)] + [kc, vc]
    gt = jnp.concatenate([gt, jnp.zeros((w.shape[0], GT_ROWS - gt.shape[1]), w.dtype)], axis=1)
    tr = jnp.concatenate([q_n, q_s, vs, vw, v_s, gt], axis=1).T
    return jnp.concatenate(std, axis=1).astype(BF16), tr.astype(BF16)


def _layout_w1(w1):
    w = w1.reshape(2, CMP_STRIDE, HEAD_DIM, CMP_HIDDEN)
    z = jnp.zeros_like(w)
    top = jnp.concatenate([w, z], axis=-1)
    bot = jnp.concatenate([z, w], axis=-1)
    e = jnp.stack([top, bot], axis=2)
    return e.reshape(2, CMP_STRIDE * 2 * HEAD_DIM, 2 * CMP_HIDDEN).astype(BF16)


def _layout_w2(w2):
    z = jnp.zeros((CMP_HIDDEN, HEAD_DIM), w2.dtype)
    top = jnp.concatenate([w2, z, z, z], axis=1)
    bot = jnp.concatenate([z, z, w2, z], axis=1)
    return jnp.concatenate([top, bot], axis=0).astype(BF16)


def _layout_pe(pe):
    p = pe.reshape(2, CMP_STRIDE, 1, HEAD_DIM)
    return jnp.broadcast_to(p, (2, CMP_STRIDE, 2, HEAD_DIM)).reshape(2, 1, CMP_STRIDE * 2 * HEAD_DIM)


def kernel(x, norm_mix_pre, w_in, cmp_pe_k, cmp_w1_k, cmp_w2_k, cmp_pe_v, cmp_w1_v, cmp_w2_v,
           sinks, w_out, norm_mix_post, norm_mlp_pre, w_up, w_down, norm_mlp_post):
    b, seq, _ = x.shape
    assert seq // SEL_BLOCK == N_SEL and seq <= POS_SPLIT * 256
    depth = w_in.shape[0]
    h = x.reshape(b * seq, D_MODEL)
    for li in range(depth):
        w_p, wT_p = _layout_w_in(w_in[li])
        main, kc, vc, qT, qsT, vT, gT = _inproj(h, norm_mix_pre[li][None], w_p, wT_p, b, seq)
        z = jnp.stack([kc, vc]).reshape(2, b, seq // CMP_STRIDE, CMP_STRIDE * LANE)
        pe2 = jnp.stack([_layout_pe(cmp_pe_k[li]), _layout_pe(cmp_pe_v[li])])
        w1e = jnp.stack([_layout_w1(cmp_w1_k[li]), _layout_w1(cmp_w1_v[li])])
        w2e = jnp.stack([_layout_w2(cmp_w2_k[li]), _layout_w2(cmp_w2_v[li])])
        cmp_kv, cmp_kvT = _compress(z, pe2, w1e, w2e, jnp.swapaxes(w2e, 1, 2))
        ocT, qaT = _cmp(qT, cmp_kv, cmp_kvT, b, seq)
        onsa = _slcwin(qaT, main, vT, gT, ocT, b, seq)
        oswa = _swa(sinks[li], qsT, main, vT, b, seq)
        h = _outmlp(onsa, oswa, h, w_out[li].astype(BF16), norm_mix_post[li][None], norm_mlp_pre[li][None],
                    w_up[li].astype(BF16), w_down[li].astype(BF16), norm_mlp_post[li][None])
    return h.reshape(b, seq, D_MODEL)
```

```python
import functools
import math

import jax
import jax.numpy as jnp
import numpy as np
from jax import lax
from jax.experimental import pallas as pl
from jax.experimental.pallas import tpu as pltpu

F32 = jnp.float32
BF16 = jnp.bfloat16

D_MODEL = 1024
HEAD_DIM = 64
N_HEADS = 16
N_GROUPS = 2
N_REP = 4
N_QH = N_GROUPS * N_REP
CMP_BLOCK = 32
CMP_STRIDE = 16
CMP_HIDDEN = 4 * HEAD_DIM
SEL_BLOCK = 64
SEL_TOPN = 8
N_SEL = 32
NSA_WINDOW = 512
SWA_WINDOW = 128
D_FF = 4 * D_MODEL
NORM_EPS = 1e-6
N_BRANCH = 3

LANE = 128
LOG2E = math.log2(math.e)
NEG = -1e30
SEL_NEG = -(2.0 ** 100)
VMEM_LIMIT = 56 * 1024 * 1024

TK = 256
VT_ROWS = 80
SEL_ROW = HEAD_DIM
POS_ROW = HEAD_DIM + N_SEL
N_PIECE = 3
POS_SPLIT = 256

KS_COL, KW_COL, KSW_COL = 0, 2, 4
MAIN_COLS = 6 * LANE
STD_COLS = 5 * LANE
QT_ROWS = N_QH * HEAD_DIM
VT_ROW0 = 2 * QT_ROWS
GT_ROW = VT_ROW0 + 3 * N_GROUPS * HEAD_DIM
GT_ROWS = 32
T_ROWS = GT_ROW + GT_ROWS


def _slopes():
    s = 2.0 ** (-8.0 * (np.arange(N_HEADS) + 1) / N_HEADS)
    nsa = s[0::2].reshape(N_GROUPS, N_REP)
    swa = s[1::2].reshape(N_GROUPS, N_REP)
    return nsa, swa


SLOPES_NSA, SLOPES_SWA = _slopes()


def _bf16_round(x):
    u = np.float32(x).reshape(1).view(np.uint32)
    u = (u + (((u >> 16) & 1) + 0x7FFF)) & np.uint32(0xFFFF0000)
    return float(u.view(np.float32)[0])


def _bf16_pieces(x, n=N_PIECE):
    out, rem = [], float(np.float32(x))
    for _ in range(n):
        p = _bf16_round(rem)
        out.append(p)
        rem = float(np.float32(rem - p))
    return out


_NT = (((1,), (1,)), ((), ()))


def _dot_nt(a, b, **kw):
    return lax.dot_general(a, b, _NT, preferred_element_type=F32, **kw)


def _dot(a, b, **kw):
    return jnp.dot(a, b, preferred_element_type=F32, **kw)


def _rms(v, g):
    return v * lax.rsqrt(jnp.mean(v * v, axis=-1, keepdims=True) + NORM_EPS) * g


def _slope_rows(slope, tq):
    pieces = _bf16_pieces(slope * LOG2E)
    vals = pieces + [p * POS_SPLIT for p in pieces]
    prow = lax.broadcasted_iota(jnp.int32, (LANE - POS_ROW, tq), 0)
    feat = jnp.zeros((LANE - POS_ROW, tq), F32)
    for k, v in enumerate(vals):
        feat = jnp.where(prow == k, v, feat)
    return feat.astype(BF16)


def _inproj_kernel(x_ref, g_ref, w_ref, wT_ref, main_ref, kc_ref, vc_ref, qT_ref, qsT_ref, vT_ref, gT_ref,
                   *, tm, seq):
    qscale = LOG2E * HEAD_DIM ** -0.5
    lane = lax.broadcasted_iota(jnp.int32, (TK, LANE), 1)
    lo = lane < HEAD_DIM
    zmid = jnp.zeros((N_SEL, TK), BF16)
    ones_blk = jnp.where(lax.broadcasted_iota(jnp.int32, (VT_ROWS - HEAD_DIM, TK), 0) == 0, 1.0, 0.0).astype(BF16)
    t0 = (pl.program_id(0) * tm) % seq

    for st in range(tm // TK):
        rs = slice(st * TK, (st + 1) * TK)
        a = _rms(x_ref[rs, :], g_ref[...]).astype(BF16)

        res = _dot(a, w_ref[...])
        pos = t0 + st * TK + lax.broadcasted_iota(jnp.int32, (TK, LANE), 0)
        posf = jnp.where((lane >= POS_ROW) & (lane < POS_ROW + N_PIECE), (pos % POS_SPLIT).astype(F32), 0.0)
        posf = jnp.where((lane >= POS_ROW + N_PIECE) & (lane < POS_ROW + 2 * N_PIECE),
                         (pos // POS_SPLIT).astype(F32), posf)
        onehot = jnp.where(lane == pos // SEL_BLOCK + SEL_ROW, 1.0, 0.0)
        for t in range(3):
            blk = res[:, t * LANE:(t + 1) * LANE]
            extra = posf + onehot if 2 * t == KS_COL else posf
            for g, src in enumerate((blk, pltpu.roll(blk, HEAD_DIM, 1))):
                c = 2 * t + g
                main_ref[rs, c * LANE:(c + 1) * LANE] = (jnp.where(lo, src, 0.0) + extra).astype(BF16)
        kc_ref[rs, :] = res[:, 3 * LANE:4 * LANE].astype(kc_ref.dtype)
        vc_ref[rs, :] = res[:, 4 * LANE:STD_COLS].astype(vc_ref.dtype)

        resT = _dot_nt(wT_ref[...], a)
        qT_ref[0, :, rs] = (resT[0:QT_ROWS] * qscale).astype(BF16)
        for h in range(N_QH):
            r0 = QT_ROWS + h * HEAD_DIM
            qsT_ref[0, h, :, rs] = jnp.concatenate(
                [(resT[r0:r0 + HEAD_DIM] * qscale).astype(BF16), zmid,
                 _slope_rows(SLOPES_SWA[h // N_REP, h % N_REP], TK)], axis=0)
        for tg in range(3 * N_GROUPS):
            r0 = VT_ROW0 + tg * HEAD_DIM
            vT_ref[0, tg, st, 0:HEAD_DIM, :] = resT[r0:r0 + HEAD_DIM, :].astype(BF16)
            vT_ref[0, tg, st, HEAD_DIM:VT_ROWS, :] = ones_blk
        gT_ref[0, :, rs] = jax.nn.sigmoid(resT[GT_ROW:T_ROWS])


def _inproj(x2, gain, w_p, wT_p, b, seq, tm=512):
    n = x2.shape[0]
    nt = seq // tm
    return pl.pallas_call(
        functools.partial(_inproj_kernel, tm=tm, seq=seq),
        grid=(n // tm,),
        in_specs=[
            pl.BlockSpec((tm, D_MODEL), lambda i: (i, 0)),
            pl.BlockSpec((1, D_MODEL), lambda i: (0, 0)),
            pl.BlockSpec((D_MODEL, STD_COLS), lambda i: (0, 0)),
            pl.BlockSpec((T_ROWS, D_MODEL), lambda i: (0, 0)),
        ],
        out_specs=[
            pl.BlockSpec((tm, MAIN_COLS), lambda i: (i, 0)),
            pl.BlockSpec((tm, LANE), lambda i: (i, 0)),
            pl.BlockSpec((tm, LANE), lambda i: (i, 0)),
            pl.BlockSpec((1, QT_ROWS, tm), lambda i: (i // nt, 0, i % nt)),
            pl.BlockSpec((1, N_QH, LANE, tm), lambda i: (i // nt, 0, 0, i % nt)),
            pl.BlockSpec((1, 3 * N_GROUPS, tm // TK, VT_ROWS, TK), lambda i: (i // nt, 0, i % nt, 0, 0)),
            pl.BlockSpec((1, GT_ROWS, tm), lambda i: (i // nt, 0, i % nt)),
        ],
        out_shape=[
            jax.ShapeDtypeStruct((n, MAIN_COLS), BF16),
            jax.ShapeDtypeStruct((n, LANE), BF16),
            jax.ShapeDtypeStruct((n, LANE), BF16),
            jax.ShapeDtypeStruct((b, QT_ROWS, seq), BF16),
            jax.ShapeDtypeStruct((b, N_QH, LANE, seq), BF16),
            jax.ShapeDtypeStruct((b, 3 * N_GROUPS, seq // TK, VT_ROWS, TK), BF16),
            jax.ShapeDtypeStruct((b, GT_ROWS, seq), F32),
        ],
        compiler_params=pltpu.CompilerParams(
            dimension_semantics=("arbitrary",), vmem_limit_bytes=VMEM_LIMIT),
        name="inproj",
    )(x2, gain, w_p, wT_p)


def _compress_kernel(z_ref, pe_ref, w1_ref, w2_ref, w2T_ref, o_ref, oT_ref):
    z = z_ref[0, 0].astype(F32)
    zt = (z + pe_ref[0, 0]).astype(BF16)
    zb = (z + pe_ref[0, 1]).astype(BF16)
    a = _dot(zt, w1_ref[0, 0])
    bm = _dot(zb, w1_ref[0, 1])
    n_rows = a.shape[0]
    h = a + pltpu.roll(bm, n_rows - 1, 0)
    hg = jax.nn.gelu(h).astype(BF16)
    o_ref[0, 0] = _dot(hg, w2_ref[0]).astype(o_ref.dtype)
    oT_ref[0, 0] = _dot_nt(w2T_ref[0], hg).astype(oT_ref.dtype)


def _compress(z, pe2, w1e, w2e, w2eT):
    _, b, n, k = z.shape
    return pl.pallas_call(
        _compress_kernel,
        grid=(2, b),
        in_specs=[
            pl.BlockSpec((1, 1, n, k), lambda t, i: (t, i, 0, 0)),
            pl.BlockSpec((1, 2, 1, k), lambda t, i: (t, 0, 0, 0)),
            pl.BlockSpec((1, 2, k, 2 * CMP_HIDDEN), lambda t, i: (t, 0, 0, 0)),
            pl.BlockSpec((1, 2 * CMP_HIDDEN, 2 * LANE), lambda t, i: (t, 0, 0)),
            pl.BlockSpec((1, 2 * LANE, 2 * CMP_HIDDEN), lambda t, i: (t, 0, 0)),
        ],
        out_specs=[
            pl.BlockSpec((1, 1, n, 2 * LANE), lambda t, i: (t, i, 0, 0)),
            pl.BlockSpec((1, 1, 2 * LANE, n), lambda t, i: (t, i, 0, 0)),
        ],
        out_shape=[
            jax.ShapeDtypeStruct((2, b, n, 2 * LANE), BF16),
            jax.ShapeDtypeStruct((2, b, 2 * LANE, n), BF16),
        ],
        compiler_params=pltpu.CompilerParams(
            dimension_semantics=("arbitrary", "arbitrary"), vmem_limit_bytes=VMEM_LIMIT),
        name="compress",
    )(z, pe2, w1e, w2e, w2eT)


def _f32_dot_exact_lhs(a_bf16, x):
    out = None
    rem = x
    for _ in range(N_PIECE):
        piece = rem.astype(BF16)
        rem = rem - piece.astype(F32)
        d = _dot(a_bf16, piece)
        out = d if out is None else out + d
    return out


def _select_mask(score, tq):
    sub = 8
    n_chunk = N_SEL // sub
    chunks = [score[sub * c:sub * (c + 1)] for c in range(n_chunk)]
    ranks = [jnp.zeros((sub, tq), jnp.int32) for _ in range(n_chunk)]
    jrow = lax.broadcasted_iota(jnp.int32, (sub, tq), 0)
    for k in range(N_SEL):
        sk = score[k:k + 1, :]
        for c in range(n_chunk):
            if k < sub * c:
                ahead = sk >= chunks[c]
            elif k >= sub * (c + 1):
                ahead = sk > chunks[c]
            else:
                ahead = (sk > chunks[c]) | ((sk == chunks[c]) & (jrow + sub * c > k))
            ranks[c] = ranks[c] + ahead.astype(jnp.int32)
    rank = jnp.concatenate(ranks, axis=0)
    return jnp.where(rank < SEL_TOPN, 0.0, SEL_NEG).astype(BF16)


def _cmp_kernel(qT_ref, k_ref, vT_ref, ocT_ref, qaT_ref, s_scr, *, tq, nsub, n_cmp):
    i = pl.program_id(1)
    n_pad = k_ref.shape[2]
    nn = lax.broadcasted_iota(jnp.int32, (n_pad, tq), 0)
    cend = (lax.broadcasted_iota(jnp.int32, (n_pad, LANE), 0) * CMP_STRIDE + CMP_BLOCK - 1).astype(F32)

    jr = lax.broadcasted_iota(jnp.int32, (N_SEL, n_pad), 0)
    nc = lax.broadcasted_iota(jnp.int32, (N_SEL, n_pad), 1)
    ov = ((nc * CMP_STRIDE < jr * SEL_BLOCK + SEL_BLOCK) & (nc * CMP_STRIDE + CMP_BLOCK > jr * SEL_BLOCK)
          & (nc < n_cmp))
    ovT = jnp.where(ov, 1.0, 0.0).astype(BF16)
    jj = lax.broadcasted_iota(jnp.int32, (N_SEL, tq), 0)

    for sub in range(nsub):
        qs = slice(sub * tq, (sub + 1) * tq)
        for g in range(N_GROUPS):
            kc = k_ref[0, 0][:, g * LANE:g * LANE + HEAD_DIM]
            for r in range(N_REP):
                h = N_REP * g + r
                bias = float(SLOPES_NSA[g, r] * LOG2E) * cend
                s_scr[sub * N_QH + h] = (_dot(kc, qT_ref[0, h * HEAD_DIM:(h + 1) * HEAD_DIM, qs])
                                         + jnp.concatenate([bias] * (tq // LANE), axis=1))

    for sub in range(nsub):
        qs = slice(sub * tq, (sub + 1) * tq)
        t0 = (i * nsub + sub) * tq
        tt = t0 + lax.broadcasted_iota(jnp.int32, (n_pad, tq), 1)
        okc = (tt >= nn * CMP_STRIDE + CMP_BLOCK - 1) & (nn < n_cmp)
        blk_t = (t0 + lax.broadcasted_iota(jnp.int32, (N_SEL, tq), 1)) // SEL_BLOCK
        valid = jj <= blk_t
        forced = (jj == 0) | (jj == blk_t) | (jj == blk_t - 1)

        for g in range(N_GROUPS):
            vcT = vT_ref[0, 0][g * LANE:g * LANE + HEAD_DIM, :]
            psum = jnp.zeros((n_pad, tq), F32)
            qTs = []
            for r in range(N_REP):
                h = N_REP * g + r
                qTs.append(qT_ref[0, h * HEAD_DIM:(h + 1) * HEAD_DIM, qs])
                sm = jnp.where(okc, s_scr[sub * N_QH + h], NEG)
                m = jnp.max(sm, axis=0, keepdims=True)
                e = jnp.where(okc, jnp.exp2(sm - m), 0.0)
                den = jnp.sum(e, axis=0, keepdims=True)
                p = e * (1.0 / jnp.where(den > 0, den, 1.0))
                ocT_ref[0, h * HEAD_DIM:(h + 1) * HEAD_DIM, qs] = _dot(vcT, p.astype(BF16))
                psum = psum + p

            imp = _f32_dot_exact_lhs(ovT, psum)
            score = jnp.where(valid, imp, -jnp.inf)
            score = jnp.where(forced & valid, jnp.inf, score)
            negm = _select_mask(score, tq)
            for r in range(N_REP):
                qaT_ref[0, N_REP * g + r, :, qs] = jnp.concatenate(
                    [qTs[r], negm, _slope_rows(SLOPES_NSA[g, r], tq)], axis=0)


def _cmp(qT, cmp_k, cmp_vT, b, seq, tq=256, nsub=4):
    ts = tq * nsub
    nq = seq // ts
    n_cmp = (seq - CMP_BLOCK) // CMP_STRIDE + 1
    n_pad = cmp_k.shape[2]
    return pl.pallas_call(
        functools.partial(_cmp_kernel, tq=tq, nsub=nsub, n_cmp=n_cmp),
        grid=(b, nq),
        in_specs=[
            pl.BlockSpec((1, QT_ROWS, ts), lambda bi, i: (bi, 0, i)),
            pl.BlockSpec((1, 1, n_pad, 2 * LANE), lambda bi, i: (0, bi, 0, 0)),
            pl.BlockSpec((1, 1, 2 * LANE, n_pad), lambda bi, i: (1, bi, 0, 0)),
        ],
        out_specs=[
            pl.BlockSpec((1, QT_ROWS, ts), lambda bi, i: (bi, 0, i)),
            pl.BlockSpec((1, N_QH, LANE, ts), lambda bi, i: (bi, 0, 0, i)),
        ],
        out_shape=[
            jax.ShapeDtypeStruct((b, QT_ROWS, seq), F32),
            jax.ShapeDtypeStruct((b, N_QH, LANE, seq), BF16),
        ],
        scratch_shapes=[pltpu.VMEM((nsub * N_QH, n_pad, tq), F32)],
        compiler_params=pltpu.CompilerParams(
            dimension_semantics=("arbitrary", "arbitrary"), vmem_limit_bytes=VMEM_LIMIT),
        name="cmp",
    )(qT, cmp_k, cmp_vT)


SLC_AHEAD = 2
SLC_BUFS = SLC_AHEAD + 1

def _score_phase(qT_ref, k_tiles, mask, s_buf, mt_buf):
    for g in range(N_GROUPS):
        for r in range(N_REP):
            h = N_REP * g + r
            s = _dot(k_tiles[g], qT_ref[0, h])
            if mask is not None:
                s = s + mask
            s_buf[h] = s
            mt_buf[h] = jnp.max(s, axis=0, keepdims=True)


def _value_phase(vT_tiles, s_buf, mt_buf, m_scr, acc_scr):
    for g in range(N_GROUPS):
        for r in range(N_REP):
            h = N_REP * g + r
            m_old = m_scr[h]
            m_new = jnp.maximum(m_old, mt_buf[h])
            alpha = jnp.exp2(m_old - m_new)
            pT = jnp.exp2(s_buf[h] - m_new).astype(BF16)
            acc_scr[h] = alpha * acc_scr[h] + _dot(vT_tiles[g], pT)
            m_scr[h] = m_new


def _sweep_reset(m_scr, acc_scr):
    m_scr[...] = jnp.full(m_scr.shape, -3e38, F32)
    acc_scr[...] = jnp.zeros(acc_scr.shape, F32)


def _heads_out(o_ref, outs, rows=slice(None)):
    for g in range(N_GROUPS):
        oT = jnp.concatenate(outs[N_REP * g:N_REP * (g + 1)], axis=0)
        o_ref[rows, g * 2 * LANE:(g + 1) * 2 * LANE] = oT.T.astype(o_ref.dtype)


def _slcwin_kernel(qT_ref, ks0, ks1, kw0, kw1, vsT_ref, vwT_ref, gT_ref, ocT_ref,
                   o_ref, caus, s_scr, mt_scr, m_s, acc_s, m_w, acc_w, *, tq, nq):
    tk = TK
    bi = pl.program_id(0)
    i = pl.program_id(1)
    assert NSA_WINDOW // tk == 2 and tq == tk

    @pl.when((bi == 0) & (i == 0))
    def _init():
        kk = lax.broadcasted_iota(jnp.int32, (tk, tq), 0)
        qq = lax.broadcasted_iota(jnp.int32, (tk, tq), 1)
        caus[0] = jnp.where(kk <= qq, 0.0, NEG)
        caus[1] = jnp.where(kk > qq, 0.0, NEG)

    ks = (ks0, ks1)
    kw = (kw0, kw1)

    def run(ii):
        _sweep_reset(m_s, acc_s)
        _sweep_reset(m_w, acc_w)
        stream = []
        for d in (2, 1, 0):
            if ii - d >= 0:
                stream.append((kw, vwT_ref, ii - d, {0: 0, 2: 1}.get(d), m_w, acc_w))
        for j in range(ii + 1):
            stream.append((ks, vsT_ref, j, 0 if j == ii else None, m_s, acc_s))

        def score(p):
            refs, _, idx, mask_id, _, _ = stream[p]
            buf = p % SLC_BUFS
            _score_phase(qT_ref, [refs[g][idx * tk:(idx + 1) * tk, :] for g in range(N_GROUPS)],
                         None if mask_id is None else caus[mask_id], s_scr.at[buf], mt_scr.at[buf])

        def value(p):
            _, vref, idx, _, m_scr, acc_scr = stream[p]
            buf = p % SLC_BUFS
            _value_phase([vref[0, g, idx] for g in range(N_GROUPS)], s_scr.at[buf], mt_scr.at[buf], m_scr, acc_scr)

        for p in range(len(stream) + SLC_AHEAD):
            if p < len(stream):
                score(p)
            if p >= SLC_AHEAD:
                value(p - SLC_AHEAD)

        gT = gT_ref[0]
        outs = []
        for h in range(N_QH):
            gc = h * N_BRANCH
            f_slc = gT[gc + 1:gc + 2] * (1.0 / acc_s[h, HEAD_DIM:HEAD_DIM + 1, :])
            f_win = gT[gc + 2:gc + 3] * (1.0 / acc_w[h, HEAD_DIM:HEAD_DIM + 1, :])
            outs.append(gT[gc:gc + 1] * ocT_ref[0, h * HEAD_DIM:(h + 1) * HEAD_DIM, :]
                        + f_slc * acc_s[h, 0:HEAD_DIM, :]
                        + f_win * acc_w[h, 0:HEAD_DIM, :])
        _heads_out(o_ref, outs)

    for ii in range(nq):
        pl.when(i == ii)(functools.partial(run, ii))


def _slcwin(qaT, main, vT, gT, ocT, b, seq, tq=TK):
    nq = seq // tq
    n = b * seq

    def col(c):
        return pl.BlockSpec((seq, LANE), lambda bi, i, c=c: (bi, c))

    return pl.pallas_call(
        functools.partial(_slcwin_kernel, tq=tq, nq=nq),
        grid=(b, nq),
        in_specs=[
            pl.BlockSpec((1, N_QH, LANE, tq), lambda bi, i: (bi, 0, 0, i)),
            col(KS_COL), col(KS_COL + 1), col(KW_COL), col(KW_COL + 1),
            pl.BlockSpec((1, N_GROUPS, seq // TK, VT_ROWS, TK), lambda bi, i: (bi, 0, 0, 0, 0)),
            pl.BlockSpec((1, N_GROUPS, seq // TK, VT_ROWS, TK), lambda bi, i: (bi, 1, 0, 0, 0)),
            pl.BlockSpec((1, GT_ROWS, tq), lambda bi, i: (bi, 0, i)),
            pl.BlockSpec((1, QT_ROWS, tq), lambda bi, i: (bi, 0, i)),
        ],
        out_specs=pl.BlockSpec((tq, 4 * LANE), lambda bi, i: (bi * nq + i, 0)),
        out_shape=jax.ShapeDtypeStruct((n, 4 * LANE), BF16),
        scratch_shapes=[
            pltpu.VMEM((2, TK, tq), F32),
            pltpu.VMEM((SLC_BUFS, N_QH, TK, tq), F32),
            pltpu.VMEM((SLC_BUFS, N_QH, 1, tq), F32),
            pltpu.VMEM((N_QH, 1, tq), F32),
            pltpu.VMEM((N_QH, VT_ROWS, tq), F32),
            pltpu.VMEM((N_QH, 1, tq), F32),
            pltpu.VMEM((N_QH, VT_ROWS, tq), F32),
        ],
        compiler_params=pltpu.CompilerParams(
            dimension_semantics=("arbitrary", "arbitrary"), vmem_limit_bytes=VMEM_LIMIT),
        name="slcwin",
    )(qaT, main, main, main, main, vT, vT, gT, ocT)


SWA_AHEAD = 3
SWA_BUFS = SWA_AHEAD + 1

def _swa_kernel(sink_ref, qT_ref, k0, k1, vT_ref, o_ref, mfar, mmain, s_far, s_main, *, tq, nsub):
    bi = pl.program_id(0)
    i = pl.program_id(1)
    w = SWA_WINDOW
    assert tq == TK and w == LANE

    @pl.when((bi == 0) & (i == 0))
    def _init():
        kk = lax.broadcasted_iota(jnp.int32, (w, tq), 0)
        qq = lax.broadcasted_iota(jnp.int32, (w, tq), 1)
        mfar[0] = jnp.where(kk > qq, 0.0, NEG)
        mfar[1] = jnp.full((w, tq), NEG, F32)
        kk = lax.broadcasted_iota(jnp.int32, (tq, tq), 0)
        qq = lax.broadcasted_iota(jnp.int32, (tq, tq), 1)
        mmain[...] = jnp.where((kk <= qq) & (qq - kk < w), 0.0, NEG)

    kk_ref = (k0, k1)
    mask_main = mmain[...]
    mts, sinks, outs = {}, {}, {}

    def geometry(sub):
        tile = i * nsub + sub
        t0 = tile * tq
        return tile, t0

    def score(n):
        sub, h = divmod(n, N_QH)
        g, r = divmod(h, N_REP)
        tile, t0 = geometry(sub)
        far0 = pl.multiple_of(jnp.maximum(t0 - w, 0), w)
        main0 = pl.multiple_of(t0, tq)
        mask_far = mfar[jnp.where(tile == 0, 1, 0)]
        qaT = qT_ref[0, h, :, sub * tq:(sub + 1) * tq]
        sf = _dot(kk_ref[g][pl.ds(far0, w), :], qaT) + mask_far
        sm = _dot(kk_ref[g][pl.ds(main0, tq), :], qaT) + mask_main
        s_far[n % SWA_BUFS] = sf
        s_main[n % SWA_BUFS] = sm
        tpos = (t0 + lax.broadcasted_iota(jnp.int32, (1, tq), 1)).astype(F32)
        sink = (sink_ref[g, r] * LOG2E) + float(SLOPES_SWA[g, r] * LOG2E) * tpos
        mts[n] = jnp.maximum(jnp.maximum(jnp.max(sf, axis=0, keepdims=True),
                                         jnp.max(sm, axis=0, keepdims=True)), sink)
        sinks[n] = sink

    def value(n):
        sub, h = divmod(n, N_QH)
        g = h // N_REP
        tile, _ = geometry(sub)
        vT_far = vT_ref[0, g, jnp.maximum(tile - 1, 0)][:, w:2 * w]
        vT_main = vT_ref[0, g, tile]
        m = mts.pop(n)
        pf = jnp.exp2(s_far[n % SWA_BUFS] - m).astype(BF16)
        pm = jnp.exp2(s_main[n % SWA_BUFS] - m).astype(BF16)
        acc = _dot(vT_far, pf) + _dot(vT_main, pm)
        den = acc[HEAD_DIM:HEAD_DIM + 1, :] + jnp.exp2(sinks.pop(n) - m)
        outs[n] = acc[0:HEAD_DIM, :] * (1.0 / den)
        if h == N_QH - 1:
            _heads_out(o_ref, [outs.pop(sub * N_QH + hh) for hh in range(N_QH)], rows=slice(sub * tq, (sub + 1) * tq))

    n_items = nsub * N_QH
    for n in range(n_items + SWA_AHEAD):
        if n < n_items:
            score(n)
        if n >= SWA_AHEAD:
            value(n - SWA_AHEAD)


def _swa(sinks, qsT, main, vT, b, seq, tq=TK, nsub=4):
    ts = tq * nsub
    nq = seq // ts
    n = b * seq

    def col(c):
        return pl.BlockSpec((seq, LANE), lambda bi, i, c=c: (bi, c))

    return pl.pallas_call(
        functools.partial(_swa_kernel, tq=tq, nsub=nsub),
        grid=(b, nq),
        in_specs=[
            pl.BlockSpec(memory_space=pltpu.SMEM),
            pl.BlockSpec((1, N_QH, LANE, ts), lambda bi, i: (bi, 0, 0, i)),
            col(KSW_COL), col(KSW_COL + 1),
            pl.BlockSpec((1, N_GROUPS, seq // TK, VT_ROWS, TK), lambda bi, i: (bi, 2, 0, 0, 0)),
        ],
        out_specs=pl.BlockSpec((ts, 4 * LANE), lambda bi, i: (bi * nq + i, 0)),
        out_shape=jax.ShapeDtypeStruct((n, 4 * LANE), BF16),
        scratch_shapes=[
            pltpu.VMEM((2, SWA_WINDOW, tq), F32),
            pltpu.VMEM((tq, tq), F32),
            pltpu.VMEM((SWA_BUFS, SWA_WINDOW, tq), F32),
            pltpu.VMEM((SWA_BUFS, tq, tq), F32),
        ],
        compiler_params=pltpu.CompilerParams(
            dimension_semantics=("arbitrary", "arbitrary"), vmem_limit_bytes=VMEM_LIMIT),
        name="swa",
    )(sinks, qsT, main, main, vT)


def _outmlp_kernel(on_ref, os_ref, x_ref, wo_ref, g2_ref, g3_ref, wu_ref, wd_ref, g4_ref, o_ref, *, ff_chunk, n_sub):
    half = N_HEADS * HEAD_DIM // 2
    tm = x_ref.shape[0]
    subs = [slice(k * tm // n_sub, (k + 1) * tm // n_sub) for k in range(n_sub)]
    mixes = [_dot(on_ref[rs, :], wo_ref[0:half, :]) + _dot(os_ref[rs, :], wo_ref[half:2 * half, :]) for rs in subs]
    for rs, mix in zip(subs, mixes):
        h1 = x_ref[rs, :] + _rms(mix, g2_ref[...])
        m = _rms(h1, g3_ref[...]).astype(BF16)
        acc = jnp.zeros(h1.shape, F32)
        for c in range(D_FF // ff_chunk):
            u = _dot(m, wu_ref[:, c * ff_chunk:(c + 1) * ff_chunk])
            u = jnp.square(jnp.maximum(u, 0.0)).astype(BF16)
            acc = acc + _dot(u, wd_ref[c * ff_chunk:(c + 1) * ff_chunk, :])
        o_ref[rs, :] = h1 + _rms(acc, g4_ref[...])


def _outmlp(onsa, oswa, x2, wo, g2, g3, wu, wd, g4, tm=512, ff_chunk=1024, n_sub=2):
    n = x2.shape[0]

    def const(shape):
        return pl.BlockSpec(shape, lambda i: (0, 0), pipeline_mode=pl.Buffered(1))

    return pl.pallas_call(
        functools.partial(_outmlp_kernel, ff_chunk=ff_chunk, n_sub=n_sub),
        grid=(n // tm,),
        in_specs=[
            pl.BlockSpec((tm, 4 * LANE), lambda i: (i, 0)),
            pl.BlockSpec((tm, 4 * LANE), lambda i: (i, 0)),
            pl.BlockSpec((tm, D_MODEL), lambda i: (i, 0)),
            const((D_MODEL, D_MODEL)),
            const((1, D_MODEL)),
            const((1, D_MODEL)),
            const((D_MODEL, D_FF)),
            const((D_FF, D_MODEL)),
            const((1, D_MODEL)),
        ],
        out_specs=pl.BlockSpec((tm, D_MODEL), lambda i: (i, 0)),
        out_shape=jax.ShapeDtypeStruct((n, D_MODEL), F32),
        compiler_params=pltpu.CompilerParams(
            dimension_semantics=("arbitrary",), vmem_limit_bytes=VMEM_LIMIT),
        name="outmlp",
    )(onsa, oswa, x2, wo, g2, g3, wu, wd, g4)


def _layout_w_in(w):
    sizes = [512, 128, 128, 128, 128, 128, 128, N_HEADS // 2 * N_BRANCH, 512, 128, 128]
    offs = np.concatenate([[0], np.cumsum(sizes)])
    q_n, kc, vc, ks, vs, kw, vw, gt, q_s, k_s, v_s = [w[:, offs[k]:offs[k + 1]] for k in range(len(sizes))]
    std = [ks, kw, k_s, kc, vc]
    gt = jnp.concatenate([gt, jnp.zeros((w.shape[0], GT_ROWS - gt.shape[1]), w.dtype)], axis=1)
    tr = jnp.concatenate([q_n, q_s, vs, vw, v_s, gt], axis=1).T
    return jnp.concatenate(std, axis=1).astype(BF16), tr.astype(BF16)


def _layout_w1(w1):
    w = w1.reshape(2, CMP_STRIDE, HEAD_DIM, CMP_HIDDEN)
    z = jnp.zeros_like(w)
    top = jnp.concatenate([w, z], axis=-1)
    bot = jnp.concatenate([z, w], axis=-1)
    e = jnp.stack([top, bot], axis=2)
    return e.reshape(2, CMP_STRIDE * 2 * HEAD_DIM, 2 * CMP_HIDDEN).astype(BF16)


def _layout_w2(w2):
    z = jnp.zeros((CMP_HIDDEN, HEAD_DIM), w2.dtype)
    top = jnp.concatenate([w2, z, z, z], axis=1)
    bot = jnp.concatenate([z, z, w2, z], axis=1)
    return jnp.concatenate([top, bot], axis=0).astype(BF16)


def _layout_pe(pe):
    p = pe.reshape(2, CMP_STRIDE, 1, HEAD_DIM)
    return jnp.broadcast_to(p, (2, CMP_STRIDE, 2, HEAD_DIM)).reshape(2, 1, CMP_STRIDE * 2 * HEAD_DIM)


def kernel(x, norm_mix_pre, w_in, cmp_pe_k, cmp_w1_k, cmp_w2_k, cmp_pe_v, cmp_w1_v, cmp_w2_v,
           sinks, w_out, norm_mix_post, norm_mlp_pre, w_up, w_down, norm_mlp_post):
    b, seq, _ = x.shape
    assert seq // SEL_BLOCK == N_SEL and seq <= POS_SPLIT * 256
    depth = w_in.shape[0]
    h = x.reshape(b * seq, D_MODEL)
    for li in range(depth):
        w_p, wT_p = _layout_w_in(w_in[li])
        main, kc, vc, qT, qsT, vT, gT = _inproj(h, norm_mix_pre[li][None], w_p, wT_p, b, seq)
        z = jnp.stack([kc, vc]).reshape(2, b, seq // CMP_STRIDE, CMP_STRIDE * LANE)
        pe2 = jnp.stack([_layout_pe(cmp_pe_k[li]), _layout_pe(cmp_pe_v[li])])
        w1e = jnp.stack([_layout_w1(cmp_w1_k[li]), _layout_w1(cmp_w1_v[li])])
        w2e = jnp.stack([_layout_w2(cmp_w2_k[li]), _layout_w2(cmp_w2_v[li])])
        cmp_kv, cmp_kvT = _compress(z, pe2, w1e, w2e, jnp.swapaxes(w2e, 1, 2))
        ocT, qaT = _cmp(qT, cmp_kv, cmp_kvT, b, seq)
        onsa = _slcwin(qaT, main, vT, gT, ocT, b, seq)
        oswa = _swa(sinks[li], qsT, main, vT, b, seq)
        h = _outmlp(onsa, oswa, h, w_out[li].astype(BF16), norm_mix_post[li][None], norm_mlp_pre[li][None],
                    w_up[li].astype(BF16), w_down[li].astype(BF16), norm_mlp_post[li][None])
    return h.reshape(b, seq, D_MODEL)
```

```python
import functools
import math

import jax
import jax.numpy as jnp
import numpy as np
from jax import lax
from jax.experimental import pallas as pl
from jax.experimental.pallas import tpu as pltpu

F32 = jnp.float32
BF16 = jnp.bfloat16

D_MODEL = 1024
HEAD_DIM = 64
N_HEADS = 16
N_GROUPS = 2
N_REP = 4
N_QH = N_GROUPS * N_REP
CMP_BLOCK = 32
CMP_STRIDE = 16
CMP_HIDDEN = 4 * HEAD_DIM
SEL_BLOCK = 64
SEL_TOPN = 8
N_SEL = 32
NSA_WINDOW = 512
SWA_WINDOW = 128
D_FF = 4 * D_MODEL
NORM_EPS = 1e-6
N_BRANCH = 3

LANE = 128
LOG2E = math.log2(math.e)
NEG = -1e30
SEL_NEG = -(2.0 ** 100)
VMEM_LIMIT = 56 * 1024 * 1024

TK = 256
VT_ROWS = 80
SEL_ROW = HEAD_DIM
POS_ROW = HEAD_DIM + N_SEL
N_PIECE = 3
POS_SPLIT = 256

KS_COL, KW_COL, KSW_COL = 0, 2, 4
MAIN_COLS = 6 * LANE
STD_COLS = 5 * LANE
QT_ROWS = N_QH * HEAD_DIM
VT_ROW0 = 2 * QT_ROWS
GT_ROW = VT_ROW0 + 3 * N_GROUPS * HEAD_DIM
GT_ROWS = 32
T_ROWS = GT_ROW + GT_ROWS


def _slopes():
    s = 2.0 ** (-8.0 * (np.arange(N_HEADS) + 1) / N_HEADS)
    nsa = s[0::2].reshape(N_GROUPS, N_REP)
    swa = s[1::2].reshape(N_GROUPS, N_REP)
    return nsa, swa


SLOPES_NSA, SLOPES_SWA = _slopes()


def _bf16_round(x):
    u = np.float32(x).reshape(1).view(np.uint32)
    u = (u + (((u >> 16) & 1) + 0x7FFF)) & np.uint32(0xFFFF0000)
    return float(u.view(np.float32)[0])


def _bf16_pieces(x, n=N_PIECE):
    out, rem = [], float(np.float32(x))
    for _ in range(n):
        p = _bf16_round(rem)
        out.append(p)
        rem = float(np.float32(rem - p))
    return out


_NT = (((1,), (1,)), ((), ()))


def _dot_nt(a, b, **kw):
    return lax.dot_general(a, b, _NT, preferred_element_type=F32, **kw)


def _dot(a, b, **kw):
    return jnp.dot(a, b, preferred_element_type=F32, **kw)


def _rms(v, g):
    return v * lax.rsqrt(jnp.mean(v * v, axis=-1, keepdims=True) + NORM_EPS) * g


def _slope_rows(slope, tq):
    pieces = _bf16_pieces(slope * LOG2E)
    vals = pieces + [p * POS_SPLIT for p in pieces]
    prow = lax.broadcasted_iota(jnp.int32, (LANE - POS_ROW, tq), 0)
    feat = jnp.zeros((LANE - POS_ROW, tq), F32)
    for k, v in enumerate(vals):
        feat = jnp.where(prow == k, v, feat)
    return feat.astype(BF16)


def _inproj_kernel(x_ref, g_ref, w_ref, wT_ref, main_ref, z_ref, qT_ref, qsT_ref, vT_ref, gT_ref,
                   kc_scr, vc_scr, *, tm, seq):
    qscale = LOG2E * HEAD_DIM ** -0.5
    lane = lax.broadcasted_iota(jnp.int32, (TK, LANE), 1)
    lo = lane < HEAD_DIM
    zmid = jnp.zeros((N_SEL, TK), BF16)
    ones_blk = jnp.where(lax.broadcasted_iota(jnp.int32, (VT_ROWS - HEAD_DIM, TK), 0) == 0, 1.0, 0.0).astype(BF16)
    t0 = (pl.program_id(0) * tm) % seq

    for st in range(tm // TK):
        rs = slice(st * TK, (st + 1) * TK)
        a = _rms(x_ref[rs, :], g_ref[...]).astype(BF16)

        res = _dot(a, w_ref[...])
        pos = t0 + st * TK + lax.broadcasted_iota(jnp.int32, (TK, LANE), 0)
        posf = jnp.where((lane >= POS_ROW) & (lane < POS_ROW + N_PIECE), (pos % POS_SPLIT).astype(F32), 0.0)
        posf = jnp.where((lane >= POS_ROW + N_PIECE) & (lane < POS_ROW + 2 * N_PIECE),
                         (pos // POS_SPLIT).astype(F32), posf)
        onehot = jnp.where(lane == pos // SEL_BLOCK + SEL_ROW, 1.0, 0.0)
        for t in range(3):
            blk = res[:, t * LANE:(t + 1) * LANE]
            extra = posf + onehot if 2 * t == KS_COL else posf
            for g, src in enumerate((blk, pltpu.roll(blk, HEAD_DIM, 1))):
                c = 2 * t + g
                main_ref[rs, c * LANE:(c + 1) * LANE] = (jnp.where(lo, src, 0.0) + extra).astype(BF16)
        kc_scr[...] = res[:, 3 * LANE:4 * LANE]
        vc_scr[...] = res[:, 4 * LANE:STD_COLS]
        zr = TK // CMP_STRIDE
        for c in range(CMP_STRIDE):
            for t, scr in enumerate((kc_scr, vc_scr)):
                z_ref[t, 0, st * zr:(st + 1) * zr, c * LANE:(c + 1) * LANE] = (
                    scr[pl.ds(c, zr, stride=CMP_STRIDE), :].astype(BF16))

        resT = _dot_nt(wT_ref[...], a)
        qT_ref[0, :, rs] = (resT[0:QT_ROWS] * qscale).astype(BF16)
        for h in range(N_QH):
            r0 = QT_ROWS + h * HEAD_DIM
            qsT_ref[0, h, :, rs] = jnp.concatenate(
                [(resT[r0:r0 + HEAD_DIM] * qscale).astype(BF16), zmid,
                 _slope_rows(SLOPES_SWA[h // N_REP, h % N_REP], TK)], axis=0)
        for tg in range(3 * N_GROUPS):
            r0 = VT_ROW0 + tg * HEAD_DIM
            vT_ref[0, tg, st, 0:HEAD_DIM, :] = resT[r0:r0 + HEAD_DIM, :].astype(BF16)
            vT_ref[0, tg, st, HEAD_DIM:VT_ROWS, :] = ones_blk
        gT_ref[0, :, rs] = jax.nn.sigmoid(resT[GT_ROW:T_ROWS])


def _inproj(x2, gain, w_p, wT_p, b, seq, tm=512):
    n = x2.shape[0]
    nt = seq // tm
    return pl.pallas_call(
        functools.partial(_inproj_kernel, tm=tm, seq=seq),
        grid=(n // tm,),
        in_specs=[
            pl.BlockSpec((tm, D_MODEL), lambda i: (i, 0)),
            pl.BlockSpec((1, D_MODEL), lambda i: (0, 0)),
            pl.BlockSpec((D_MODEL, STD_COLS), lambda i: (0, 0)),
            pl.BlockSpec((T_ROWS, D_MODEL), lambda i: (0, 0)),
        ],
        out_specs=[
            pl.BlockSpec((tm, MAIN_COLS), lambda i: (i, 0)),
            pl.BlockSpec((2, 1, tm // CMP_STRIDE, CMP_STRIDE * LANE), lambda i: (0, i // nt, i % nt, 0)),
            pl.BlockSpec((1, QT_ROWS, tm), lambda i: (i // nt, 0, i % nt)),
            pl.BlockSpec((1, N_QH, LANE, tm), lambda i: (i // nt, 0, 0, i % nt)),
            pl.BlockSpec((1, 3 * N_GROUPS, tm // TK, VT_ROWS, TK), lambda i: (i // nt, 0, i % nt, 0, 0)),
            pl.BlockSpec((1, GT_ROWS, tm), lambda i: (i // nt, 0, i % nt)),
        ],
        out_shape=[
            jax.ShapeDtypeStruct((n, MAIN_COLS), BF16),
            jax.ShapeDtypeStruct((2, b, seq // CMP_STRIDE, CMP_STRIDE * LANE), BF16),
            jax.ShapeDtypeStruct((b, QT_ROWS, seq), BF16),
            jax.ShapeDtypeStruct((b, N_QH, LANE, seq), BF16),
            jax.ShapeDtypeStruct((b, 3 * N_GROUPS, seq // TK, VT_ROWS, TK), BF16),
            jax.ShapeDtypeStruct((b, GT_ROWS, seq), F32),
        ],
        scratch_shapes=[pltpu.VMEM((TK, LANE), F32), pltpu.VMEM((TK, LANE), F32)],
        compiler_params=pltpu.CompilerParams(
            dimension_semantics=("arbitrary",), vmem_limit_bytes=VMEM_LIMIT),
        name="inproj",
    )(x2, gain, w_p, wT_p)


def _compress_kernel(z_ref, pe_ref, w1_ref, w2_ref, w2T_ref, o_ref, oT_ref):
    nb, n, k = z_ref.shape[1:]
    z = z_ref[0].reshape(nb * n, k).astype(F32)
    zt = (z + pe_ref[0, 0]).astype(BF16)
    zb = (z + pe_ref[0, 1]).astype(BF16)
    a = _dot(zt, w1_ref[0, 0])
    bm = _dot(zb, w1_ref[0, 1])
    h = a + pltpu.roll(bm, nb * n - 1, 0)
    hg = jax.nn.gelu(h).astype(BF16)
    o = _dot(hg, w2_ref[0])
    for e in range(nb):
        o_ref[0, e] = o[e * n:(e + 1) * n].astype(o_ref.dtype)
        oT_ref[0, e] = _dot_nt(w2T_ref[0], hg[e * n:(e + 1) * n]).astype(oT_ref.dtype)


def _compress(z, pe2, w1e, w2e, w2eT, nb=4):
    _, b, n, k = z.shape
    nb = math.gcd(b, nb)
    return pl.pallas_call(
        _compress_kernel,
        grid=(2, b // nb),
        in_specs=[
            pl.BlockSpec((1, nb, n, k), lambda t, i: (t, i, 0, 0)),
            pl.BlockSpec((1, 2, 1, k), lambda t, i: (t, 0, 0, 0)),
            pl.BlockSpec((1, 2, k, 2 * CMP_HIDDEN), lambda t, i: (t, 0, 0, 0)),
            pl.BlockSpec((1, 2 * CMP_HIDDEN, 2 * LANE), lambda t, i: (t, 0, 0)),
            pl.BlockSpec((1, 2 * LANE, 2 * CMP_HIDDEN), lambda t, i: (t, 0, 0)),
        ],
        out_specs=[
            pl.BlockSpec((1, nb, n, 2 * LANE), lambda t, i: (t, i, 0, 0)),
            pl.BlockSpec((1, nb, 2 * LANE, n), lambda t, i: (t, i, 0, 0)),
        ],
        out_shape=[
            jax.ShapeDtypeStruct((2, b, n, 2 * LANE), BF16),
            jax.ShapeDtypeStruct((2, b, 2 * LANE, n), BF16),
        ],
        compiler_params=pltpu.CompilerParams(
            dimension_semantics=("arbitrary", "arbitrary"), vmem_limit_bytes=VMEM_LIMIT),
        name="compress",
    )(z, pe2, w1e, w2e, w2eT)


def _f32_dot_exact_lhs(a_bf16, x):
    out = None
    rem = x
    for _ in range(N_PIECE):
        piece = rem.astype(BF16)
        rem = rem - piece.astype(F32)
        d = _dot(a_bf16, piece)
        out = d if out is None else out + d
    return out


def _select_mask(score, tq):
    sub = 8
    n_chunk = N_SEL // sub
    chunks = [score[sub * c:sub * (c + 1)] for c in range(n_chunk)]
    ranks = [jnp.zeros((sub, tq), jnp.int32) for _ in range(n_chunk)]
    jrow = lax.broadcasted_iota(jnp.int32, (sub, tq), 0)
    for k in range(N_SEL):
        sk = score[k:k + 1, :]
        for c in range(n_chunk):
            if k < sub * c:
                ahead = sk >= chunks[c]
            elif k >= sub * (c + 1):
                ahead = sk > chunks[c]
            else:
                ahead = (sk > chunks[c]) | ((sk == chunks[c]) & (jrow + sub * c > k))
            ranks[c] = ranks[c] + ahead.astype(jnp.int32)
    rank = jnp.concatenate(ranks, axis=0)
    return jnp.where(rank < SEL_TOPN, 0.0, SEL_NEG).astype(BF16)


def _cmp_kernel(qT_ref, k_ref, vT_ref, ocT_ref, qaT_ref, s_scr, *, tq, nsub, n_cmp):
    i = pl.program_id(1)
    n_pad = k_ref.shape[2]
    nn = lax.broadcasted_iota(jnp.int32, (n_pad, tq), 0)
    cend = (lax.broadcasted_iota(jnp.int32, (n_pad, LANE), 0) * CMP_STRIDE + CMP_BLOCK - 1).astype(F32)

    jr = lax.broadcasted_iota(jnp.int32, (N_SEL, n_pad), 0)
    nc = lax.broadcasted_iota(jnp.int32, (N_SEL, n_pad), 1)
    ov = ((nc * CMP_STRIDE < jr * SEL_BLOCK + SEL_BLOCK) & (nc * CMP_STRIDE + CMP_BLOCK > jr * SEL_BLOCK)
          & (nc < n_cmp))
    ovT = jnp.where(ov, 1.0, 0.0).astype(BF16)
    jj = lax.broadcasted_iota(jnp.int32, (N_SEL, tq), 0)

    for sub in range(nsub):
        qs = slice(sub * tq, (sub + 1) * tq)
        for g in range(N_GROUPS):
            kc = k_ref[0, 0][:, g * LANE:g * LANE + HEAD_DIM]
            for r in range(N_REP):
                h = N_REP * g + r
                bias = float(SLOPES_NSA[g, r] * LOG2E) * cend
                s_scr[sub * N_QH + h] = (_dot(kc, qT_ref[0, h * HEAD_DIM:(h + 1) * HEAD_DIM, qs])
                                         + jnp.concatenate([bias] * (tq // LANE), axis=1))

    for sub in range(nsub):
        qs = slice(sub * tq, (sub + 1) * tq)
        t0 = (i * nsub + sub) * tq
        tt = t0 + lax.broadcasted_iota(jnp.int32, (n_pad, tq), 1)
        okc = (tt >= nn * CMP_STRIDE + CMP_BLOCK - 1) & (nn < n_cmp)
        blk_t = (t0 + lax.broadcasted_iota(jnp.int32, (N_SEL, tq), 1)) // SEL_BLOCK
        valid = jj <= blk_t
        forced = (jj == 0) | (jj == blk_t) | (jj == blk_t - 1)

        for g in range(N_GROUPS):
            vcT = vT_ref[0, 0][g * LANE:g * LANE + HEAD_DIM, :]
            psum = jnp.zeros((n_pad, tq), F32)
            qTs = []
            for r in range(N_REP):
                h = N_REP * g + r
                qTs.append(qT_ref[0, h * HEAD_DIM:(h + 1) * HEAD_DIM, qs])
                sm = jnp.where(okc, s_scr[sub * N_QH + h], NEG)
                m = jnp.max(sm, axis=0, keepdims=True)
                e = jnp.where(okc, jnp.exp2(sm - m), 0.0)
                den = jnp.sum(e, axis=0, keepdims=True)
                p = e * (1.0 / jnp.where(den > 0, den, 1.0))
                ocT_ref[0, h * HEAD_DIM:(h + 1) * HEAD_DIM, qs] = _dot(vcT, p.astype(BF16))
                psum = psum + p

            imp = _f32_dot_exact_lhs(ovT, psum)
            score = jnp.where(valid, imp, -jnp.inf)
            score = jnp.where(forced & valid, jnp.inf, score)
            negm = _select_mask(score, tq)
            for r in range(N_REP):
                qaT_ref[0, N_REP * g + r, :, qs] = jnp.concatenate(
                    [qTs[r], negm, _slope_rows(SLOPES_NSA[g, r], tq)], axis=0)


def _cmp(qT, cmp_k, cmp_vT, b, seq, tq=256, nsub=4):
    ts = tq * nsub
    nq = seq // ts
    n_cmp = (seq - CMP_BLOCK) // CMP_STRIDE + 1
    n_pad = cmp_k.shape[2]
    return pl.pallas_call(
        functools.partial(_cmp_kernel, tq=tq, nsub=nsub, n_cmp=n_cmp),
        grid=(b, nq),
        in_specs=[
            pl.BlockSpec((1, QT_ROWS, ts), lambda bi, i: (bi, 0, i)),
            pl.BlockSpec((1, 1, n_pad, 2 * LANE), lambda bi, i: (0, bi, 0, 0)),
            pl.BlockSpec((1, 1, 2 * LANE, n_pad), lambda bi, i: (1, bi, 0, 0)),
        ],
        out_specs=[
            pl.BlockSpec((1, QT_ROWS, ts), lambda bi, i: (bi, 0, i)),
            pl.BlockSpec((1, N_QH, LANE, ts), lambda bi, i: (bi, 0, 0, i)),
        ],
        out_shape=[
            jax.ShapeDtypeStruct((b, QT_ROWS, seq), F32),
            jax.ShapeDtypeStruct((b, N_QH, LANE, seq), BF16),
        ],
        scratch_shapes=[pltpu.VMEM((nsub * N_QH, n_pad, tq), F32)],
        compiler_params=pltpu.CompilerParams(
            dimension_semantics=("arbitrary", "arbitrary"), vmem_limit_bytes=VMEM_LIMIT),
        name="cmp",
    )(qT, cmp_k, cmp_vT)


SLC_AHEAD = 2
SLC_BUFS = SLC_AHEAD + 1

def _score_phase(qT_ref, k_tiles, mask, s_buf, mt_buf):
    for g in range(N_GROUPS):
        for r in range(N_REP):
            h = N_REP * g + r
            s = _dot(k_tiles[g], qT_ref[0, h])
            if mask is not None:
                s = s + mask
            s_buf[h] = s
            mt_buf[h] = jnp.max(s, axis=0, keepdims=True)


def _value_phase(vT_tiles, s_buf, mt_buf, m_scr, acc_scr):
    for g in range(N_GROUPS):
        for r in range(N_REP):
            h = N_REP * g + r
            m_old = m_scr[h]
            m_new = jnp.maximum(m_old, mt_buf[h])
            alpha = jnp.exp2(m_old - m_new)
            pT = jnp.exp2(s_buf[h] - m_new).astype(BF16)
            acc_scr[h] = alpha * acc_scr[h] + _dot(vT_tiles[g], pT)
            m_scr[h] = m_new


def _sweep_reset(m_scr, acc_scr):
    m_scr[...] = jnp.full(m_scr.shape, -3e38, F32)
    acc_scr[...] = jnp.zeros(acc_scr.shape, F32)


def _heads_out(o_ref, outs, rows=slice(None)):
    for g in range(N_GROUPS):
        oT = jnp.concatenate(outs[N_REP * g:N_REP * (g + 1)], axis=0)
        o_ref[rows, g * 2 * LANE:(g + 1) * 2 * LANE] = oT.T.astype(o_ref.dtype)


def _slcwin_kernel(qT_ref, ks0, ks1, kw0, kw1, vsT_ref, vwT_ref, gT_ref, ocT_ref,
                   o_ref, caus, s_scr, mt_scr, m_s, acc_s, m_w, acc_w, *, tq, nq):
    tk = TK
    bi = pl.program_id(0)
    i = pl.program_id(1)
    assert NSA_WINDOW // tk == 2 and tq == tk

    @pl.when((bi == 0) & (i == 0))
    def _init():
        kk = lax.broadcasted_iota(jnp.int32, (tk, tq), 0)
        qq = lax.broadcasted_iota(jnp.int32, (tk, tq), 1)
        caus[0] = jnp.where(kk <= qq, 0.0, NEG)
        caus[1] = jnp.where(kk > qq, 0.0, NEG)

    ks = (ks0, ks1)
    kw = (kw0, kw1)

    def run(ii):
        _sweep_reset(m_s, acc_s)
        _sweep_reset(m_w, acc_w)
        stream = []
        for d in (2, 1, 0):
            if ii - d >= 0:
                stream.append((kw, vwT_ref, ii - d, {0: 0, 2: 1}.get(d), m_w, acc_w))
        for j in range(ii + 1):
            stream.append((ks, vsT_ref, j, 0 if j == ii else None, m_s, acc_s))

        def score(p):
            refs, _, idx, mask_id, _, _ = stream[p]
            buf = p % SLC_BUFS
            _score_phase(qT_ref, [refs[g][idx * tk:(idx + 1) * tk, :] for g in range(N_GROUPS)],
                         None if mask_id is None else caus[mask_id], s_scr.at[buf], mt_scr.at[buf])

        def value(p):
            _, vref, idx, _, m_scr, acc_scr = stream[p]
            buf = p % SLC_BUFS
            _value_phase([vref[0, g, idx] for g in range(N_GROUPS)], s_scr.at[buf], mt_scr.at[buf], m_scr, acc_scr)

        for p in range(len(stream) + SLC_AHEAD):
            if p < len(stream):
                score(p)
            if p >= SLC_AHEAD:
                value(p - SLC_AHEAD)

        gT = gT_ref[0]
        outs = []
        for h in range(N_QH):
            gc = h * N_BRANCH
            f_slc = gT[gc + 1:gc + 2] * (1.0 / acc_s[h, HEAD_DIM:HEAD_DIM + 1, :])
            f_win = gT[gc + 2:gc + 3] * (1.0 / acc_w[h, HEAD_DIM:HEAD_DIM + 1, :])
            outs.append(gT[gc:gc + 1] * ocT_ref[0, h * HEAD_DIM:(h + 1) * HEAD_DIM, :]
                        + f_slc * acc_s[h, 0:HEAD_DIM, :]
                        + f_win * acc_w[h, 0:HEAD_DIM, :])
        _heads_out(o_ref, outs)

    for ii in range(nq):
        pl.when(i == ii)(functools.partial(run, ii))


def _slcwin(qaT, main, vT, gT, ocT, b, seq, tq=TK):
    nq = seq // tq
    n = b * seq

    def col(c):
        return pl.BlockSpec((seq, LANE), lambda bi, i, c=c: (bi, c))

    return pl.pallas_call(
        functools.partial(_slcwin_kernel, tq=tq, nq=nq),
        grid=(b, nq),
        in_specs=[
            pl.BlockSpec((1, N_QH, LANE, tq), lambda bi, i: (bi, 0, 0, i)),
            col(KS_COL), col(KS_COL + 1), col(KW_COL), col(KW_COL + 1),
            pl.BlockSpec((1, N_GROUPS, seq // TK, VT_ROWS, TK), lambda bi, i: (bi, 0, 0, 0, 0)),
            pl.BlockSpec((1, N_GROUPS, seq // TK, VT_ROWS, TK), lambda bi, i: (bi, 1, 0, 0, 0)),
            pl.BlockSpec((1, GT_ROWS, tq), lambda bi, i: (bi, 0, i)),
            pl.BlockSpec((1, QT_ROWS, tq), lambda bi, i: (bi, 0, i)),
        ],
        out_specs=pl.BlockSpec((tq, 4 * LANE), lambda bi, i: (bi * nq + i, 0)),
        out_shape=jax.ShapeDtypeStruct((n, 4 * LANE), BF16),
        scratch_shapes=[
            pltpu.VMEM((2, TK, tq), F32),
            pltpu.VMEM((SLC_BUFS, N_QH, TK, tq), F32),
            pltpu.VMEM((SLC_BUFS, N_QH, 1, tq), F32),
            pltpu.VMEM((N_QH, 1, tq), F32),
            pltpu.VMEM((N_QH, VT_ROWS, tq), F32),
            pltpu.VMEM((N_QH, 1, tq), F32),
            pltpu.VMEM((N_QH, VT_ROWS, tq), F32),
        ],
        compiler_params=pltpu.CompilerParams(
            dimension_semantics=("arbitrary", "arbitrary"), vmem_limit_bytes=VMEM_LIMIT),
        name="slcwin",
    )(qaT, main, main, main, main, vT, vT, gT, ocT)


SWA_AHEAD = 3
SWA_BUFS = SWA_AHEAD + 1

def _swa_kernel(sink_ref, qT_ref, k0, k1, vT_ref, o_ref, mfar, mmain, s_far, s_main, *, tq, nsub):
    bi = pl.program_id(0)
    i = pl.program_id(1)
    w = SWA_WINDOW
    assert tq == TK and w == LANE

    @pl.when((bi == 0) & (i == 0))
    def _init():
        kk = lax.broadcasted_iota(jnp.int32, (w, tq), 0)
        qq = lax.broadcasted_iota(jnp.int32, (w, tq), 1)
        mfar[0] = jnp.where(kk > qq, 0.0, NEG)
        mfar[1] = jnp.full((w, tq), NEG, F32)
        kk = lax.broadcasted_iota(jnp.int32, (tq, tq), 0)
        qq = lax.broadcasted_iota(jnp.int32, (tq, tq), 1)
        mmain[...] = jnp.where((kk <= qq) & (qq - kk < w), 0.0, NEG)

    kk_ref = (k0, k1)
    mask_main = mmain[...]
    mts, sinks, outs = {}, {}, {}

    def geometry(sub):
        tile = i * nsub + sub
        t0 = tile * tq
        return tile, t0

    def score(n):
        sub, h = divmod(n, N_QH)
        g, r = divmod(h, N_REP)
        tile, t0 = geometry(sub)
        far0 = pl.multiple_of(jnp.maximum(t0 - w, 0), w)
        main0 = pl.multiple_of(t0, tq)
        mask_far = mfar[jnp.where(tile == 0, 1, 0)]
        qaT = qT_ref[0, h, :, sub * tq:(sub + 1) * tq]
        sf = _dot(kk_ref[g][pl.ds(far0, w), :], qaT) + mask_far
        sm = _dot(kk_ref[g][pl.ds(main0, tq), :], qaT) + mask_main
        s_far[n % SWA_BUFS] = sf
        s_main[n % SWA_BUFS] = sm
        tpos = (t0 + lax.broadcasted_iota(jnp.int32, (1, tq), 1)).astype(F32)
        sink = (sink_ref[g, r] * LOG2E) + float(SLOPES_SWA[g, r] * LOG2E) * tpos
        mts[n] = jnp.maximum(jnp.maximum(jnp.max(sf, axis=0, keepdims=True),
                                         jnp.max(sm, axis=0, keepdims=True)), sink)
        sinks[n] = sink

    def value(n):
        sub, h = divmod(n, N_QH)
        g = h // N_REP
        tile, _ = geometry(sub)
        vT_far = vT_ref[0, g, jnp.maximum(tile - 1, 0)][:, w:2 * w]
        vT_main = vT_ref[0, g, tile]
        m = mts.pop(n)
        pf = jnp.exp2(s_far[n % SWA_BUFS] - m).astype(BF16)
        pm = jnp.exp2(s_main[n % SWA_BUFS] - m).astype(BF16)
        acc = _dot(vT_far, pf) + _dot(vT_main, pm)
        den = acc[HEAD_DIM:HEAD_DIM + 1, :] + jnp.exp2(sinks.pop(n) - m)
        outs[n] = acc[0:HEAD_DIM, :] * (1.0 / den)
        if h == N_QH - 1:
            _heads_out(o_ref, [outs.pop(sub * N_QH + hh) for hh in range(N_QH)], rows=slice(sub * tq, (sub + 1) * tq))

    n_items = nsub * N_QH
    for n in range(n_items + SWA_AHEAD):
        if n < n_items:
            score(n)
        if n >= SWA_AHEAD:
            value(n - SWA_AHEAD)


def _swa(sinks, qsT, main, vT, b, seq, tq=TK, nsub=4):
    ts = tq * nsub
    nq = seq // ts
    n = b * seq

    def col(c):
        return pl.BlockSpec((seq, LANE), lambda bi, i, c=c: (bi, c))

    return pl.pallas_call(
        functools.partial(_swa_kernel, tq=tq, nsub=nsub),
        grid=(b, nq),
        in_specs=[
            pl.BlockSpec(memory_space=pltpu.SMEM),
            pl.BlockSpec((1, N_QH, LANE, ts), lambda bi, i: (bi, 0, 0, i)),
            col(KSW_COL), col(KSW_COL + 1),
            pl.BlockSpec((1, N_GROUPS, seq // TK, VT_ROWS, TK), lambda bi, i: (bi, 2, 0, 0, 0)),
        ],
        out_specs=pl.BlockSpec((ts, 4 * LANE), lambda bi, i: (bi * nq + i, 0)),
        out_shape=jax.ShapeDtypeStruct((n, 4 * LANE), BF16),
        scratch_shapes=[
            pltpu.VMEM((2, SWA_WINDOW, tq), F32),
            pltpu.VMEM((tq, tq), F32),
            pltpu.VMEM((SWA_BUFS, SWA_WINDOW, tq), F32),
            pltpu.VMEM((SWA_BUFS, tq, tq), F32),
        ],
        compiler_params=pltpu.CompilerParams(
            dimension_semantics=("arbitrary", "arbitrary"), vmem_limit_bytes=VMEM_LIMIT),
        name="swa",
    )(sinks, qsT, main, main, vT)


def _outmlp_kernel(on_ref, os_ref, x_ref, wo_ref, g2_ref, g3_ref, wu_ref, wd_ref, g4_ref, o_ref, *, ff_chunk, n_sub):
    half = N_HEADS * HEAD_DIM // 2
    tm = x_ref.shape[0]
    subs = [slice(k * tm // n_sub, (k + 1) * tm // n_sub) for k in range(n_sub)]
    mixes = [_dot(on_ref[rs, :], wo_ref[0:half, :]) + _dot(os_ref[rs, :], wo_ref[half:2 * half, :]) for rs in subs]
    for rs, mix in zip(subs, mixes):
        h1 = x_ref[rs, :] + _rms(mix, g2_ref[...])
        m = _rms(h1, g3_ref[...]).astype(BF16)
        acc = jnp.zeros(h1.shape, F32)
        for c in range(D_FF // ff_chunk):
            u = _dot(m, wu_ref[:, c * ff_chunk:(c + 1) * ff_chunk])
            u = jnp.square(jnp.maximum(u, 0.0)).astype(BF16)
            acc = acc + _dot(u, wd_ref[c * ff_chunk:(c + 1) * ff_chunk, :])
        o_ref[rs, :] = h1 + _rms(acc, g4_ref[...])


def _outmlp(onsa, oswa, x2, wo, g2, g3, wu, wd, g4, tm=512, ff_chunk=1024, n_sub=2):
    n = x2.shape[0]

    def const(shape):
        return pl.BlockSpec(shape, lambda i: (0, 0), pipeline_mode=pl.Buffered(1))

    return pl.pallas_call(
        functools.partial(_outmlp_kernel, ff_chunk=ff_chunk, n_sub=n_sub),
        grid=(n // tm,),
        in_specs=[
            pl.BlockSpec((tm, 4 * LANE), lambda i: (i, 0)),
            pl.BlockSpec((tm, 4 * LANE), lambda i: (i, 0)),
            pl.BlockSpec((tm, D_MODEL), lambda i: (i, 0)),
            const((D_MODEL, D_MODEL)),
            const((1, D_MODEL)),
            const((1, D_MODEL)),
            const((D_MODEL, D_FF)),
            const((D_FF, D_MODEL)),
            const((1, D_MODEL)),
        ],
        out_specs=pl.BlockSpec((tm, D_MODEL), lambda i: (i, 0)),
        out_shape=jax.ShapeDtypeStruct((n, D_MODEL), F32),
        compiler_params=pltpu.CompilerParams(
            dimension_semantics=("arbitrary",), vmem_limit_bytes=VMEM_LIMIT),
        name="outmlp",
    )(onsa, oswa, x2, wo, g2, g3, wu, wd, g4)


def _layout_w_in(w):
    sizes = [512, 128, 128, 128, 128, 128, 128, N_HEADS // 2 * N_BRANCH, 512, 128, 128]
    offs = np.concatenate([[0], np.cumsum(sizes)])
    q_n, kc, vc, ks, vs, kw, vw, gt, q_s, k_s, v_s = [w[:, offs[k]:offs[k + 1]] for k in range(len(sizes))]
    std = [ks, kw, k_s, kc, vc]
    gt = jnp.concatenate([gt, jnp.zeros((w.shape[0], GT_ROWS - gt.shape[1]), w.dtype)], axis=1)
    tr = jnp.concatenate([q_n, q_s, vs, vw, v_s, gt], axis=1).T
    return jnp.concatenate(std, axis=1).astype(BF16), tr.astype(BF16)


def _layout_w1(w1):
    w = w1.reshape(2, CMP_STRIDE, HEAD_DIM, CMP_HIDDEN)
    z = jnp.zeros_like(w)
    top = jnp.concatenate([w, z], axis=-1)
    bot = jnp.concatenate([z, w], axis=-1)
    e = jnp.stack([top, bot], axis=2)
    return e.reshape(2, CMP_STRIDE * 2 * HEAD_DIM, 2 * CMP_HIDDEN).astype(BF16)


def _layout_w2(w2):
    z = jnp.zeros((CMP_HIDDEN, HEAD_DIM), w2.dtype)
    top = jnp.concatenate([w2, z, z, z], axis=1)
    bot = jnp.concatenate([z, z, w2, z], axis=1)
    return jnp.concatenate([top, bot], axis=0).astype(BF16)


def _layout_pe(pe):
    p = pe.reshape(2, CMP_STRIDE, 1, HEAD_DIM)
    return jnp.broadcast_to(p, (2, CMP_STRIDE, 2, HEAD_DIM)).reshape(2, 1, CMP_STRIDE * 2 * HEAD_DIM)


def kernel(x, norm_mix_pre, w_in, cmp_pe_k, cmp_w1_k, cmp_w2_k, cmp_pe_v, cmp_w1_v, cmp_w2_v,
           sinks, w_out, norm_mix_post, norm_mlp_pre, w_up, w_down, norm_mlp_post):
    b, seq, _ = x.shape
    assert seq // SEL_BLOCK == N_SEL and seq <= POS_SPLIT * 256
    depth = w_in.shape[0]
    h = x.reshape(b * seq, D_MODEL)
    for li in range(depth):
        w_p, wT_p = _layout_w_in(w_in[li])
        main, z, qT, qsT, vT, gT = _inproj(h, norm_mix_pre[li][None], w_p, wT_p, b, seq)
        pe2 = jnp.stack([_layout_pe(cmp_pe_k[li]), _layout_pe(cmp_pe_v[li])])
        w1e = jnp.stack([_layout_w1(cmp_w1_k[li]), _layout_w1(cmp_w1_v[li])])
        w2e = jnp.stack([_layout_w2(cmp_w2_k[li]), _layout_w2(cmp_w2_v[li])])
        cmp_kv, cmp_kvT = _compress(z, pe2, w1e, w2e, jnp.swapaxes(w2e, 1, 2))
        ocT, qaT = _cmp(qT, cmp_kv, cmp_kvT, b, seq)
        onsa = _slcwin(qaT, main, vT, gT, ocT, b, seq)
        oswa = _swa(sinks[li], qsT, main, vT, b, seq)
        h = _outmlp(onsa, oswa, h, w_out[li].astype(BF16), norm_mix_post[li][None], norm_mlp_pre[li][None],
                    w_up[li].astype(BF16), w_down[li].astype(BF16), norm_mlp_post[li][None])
    return h.reshape(b, seq, D_MODEL)
```

```python
import functools
import math

import jax
import jax.numpy as jnp
import numpy as np
from jax import lax
from jax.experimental import pallas as pl
from jax.experimental.pallas import tpu as pltpu

F32 = jnp.float32
BF16 = jnp.bfloat16

D_MODEL = 1024
HEAD_DIM = 64
N_HEADS = 16
N_GROUPS = 2
N_REP = 4
N_QH = N_GROUPS * N_REP
CMP_BLOCK = 32
CMP_STRIDE = 16
CMP_HIDDEN = 4 * HEAD_DIM
SEL_BLOCK = 64
SEL_TOPN = 8
N_SEL = 32
NSA_WINDOW = 512
SWA_WINDOW = 128
D_FF = 4 * D_MODEL
NORM_EPS = 1e-6
N_BRANCH = 3

LANE = 128
LOG2E = math.log2(math.e)
NEG = -1e30
SEL_NEG = -(2.0 ** 100)
VMEM_LIMIT = 56 * 1024 * 1024

TK = 256
VT_ROWS = 80
SEL_ROW = HEAD_DIM
POS_ROW = HEAD_DIM + N_SEL
N_PIECE = 3
CMP_FEAT = 16
POS_SPLIT = 256

KS_COL, KW_COL, KSW_COL = 0, 2, 4
MAIN_COLS = 6 * LANE
STD_COLS = 5 * LANE
QT_ROWS = N_QH * HEAD_DIM
VT_ROW0 = 2 * QT_ROWS
GT_ROW = VT_ROW0 + 3 * N_GROUPS * HEAD_DIM
GT_ROWS = 32
T_ROWS = GT_ROW + GT_ROWS


def _slopes():
    s = 2.0 ** (-8.0 * (np.arange(N_HEADS) + 1) / N_HEADS)
    nsa = s[0::2].reshape(N_GROUPS, N_REP)
    swa = s[1::2].reshape(N_GROUPS, N_REP)
    return nsa, swa


SLOPES_NSA, SLOPES_SWA = _slopes()


def _bf16_round(x):
    u = np.float32(x).reshape(1).view(np.uint32)
    u = (u + (((u >> 16) & 1) + 0x7FFF)) & np.uint32(0xFFFF0000)
    return float(u.view(np.float32)[0])


def _bf16_pieces(x, n=N_PIECE):
    out, rem = [], float(np.float32(x))
    for _ in range(n):
        p = _bf16_round(rem)
        out.append(p)
        rem = float(np.float32(rem - p))
    return out


_NT = (((1,), (1,)), ((), ()))


def _dot_nt(a, b, **kw):
    return lax.dot_general(a, b, _NT, preferred_element_type=F32, **kw)


def _dot(a, b, **kw):
    return jnp.dot(a, b, preferred_element_type=F32, **kw)


def _rms(v, g):
    return v * lax.rsqrt(jnp.mean(v * v, axis=-1, keepdims=True) + NORM_EPS) * g


def _slope_rows(slope, tq):
    pieces = _bf16_pieces(slope * LOG2E)
    vals = pieces + [p * POS_SPLIT for p in pieces]
    prow = lax.broadcasted_iota(jnp.int32, (LANE - POS_ROW, tq), 0)
    feat = jnp.zeros((LANE - POS_ROW, tq), F32)
    for k, v in enumerate(vals):
        feat = jnp.where(prow == k, v, feat)
    return feat.astype(BF16)


def _inproj_kernel(x_ref, g_ref, w_ref, wT_ref, main_ref, z_ref, qT_ref, qsT_ref, vT_ref, gT_ref,
                   kc_scr, vc_scr, *, tm, seq):
    qscale = LOG2E * HEAD_DIM ** -0.5
    lane = lax.broadcasted_iota(jnp.int32, (TK, LANE), 1)
    lo = lane < HEAD_DIM
    zmid = jnp.zeros((N_SEL, TK), BF16)
    ones_blk = jnp.where(lax.broadcasted_iota(jnp.int32, (VT_ROWS - HEAD_DIM, TK), 0) == 0, 1.0, 0.0).astype(BF16)
    t0 = (pl.program_id(0) * tm) % seq

    for st in range(tm // TK):
        rs = slice(st * TK, (st + 1) * TK)
        a = _rms(x_ref[rs, :], g_ref[...]).astype(BF16)

        res = _dot(a, w_ref[...])
        pos = t0 + st * TK + lax.broadcasted_iota(jnp.int32, (TK, LANE), 0)
        posf = jnp.where((lane >= POS_ROW) & (lane < POS_ROW + N_PIECE), (pos % POS_SPLIT).astype(F32), 0.0)
        posf = jnp.where((lane >= POS_ROW + N_PIECE) & (lane < POS_ROW + 2 * N_PIECE),
                         (pos // POS_SPLIT).astype(F32), posf)
        onehot = jnp.where(lane == pos // SEL_BLOCK + SEL_ROW, 1.0, 0.0)
        for t in range(3):
            blk = res[:, t * LANE:(t + 1) * LANE]
            extra = posf + onehot if 2 * t == KS_COL else posf
            for g, src in enumerate((blk, pltpu.roll(blk, HEAD_DIM, 1))):
                c = 2 * t + g
                main_ref[rs, c * LANE:(c + 1) * LANE] = (jnp.where(lo, src, 0.0) + extra).astype(BF16)
        kc_scr[...] = res[:, 3 * LANE:4 * LANE]
        vc_scr[...] = res[:, 4 * LANE:STD_COLS]
        zr = TK // CMP_STRIDE
        for c in range(CMP_STRIDE):
            for t, scr in enumerate((kc_scr, vc_scr)):
                z_ref[t, 0, st * zr:(st + 1) * zr, c * LANE:(c + 1) * LANE] = (
                    scr[pl.ds(c, zr, stride=CMP_STRIDE), :].astype(BF16))

        resT = _dot_nt(wT_ref[...], a)
        qT_ref[0, :, rs] = (resT[0:QT_ROWS] * qscale).astype(BF16)
        for h in range(N_QH):
            r0 = QT_ROWS + h * HEAD_DIM
            qsT_ref[0, h, :, rs] = jnp.concatenate(
                [(resT[r0:r0 + HEAD_DIM] * qscale).astype(BF16), zmid,
                 _slope_rows(SLOPES_SWA[h // N_REP, h % N_REP], TK)], axis=0)
        for tg in range(3 * N_GROUPS):
            r0 = VT_ROW0 + tg * HEAD_DIM
            vT_ref[0, tg, st, 0:HEAD_DIM, :] = resT[r0:r0 + HEAD_DIM, :].astype(BF16)
            vT_ref[0, tg, st, HEAD_DIM:VT_ROWS, :] = ones_blk
        gT_ref[0, :, rs] = jax.nn.sigmoid(resT[GT_ROW:T_ROWS])


def _inproj(x2, gain, w_p, wT_p, b, seq, tm=1024):
    n = x2.shape[0]
    nt = seq // tm
    return pl.pallas_call(
        functools.partial(_inproj_kernel, tm=tm, seq=seq),
        grid=(n // tm,),
        in_specs=[
            pl.BlockSpec((tm, D_MODEL), lambda i: (i, 0)),
            pl.BlockSpec((1, D_MODEL), lambda i: (0, 0)),
            pl.BlockSpec((D_MODEL, STD_COLS), lambda i: (0, 0)),
            pl.BlockSpec((T_ROWS, D_MODEL), lambda i: (0, 0)),
        ],
        out_specs=[
            pl.BlockSpec((tm, MAIN_COLS), lambda i: (i, 0)),
            pl.BlockSpec((2, 1, tm // CMP_STRIDE, CMP_STRIDE * LANE), lambda i: (0, i // nt, i % nt, 0)),
            pl.BlockSpec((1, QT_ROWS, tm), lambda i: (i // nt, 0, i % nt)),
            pl.BlockSpec((1, N_QH, LANE, tm), lambda i: (i // nt, 0, 0, i % nt)),
            pl.BlockSpec((1, 3 * N_GROUPS, tm // TK, VT_ROWS, TK), lambda i: (i // nt, 0, i % nt, 0, 0)),
            pl.BlockSpec((1, GT_ROWS, tm), lambda i: (i // nt, 0, i % nt)),
        ],
        out_shape=[
            jax.ShapeDtypeStruct((n, MAIN_COLS), BF16),
            jax.ShapeDtypeStruct((2, b, seq // CMP_STRIDE, CMP_STRIDE * LANE), BF16),
            jax.ShapeDtypeStruct((b, QT_ROWS, seq), BF16),
            jax.ShapeDtypeStruct((b, N_QH, LANE, seq), BF16),
            jax.ShapeDtypeStruct((b, 3 * N_GROUPS, seq // TK, VT_ROWS, TK), BF16),
            jax.ShapeDtypeStruct((b, GT_ROWS, seq), F32),
        ],
        scratch_shapes=[pltpu.VMEM((TK, LANE), F32), pltpu.VMEM((TK, LANE), F32)],
        compiler_params=pltpu.CompilerParams(
            dimension_semantics=("arbitrary",), vmem_limit_bytes=VMEM_LIMIT),
        name="inproj",
    )(x2, gain, w_p, wT_p)


def _compress_kernel(z_ref, pe_ref, w1_ref, w2_ref, w2T_ref, o_ref, oT_ref):
    nb, n, k = z_ref.shape[1:]
    z = z_ref[0].reshape(nb * n, k).astype(F32)
    zt = (z + pe_ref[0, 0]).astype(BF16)
    zb = (z + pe_ref[0, 1]).astype(BF16)
    a = _dot(zt, w1_ref[0, 0])
    bm = _dot(zb, w1_ref[0, 1])
    h = a + pltpu.roll(bm, nb * n - 1, 0)
    hg = jax.nn.gelu(h).astype(BF16)
    o = _dot(hg, w2_ref[0])
    lane = lax.broadcasted_iota(jnp.int32, o.shape, 1) % LANE
    nidx = lax.broadcasted_iota(jnp.int32, o.shape, 0) % n
    o = o + jnp.where((lane >= HEAD_DIM) & (lane < HEAD_DIM + N_PIECE), nidx.astype(F32), 0.0)
    for e in range(nb):
        o_ref[0, e] = o[e * n:(e + 1) * n].astype(o_ref.dtype)
        oT_ref[0, e] = _dot_nt(w2T_ref[0], hg[e * n:(e + 1) * n]).astype(oT_ref.dtype)


def _compress(z, pe2, w1e, w2e, w2eT, nb=4):
    _, b, n, k = z.shape
    nb = math.gcd(b, nb)
    return pl.pallas_call(
        _compress_kernel,
        grid=(2, b // nb),
        in_specs=[
            pl.BlockSpec((1, nb, n, k), lambda t, i: (t, i, 0, 0)),
            pl.BlockSpec((1, 2, 1, k), lambda t, i: (t, 0, 0, 0)),
            pl.BlockSpec((1, 2, k, 2 * CMP_HIDDEN), lambda t, i: (t, 0, 0, 0)),
            pl.BlockSpec((1, 2 * CMP_HIDDEN, 2 * LANE), lambda t, i: (t, 0, 0)),
            pl.BlockSpec((1, 2 * LANE, 2 * CMP_HIDDEN), lambda t, i: (t, 0, 0)),
        ],
        out_specs=[
            pl.BlockSpec((1, nb, n, 2 * LANE), lambda t, i: (t, i, 0, 0)),
            pl.BlockSpec((1, nb, 2 * LANE, n), lambda t, i: (t, i, 0, 0)),
        ],
        out_shape=[
            jax.ShapeDtypeStruct((2, b, n, 2 * LANE), BF16),
            jax.ShapeDtypeStruct((2, b, 2 * LANE, n), BF16),
        ],
        compiler_params=pltpu.CompilerParams(
            dimension_semantics=("arbitrary", "arbitrary"), vmem_limit_bytes=VMEM_LIMIT),
        name="compress",
    )(z, pe2, w1e, w2e, w2eT)


def _f32_dot_exact_lhs(a_bf16, x):
    out = None
    rem = x
    for _ in range(N_PIECE):
        piece = rem.astype(BF16)
        rem = rem - piece.astype(F32)
        d = _dot(a_bf16, piece)
        out = d if out is None else out + d
    return out


def _select_mask(score, tq):
    sub = 8
    n_chunk = N_SEL // sub
    chunks = [score[sub * c:sub * (c + 1)] for c in range(n_chunk)]
    ranks = [jnp.zeros((sub, tq), jnp.int32) for _ in range(n_chunk)]
    jrow = lax.broadcasted_iota(jnp.int32, (sub, tq), 0)
    for k in range(N_SEL):
        sk = score[k:k + 1, :]
        for c in range(n_chunk):
            if k < sub * c:
                ahead = sk >= chunks[c]
            elif k >= sub * (c + 1):
                ahead = sk > chunks[c]
            else:
                ahead = (sk > chunks[c]) | ((sk == chunks[c]) & (jrow + sub * c > k))
            ranks[c] = ranks[c] + ahead.astype(jnp.int32)
    rank = jnp.concatenate(ranks, axis=0)
    return jnp.where(rank < SEL_TOPN, 0.0, SEL_NEG).astype(BF16)


def _cmp_kernel(qT_ref, k_ref, vT_ref, ocT_ref, qaT_ref, s_scr, *, tq, nsub, n_cmp):
    i = pl.program_id(1)
    n_pad = k_ref.shape[2]
    nn = lax.broadcasted_iota(jnp.int32, (n_pad, tq), 0)
    frow = lax.broadcasted_iota(jnp.int32, (CMP_FEAT, tq), 0)

    jr = lax.broadcasted_iota(jnp.int32, (N_SEL, n_pad), 0)
    nc = lax.broadcasted_iota(jnp.int32, (N_SEL, n_pad), 1)
    ov = ((nc * CMP_STRIDE < jr * SEL_BLOCK + SEL_BLOCK) & (nc * CMP_STRIDE + CMP_BLOCK > jr * SEL_BLOCK)
          & (nc < n_cmp))
    ovT = jnp.where(ov, 1.0, 0.0).astype(BF16)
    jj = lax.broadcasted_iota(jnp.int32, (N_SEL, tq), 0)

    for sub in range(nsub):
        qs = slice(sub * tq, (sub + 1) * tq)
        for g in range(N_GROUPS):
            kc = k_ref[0, 0][:, g * LANE:g * LANE + HEAD_DIM + CMP_FEAT]
            for r in range(N_REP):
                h = N_REP * g + r
                feat = jnp.zeros((CMP_FEAT, tq), F32)
                for k, pc in enumerate(_bf16_pieces(SLOPES_NSA[g, r] * LOG2E)):
                    feat = jnp.where(frow == k, pc * CMP_STRIDE, feat)
                qTa = jnp.concatenate([qT_ref[0, h * HEAD_DIM:(h + 1) * HEAD_DIM, qs], feat.astype(BF16)], axis=0)
                s_scr[sub * N_QH + h] = _dot(kc, qTa)

    for sub in range(nsub):
        qs = slice(sub * tq, (sub + 1) * tq)
        t0 = (i * nsub + sub) * tq
        tt = t0 + lax.broadcasted_iota(jnp.int32, (n_pad, tq), 1)
        maskadd = jnp.where((tt >= nn * CMP_STRIDE + CMP_BLOCK - 1) & (nn < n_cmp), 0.0, NEG)
        colvalid = jnp.where(t0 + lax.broadcasted_iota(jnp.int32, (1, tq), 1) >= CMP_BLOCK - 1, 1.0, 0.0)
        blk_t = (t0 + lax.broadcasted_iota(jnp.int32, (N_SEL, tq), 1)) // SEL_BLOCK
        valid = jj <= blk_t
        forced = (jj == 0) | (jj == blk_t) | (jj == blk_t - 1)

        for g in range(N_GROUPS):
            vcT = vT_ref[0, 0][g * LANE:g * LANE + HEAD_DIM, :]
            psum = jnp.zeros((n_pad, tq), F32)
            qTs = []
            for r in range(N_REP):
                h = N_REP * g + r
                qTs.append(qT_ref[0, h * HEAD_DIM:(h + 1) * HEAD_DIM, qs])
                sm = s_scr[sub * N_QH + h] + maskadd
                m = jnp.max(sm, axis=0, keepdims=True)
                e = jnp.exp2(sm - m)
                den = jnp.sum(e, axis=0, keepdims=True)
                p = e * (colvalid / den)
                ocT_ref[0, h * HEAD_DIM:(h + 1) * HEAD_DIM, qs] = _dot(vcT, p.astype(BF16))
                psum = psum + p

            imp = _f32_dot_exact_lhs(ovT, psum)
            score = jnp.where(valid, imp, -jnp.inf)
            score = jnp.where(forced & valid, jnp.inf, score)
            negm = _select_mask(score, tq)
            for r in range(N_REP):
                qaT_ref[0, N_REP * g + r, :, qs] = jnp.concatenate(
                    [qTs[r], negm, _slope_rows(SLOPES_NSA[g, r], tq)], axis=0)


def _cmp(qT, cmp_k, cmp_vT, b, seq, tq=256, nsub=4):
    ts = tq * nsub
    nq = seq // ts
    n_cmp = (seq - CMP_BLOCK) // CMP_STRIDE + 1
    n_pad = cmp_k.shape[2]
    return pl.pallas_call(
        functools.partial(_cmp_kernel, tq=tq, nsub=nsub, n_cmp=n_cmp),
        grid=(b, nq),
        in_specs=[
            pl.BlockSpec((1, QT_ROWS, ts), lambda bi, i: (bi, 0, i)),
            pl.BlockSpec((1, 1, n_pad, 2 * LANE), lambda bi, i: (0, bi, 0, 0)),
            pl.BlockSpec((1, 1, 2 * LANE, n_pad), lambda bi, i: (1, bi, 0, 0)),
        ],
        out_specs=[
            pl.BlockSpec((1, QT_ROWS, ts), lambda bi, i: (bi, 0, i)),
            pl.BlockSpec((1, N_QH, LANE, ts), lambda bi, i: (bi, 0, 0, i)),
        ],
        out_shape=[
            jax.ShapeDtypeStruct((b, QT_ROWS, seq), F32),
            jax.ShapeDtypeStruct((b, N_QH, LANE, seq), BF16),
        ],
        scratch_shapes=[pltpu.VMEM((nsub * N_QH, n_pad, tq), F32)],
        compiler_params=pltpu.CompilerParams(
            dimension_semantics=("arbitrary", "arbitrary"), vmem_limit_bytes=VMEM_LIMIT),
        name="cmp",
    )(qT, cmp_k, cmp_vT)


SLC_AHEAD = 2
SLC_BUFS = SLC_AHEAD + 1

def _score_phase(qT_ref, k_tiles, mask, s_buf, mt_buf):
    for g in range(N_GROUPS):
        for r in range(N_REP):
            h = N_REP * g + r
            s = _dot(k_tiles[g], qT_ref[0, h])
            if mask is not None:
                s = s + mask
            s_buf[h] = s
            mt_buf[h] = jnp.max(s, axis=0, keepdims=True)


def _value_phase(vT_tiles, s_buf, mt_buf, m_scr, acc_scr):
    for g in range(N_GROUPS):
        for r in range(N_REP):
            h = N_REP * g + r
            m_old = m_scr[h]
            m_new = jnp.maximum(m_old, mt_buf[h])
            alpha = jnp.exp2(m_old - m_new)
            pT = jnp.exp2(s_buf[h] - m_new).astype(BF16)
            acc_scr[h] = alpha * acc_scr[h] + _dot(vT_tiles[g], pT)
            m_scr[h] = m_new


def _sweep_reset(m_scr, acc_scr):
    m_scr[...] = jnp.full(m_scr.shape, -3e38, F32)
    acc_scr[...] = jnp.zeros(acc_scr.shape, F32)


def _heads_out(o_ref, outs, rows=slice(None)):
    for g in range(N_GROUPS):
        oT = jnp.concatenate(outs[N_REP * g:N_REP * (g + 1)], axis=0)
        o_ref[rows, g * 2 * LANE:(g + 1) * 2 * LANE] = oT.T.astype(o_ref.dtype)


def _slcwin_kernel(qT_ref, ks0, ks1, kw0, kw1, vsT_ref, vwT_ref, gT_ref, ocT_ref,
                   o_ref, caus, s_scr, mt_scr, m_s, acc_s, m_w, acc_w, *, tq, nq):
    tk = TK
    bi = pl.program_id(0)
    i = pl.program_id(1)
    assert NSA_WINDOW // tk == 2 and tq == tk

    @pl.when((bi == 0) & (i == 0))
    def _init():
        kk = lax.broadcasted_iota(jnp.int32, (tk, tq), 0)
        qq = lax.broadcasted_iota(jnp.int32, (tk, tq), 1)
        caus[0] = jnp.where(kk <= qq, 0.0, NEG)
        caus[1] = jnp.where(kk > qq, 0.0, NEG)

    ks = (ks0, ks1)
    kw = (kw0, kw1)

    def run(ii):
        _sweep_reset(m_s, acc_s)
        _sweep_reset(m_w, acc_w)
        stream = []
        for d in (2, 1, 0):
            if ii - d >= 0:
                stream.append((kw, vwT_ref, ii - d, {0: 0, 2: 1}.get(d), m_w, acc_w))
        for j in range(ii + 1):
            stream.append((ks, vsT_ref, j, 0 if j == ii else None, m_s, acc_s))

        def score(p):
            refs, _, idx, mask_id, _, _ = stream[p]
            buf = p % SLC_BUFS
            _score_phase(qT_ref, [refs[g][idx * tk:(idx + 1) * tk, :] for g in range(N_GROUPS)],
                         None if mask_id is None else caus[mask_id], s_scr.at[buf], mt_scr.at[buf])

        def value(p):
            _, vref, idx, _, m_scr, acc_scr = stream[p]
            buf = p % SLC_BUFS
            _value_phase([vref[0, g, idx] for g in range(N_GROUPS)], s_scr.at[buf], mt_scr.at[buf], m_scr, acc_scr)

        for p in range(len(stream) + SLC_AHEAD):
            if p < len(stream):
                score(p)
            if p >= SLC_AHEAD:
                value(p - SLC_AHEAD)

        gT = gT_ref[0]
        outs = []
        for h in range(N_QH):
            gc = h * N_BRANCH
            f_slc = gT[gc + 1:gc + 2] * (1.0 / acc_s[h, HEAD_DIM:HEAD_DIM + 1, :])
            f_win = gT[gc + 2:gc + 3] * (1.0 / acc_w[h, HEAD_DIM:HEAD_DIM + 1, :])
            outs.append(gT[gc:gc + 1] * ocT_ref[0, h * HEAD_DIM:(h + 1) * HEAD_DIM, :]
                        + f_slc * acc_s[h, 0:HEAD_DIM, :]
                        + f_win * acc_w[h, 0:HEAD_DIM, :])
        _heads_out(o_ref, outs)

    for ii in range(nq):
        pl.when(i == ii)(functools.partial(run, ii))


def _slcwin(qaT, main, vT, gT, ocT, b, seq, tq=TK):
    nq = seq // tq
    n = b * seq

    def col(c):
        return pl.BlockSpec((seq, LANE), lambda bi, i, c=c: (bi, c))

    return pl.pallas_call(
        functools.partial(_slcwin_kernel, tq=tq, nq=nq),
        grid=(b, nq),
        in_specs=[
            pl.BlockSpec((1, N_QH, LANE, tq), lambda bi, i: (bi, 0, 0, i)),
            col(KS_COL), col(KS_COL + 1), col(KW_COL), col(KW_COL + 1),
            pl.BlockSpec((1, N_GROUPS, seq // TK, VT_ROWS, TK), lambda bi, i: (bi, 0, 0, 0, 0)),
            pl.BlockSpec((1, N_GROUPS, seq // TK, VT_ROWS, TK), lambda bi, i: (bi, 1, 0, 0, 0)),
            pl.BlockSpec((1, GT_ROWS, tq), lambda bi, i: (bi, 0, i)),
            pl.BlockSpec((1, QT_ROWS, tq), lambda bi, i: (bi, 0, i)),
        ],
        out_specs=pl.BlockSpec((tq, 4 * LANE), lambda bi, i: (bi * nq + i, 0)),
        out_shape=jax.ShapeDtypeStruct((n, 4 * LANE), BF16),
        scratch_shapes=[
            pltpu.VMEM((2, TK, tq), F32),
            pltpu.VMEM((SLC_BUFS, N_QH, TK, tq), F32),
            pltpu.VMEM((SLC_BUFS, N_QH, 1, tq), F32),
            pltpu.VMEM((N_QH, 1, tq), F32),
            pltpu.VMEM((N_QH, VT_ROWS, tq), F32),
            pltpu.VMEM((N_QH, 1, tq), F32),
            pltpu.VMEM((N_QH, VT_ROWS, tq), F32),
        ],
        compiler_params=pltpu.CompilerParams(
            dimension_semantics=("arbitrary", "arbitrary"), vmem_limit_bytes=VMEM_LIMIT),
        name="slcwin",
    )(qaT, main, main, main, main, vT, vT, gT, ocT)


SWA_AHEAD = 3
SWA_BUFS = SWA_AHEAD + 1

def _swa_kernel(sink_ref, qT_ref, k0, k1, vT_ref, o_ref, mfar, mmain, s_far, s_main, *, tq, nsub):
    bi = pl.program_id(0)
    i = pl.program_id(1)
    w = SWA_WINDOW
    assert tq == TK and w == LANE

    @pl.when((bi == 0) & (i == 0))
    def _init():
        kk = lax.broadcasted_iota(jnp.int32, (w, tq), 0)
        qq = lax.broadcasted_iota(jnp.int32, (w, tq), 1)
        mfar[0] = jnp.where(kk > qq, 0.0, NEG)
        mfar[1] = jnp.full((w, tq), NEG, F32)
        kk = lax.broadcasted_iota(jnp.int32, (tq, tq), 0)
        qq = lax.broadcasted_iota(jnp.int32, (tq, tq), 1)
        mmain[...] = jnp.where((kk <= qq) & (qq - kk < w), 0.0, NEG)

    kk_ref = (k0, k1)
    mask_main = mmain[...]
    mts, sinks, outs = {}, {}, {}

    def geometry(sub):
        tile = i * nsub + sub
        t0 = tile * tq
        return tile, t0

    def score(n):
        sub, h = divmod(n, N_QH)
        g, r = divmod(h, N_REP)
        tile, t0 = geometry(sub)
        far0 = pl.multiple_of(jnp.maximum(t0 - w, 0), w)
        main0 = pl.multiple_of(t0, tq)
        mask_far = mfar[jnp.where(tile == 0, 1, 0)]
        qaT = qT_ref[0, h, :, sub * tq:(sub + 1) * tq]
        sf = _dot(kk_ref[g][pl.ds(far0, w), :], qaT) + mask_far
        sm = _dot(kk_ref[g][pl.ds(main0, tq), :], qaT) + mask_main
        s_far[n % SWA_BUFS] = sf
        s_main[n % SWA_BUFS] = sm
        tpos = (t0 + lax.broadcasted_iota(jnp.int32, (1, tq), 1)).astype(F32)
        sink = (sink_ref[g, r] * LOG2E) + float(SLOPES_SWA[g, r] * LOG2E) * tpos
        mts[n] = jnp.maximum(jnp.maximum(jnp.max(sf, axis=0, keepdims=True),
                                         jnp.max(sm, axis=0, keepdims=True)), sink)
        sinks[n] = sink

    def value(n):
        sub, h = divmod(n, N_QH)
        g = h // N_REP
        tile, _ = geometry(sub)
        vT_far = vT_ref[0, g, jnp.maximum(tile - 1, 0)][:, w:2 * w]
        vT_main = vT_ref[0, g, tile]
        m = mts.pop(n)
        pf = jnp.exp2(s_far[n % SWA_BUFS] - m).astype(BF16)
        pm = jnp.exp2(s_main[n % SWA_BUFS] - m).astype(BF16)
        acc = _dot(vT_far, pf) + _dot(vT_main, pm)
        den = acc[HEAD_DIM:HEAD_DIM + 1, :] + jnp.exp2(sinks.pop(n) - m)
        outs[n] = acc[0:HEAD_DIM, :] * (1.0 / den)
        if h == N_QH - 1:
            _heads_out(o_ref, [outs.pop(sub * N_QH + hh) for hh in range(N_QH)], rows=slice(sub * tq, (sub + 1) * tq))

    n_items = nsub * N_QH
    for n in range(n_items + SWA_AHEAD):
        if n < n_items:
            score(n)
        if n >= SWA_AHEAD:
            value(n - SWA_AHEAD)


def _swa(sinks, qsT, main, vT, b, seq, tq=TK, nsub=4):
    ts = tq * nsub
    nq = seq // ts
    n = b * seq

    def col(c):
        return pl.BlockSpec((seq, LANE), lambda bi, i, c=c: (bi, c))

    return pl.pallas_call(
        functools.partial(_swa_kernel, tq=tq, nsub=nsub),
        grid=(b, nq),
        in_specs=[
            pl.BlockSpec(memory_space=pltpu.SMEM),
            pl.BlockSpec((1, N_QH, LANE, ts), lambda bi, i: (bi, 0, 0, i)),
            col(KSW_COL), col(KSW_COL + 1),
            pl.BlockSpec((1, N_GROUPS, seq // TK, VT_ROWS, TK), lambda bi, i: (bi, 2, 0, 0, 0)),
        ],
        out_specs=pl.BlockSpec((ts, 4 * LANE), lambda bi, i: (bi * nq + i, 0)),
        out_shape=jax.ShapeDtypeStruct((n, 4 * LANE), BF16),
        scratch_shapes=[
            pltpu.VMEM((2, SWA_WINDOW, tq), F32),
            pltpu.VMEM((tq, tq), F32),
            pltpu.VMEM((SWA_BUFS, SWA_WINDOW, tq), F32),
            pltpu.VMEM((SWA_BUFS, tq, tq), F32),
        ],
        compiler_params=pltpu.CompilerParams(
            dimension_semantics=("arbitrary", "arbitrary"), vmem_limit_bytes=VMEM_LIMIT),
        name="swa",
    )(sinks, qsT, main, main, vT)


def _outmlp_kernel(on_ref, os_ref, x_ref, wo_ref, g2_ref, g3_ref, wu_ref, wd_ref, g4_ref, o_ref, *, ff_chunk, n_sub):
    half = N_HEADS * HEAD_DIM // 2
    tm = x_ref.shape[0]
    subs = [slice(k * tm // n_sub, (k + 1) * tm // n_sub) for k in range(n_sub)]
    mixes = [_dot(on_ref[rs, :], wo_ref[0:half, :]) + _dot(os_ref[rs, :], wo_ref[half:2 * half, :]) for rs in subs]
    for rs, mix in zip(subs, mixes):
        h1 = x_ref[rs, :] + _rms(mix, g2_ref[...])
        m = _rms(h1, g3_ref[...]).astype(BF16)
        acc = jnp.zeros(h1.shape, F32)
        for c in range(D_FF // ff_chunk):
            u = _dot(m, wu_ref[:, c * ff_chunk:(c + 1) * ff_chunk])
            u = jnp.square(jnp.maximum(u, 0.0)).astype(BF16)
            acc = acc + _dot(u, wd_ref[c * ff_chunk:(c + 1) * ff_chunk, :])
        o_ref[rs, :] = h1 + _rms(acc, g4_ref[...])


def _outmlp(onsa, oswa, x2, wo, g2, g3, wu, wd, g4, tm=1024, ff_chunk=1024, n_sub=4):
    n = x2.shape[0]

    def const(shape):
        return pl.BlockSpec(shape, lambda i: (0, 0), pipeline_mode=pl.Buffered(1))

    return pl.pallas_call(
        functools.partial(_outmlp_kernel, ff_chunk=ff_chunk, n_sub=n_sub),
        grid=(n // tm,),
        in_specs=[
            pl.BlockSpec((tm, 4 * LANE), lambda i: (i, 0)),
            pl.BlockSpec((tm, 4 * LANE), lambda i: (i, 0)),
            pl.BlockSpec((tm, D_MODEL), lambda i: (i, 0)),
            const((D_MODEL, D_MODEL)),
            const((1, D_MODEL)),
            const((1, D_MODEL)),
            const((D_MODEL, D_FF)),
            const((D_FF, D_MODEL)),
            const((1, D_MODEL)),
        ],
        out_specs=pl.BlockSpec((tm, D_MODEL), lambda i: (i, 0)),
        out_shape=jax.ShapeDtypeStruct((n, D_MODEL), F32),
        compiler_params=pltpu.CompilerParams(
            dimension_semantics=("arbitrary",), vmem_limit_bytes=VMEM_LIMIT),
        name="outmlp",
    )(onsa, oswa, x2, wo, g2, g3, wu, wd, g4)


def _layout_w_in(w):
    sizes = [512, 128, 128, 128, 128, 128, 128, N_HEADS // 2 * N_BRANCH, 512, 128, 128]
    offs = np.concatenate([[0], np.cumsum(sizes)])
    q_n, kc, vc, ks, vs, kw, vw, gt, q_s, k_s, v_s = [w[:, offs[k]:offs[k + 1]] for k in range(len(sizes))]
    std = [ks, kw, k_s, kc, vc]
    gt = jnp.concatenate([gt, jnp.zeros((w.shape[0], GT_ROWS - gt.shape[1]), w.dtype)], axis=1)
    tr = jnp.concatenate([q_n, q_s, vs, vw, v_s, gt], axis=1).T
    return jnp.concatenate(std, axis=1).astype(BF16), tr.astype(BF16)


def _layout_w1(w1):
    w = w1.reshape(2, CMP_STRIDE, HEAD_DIM, CMP_HIDDEN)
    z = jnp.zeros_like(w)
    top = jnp.concatenate([w, z], axis=-1)
    bot = jnp.concatenate([z, w], axis=-1)
    e = jnp.stack([top, bot], axis=2)
    return e.reshape(2, CMP_STRIDE * 2 * HEAD_DIM, 2 * CMP_HIDDEN).astype(BF16)


def _layout_w2(w2):
    z = jnp.zeros((CMP_HIDDEN, HEAD_DIM), w2.dtype)
    top = jnp.concatenate([w2, z, z, z], axis=1)
    bot = jnp.concatenate([z, z, w2, z], axis=1)
    return jnp.concatenate([top, bot], axis=0).astype(BF16)


def _layout_pe(pe):
    p = pe.reshape(2, CMP_STRIDE, 1, HEAD_DIM)
    return jnp.broadcast_to(p, (2, CMP_STRIDE, 2, HEAD_DIM)).reshape(2, 1, CMP_STRIDE * 2 * HEAD_DIM)


def kernel(x, norm_mix_pre, w_in, cmp_pe_k, cmp_w1_k, cmp_w2_k, cmp_pe_v, cmp_w1_v, cmp_w2_v,
           sinks, w_out, norm_mix_post, norm_mlp_pre, w_up, w_down, norm_mlp_post):
    b, seq, _ = x.shape
    assert seq // SEL_BLOCK == N_SEL and seq <= POS_SPLIT * 256
    depth = w_in.shape[0]
    h = x.reshape(b * seq, D_MODEL)
    for li in range(depth):
        w_p, wT_p = _layout_w_in(w_in[li])
        main, z, qT, qsT, vT, gT = _inproj(h, norm_mix_pre[li][None], w_p, wT_p, b, seq)
        pe2 = jnp.stack([_layout_pe(cmp_pe_k[li]), _layout_pe(cmp_pe_v[li])])
        w1e = jnp.stack([_layout_w1(cmp_w1_k[li]), _layout_w1(cmp_w1_v[li])])
        w2e = jnp.stack([_layout_w2(cmp_w2_k[li]), _layout_w2(cmp_w2_v[li])])
        cmp_kv, cmp_kvT = _compress(z, pe2, w1e, w2e, jnp.swapaxes(w2e, 1, 2))
        ocT, qaT = _cmp(qT, cmp_kv, cmp_kvT, b, seq)
        onsa = _slcwin(qaT, main, vT, gT, ocT, b, seq)
        oswa = _swa(sinks[li], qsT, main, vT, b, seq)
        h = _outmlp(onsa, oswa, h, w_out[li].astype(BF16), norm_mix_post[li][None], norm_mlp_pre[li][None],
                    w_up[li].astype(BF16), w_down[li].astype(BF16), norm_mlp_post[li][None])
    return h.reshape(b, seq, D_MODEL)
```

```python
import functools
import math

import jax
import jax.numpy as jnp
import numpy as np
from jax import lax
from jax.experimental import pallas as pl
from jax.experimental.pallas import tpu as pltpu

F32 = jnp.float32
BF16 = jnp.bfloat16

D_MODEL = 1024
HEAD_DIM = 64
N_HEADS = 16
N_GROUPS = 2
N_REP = 4
N_QH = N_GROUPS * N_REP
CMP_BLOCK = 32
CMP_STRIDE = 16
CMP_HIDDEN = 4 * HEAD_DIM
SEL_BLOCK = 64
SEL_TOPN = 8
N_SEL = 32
NSA_WINDOW = 512
SWA_WINDOW = 128
D_FF = 4 * D_MODEL
NORM_EPS = 1e-6
N_BRANCH = 3

LANE = 128
LOG2E = math.log2(math.e)
NEG = -1e30
SEL_NEG = -(2.0 ** 100)
VMEM_LIMIT = 56 * 1024 * 1024

TK = 256
VT_ROWS = 80
SEL_ROW = HEAD_DIM
POS_ROW = HEAD_DIM + N_SEL
N_PIECE = 3
CMP_FEAT = 16
POS_SPLIT = 256

KS_COL, KW_COL, KSW_COL = 0, 2, 4
MAIN_COLS = 6 * LANE
STD_COLS = 5 * LANE
QT_ROWS = N_QH * HEAD_DIM
VT_ROW0 = 2 * QT_ROWS
GT_ROW = VT_ROW0 + 3 * N_GROUPS * HEAD_DIM
GT_ROWS = 32
T_ROWS = GT_ROW + GT_ROWS


def _slopes():
    s = 2.0 ** (-8.0 * (np.arange(N_HEADS) + 1) / N_HEADS)
    nsa = s[0::2].reshape(N_GROUPS, N_REP)
    swa = s[1::2].reshape(N_GROUPS, N_REP)
    return nsa, swa


SLOPES_NSA, SLOPES_SWA = _slopes()


def _bf16_round(x):
    u = np.float32(x).reshape(1).view(np.uint32)
    u = (u + (((u >> 16) & 1) + 0x7FFF)) & np.uint32(0xFFFF0000)
    return float(u.view(np.float32)[0])


def _bf16_pieces(x, n=N_PIECE):
    out, rem = [], float(np.float32(x))
    for _ in range(n):
        p = _bf16_round(rem)
        out.append(p)
        rem = float(np.float32(rem - p))
    return out


_NT = (((1,), (1,)), ((), ()))


def _dot_nt(a, b, **kw):
    return lax.dot_general(a, b, _NT, preferred_element_type=F32, **kw)


def _dot(a, b, **kw):
    return jnp.dot(a, b, preferred_element_type=F32, **kw)


def _rms(v, g):
    return v * lax.rsqrt(jnp.mean(v * v, axis=-1, keepdims=True) + NORM_EPS) * g


def _slope_rows(slope, tq):
    pieces = _bf16_pieces(slope * LOG2E)
    vals = pieces + [p * POS_SPLIT for p in pieces]
    prow = lax.broadcasted_iota(jnp.int32, (LANE - POS_ROW, tq), 0)
    feat = jnp.zeros((LANE - POS_ROW, tq), F32)
    for k, v in enumerate(vals):
        feat = jnp.where(prow == k, v, feat)
    return feat.astype(BF16)


def _inproj_kernel(x_ref, g_ref, w_ref, wT_ref, main_ref, z_ref, qT_ref, qsT_ref, vT_ref, gT_ref,
                   kc_scr, vc_scr, *, tm, seq):
    qscale = LOG2E * HEAD_DIM ** -0.5
    lane = lax.broadcasted_iota(jnp.int32, (TK, LANE), 1)
    lo = lane < HEAD_DIM
    zmid = jnp.zeros((N_SEL, TK), BF16)
    ones_blk = jnp.where(lax.broadcasted_iota(jnp.int32, (VT_ROWS - HEAD_DIM, TK), 0) == 0, 1.0, 0.0).astype(BF16)
    t0 = (pl.program_id(0) * tm) % seq

    for st in range(tm // TK):
        rs = slice(st * TK, (st + 1) * TK)
        a = _rms(x_ref[rs, :], g_ref[...]).astype(BF16)

        res = _dot(a, w_ref[...])
        pos = t0 + st * TK + lax.broadcasted_iota(jnp.int32, (TK, LANE), 0)
        posf = jnp.where((lane >= POS_ROW) & (lane < POS_ROW + N_PIECE), (pos % POS_SPLIT).astype(F32), 0.0)
        posf = jnp.where((lane >= POS_ROW + N_PIECE) & (lane < POS_ROW + 2 * N_PIECE),
                         (pos // POS_SPLIT).astype(F32), posf)
        onehot = jnp.where(lane == pos // SEL_BLOCK + SEL_ROW, 1.0, 0.0)
        for t in range(3):
            blk = res[:, t * LANE:(t + 1) * LANE]
            extra = posf + onehot if 2 * t == KS_COL else posf
            for g, src in enumerate((blk, pltpu.roll(blk, HEAD_DIM, 1))):
                c = 2 * t + g
                main_ref[rs, c * LANE:(c + 1) * LANE] = (jnp.where(lo, src, 0.0) + extra).astype(BF16)
        kc_scr[...] = res[:, 3 * LANE:4 * LANE]
        vc_scr[...] = res[:, 4 * LANE:STD_COLS]
        zr = TK // CMP_STRIDE
        for c in range(CMP_STRIDE):
            for t, scr in enumerate((kc_scr, vc_scr)):
                z_ref[t, 0, st * zr:(st + 1) * zr, c * LANE:(c + 1) * LANE] = (
                    scr[pl.ds(c, zr, stride=CMP_STRIDE), :].astype(BF16))

        resT = _dot_nt(wT_ref[...], a)
        qT_ref[0, :, rs] = (resT[0:QT_ROWS] * qscale).astype(BF16)
        for h in range(N_QH):
            r0 = QT_ROWS + h * HEAD_DIM
            qsT_ref[0, h, :, rs] = jnp.concatenate(
                [(resT[r0:r0 + HEAD_DIM] * qscale).astype(BF16), zmid,
                 _slope_rows(SLOPES_SWA[h // N_REP, h % N_REP], TK)], axis=0)
        for tg in range(3 * N_GROUPS):
            r0 = VT_ROW0 + tg * HEAD_DIM
            vT_ref[0, tg, st, 0:HEAD_DIM, :] = resT[r0:r0 + HEAD_DIM, :].astype(BF16)
            vT_ref[0, tg, st, HEAD_DIM:VT_ROWS, :] = ones_blk
        gT_ref[0, :, rs] = jax.nn.sigmoid(resT[GT_ROW:T_ROWS])


def _inproj(x2, gain, w_p, wT_p, b, seq, tm=1024):
    n = x2.shape[0]
    nt = seq // tm
    return pl.pallas_call(
        functools.partial(_inproj_kernel, tm=tm, seq=seq),
        grid=(n // tm,),
        in_specs=[
            pl.BlockSpec((tm, D_MODEL), lambda i: (i, 0)),
            pl.BlockSpec((1, D_MODEL), lambda i: (0, 0)),
            pl.BlockSpec((D_MODEL, STD_COLS), lambda i: (0, 0)),
            pl.BlockSpec((T_ROWS, D_MODEL), lambda i: (0, 0)),
        ],
        out_specs=[
            pl.BlockSpec((tm, MAIN_COLS), lambda i: (i, 0)),
            pl.BlockSpec((2, 1, tm // CMP_STRIDE, CMP_STRIDE * LANE), lambda i: (0, i // nt, i % nt, 0)),
            pl.BlockSpec((1, QT_ROWS, tm), lambda i: (i // nt, 0, i % nt)),
            pl.BlockSpec((1, N_QH, LANE, tm), lambda i: (i // nt, 0, 0, i % nt)),
            pl.BlockSpec((1, 3 * N_GROUPS, tm // TK, VT_ROWS, TK), lambda i: (i // nt, 0, i % nt, 0, 0)),
            pl.BlockSpec((1, GT_ROWS, tm), lambda i: (i // nt, 0, i % nt)),
        ],
        out_shape=[
            jax.ShapeDtypeStruct((n, MAIN_COLS), BF16),
            jax.ShapeDtypeStruct((2, b, seq // CMP_STRIDE, CMP_STRIDE * LANE), BF16),
            jax.ShapeDtypeStruct((b, QT_ROWS, seq), BF16),
            jax.ShapeDtypeStruct((b, N_QH, LANE, seq), BF16),
            jax.ShapeDtypeStruct((b, 3 * N_GROUPS, seq // TK, VT_ROWS, TK), BF16),
            jax.ShapeDtypeStruct((b, GT_ROWS, seq), F32),
        ],
        scratch_shapes=[pltpu.VMEM((TK, LANE), F32), pltpu.VMEM((TK, LANE), F32)],
        compiler_params=pltpu.CompilerParams(
            dimension_semantics=("arbitrary",), vmem_limit_bytes=VMEM_LIMIT),
        name="inproj",
    )(x2, gain, w_p, wT_p)


def _compress_kernel(z_ref, pe_ref, w1_ref, w2_ref, w2T_ref, o_ref, oT_ref):
    nb, n, k = z_ref.shape[1:]
    z = z_ref[0].reshape(nb * n, k).astype(F32)
    zt = (z + pe_ref[0, 0]).astype(BF16)
    zb = (z + pe_ref[0, 1]).astype(BF16)
    a = _dot(zt, w1_ref[0, 0])
    bm = _dot(zb, w1_ref[0, 1])
    h = a + pltpu.roll(bm, nb * n - 1, 0)
    hg = jax.nn.gelu(h).astype(BF16)
    o = _dot(hg, w2_ref[0])
    lane = lax.broadcasted_iota(jnp.int32, o.shape, 1) % LANE
    nidx = lax.broadcasted_iota(jnp.int32, o.shape, 0) % n
    o = o + jnp.where((lane >= HEAD_DIM) & (lane < HEAD_DIM + N_PIECE), nidx.astype(F32), 0.0)
    for e in range(nb):
        o_ref[0, e] = o[e * n:(e + 1) * n].astype(o_ref.dtype)
        oT_ref[0, e] = _dot_nt(w2T_ref[0], hg[e * n:(e + 1) * n]).astype(oT_ref.dtype)


def _compress(z, pe2, w1e, w2e, w2eT, nb=4):
    _, b, n, k = z.shape
    nb = math.gcd(b, nb)
    return pl.pallas_call(
        _compress_kernel,
        grid=(2, b // nb),
        in_specs=[
            pl.BlockSpec((1, nb, n, k), lambda t, i: (t, i, 0, 0)),
            pl.BlockSpec((1, 2, 1, k), lambda t, i: (t, 0, 0, 0)),
            pl.BlockSpec((1, 2, k, 2 * CMP_HIDDEN), lambda t, i: (t, 0, 0, 0)),
            pl.BlockSpec((1, 2 * CMP_HIDDEN, 2 * LANE), lambda t, i: (t, 0, 0)),
            pl.BlockSpec((1, 2 * LANE, 2 * CMP_HIDDEN), lambda t, i: (t, 0, 0)),
        ],
        out_specs=[
            pl.BlockSpec((1, nb, n, 2 * LANE), lambda t, i: (t, i, 0, 0)),
            pl.BlockSpec((1, nb, 2 * LANE, n), lambda t, i: (t, i, 0, 0)),
        ],
        out_shape=[
            jax.ShapeDtypeStruct((2, b, n, 2 * LANE), BF16),
            jax.ShapeDtypeStruct((2, b, 2 * LANE, n), BF16),
        ],
        compiler_params=pltpu.CompilerParams(
            dimension_semantics=("arbitrary", "arbitrary"), vmem_limit_bytes=VMEM_LIMIT),
        name="compress",
    )(z, pe2, w1e, w2e, w2eT)


def _f32_dot_exact_lhs(a_bf16, x):
    out = None
    rem = x
    for _ in range(N_PIECE):
        piece = rem.astype(BF16)
        rem = rem - piece.astype(F32)
        d = _dot(a_bf16, piece)
        out = d if out is None else out + d
    return out


def _select_mask(score, tq):
    sub = 8
    n_chunk = N_SEL // sub
    chunks = [score[sub * c:sub * (c + 1)] for c in range(n_chunk)]
    ranks = [jnp.zeros((sub, tq), jnp.int32) for _ in range(n_chunk)]
    jrow = lax.broadcasted_iota(jnp.int32, (sub, tq), 0)
    for k in range(N_SEL):
        sk = score[k:k + 1, :]
        for c in range(n_chunk):
            if k < sub * c:
                ahead = sk >= chunks[c]
            elif k >= sub * (c + 1):
                ahead = sk > chunks[c]
            else:
                ahead = (sk > chunks[c]) | ((sk == chunks[c]) & (jrow + sub * c > k))
            ranks[c] = jnp.where(ahead, ranks[c] + 1, ranks[c])
    rank = jnp.concatenate(ranks, axis=0)
    return jnp.where(rank < SEL_TOPN, 0.0, SEL_NEG).astype(BF16)


def _cmp_kernel(qT_ref, k_ref, vT_ref, ocT_ref, qaT_ref, s_scr, *, tq, nsub, n_cmp):
    i = pl.program_id(1)
    n_pad = k_ref.shape[2]
    nn = lax.broadcasted_iota(jnp.int32, (n_pad, tq), 0)
    frow = lax.broadcasted_iota(jnp.int32, (CMP_FEAT, tq), 0)

    jr = lax.broadcasted_iota(jnp.int32, (N_SEL, n_pad), 0)
    nc = lax.broadcasted_iota(jnp.int32, (N_SEL, n_pad), 1)
    ov = ((nc * CMP_STRIDE < jr * SEL_BLOCK + SEL_BLOCK) & (nc * CMP_STRIDE + CMP_BLOCK > jr * SEL_BLOCK)
          & (nc < n_cmp))
    ovT = jnp.where(ov, 1.0, 0.0).astype(BF16)
    jj = lax.broadcasted_iota(jnp.int32, (N_SEL, tq), 0)

    for sub in range(nsub):
        qs = slice(sub * tq, (sub + 1) * tq)
        for g in range(N_GROUPS):
            kc = k_ref[0, 0][:, g * LANE:g * LANE + HEAD_DIM + CMP_FEAT]
            for r in range(N_REP):
                h = N_REP * g + r
                feat = jnp.zeros((CMP_FEAT, tq), F32)
                for k, pc in enumerate(_bf16_pieces(SLOPES_NSA[g, r] * LOG2E)):
                    feat = jnp.where(frow == k, pc * CMP_STRIDE, feat)
                qTa = jnp.concatenate([qT_ref[0, h * HEAD_DIM:(h + 1) * HEAD_DIM, qs], feat.astype(BF16)], axis=0)
                s_scr[sub * N_QH + h] = _dot(kc, qTa)

    for sub in range(nsub):
        qs = slice(sub * tq, (sub + 1) * tq)
        t0 = (i * nsub + sub) * tq
        tt = t0 + lax.broadcasted_iota(jnp.int32, (n_pad, tq), 1)
        maskadd = jnp.where((tt >= nn * CMP_STRIDE + CMP_BLOCK - 1) & (nn < n_cmp), 0.0, NEG)
        colvalid = jnp.where(t0 + lax.broadcasted_iota(jnp.int32, (1, tq), 1) >= CMP_BLOCK - 1, 1.0, 0.0)
        blk_t = (t0 + lax.broadcasted_iota(jnp.int32, (N_SEL, tq), 1)) // SEL_BLOCK
        valid = jj <= blk_t
        forced = (jj == 0) | (jj == blk_t) | (jj == blk_t - 1)

        for g in range(N_GROUPS):
            vcT = vT_ref[0, 0][g * LANE:g * LANE + HEAD_DIM, :]
            psum = jnp.zeros((n_pad, tq), F32)
            qTs = []
            for r in range(N_REP):
                h = N_REP * g + r
                qTs.append(qT_ref[0, h * HEAD_DIM:(h + 1) * HEAD_DIM, qs])
                sm = s_scr[sub * N_QH + h] + maskadd
                m = jnp.max(sm, axis=0, keepdims=True)
                e = jnp.exp2(sm - m)
                den = jnp.sum(e, axis=0, keepdims=True)
                p = e * (colvalid / den)
                ocT_ref[0, h * HEAD_DIM:(h + 1) * HEAD_DIM, qs] = _dot(vcT, p.astype(BF16))
                psum = psum + p

            imp = _f32_dot_exact_lhs(ovT, psum)
            score = jnp.where(valid, imp, -jnp.inf)
            score = jnp.where(forced & valid, jnp.inf, score)
            negm = _select_mask(score, tq)
            for r in range(N_REP):
                qaT_ref[0, N_REP * g + r, :, qs] = jnp.concatenate(
                    [qTs[r], negm, _slope_rows(SLOPES_NSA[g, r], tq)], axis=0)


def _cmp(qT, cmp_k, cmp_vT, b, seq, tq=256, nsub=4):
    ts = tq * nsub
    nq = seq // ts
    n_cmp = (seq - CMP_BLOCK) // CMP_STRIDE + 1
    n_pad = cmp_k.shape[2]
    return pl.pallas_call(
        functools.partial(_cmp_kernel, tq=tq, nsub=nsub, n_cmp=n_cmp),
        grid=(b, nq),
        in_specs=[
            pl.BlockSpec((1, QT_ROWS, ts), lambda bi, i: (bi, 0, i)),
            pl.BlockSpec((1, 1, n_pad, 2 * LANE), lambda bi, i: (0, bi, 0, 0)),
            pl.BlockSpec((1, 1, 2 * LANE, n_pad), lambda bi, i: (1, bi, 0, 0)),
        ],
        out_specs=[
            pl.BlockSpec((1, QT_ROWS, ts), lambda bi, i: (bi, 0, i)),
            pl.BlockSpec((1, N_QH, LANE, ts), lambda bi, i: (bi, 0, 0, i)),
        ],
        out_shape=[
            jax.ShapeDtypeStruct((b, QT_ROWS, seq), F32),
            jax.ShapeDtypeStruct((b, N_QH, LANE, seq), BF16),
        ],
        scratch_shapes=[pltpu.VMEM((nsub * N_QH, n_pad, tq), F32)],
        compiler_params=pltpu.CompilerParams(
            dimension_semantics=("arbitrary", "arbitrary"), vmem_limit_bytes=VMEM_LIMIT),
        name="cmp",
    )(qT, cmp_k, cmp_vT)


SLC_KC = TK
SLC_AHEAD = 2
SLC_BUFS = SLC_AHEAD + 1

def _score_phase(qT_ref, k_tiles, mask, s_buf, mt_buf):
    for g in range(N_GROUPS):
        for r in range(N_REP):
            h = N_REP * g + r
            qT = qT_ref[0, h]
            mt = None
            for c in range(TK // SLC_KC):
                ks = slice(c * SLC_KC, (c + 1) * SLC_KC)
                s = _dot(k_tiles[g][ks], qT)
                if mask is not None:
                    s = s + mask[ks]
                s_buf[h, ks] = s
                cm = jnp.max(s, axis=0, keepdims=True)
                mt = cm if mt is None else jnp.maximum(mt, cm)
            mt_buf[h] = mt


def _value_phase(vT_tiles, s_buf, mt_buf, m_scr, acc_scr):
    for g in range(N_GROUPS):
        for r in range(N_REP):
            h = N_REP * g + r
            m_old = m_scr[h]
            m_new = jnp.maximum(m_old, mt_buf[h])
            alpha = jnp.exp2(m_old - m_new)
            pT = jnp.exp2(s_buf[h] - m_new).astype(BF16)
            acc_scr[h] = alpha * acc_scr[h] + _dot(vT_tiles[g], pT)
            m_scr[h] = m_new


def _sweep_reset(m_scr, acc_scr):
    m_scr[...] = jnp.full(m_scr.shape, -3e38, F32)
    acc_scr[...] = jnp.zeros(acc_scr.shape, F32)


def _heads_out(o_ref, outs, rows=slice(None)):
    for g in range(N_GROUPS):
        oT = jnp.concatenate(outs[N_REP * g:N_REP * (g + 1)], axis=0)
        o_ref[rows, g * 2 * LANE:(g + 1) * 2 * LANE] = oT.T.astype(o_ref.dtype)


def _slcwin_kernel(qT_ref, ks0, ks1, kw0, kw1, vsT_ref, vwT_ref, gT_ref, ocT_ref,
                   o_ref, caus, s_scr, mt_scr, m_s, acc_s, m_w, acc_w, *, tq, nq):
    tk = TK
    bi = pl.program_id(0)
    i = pl.program_id(1)
    assert NSA_WINDOW // tk == 2 and tq == tk

    @pl.when((bi == 0) & (i == 0))
    def _init():
        kk = lax.broadcasted_iota(jnp.int32, (tk, tq), 0)
        qq = lax.broadcasted_iota(jnp.int32, (tk, tq), 1)
        caus[0] = jnp.where(kk <= qq, 0.0, NEG)
        caus[1] = jnp.where(kk > qq, 0.0, NEG)

    ks = (ks0, ks1)
    kw = (kw0, kw1)

    def run(ii):
        _sweep_reset(m_s, acc_s)
        _sweep_reset(m_w, acc_w)
        stream = []
        for d in (2, 1, 0):
            if ii - d >= 0:
                stream.append((kw, vwT_ref, ii - d, {0: 0, 2: 1}.get(d), m_w, acc_w))
        for j in range(ii + 1):
            stream.append((ks, vsT_ref, j, 0 if j == ii else None, m_s, acc_s))

        def score(p):
            refs, _, idx, mask_id, _, _ = stream[p]
            buf = p % SLC_BUFS
            _score_phase(qT_ref, [refs[g][idx * tk:(idx + 1) * tk, :] for g in range(N_GROUPS)],
                         None if mask_id is None else caus[mask_id], s_scr.at[buf], mt_scr.at[buf])

        def value(p):
            _, vref, idx, _, m_scr, acc_scr = stream[p]
            buf = p % SLC_BUFS
            _value_phase([vref[0, g, idx] for g in range(N_GROUPS)], s_scr.at[buf], mt_scr.at[buf], m_scr, acc_scr)

        for p in range(len(stream) + SLC_AHEAD):
            if p < len(stream):
                score(p)
            if p >= SLC_AHEAD:
                value(p - SLC_AHEAD)

        gT = gT_ref[0]
        outs = []
        for h in range(N_QH):
            gc = h * N_BRANCH
            f_slc = gT[gc + 1:gc + 2] * (1.0 / acc_s[h, HEAD_DIM:HEAD_DIM + 1, :])
            f_win = gT[gc + 2:gc + 3] * (1.0 / acc_w[h, HEAD_DIM:HEAD_DIM + 1, :])
            outs.append(gT[gc:gc + 1] * ocT_ref[0, h * HEAD_DIM:(h + 1) * HEAD_DIM, :]
                        + f_slc * acc_s[h, 0:HEAD_DIM, :]
                        + f_win * acc_w[h, 0:HEAD_DIM, :])
        _heads_out(o_ref, outs)

    for ii in range(nq):
        pl.when(i == ii)(functools.partial(run, ii))


def _slcwin(qaT, main, vT, gT, ocT, b, seq, tq=TK):
    nq = seq // tq
    n = b * seq

    def col(c):
        return pl.BlockSpec((seq, LANE), lambda bi, i, c=c: (bi, c))

    return pl.pallas_call(
        functools.partial(_slcwin_kernel, tq=tq, nq=nq),
        grid=(b, nq),
        in_specs=[
            pl.BlockSpec((1, N_QH, LANE, tq), lambda bi, i: (bi, 0, 0, i)),
            col(KS_COL), col(KS_COL + 1), col(KW_COL), col(KW_COL + 1),
            pl.BlockSpec((1, N_GROUPS, seq // TK, VT_ROWS, TK), lambda bi, i: (bi, 0, 0, 0, 0)),
            pl.BlockSpec((1, N_GROUPS, seq // TK, VT_ROWS, TK), lambda bi, i: (bi, 1, 0, 0, 0)),
            pl.BlockSpec((1, GT_ROWS, tq), lambda bi, i: (bi, 0, i)),
            pl.BlockSpec((1, QT_ROWS, tq), lambda bi, i: (bi, 0, i)),
        ],
        out_specs=pl.BlockSpec((tq, 4 * LANE), lambda bi, i: (bi * nq + i, 0)),
        out_shape=jax.ShapeDtypeStruct((n, 4 * LANE), BF16),
        scratch_shapes=[
            pltpu.VMEM((2, TK, tq), F32),
            pltpu.VMEM((SLC_BUFS, N_QH, TK, tq), F32),
            pltpu.VMEM((SLC_BUFS, N_QH, 1, tq), F32),
            pltpu.VMEM((N_QH, 1, tq), F32),
            pltpu.VMEM((N_QH, VT_ROWS, tq), F32),
            pltpu.VMEM((N_QH, 1, tq), F32),
            pltpu.VMEM((N_QH, VT_ROWS, tq), F32),
        ],
        compiler_params=pltpu.CompilerParams(
            dimension_semantics=("arbitrary", "arbitrary"), vmem_limit_bytes=VMEM_LIMIT),
        name="slcwin",
    )(qaT, main, main, main, main, vT, vT, gT, ocT)


SWA_AHEAD = 3
SWA_BUFS = SWA_AHEAD + 1

def _swa_kernel(sink_ref, qT_ref, k0, k1, vT_ref, o_ref, tri, mab, s_far, s_ab, s_c, *, tq, nsub):
    bi = pl.program_id(0)
    i = pl.program_id(1)
    w = SWA_WINDOW
    assert tq == 2 * w and w == LANE

    @pl.when((bi == 0) & (i == 0))
    def _init():
        kk = lax.broadcasted_iota(jnp.int32, (w, w), 0)
        qq = lax.broadcasted_iota(jnp.int32, (w, w), 1)
        causal = jnp.where(kk <= qq, 0.0, NEG)
        far = jnp.where(kk > qq, 0.0, NEG)
        tri[0] = far
        tri[1] = causal
        tri[2] = jnp.full((w, w), NEG, F32)
        mab[...] = jnp.concatenate([causal, far], axis=1)

    kk_ref = (k0, k1)
    mts, sinks, outs = {}, {}, {}

    def geometry(sub):
        tile = i * nsub + sub
        return tile, tile * tq

    def score(n):
        sub, h = divmod(n, N_QH)
        g, r = divmod(h, N_REP)
        tile, t0 = geometry(sub)
        far0 = pl.multiple_of(jnp.maximum(t0 - w, 0), w)
        main0 = pl.multiple_of(t0, tq)
        buf = n % SWA_BUFS
        qaT = qT_ref[0, h, :, sub * tq:(sub + 1) * tq]
        sf = _dot(kk_ref[g][pl.ds(far0, w), :], qaT[:, 0:w]) + tri[jnp.where(tile == 0, 2, 0)]
        sab = _dot(kk_ref[g][pl.ds(main0, w), :], qaT) + mab[...]
        sc = _dot(kk_ref[g][pl.ds(main0 + w, w), :], qaT[:, w:tq]) + tri[1]
        s_far[buf] = sf
        s_ab[buf] = sab
        s_c[buf] = sc
        tpos = (t0 + lax.broadcasted_iota(jnp.int32, (1, tq), 1)).astype(F32)
        sink = (sink_ref[g, r] * LOG2E) + float(SLOPES_SWA[g, r] * LOG2E) * tpos
        m_side = jnp.concatenate([jnp.max(sf, axis=0, keepdims=True), jnp.max(sc, axis=0, keepdims=True)], axis=1)
        mts[n] = jnp.maximum(jnp.maximum(jnp.max(sab, axis=0, keepdims=True), m_side), sink)
        sinks[n] = sink

    def value(n):
        sub, h = divmod(n, N_QH)
        g = h // N_REP
        tile, _ = geometry(sub)
        buf = n % SWA_BUFS
        vT_far = vT_ref[0, g, jnp.maximum(tile - 1, 0)][:, w:2 * w]
        vT_main = vT_ref[0, g, tile]
        m = mts.pop(n)
        pf = jnp.exp2(s_far[buf] - m[:, 0:w]).astype(BF16)
        pab = jnp.exp2(s_ab[buf] - m).astype(BF16)
        pc = jnp.exp2(s_c[buf] - m[:, w:tq]).astype(BF16)
        acc = _dot(vT_main[:, 0:w], pab) + jnp.concatenate([_dot(vT_far, pf), _dot(vT_main[:, w:tq], pc)], axis=1)
        den = acc[HEAD_DIM:HEAD_DIM + 1, :] + jnp.exp2(sinks.pop(n) - m)
        outs[n] = acc[0:HEAD_DIM, :] * (1.0 / den)
        if h == N_QH - 1:
            _heads_out(o_ref, [outs.pop(sub * N_QH + hh) for hh in range(N_QH)], rows=slice(sub * tq, (sub + 1) * tq))

    n_items = nsub * N_QH
    for n in range(n_items + SWA_AHEAD):
        if n < n_items:
            score(n)
        if n >= SWA_AHEAD:
            value(n - SWA_AHEAD)


def _swa(sinks, qsT, main, vT, b, seq, tq=TK, nsub=4):
    ts = tq * nsub
    nq = seq // ts
    n = b * seq

    def col(c):
        return pl.BlockSpec((seq, LANE), lambda bi, i, c=c: (bi, c))

    return pl.pallas_call(
        functools.partial(_swa_kernel, tq=tq, nsub=nsub),
        grid=(b, nq),
        in_specs=[
            pl.BlockSpec(memory_space=pltpu.SMEM),
            pl.BlockSpec((1, N_QH, LANE, ts), lambda bi, i: (bi, 0, 0, i)),
            col(KSW_COL), col(KSW_COL + 1),
            pl.BlockSpec((1, N_GROUPS, seq // TK, VT_ROWS, TK), lambda bi, i: (bi, 2, 0, 0, 0)),
        ],
        out_specs=pl.BlockSpec((ts, 4 * LANE), lambda bi, i: (bi * nq + i, 0)),
        out_shape=jax.ShapeDtypeStruct((n, 4 * LANE), BF16),
        scratch_shapes=[
            pltpu.VMEM((3, SWA_WINDOW, SWA_WINDOW), F32),
            pltpu.VMEM((SWA_WINDOW, tq), F32),
            pltpu.VMEM((SWA_BUFS, SWA_WINDOW, SWA_WINDOW), F32),
            pltpu.VMEM((SWA_BUFS, SWA_WINDOW, tq), F32),
            pltpu.VMEM((SWA_BUFS, SWA_WINDOW, SWA_WINDOW), F32),
        ],
        compiler_params=pltpu.CompilerParams(
            dimension_semantics=("arbitrary", "arbitrary"), vmem_limit_bytes=VMEM_LIMIT),
        name="swa",
    )(sinks, qsT, main, main, vT)


def _outmlp_kernel(on_ref, os_ref, x_ref, wo_ref, g2_ref, g3_ref, wu_ref, wd_ref, g4_ref, o_ref, *, ff_chunk, n_sub):
    half = N_HEADS * HEAD_DIM // 2
    tm = x_ref.shape[0]
    subs = [slice(k * tm // n_sub, (k + 1) * tm // n_sub) for k in range(n_sub)]
    mixes = [_dot(on_ref[rs, :], wo_ref[0:half, :]) + _dot(os_ref[rs, :], wo_ref[half:2 * half, :]) for rs in subs]
    for rs, mix in zip(subs, mixes):
        h1 = x_ref[rs, :] + _rms(mix, g2_ref[...])
        m = _rms(h1, g3_ref[...]).astype(BF16)
        acc = jnp.zeros(h1.shape, F32)
        for c in range(D_FF // ff_chunk):
            u = _dot(m, wu_ref[:, c * ff_chunk:(c + 1) * ff_chunk])
            u = jnp.square(jnp.maximum(u, 0.0)).astype(BF16)
            acc = acc + _dot(u, wd_ref[c * ff_chunk:(c + 1) * ff_chunk, :])
        o_ref[rs, :] = h1 + _rms(acc, g4_ref[...])


def _outmlp(onsa, oswa, x2, wo, g2, g3, wu, wd, g4, tm=1024, ff_chunk=1024, n_sub=4):
    n = x2.shape[0]

    def const(shape):
        return pl.BlockSpec(shape, lambda i: (0, 0), pipeline_mode=pl.Buffered(1))

    return pl.pallas_call(
        functools.partial(_outmlp_kernel, ff_chunk=ff_chunk, n_sub=n_sub),
        grid=(n // tm,),
        in_specs=[
            pl.BlockSpec((tm, 4 * LANE), lambda i: (i, 0)),
            pl.BlockSpec((tm, 4 * LANE), lambda i: (i, 0)),
            pl.BlockSpec((tm, D_MODEL), lambda i: (i, 0)),
            const((D_MODEL, D_MODEL)),
            const((1, D_MODEL)),
            const((1, D_MODEL)),
            const((D_MODEL, D_FF)),
            const((D_FF, D_MODEL)),
            const((1, D_MODEL)),
        ],
        out_specs=pl.BlockSpec((tm, D_MODEL), lambda i: (i, 0)),
        out_shape=jax.ShapeDtypeStruct((n, D_MODEL), F32),
        compiler_params=pltpu.CompilerParams(
            dimension_semantics=("arbitrary",), vmem_limit_bytes=VMEM_LIMIT),
        name="outmlp",
    )(onsa, oswa, x2, wo, g2, g3, wu, wd, g4)


def _layout_w_in(w):
    sizes = [512, 128, 128, 128, 128, 128, 128, N_HEADS // 2 * N_BRANCH, 512, 128, 128]
    offs = np.concatenate([[0], np.cumsum(sizes)])
    q_n, kc, vc, ks, vs, kw, vw, gt, q_s, k_s, v_s = [w[:, offs[k]:offs[k + 1]] for k in range(len(sizes))]
    std = [ks, kw, k_s, kc, vc]
    gt = jnp.concatenate([gt, jnp.zeros((w.shape[0], GT_ROWS - gt.shape[1]), w.dtype)], axis=1)
    tr = jnp.concatenate([q_n, q_s, vs, vw, v_s, gt], axis=1).T
    return jnp.concatenate(std, axis=1).astype(BF16), tr.astype(BF16)


def _layout_w1(w1):
    w = w1.reshape(2, CMP_STRIDE, HEAD_DIM, CMP_HIDDEN)
    z = jnp.zeros_like(w)
    top = jnp.concatenate([w, z], axis=-1)
    bot = jnp.concatenate([z, w], axis=-1)
    e = jnp.stack([top, bot], axis=2)
    return e.reshape(2, CMP_STRIDE * 2 * HEAD_DIM, 2 * CMP_HIDDEN).astype(BF16)


def _layout_w2(w2):
    z = jnp.zeros((CMP_HIDDEN, HEAD_DIM), w2.dtype)
    top = jnp.concatenate([w2, z, z, z], axis=1)
    bot = jnp.concatenate([z, z, w2, z], axis=1)
    return jnp.concatenate([top, bot], axis=0).astype(BF16)


def _layout_pe(pe):
    p = pe.reshape(2, CMP_STRIDE, 1, HEAD_DIM)
    return jnp.broadcast_to(p, (2, CMP_STRIDE, 2, HEAD_DIM)).reshape(2, 1, CMP_STRIDE * 2 * HEAD_DIM)


def kernel(x, norm_mix_pre, w_in, cmp_pe_k, cmp_w1_k, cmp_w2_k, cmp_pe_v, cmp_w1_v, cmp_w2_v,
           sinks, w_out, norm_mix_post, norm_mlp_pre, w_up, w_down, norm_mlp_post):
    b, seq, _ = x.shape
    assert seq // SEL_BLOCK == N_SEL and seq <= POS_SPLIT * 256
    depth = w_in.shape[0]
    h = x.reshape(b * seq, D_MODEL)
    for li in range(depth):
        w_p, wT_p = _layout_w_in(w_in[li])
        main, z, qT, qsT, vT, gT = _inproj(h, norm_mix_pre[li][None], w_p, wT_p, b, seq)
        pe2 = jnp.stack([_layout_pe(cmp_pe_k[li]), _layout_pe(cmp_pe_v[li])])
        w1e = jnp.stack([_layout_w1(cmp_w1_k[li]), _layout_w1(cmp_w1_v[li])])
        w2e = jnp.stack([_layout_w2(cmp_w2_k[li]), _layout_w2(cmp_w2_v[li])])
        cmp_kv, cmp_kvT = _compress(z, pe2, w1e, w2e, jnp.swapaxes(w2e, 1, 2))
        ocT, qaT = _cmp(qT, cmp_kv, cmp_kvT, b, seq)
        onsa = _slcwin(qaT, main, vT, gT, ocT, b, seq)
        oswa = _swa(sinks[li], qsT, main, vT, b, seq)
        h = _outmlp(onsa, oswa, h, w_out[li].astype(BF16), norm_mix_post[li][None], norm_mlp_pre[li][None],
                    w_up[li].astype(BF16), w_down[li].astype(BF16), norm_mlp_post[li][None])
    return h.reshape(b, seq, D_MODEL)
```

```python
import functools
import math

import jax
import jax.numpy as jnp
import numpy as np
from jax import lax
from jax.experimental import pallas as pl
from jax.experimental.pallas import tpu as pltpu

F32 = jnp.float32
BF16 = jnp.bfloat16

D_MODEL = 1024
HEAD_DIM = 64
N_HEADS = 16
N_GROUPS = 2
N_REP = 4
N_QH = N_GROUPS * N_REP
CMP_BLOCK = 32
CMP_STRIDE = 16
CMP_HIDDEN = 4 * HEAD_DIM
SEL_BLOCK = 64
SEL_TOPN = 8
N_SEL = 32
NSA_WINDOW = 512
SWA_WINDOW = 128
D_FF = 4 * D_MODEL
NORM_EPS = 1e-6
N_BRANCH = 3

LANE = 128
LOG2E = math.log2(math.e)
NEG = -1e30
SEL_NEG = -(2.0 ** 100)
VMEM_LIMIT = 56 * 1024 * 1024

TK = 256
VT_ROWS = 80
SEL_ROW = HEAD_DIM
POS_ROW = HEAD_DIM + N_SEL
N_PIECE = 3
CMP_FEAT = 16
POS_SPLIT = 256

KS_COL, KW_COL, KSW_COL = 0, 2, 4
MAIN_COLS = 6 * LANE
STD_COLS = 5 * LANE
QT_ROWS = N_QH * HEAD_DIM
VT_ROW0 = 2 * QT_ROWS
GT_ROW = VT_ROW0 + 3 * N_GROUPS * HEAD_DIM
GT_ROWS = 32
T_ROWS = GT_ROW + GT_ROWS


def _slopes():
    s = 2.0 ** (-8.0 * (np.arange(N_HEADS) + 1) / N_HEADS)
    nsa = s[0::2].reshape(N_GROUPS, N_REP)
    swa = s[1::2].reshape(N_GROUPS, N_REP)
    return nsa, swa


SLOPES_NSA, SLOPES_SWA = _slopes()


def _bf16_round(x):
    u = np.float32(x).reshape(1).view(np.uint32)
    u = (u + (((u >> 16) & 1) + 0x7FFF)) & np.uint32(0xFFFF0000)
    return float(u.view(np.float32)[0])


def _bf16_pieces(x, n=N_PIECE):
    out, rem = [], float(np.float32(x))
    for _ in range(n):
        p = _bf16_round(rem)
        out.append(p)
        rem = float(np.float32(rem - p))
    return out


_NT = (((1,), (1,)), ((), ()))


def _dot_nt(a, b, **kw):
    return lax.dot_general(a, b, _NT, preferred_element_type=F32, **kw)


def _dot(a, b, **kw):
    return jnp.dot(a, b, preferred_element_type=F32, **kw)


def _rms(v, g):
    return v * lax.rsqrt(jnp.mean(v * v, axis=-1, keepdims=True) + NORM_EPS) * g


def _slope_rows(slope, tq):
    pieces = _bf16_pieces(slope * LOG2E)
    vals = pieces + [p * POS_SPLIT for p in pieces]
    prow = lax.broadcasted_iota(jnp.int32, (LANE - POS_ROW, tq), 0)
    feat = jnp.zeros((LANE - POS_ROW, tq), F32)
    for k, v in enumerate(vals):
        feat = jnp.where(prow == k, v, feat)
    return feat.astype(BF16)


def _inproj_kernel(x_ref, g_ref, w_ref, wT_ref, main_ref, z_ref, qT_ref, qsT_ref, vT_ref, gT_ref,
                   kc_scr, vc_scr, *, tm, seq):
    qscale = LOG2E * HEAD_DIM ** -0.5
    lane = lax.broadcasted_iota(jnp.int32, (TK, LANE), 1)
    lo = lane < HEAD_DIM
    zmid = jnp.zeros((N_SEL, TK), BF16)
    ones_blk = jnp.where(lax.broadcasted_iota(jnp.int32, (VT_ROWS - HEAD_DIM, TK), 0) == 0, 1.0, 0.0).astype(BF16)
    t0 = (pl.program_id(0) * tm) % seq

    for st in range(tm // TK):
        rs = slice(st * TK, (st + 1) * TK)
        a = _rms(x_ref[rs, :], g_ref[...]).astype(BF16)

        res = _dot(a, w_ref[...])
        pos = t0 + st * TK + lax.broadcasted_iota(jnp.int32, (TK, LANE), 0)
        posf = jnp.where((lane >= POS_ROW) & (lane < POS_ROW + N_PIECE), (pos % POS_SPLIT).astype(F32), 0.0)
        posf = jnp.where((lane >= POS_ROW + N_PIECE) & (lane < POS_ROW + 2 * N_PIECE),
                         (pos // POS_SPLIT).astype(F32), posf)
        onehot = jnp.where(lane == pos // SEL_BLOCK + SEL_ROW, 1.0, 0.0)
        for t in range(3):
            blk = res[:, t * LANE:(t + 1) * LANE]
            extra = posf + onehot if 2 * t == KS_COL else posf
            for g, src in enumerate((blk, pltpu.roll(blk, HEAD_DIM, 1))):
                c = 2 * t + g
                main_ref[rs, c * LANE:(c + 1) * LANE] = (jnp.where(lo, src, 0.0) + extra).astype(BF16)
        kc_scr[...] = res[:, 3 * LANE:4 * LANE]
        vc_scr[...] = res[:, 4 * LANE:STD_COLS]
        zr = TK // CMP_STRIDE
        for c in range(CMP_STRIDE):
            for t, scr in enumerate((kc_scr, vc_scr)):
                z_ref[t, 0, st * zr:(st + 1) * zr, c * LANE:(c + 1) * LANE] = (
                    scr[pl.ds(c, zr, stride=CMP_STRIDE), :].astype(BF16))

        resT = _dot_nt(wT_ref[...], a)
        qT_ref[0, :, rs] = (resT[0:QT_ROWS] * qscale).astype(BF16)
        for h in range(N_QH):
            r0 = QT_ROWS + h * HEAD_DIM
            qsT_ref[0, h, :, rs] = jnp.concatenate(
                [(resT[r0:r0 + HEAD_DIM] * qscale).astype(BF16), zmid,
                 _slope_rows(SLOPES_SWA[h // N_REP, h % N_REP], TK)], axis=0)
        for tg in range(3 * N_GROUPS):
            r0 = VT_ROW0 + tg * HEAD_DIM
            vT_ref[0, tg, st, 0:HEAD_DIM, :] = resT[r0:r0 + HEAD_DIM, :].astype(BF16)
            vT_ref[0, tg, st, HEAD_DIM:VT_ROWS, :] = ones_blk
        gT_ref[0, :, rs] = jax.nn.sigmoid(resT[GT_ROW:T_ROWS])


def _inproj(x2, gain, w_p, wT_p, b, seq, tm=1024):
    n = x2.shape[0]
    nt = seq // tm
    return pl.pallas_call(
        functools.partial(_inproj_kernel, tm=tm, seq=seq),
        grid=(n // tm,),
        in_specs=[
            pl.BlockSpec((tm, D_MODEL), lambda i: (i, 0)),
            pl.BlockSpec((1, D_MODEL), lambda i: (0, 0)),
            pl.BlockSpec((D_MODEL, STD_COLS), lambda i: (0, 0)),
            pl.BlockSpec((T_ROWS, D_MODEL), lambda i: (0, 0)),
        ],
        out_specs=[
            pl.BlockSpec((tm, MAIN_COLS), lambda i: (i, 0)),
            pl.BlockSpec((2, 1, tm // CMP_STRIDE, CMP_STRIDE * LANE), lambda i: (0, i // nt, i % nt, 0)),
            pl.BlockSpec((1, QT_ROWS, tm), lambda i: (i // nt, 0, i % nt)),
            pl.BlockSpec((1, N_QH, LANE, tm), lambda i: (i // nt, 0, 0, i % nt)),
            pl.BlockSpec((1, 3 * N_GROUPS, tm // TK, VT_ROWS, TK), lambda i: (i // nt, 0, i % nt, 0, 0)),
            pl.BlockSpec((1, GT_ROWS, tm), lambda i: (i // nt, 0, i % nt)),
        ],
        out_shape=[
            jax.ShapeDtypeStruct((n, MAIN_COLS), BF16),
            jax.ShapeDtypeStruct((2, b, seq // CMP_STRIDE, CMP_STRIDE * LANE), BF16),
            jax.ShapeDtypeStruct((b, QT_ROWS, seq), BF16),
            jax.ShapeDtypeStruct((b, N_QH, LANE, seq), BF16),
            jax.ShapeDtypeStruct((b, 3 * N_GROUPS, seq // TK, VT_ROWS, TK), BF16),
            jax.ShapeDtypeStruct((b, GT_ROWS, seq), F32),
        ],
        scratch_shapes=[pltpu.VMEM((TK, LANE), F32), pltpu.VMEM((TK, LANE), F32)],
        compiler_params=pltpu.CompilerParams(
            dimension_semantics=("arbitrary",), vmem_limit_bytes=VMEM_LIMIT),
        name="inproj",
    )(x2, gain, w_p, wT_p)


def _compress_kernel(z_ref, pe_ref, w1_ref, w2_ref, w2T_ref, o_ref, oT_ref):
    nb, n, k = z_ref.shape[1:]
    z = z_ref[0].reshape(nb * n, k).astype(F32)
    zt = (z + pe_ref[0, 0]).astype(BF16)
    zb = (z + pe_ref[0, 1]).astype(BF16)
    a = _dot(zt, w1_ref[0, 0])
    bm = _dot(zb, w1_ref[0, 1])
    h = a + pltpu.roll(bm, nb * n - 1, 0)
    hg = jax.nn.gelu(h).astype(BF16)
    o = _dot(hg, w2_ref[0])
    lane = lax.broadcasted_iota(jnp.int32, o.shape, 1) % LANE
    nidx = lax.broadcasted_iota(jnp.int32, o.shape, 0) % n
    o = o + jnp.where((lane >= HEAD_DIM) & (lane < HEAD_DIM + N_PIECE), nidx.astype(F32), 0.0)
    for e in range(nb):
        o_ref[0, e] = o[e * n:(e + 1) * n].astype(o_ref.dtype)
        oT_ref[0, e] = _dot_nt(w2T_ref[0], hg[e * n:(e + 1) * n]).astype(oT_ref.dtype)


def _compress(z, pe2, w1e, w2e, w2eT, nb=4):
    _, b, n, k = z.shape
    nb = math.gcd(b, nb)
    return pl.pallas_call(
        _compress_kernel,
        grid=(2, b // nb),
        in_specs=[
            pl.BlockSpec((1, nb, n, k), lambda t, i: (t, i, 0, 0)),
            pl.BlockSpec((1, 2, 1, k), lambda t, i: (t, 0, 0, 0)),
            pl.BlockSpec((1, 2, k, 2 * CMP_HIDDEN), lambda t, i: (t, 0, 0, 0)),
            pl.BlockSpec((1, 2 * CMP_HIDDEN, 2 * LANE), lambda t, i: (t, 0, 0)),
            pl.BlockSpec((1, 2 * LANE, 2 * CMP_HIDDEN), lambda t, i: (t, 0, 0)),
        ],
        out_specs=[
            pl.BlockSpec((1, nb, n, 2 * LANE), lambda t, i: (t, i, 0, 0)),
            pl.BlockSpec((1, nb, 2 * LANE, n), lambda t, i: (t, i, 0, 0)),
        ],
        out_shape=[
            jax.ShapeDtypeStruct((2, b, n, 2 * LANE), BF16),
            jax.ShapeDtypeStruct((2, b, 2 * LANE, n), BF16),
        ],
        compiler_params=pltpu.CompilerParams(
            dimension_semantics=("arbitrary", "arbitrary"), vmem_limit_bytes=VMEM_LIMIT),
        name="compress",
    )(z, pe2, w1e, w2e, w2eT)


def _f32_dot_exact_lhs(a_bf16, x):
    out = None
    rem = x
    for _ in range(N_PIECE):
        piece = rem.astype(BF16)
        rem = rem - piece.astype(F32)
        d = _dot(a_bf16, piece)
        out = d if out is None else out + d
    return out


def _select_mask(score, tq):
    sub = 8
    n_chunk = N_SEL // sub
    chunks = [score[sub * c:sub * (c + 1)] for c in range(n_chunk)]
    ranks = [jnp.zeros((sub, tq), jnp.int32) for _ in range(n_chunk)]
    jrow = lax.broadcasted_iota(jnp.int32, (sub, tq), 0)
    for k in range(N_SEL):
        sk = score[k:k + 1, :]
        for c in range(n_chunk):
            if k < sub * c:
                ahead = sk >= chunks[c]
            elif k >= sub * (c + 1):
                ahead = sk > chunks[c]
            else:
                ahead = (sk > chunks[c]) | ((sk == chunks[c]) & (jrow + sub * c > k))
            ranks[c] = jnp.where(ahead, ranks[c] + 1, ranks[c])
    rank = jnp.concatenate(ranks, axis=0)
    return jnp.where(rank < SEL_TOPN, 0.0, SEL_NEG).astype(BF16)


def _cmp_kernel(qT_ref, k_ref, vT_ref, ocT_ref, qaT_ref, s_scr, *, tq, nsub, n_cmp):
    i = pl.program_id(1)
    n_pad = k_ref.shape[2]
    nn = lax.broadcasted_iota(jnp.int32, (n_pad, tq), 0)
    frow = lax.broadcasted_iota(jnp.int32, (CMP_FEAT, tq), 0)

    jr = lax.broadcasted_iota(jnp.int32, (N_SEL, n_pad), 0)
    nc = lax.broadcasted_iota(jnp.int32, (N_SEL, n_pad), 1)
    ov = ((nc * CMP_STRIDE < jr * SEL_BLOCK + SEL_BLOCK) & (nc * CMP_STRIDE + CMP_BLOCK > jr * SEL_BLOCK)
          & (nc < n_cmp))
    ovT = jnp.where(ov, 1.0, 0.0).astype(BF16)
    jj = lax.broadcasted_iota(jnp.int32, (N_SEL, tq), 0)

    for sub in range(nsub):
        qs = slice(sub * tq, (sub + 1) * tq)
        for g in range(N_GROUPS):
            kc = k_ref[0, 0][:, g * LANE:g * LANE + HEAD_DIM + CMP_FEAT]
            for r in range(N_REP):
                h = N_REP * g + r
                feat = jnp.zeros((CMP_FEAT, tq), F32)
                for k, pc in enumerate(_bf16_pieces(SLOPES_NSA[g, r] * LOG2E)):
                    feat = jnp.where(frow == k, pc * CMP_STRIDE, feat)
                qTa = jnp.concatenate([qT_ref[0, h * HEAD_DIM:(h + 1) * HEAD_DIM, qs], feat.astype(BF16)], axis=0)
                s_scr[sub * N_QH + h] = _dot(kc, qTa)

    for sub in range(nsub):
        qs = slice(sub * tq, (sub + 1) * tq)
        t0 = (i * nsub + sub) * tq
        tt = t0 + lax.broadcasted_iota(jnp.int32, (n_pad, tq), 1)
        maskadd = jnp.where((tt >= nn * CMP_STRIDE + CMP_BLOCK - 1) & (nn < n_cmp), 0.0, NEG)
        colvalid = jnp.where(t0 + lax.broadcasted_iota(jnp.int32, (1, tq), 1) >= CMP_BLOCK - 1, 1.0, 0.0)
        blk_t = (t0 + lax.broadcasted_iota(jnp.int32, (N_SEL, tq), 1)) // SEL_BLOCK
        valid = jj <= blk_t
        forced = (jj == 0) | (jj == blk_t) | (jj == blk_t - 1)

        for g in range(N_GROUPS):
            vcT = vT_ref[0, 0][g * LANE:g * LANE + HEAD_DIM, :]
            psum = jnp.zeros((n_pad, tq), F32)
            qTs = []
            for r in range(N_REP):
                h = N_REP * g + r
                qTs.append(qT_ref[0, h * HEAD_DIM:(h + 1) * HEAD_DIM, qs])
                sm = s_scr[sub * N_QH + h] + maskadd
                m = jnp.max(sm, axis=0, keepdims=True)
                e = jnp.exp2(sm - m)
                den = jnp.sum(e, axis=0, keepdims=True)
                p = e * (colvalid / den)
                ocT_ref[0, h * HEAD_DIM:(h + 1) * HEAD_DIM, qs] = _dot(vcT, p.astype(BF16))
                psum = psum + p

            imp = _f32_dot_exact_lhs(ovT, psum)
            score = jnp.where(valid, imp, -jnp.inf)
            score = jnp.where(forced & valid, jnp.inf, score)
            negm = _select_mask(score, tq)
            for r in range(N_REP):
                qaT_ref[0, N_REP * g + r, :, qs] = jnp.concatenate(
                    [qTs[r], negm, _slope_rows(SLOPES_NSA[g, r], tq)], axis=0)


def _cmp(qT, cmp_k, cmp_vT, b, seq, tq=256, nsub=4):
    ts = tq * nsub
    nq = seq // ts
    n_cmp = (seq - CMP_BLOCK) // CMP_STRIDE + 1
    n_pad = cmp_k.shape[2]
    return pl.pallas_call(
        functools.partial(_cmp_kernel, tq=tq, nsub=nsub, n_cmp=n_cmp),
        grid=(b, nq),
        in_specs=[
            pl.BlockSpec((1, QT_ROWS, ts), lambda bi, i: (bi, 0, i)),
            pl.BlockSpec((1, 1, n_pad, 2 * LANE), lambda bi, i: (0, bi, 0, 0)),
            pl.BlockSpec((1, 1, 2 * LANE, n_pad), lambda bi, i: (1, bi, 0, 0)),
        ],
        out_specs=[
            pl.BlockSpec((1, QT_ROWS, ts), lambda bi, i: (bi, 0, i)),
            pl.BlockSpec((1, N_QH, LANE, ts), lambda bi, i: (bi, 0, 0, i)),
        ],
        out_shape=[
            jax.ShapeDtypeStruct((b, QT_ROWS, seq), F32),
            jax.ShapeDtypeStruct((b, N_QH, LANE, seq), BF16),
        ],
        scratch_shapes=[pltpu.VMEM((nsub * N_QH, n_pad, tq), F32)],
        compiler_params=pltpu.CompilerParams(
            dimension_semantics=("arbitrary", "arbitrary"), vmem_limit_bytes=VMEM_LIMIT),
        name="cmp",
    )(qT, cmp_k, cmp_vT)


SLC_AHEAD = 2
SLC_BUFS = SLC_AHEAD + 1

HALF = TK // 2
_TRI = {0: (slice(0, HALF), slice(HALF, TK), slice(HALF, TK)),
        1: (slice(HALF, TK), slice(0, HALF), slice(0, HALF))}


def _score_phase(qT_ref, k_tiles, mask_id, caus, s_buf, mt_buf):
    for g in range(N_GROUPS):
        for r in range(N_REP):
            h = N_REP * g + r
            qT = qT_ref[0, h]
            if mask_id is None:
                s = _dot(k_tiles[g], qT)
                s_buf[h] = s
                mt_buf[h] = jnp.max(s, axis=0, keepdims=True)
                continue
            fr, hr, hl = _TRI[mask_id]
            s_full = _dot(k_tiles[g][fr], qT) + caus[mask_id, fr, :]
            s_half = _dot(k_tiles[g][hr], qT[:, hl]) + caus[mask_id, hr, hl]
            s_buf[h, fr, :] = s_full
            s_buf[h, hr, hl] = s_half
            m_half = jnp.max(s_half, axis=0, keepdims=True)
            filler = jnp.full((1, HALF), -3e38, F32)
            m_half = jnp.concatenate([filler, m_half] if hl.start else [m_half, filler], axis=1)
            mt_buf[h] = jnp.maximum(jnp.max(s_full, axis=0, keepdims=True), m_half)


def _value_phase(vT_tiles, mask_id, s_buf, mt_buf, m_scr, acc_scr):
    for g in range(N_GROUPS):
        for r in range(N_REP):
            h = N_REP * g + r
            m_old = m_scr[h]
            m_new = jnp.maximum(m_old, mt_buf[h])
            alpha = jnp.exp2(m_old - m_new)
            if mask_id is None:
                pT = jnp.exp2(s_buf[h] - m_new).astype(BF16)
                upd = _dot(vT_tiles[g], pT)
            else:
                fr, hr, hl = _TRI[mask_id]
                p_full = jnp.exp2(s_buf[h, fr, :] - m_new).astype(BF16)
                p_half = jnp.exp2(s_buf[h, hr, hl] - m_new[:, hl]).astype(BF16)
                u_half = _dot(vT_tiles[g][:, hr], p_half)
                zero = jnp.zeros_like(u_half)
                upd = _dot(vT_tiles[g][:, fr], p_full) + jnp.concatenate(
                    [zero, u_half] if hl.start else [u_half, zero], axis=1)
            acc_scr[h] = alpha * acc_scr[h] + upd
            m_scr[h] = m_new


def _sweep_reset(m_scr, acc_scr):
    m_scr[...] = jnp.full(m_scr.shape, -3e38, F32)
    acc_scr[...] = jnp.zeros(acc_scr.shape, F32)


def _heads_out(o_ref, outs, rows=slice(None)):
    for g in range(N_GROUPS):
        oT = jnp.concatenate(outs[N_REP * g:N_REP * (g + 1)], axis=0)
        o_ref[rows, g * 2 * LANE:(g + 1) * 2 * LANE] = oT.T.astype(o_ref.dtype)


def _slcwin_kernel(qT_ref, ks0, ks1, kw0, kw1, vsT_ref, vwT_ref, gT_ref, ocT_ref,
                   o_ref, caus, s_scr, mt_scr, m_s, acc_s, m_w, acc_w, *, tq, nq):
    tk = TK
    bi = pl.program_id(0)
    i = pl.program_id(1)
    assert NSA_WINDOW // tk == 2 and tq == tk

    @pl.when((bi == 0) & (i == 0))
    def _init():
        kk = lax.broadcasted_iota(jnp.int32, (tk, tq), 0)
        qq = lax.broadcasted_iota(jnp.int32, (tk, tq), 1)
        caus[0] = jnp.where(kk <= qq, 0.0, NEG)
        caus[1] = jnp.where(kk > qq, 0.0, NEG)

    ks = (ks0, ks1)
    kw = (kw0, kw1)

    def run(ii):
        _sweep_reset(m_s, acc_s)
        _sweep_reset(m_w, acc_w)
        stream = []
        for d in (2, 1, 0):
            if ii - d >= 0:
                stream.append((kw, vwT_ref, ii - d, {0: 0, 2: 1}.get(d), m_w, acc_w))
        for j in range(ii + 1):
            stream.append((ks, vsT_ref, j, 0 if j == ii else None, m_s, acc_s))

        def score(p):
            refs, _, idx, mask_id, _, _ = stream[p]
            buf = p % SLC_BUFS
            _score_phase(qT_ref, [refs[g][idx * tk:(idx + 1) * tk, :] for g in range(N_GROUPS)],
                         mask_id, caus, s_scr.at[buf], mt_scr.at[buf])

        def value(p):
            _, vref, idx, mask_id, m_scr, acc_scr = stream[p]
            buf = p % SLC_BUFS
            _value_phase([vref[0, g, idx] for g in range(N_GROUPS)], mask_id,
                         s_scr.at[buf], mt_scr.at[buf], m_scr, acc_scr)

        for p in range(len(stream) + SLC_AHEAD):
            if p < len(stream):
                score(p)
            if p >= SLC_AHEAD:
                value(p - SLC_AHEAD)

        gT = gT_ref[0]
        outs = []
        for h in range(N_QH):
            gc = h * N_BRANCH
            f_slc = gT[gc + 1:gc + 2] * (1.0 / acc_s[h, HEAD_DIM:HEAD_DIM + 1, :])
            f_win = gT[gc + 2:gc + 3] * (1.0 / acc_w[h, HEAD_DIM:HEAD_DIM + 1, :])
            outs.append(gT[gc:gc + 1] * ocT_ref[0, h * HEAD_DIM:(h + 1) * HEAD_DIM, :]
                        + f_slc * acc_s[h, 0:HEAD_DIM, :]
                        + f_win * acc_w[h, 0:HEAD_DIM, :])
        _heads_out(o_ref, outs)

    for ii in range(nq):
        pl.when(i == ii)(functools.partial(run, ii))


def _slcwin(qaT, main, vT, gT, ocT, b, seq, tq=TK):
    nq = seq // tq
    n = b * seq

    def col(c):
        return pl.BlockSpec((seq, LANE), lambda bi, i, c=c: (bi, c))

    return pl.pallas_call(
        functools.partial(_slcwin_kernel, tq=tq, nq=nq),
        grid=(b, nq),
        in_specs=[
            pl.BlockSpec((1, N_QH, LANE, tq), lambda bi, i: (bi, 0, 0, i)),
            col(KS_COL), col(KS_COL + 1), col(KW_COL), col(KW_COL + 1),
            pl.BlockSpec((1, N_GROUPS, seq // TK, VT_ROWS, TK), lambda bi, i: (bi, 0, 0, 0, 0)),
            pl.BlockSpec((1, N_GROUPS, seq // TK, VT_ROWS, TK), lambda bi, i: (bi, 1, 0, 0, 0)),
            pl.BlockSpec((1, GT_ROWS, tq), lambda bi, i: (bi, 0, i)),
            pl.BlockSpec((1, QT_ROWS, tq), lambda bi, i: (bi, 0, i)),
        ],
        out_specs=pl.BlockSpec((tq, 4 * LANE), lambda bi, i: (bi * nq + i, 0)),
        out_shape=jax.ShapeDtypeStruct((n, 4 * LANE), BF16),
        scratch_shapes=[
            pltpu.VMEM((2, TK, tq), F32),
            pltpu.VMEM((SLC_BUFS, N_QH, TK, tq), F32),
            pltpu.VMEM((SLC_BUFS, N_QH, 1, tq), F32),
            pltpu.VMEM((N_QH, 1, tq), F32),
            pltpu.VMEM((N_QH, VT_ROWS, tq), F32),
            pltpu.VMEM((N_QH, 1, tq), F32),
            pltpu.VMEM((N_QH, VT_ROWS, tq), F32),
        ],
        compiler_params=pltpu.CompilerParams(
            dimension_semantics=("arbitrary", "arbitrary"), vmem_limit_bytes=VMEM_LIMIT),
        name="slcwin",
    )(qaT, main, main, main, main, vT, vT, gT, ocT)


SWA_AHEAD = 3
SWA_BUFS = SWA_AHEAD + 1

def _swa_kernel(sink_ref, qT_ref, k0, k1, vT_ref, o_ref, tri, mab, s_far, s_ab, s_c, *, tq, nsub):
    bi = pl.program_id(0)
    i = pl.program_id(1)
    w = SWA_WINDOW
    assert tq == 2 * w and w == LANE

    @pl.when((bi == 0) & (i == 0))
    def _init():
        kk = lax.broadcasted_iota(jnp.int32, (w, w), 0)
        qq = lax.broadcasted_iota(jnp.int32, (w, w), 1)
        causal = jnp.where(kk <= qq, 0.0, NEG)
        far = jnp.where(kk > qq, 0.0, NEG)
        tri[0] = far
        tri[1] = causal
        tri[2] = jnp.full((w, w), NEG, F32)
        mab[...] = jnp.concatenate([causal, far], axis=1)

    kk_ref = (k0, k1)
    mts, sinks, outs = {}, {}, {}

    def geometry(sub):
        tile = i * nsub + sub
        return tile, tile * tq

    def score(n):
        sub, h = divmod(n, N_QH)
        g, r = divmod(h, N_REP)
        tile, t0 = geometry(sub)
        far0 = pl.multiple_of(jnp.maximum(t0 - w, 0), w)
        main0 = pl.multiple_of(t0, tq)
        buf = n % SWA_BUFS
        qaT = qT_ref[0, h, :, sub * tq:(sub + 1) * tq]
        sf = _dot(kk_ref[g][pl.ds(far0, w), :], qaT[:, 0:w]) + tri[jnp.where(tile == 0, 2, 0)]
        sab = _dot(kk_ref[g][pl.ds(main0, w), :], qaT) + mab[...]
        sc = _dot(kk_ref[g][pl.ds(main0 + w, w), :], qaT[:, w:tq]) + tri[1]
        s_far[buf] = sf
        s_ab[buf] = sab
        s_c[buf] = sc
        tpos = (t0 + lax.broadcasted_iota(jnp.int32, (1, tq), 1)).astype(F32)
        sink = (sink_ref[g, r] * LOG2E) + float(SLOPES_SWA[g, r] * LOG2E) * tpos
        m_side = jnp.concatenate([jnp.max(sf, axis=0, keepdims=True), jnp.max(sc, axis=0, keepdims=True)], axis=1)
        mts[n] = jnp.maximum(jnp.maximum(jnp.max(sab, axis=0, keepdims=True), m_side), sink)
        sinks[n] = sink

    def value(n):
        sub, h = divmod(n, N_QH)
        g = h // N_REP
        tile, _ = geometry(sub)
        buf = n % SWA_BUFS
        vT_far = vT_ref[0, g, jnp.maximum(tile - 1, 0)][:, w:2 * w]
        vT_main = vT_ref[0, g, tile]
        m = mts.pop(n)
        pf = jnp.exp2(s_far[buf] - m[:, 0:w]).astype(BF16)
        pab = jnp.exp2(s_ab[buf] - m).astype(BF16)
        pc = jnp.exp2(s_c[buf] - m[:, w:tq]).astype(BF16)
        acc = _dot(vT_main[:, 0:w], pab) + jnp.concatenate([_dot(vT_far, pf), _dot(vT_main[:, w:tq], pc)], axis=1)
        den = acc[HEAD_DIM:HEAD_DIM + 1, :] + jnp.exp2(sinks.pop(n) - m)
        outs[n] = acc[0:HEAD_DIM, :] * (1.0 / den)
        if h == N_QH - 1:
            _heads_out(o_ref, [outs.pop(sub * N_QH + hh) for hh in range(N_QH)], rows=slice(sub * tq, (sub + 1) * tq))

    n_items = nsub * N_QH
    for n in range(n_items + SWA_AHEAD):
        if n < n_items:
            score(n)
        if n >= SWA_AHEAD:
            value(n - SWA_AHEAD)


def _swa(sinks, qsT, main, vT, b, seq, tq=TK, nsub=4):
    ts = tq * nsub
    nq = seq // ts
    n = b * seq

    def col(c):
        return pl.BlockSpec((seq, LANE), lambda bi, i, c=c: (bi, c))

    return pl.pallas_call(
        functools.partial(_swa_kernel, tq=tq, nsub=nsub),
        grid=(b, nq),
        in_specs=[
            pl.BlockSpec(memory_space=pltpu.SMEM),
            pl.BlockSpec((1, N_QH, LANE, ts), lambda bi, i: (bi, 0, 0, i)),
            col(KSW_COL), col(KSW_COL + 1),
            pl.BlockSpec((1, N_GROUPS, seq // TK, VT_ROWS, TK), lambda bi, i: (bi, 2, 0, 0, 0)),
        ],
        out_specs=pl.BlockSpec((ts, 4 * LANE), lambda bi, i: (bi * nq + i, 0)),
        out_shape=jax.ShapeDtypeStruct((n, 4 * LANE), BF16),
        scratch_shapes=[
            pltpu.VMEM((3, SWA_WINDOW, SWA_WINDOW), F32),
            pltpu.VMEM((SWA_WINDOW, tq), F32),
            pltpu.VMEM((SWA_BUFS, SWA_WINDOW, SWA_WINDOW), F32),
            pltpu.VMEM((SWA_BUFS, SWA_WINDOW, tq), F32),
            pltpu.VMEM((SWA_BUFS, SWA_WINDOW, SWA_WINDOW), F32),
        ],
        compiler_params=pltpu.CompilerParams(
            dimension_semantics=("arbitrary", "arbitrary"), vmem_limit_bytes=VMEM_LIMIT),
        name="swa",
    )(sinks, qsT, main, main, vT)


def _outmlp_kernel(on_ref, os_ref, x_ref, wo_ref, g2_ref, g3_ref, wu_ref, wd_ref, g4_ref, o_ref, *, ff_chunk, n_sub):
    half = N_HEADS * HEAD_DIM // 2
    tm = x_ref.shape[0]
    subs = [slice(k * tm // n_sub, (k + 1) * tm // n_sub) for k in range(n_sub)]
    mixes = [_dot(on_ref[rs, :], wo_ref[0:half, :]) + _dot(os_ref[rs, :], wo_ref[half:2 * half, :]) for rs in subs]
    for rs, mix in zip(subs, mixes):
        h1 = x_ref[rs, :] + _rms(mix, g2_ref[...])
        m = _rms(h1, g3_ref[...]).astype(BF16)
        acc = jnp.zeros(h1.shape, F32)
        for c in range(D_FF // ff_chunk):
            u = _dot(m, wu_ref[:, c * ff_chunk:(c + 1) * ff_chunk])
            u = jnp.square(jnp.maximum(u, 0.0)).astype(BF16)
            acc = acc + _dot(u, wd_ref[c * ff_chunk:(c + 1) * ff_chunk, :])
        o_ref[rs, :] = h1 + _rms(acc, g4_ref[...])


def _outmlp(onsa, oswa, x2, wo, g2, g3, wu, wd, g4, tm=1024, ff_chunk=1024, n_sub=4):
    n = x2.shape[0]

    def const(shape):
        return pl.BlockSpec(shape, lambda i: (0, 0), pipeline_mode=pl.Buffered(1))

    return pl.pallas_call(
        functools.partial(_outmlp_kernel, ff_chunk=ff_chunk, n_sub=n_sub),
        grid=(n // tm,),
        in_specs=[
            pl.BlockSpec((tm, 4 * LANE), lambda i: (i, 0)),
            pl.BlockSpec((tm, 4 * LANE), lambda i: (i, 0)),
            pl.BlockSpec((tm, D_MODEL), lambda i: (i, 0)),
            const((D_MODEL, D_MODEL)),
            const((1, D_MODEL)),
            const((1, D_MODEL)),
            const((D_MODEL, D_FF)),
            const((D_FF, D_MODEL)),
            const((1, D_MODEL)),
        ],
        out_specs=pl.BlockSpec((tm, D_MODEL), lambda i: (i, 0)),
        out_shape=jax.ShapeDtypeStruct((n, D_MODEL), F32),
        compiler_params=pltpu.CompilerParams(
            dimension_semantics=("arbitrary",), vmem_limit_bytes=VMEM_LIMIT),
        name="outmlp",
    )(onsa, oswa, x2, wo, g2, g3, wu, wd, g4)


def _layout_w_in(w):
    sizes = [512, 128, 128, 128, 128, 128, 128, N_HEADS // 2 * N_BRANCH, 512, 128, 128]
    offs = np.concatenate([[0], np.cumsum(sizes)])
    q_n, kc, vc, ks, vs, kw, vw, gt, q_s, k_s, v_s = [w[:, offs[k]:offs[k + 1]] for k in range(len(sizes))]
    std = [ks, kw, k_s, kc, vc]
    gt = jnp.concatenate([gt, jnp.zeros((w.shape[0], GT_ROWS - gt.shape[1]), w.dtype)], axis=1)
    tr = jnp.concatenate([q_n, q_s, vs, vw, v_s, gt], axis=1).T
    return jnp.concatenate(std, axis=1).astype(BF16), tr.astype(BF16)


def _layout_w1(w1):
    w = w1.reshape(2, CMP_STRIDE, HEAD_DIM, CMP_HIDDEN)
    z = jnp.zeros_like(w)
    top = jnp.concatenate([w, z], axis=-1)
    bot = jnp.concatenate([z, w], axis=-1)
    e = jnp.stack([top, bot], axis=2)
    return e.reshape(2, CMP_STRIDE * 2 * HEAD_DIM, 2 * CMP_HIDDEN).astype(BF16)


def _layout_w2(w2):
    z = jnp.zeros((CMP_HIDDEN, HEAD_DIM), w2.dtype)
    top = jnp.concatenate([w2, z, z, z], axis=1)
    bot = jnp.concatenate([z, z, w2, z], axis=1)
    return jnp.concatenate([top, bot], axis=0).astype(BF16)


def _layout_pe(pe):
    p = pe.reshape(2, CMP_STRIDE, 1, HEAD_DIM)
    return jnp.broadcast_to(p, (2, CMP_STRIDE, 2, HEAD_DIM)).reshape(2, 1, CMP_STRIDE * 2 * HEAD_DIM)


def kernel(x, norm_mix_pre, w_in, cmp_pe_k, cmp_w1_k, cmp_w2_k, cmp_pe_v, cmp_w1_v, cmp_w2_v,
           sinks, w_out, norm_mix_post, norm_mlp_pre, w_up, w_down, norm_mlp_post):
    b, seq, _ = x.shape
    assert seq // SEL_BLOCK == N_SEL and seq <= POS_SPLIT * 256
    depth = w_in.shape[0]
    h = x.reshape(b * seq, D_MODEL)
    for li in range(depth):
        w_p, wT_p = _layout_w_in(w_in[li])
        main, z, qT, qsT, vT, gT = _inproj(h, norm_mix_pre[li][None], w_p, wT_p, b, seq)
        pe2 = jnp.stack([_layout_pe(cmp_pe_k[li]), _layout_pe(cmp_pe_v[li])])
        w1e = jnp.stack([_layout_w1(cmp_w1_k[li]), _layout_w1(cmp_w1_v[li])])
        w2e = jnp.stack([_layout_w2(cmp_w2_k[li]), _layout_w2(cmp_w2_v[li])])
        cmp_kv, cmp_kvT = _compress(z, pe2, w1e, w2e, jnp.swapaxes(w2e, 1, 2))
        ocT, qaT = _cmp(qT, cmp_kv, cmp_kvT, b, seq)
        onsa = _slcwin(qaT, main, vT, gT, ocT, b, seq)
        oswa = _swa(sinks[li], qsT, main, vT, b, seq)
        h = _outmlp(onsa, oswa, h, w_out[li].astype(BF16), norm_mix_post[li][None], norm_mlp_pre[li][None],
                    w_up[li].astype(BF16), w_down[li].astype(BF16), norm_mlp_post[li][None])
    return h.reshape(b, seq, D_MODEL)
```

```python
import functools
import math

import jax
import jax.numpy as jnp
import numpy as np
from jax import lax
from jax.experimental import pallas as pl
from jax.experimental.pallas import tpu as pltpu

F32 = jnp.float32
BF16 = jnp.bfloat16

D_MODEL = 1024
HEAD_DIM = 64
N_HEADS = 16
N_GROUPS = 2
N_REP = 4
N_QH = N_GROUPS * N_REP
CMP_BLOCK = 32
CMP_STRIDE = 16
CMP_HIDDEN = 4 * HEAD_DIM
SEL_BLOCK = 64
SEL_TOPN = 8
N_SEL = 32
NSA_WINDOW = 512
SWA_WINDOW = 128
D_FF = 4 * D_MODEL
NORM_EPS = 1e-6
N_BRANCH = 3

LANE = 128
LOG2E = math.log2(math.e)
NEG = -1e30
SEL_NEG = -(2.0 ** 100)
VMEM_LIMIT = 56 * 1024 * 1024

TK = 256
VT_ROWS = 80
SEL_ROW = HEAD_DIM
POS_ROW = HEAD_DIM + N_SEL
N_PIECE = 3
CMP_FEAT = 16
POS_SPLIT = 256

KS_COL, KW_COL, KSW_COL = 0, 2, 4
MAIN_COLS = 6 * LANE
STD_COLS = 5 * LANE
QT_ROWS = N_QH * HEAD_DIM
VT_ROW0 = 2 * QT_ROWS
GT_ROW = VT_ROW0 + 3 * N_GROUPS * HEAD_DIM
GT_ROWS = 32
T_ROWS = GT_ROW + GT_ROWS


def _slopes():
    s = 2.0 ** (-8.0 * (np.arange(N_HEADS) + 1) / N_HEADS)
    nsa = s[0::2].reshape(N_GROUPS, N_REP)
    swa = s[1::2].reshape(N_GROUPS, N_REP)
    return nsa, swa


SLOPES_NSA, SLOPES_SWA = _slopes()


def _bf16_round(x):
    u = np.float32(x).reshape(1).view(np.uint32)
    u = (u + (((u >> 16) & 1) + 0x7FFF)) & np.uint32(0xFFFF0000)
    return float(u.view(np.float32)[0])


def _bf16_pieces(x, n=N_PIECE):
    out, rem = [], float(np.float32(x))
    for _ in range(n):
        p = _bf16_round(rem)
        out.append(p)
        rem = float(np.float32(rem - p))
    return out


_NT = (((1,), (1,)), ((), ()))


def _dot_nt(a, b, **kw):
    return lax.dot_general(a, b, _NT, preferred_element_type=F32, **kw)


def _dot(a, b, **kw):
    return jnp.dot(a, b, preferred_element_type=F32, **kw)


def _rms(v, g):
    return v * lax.rsqrt(jnp.mean(v * v, axis=-1, keepdims=True) + NORM_EPS) * g


def _slope_rows(slope, tq):
    pieces = _bf16_pieces(slope * LOG2E)
    vals = pieces + [p * POS_SPLIT for p in pieces]
    prow = lax.broadcasted_iota(jnp.int32, (LANE - POS_ROW, tq), 0)
    feat = jnp.zeros((LANE - POS_ROW, tq), F32)
    for k, v in enumerate(vals):
        feat = jnp.where(prow == k, v, feat)
    return feat.astype(BF16)


def _inproj_kernel(x_ref, g_ref, w_ref, wT_ref, main_ref, z_ref, qT_ref, qsT_ref, vT_ref, gT_ref,
                   kc_scr, vc_scr, *, tm, seq):
    qscale = LOG2E * HEAD_DIM ** -0.5
    lane = lax.broadcasted_iota(jnp.int32, (TK, LANE), 1)
    lo = lane < HEAD_DIM
    zmid = jnp.zeros((N_SEL, TK), BF16)
    ones_blk = jnp.where(lax.broadcasted_iota(jnp.int32, (VT_ROWS - HEAD_DIM, TK), 0) == 0, 1.0, 0.0).astype(BF16)
    t0 = (pl.program_id(0) * tm) % seq

    for st in range(tm // TK):
        rs = slice(st * TK, (st + 1) * TK)
        a = _rms(x_ref[rs, :], g_ref[...]).astype(BF16)

        res = _dot(a, w_ref[...])
        pos = t0 + st * TK + lax.broadcasted_iota(jnp.int32, (TK, LANE), 0)
        posf = jnp.where((lane >= POS_ROW) & (lane < POS_ROW + N_PIECE), (pos % POS_SPLIT).astype(F32), 0.0)
        posf = jnp.where((lane >= POS_ROW + N_PIECE) & (lane < POS_ROW + 2 * N_PIECE),
                         (pos // POS_SPLIT).astype(F32), posf)
        onehot = jnp.where(lane == pos // SEL_BLOCK + SEL_ROW, 1.0, 0.0)
        for t in range(3):
            blk = res[:, t * LANE:(t + 1) * LANE]
            extra = posf + onehot if 2 * t == KS_COL else posf
            for g, src in enumerate((blk, pltpu.roll(blk, HEAD_DIM, 1))):
                c = 2 * t + g
                main_ref[rs, c * LANE:(c + 1) * LANE] = (jnp.where(lo, src, 0.0) + extra).astype(BF16)
        kc_scr[...] = res[:, 3 * LANE:4 * LANE]
        vc_scr[...] = res[:, 4 * LANE:STD_COLS]
        zr = TK // CMP_STRIDE
        for c in range(CMP_STRIDE):
            for t, scr in enumerate((kc_scr, vc_scr)):
                z_ref[t, 0, st * zr:(st + 1) * zr, c * LANE:(c + 1) * LANE] = (
                    scr[pl.ds(c, zr, stride=CMP_STRIDE), :].astype(BF16))

        resT = _dot_nt(wT_ref[...], a)
        qT_ref[0, :, rs] = (resT[0:QT_ROWS] * qscale).astype(BF16)
        for h in range(N_QH):
            r0 = QT_ROWS + h * HEAD_DIM
            qsT_ref[0, h, :, rs] = jnp.concatenate(
                [(resT[r0:r0 + HEAD_DIM] * qscale).astype(BF16), zmid,
                 _slope_rows(SLOPES_SWA[h // N_REP, h % N_REP], TK)], axis=0)
        for tg in range(3 * N_GROUPS):
            r0 = VT_ROW0 + tg * HEAD_DIM
            vT_ref[0, tg, st, 0:HEAD_DIM, :] = resT[r0:r0 + HEAD_DIM, :].astype(BF16)
            vT_ref[0, tg, st, HEAD_DIM:VT_ROWS, :] = ones_blk
        gT_ref[0, :, rs] = jax.nn.sigmoid(resT[GT_ROW:T_ROWS])


def _inproj(x2, gain, w_p, wT_p, b, seq, tm=1024):
    n = x2.shape[0]
    nt = seq // tm
    return pl.pallas_call(
        functools.partial(_inproj_kernel, tm=tm, seq=seq),
        grid=(n // tm,),
        in_specs=[
            pl.BlockSpec((tm, D_MODEL), lambda i: (i, 0)),
            pl.BlockSpec((1, D_MODEL), lambda i: (0, 0)),
            pl.BlockSpec((D_MODEL, STD_COLS), lambda i: (0, 0)),
            pl.BlockSpec((T_ROWS, D_MODEL), lambda i: (0, 0)),
        ],
        out_specs=[
            pl.BlockSpec((tm, MAIN_COLS), lambda i: (i, 0)),
            pl.BlockSpec((2, 1, tm // CMP_STRIDE, CMP_STRIDE * LANE), lambda i: (0, i // nt, i % nt, 0)),
            pl.BlockSpec((1, QT_ROWS, tm), lambda i: (i // nt, 0, i % nt)),
            pl.BlockSpec((1, N_QH, LANE, tm), lambda i: (i // nt, 0, 0, i % nt)),
            pl.BlockSpec((1, 3 * N_GROUPS, tm // TK, VT_ROWS, TK), lambda i: (i // nt, 0, i % nt, 0, 0)),
            pl.BlockSpec((1, GT_ROWS, tm), lambda i: (i // nt, 0, i % nt)),
        ],
        out_shape=[
            jax.ShapeDtypeStruct((n, MAIN_COLS), BF16),
            jax.ShapeDtypeStruct((2, b, seq // CMP_STRIDE, CMP_STRIDE * LANE), BF16),
            jax.ShapeDtypeStruct((b, QT_ROWS, seq), BF16),
            jax.ShapeDtypeStruct((b, N_QH, LANE, seq), BF16),
            jax.ShapeDtypeStruct((b, 3 * N_GROUPS, seq // TK, VT_ROWS, TK), BF16),
            jax.ShapeDtypeStruct((b, GT_ROWS, seq), F32),
        ],
        scratch_shapes=[pltpu.VMEM((TK, LANE), F32), pltpu.VMEM((TK, LANE), F32)],
        compiler_params=pltpu.CompilerParams(
            dimension_semantics=("arbitrary",), vmem_limit_bytes=VMEM_LIMIT),
        name="inproj",
    )(x2, gain, w_p, wT_p)


def _compress_kernel(z_ref, pe_ref, w1_ref, w2_ref, w2T_ref, o_ref, oT_ref):
    nb, n, k = z_ref.shape[1:]
    z = z_ref[0].reshape(nb * n, k).astype(F32)
    zt = (z + pe_ref[0, 0]).astype(BF16)
    zb = (z + pe_ref[0, 1]).astype(BF16)
    a = _dot(zt, w1_ref[0, 0])
    bm = _dot(zb, w1_ref[0, 1])
    h = a + pltpu.roll(bm, nb * n - 1, 0)
    hg = jax.nn.gelu(h).astype(BF16)
    o = _dot(hg, w2_ref[0])
    lane = lax.broadcasted_iota(jnp.int32, o.shape, 1) % LANE
    nidx = lax.broadcasted_iota(jnp.int32, o.shape, 0) % n
    o = o + jnp.where((lane >= HEAD_DIM) & (lane < HEAD_DIM + N_PIECE), nidx.astype(F32), 0.0)
    for e in range(nb):
        o_ref[0, e] = o[e * n:(e + 1) * n].astype(o_ref.dtype)
        oT_ref[0, e] = _dot_nt(w2T_ref[0], hg[e * n:(e + 1) * n]).astype(oT_ref.dtype)


def _compress(z, pe2, w1e, w2e, w2eT, nb=4):
    _, b, n, k = z.shape
    nb = math.gcd(b, nb)
    return pl.pallas_call(
        _compress_kernel,
        grid=(2, b // nb),
        in_specs=[
            pl.BlockSpec((1, nb, n, k), lambda t, i: (t, i, 0, 0)),
            pl.BlockSpec((1, 2, 1, k), lambda t, i: (t, 0, 0, 0)),
            pl.BlockSpec((1, 2, k, 2 * CMP_HIDDEN), lambda t, i: (t, 0, 0, 0)),
            pl.BlockSpec((1, 2 * CMP_HIDDEN, 2 * LANE), lambda t, i: (t, 0, 0)),
            pl.BlockSpec((1, 2 * LANE, 2 * CMP_HIDDEN), lambda t, i: (t, 0, 0)),
        ],
        out_specs=[
            pl.BlockSpec((1, nb, n, 2 * LANE), lambda t, i: (t, i, 0, 0)),
            pl.BlockSpec((1, nb, 2 * LANE, n), lambda t, i: (t, i, 0, 0)),
        ],
        out_shape=[
            jax.ShapeDtypeStruct((2, b, n, 2 * LANE), BF16),
            jax.ShapeDtypeStruct((2, b, 2 * LANE, n), BF16),
        ],
        compiler_params=pltpu.CompilerParams(
            dimension_semantics=("arbitrary", "arbitrary"), vmem_limit_bytes=VMEM_LIMIT),
        name="compress",
    )(z, pe2, w1e, w2e, w2eT)


def _f32_dot_exact_lhs(a_bf16, x):
    out = None
    rem = x
    for _ in range(N_PIECE):
        piece = rem.astype(BF16)
        rem = rem - piece.astype(F32)
        d = _dot(a_bf16, piece)
        out = d if out is None else out + d
    return out


def _select_mask(score, tq):
    sub = 8
    n_chunk = N_SEL // sub
    chunks = [score[sub * c:sub * (c + 1)] for c in range(n_chunk)]
    ranks = [jnp.zeros((sub, tq), jnp.int32) for _ in range(n_chunk)]
    jrow = lax.broadcasted_iota(jnp.int32, (sub, tq), 0)
    for k in range(N_SEL):
        sk = score[k:k + 1, :]
        for c in range(n_chunk):
            if k < sub * c:
                ahead = sk >= chunks[c]
            elif k >= sub * (c + 1):
                ahead = sk > chunks[c]
            else:
                ahead = (sk > chunks[c]) | ((sk == chunks[c]) & (jrow + sub * c > k))
            ranks[c] = jnp.where(ahead, ranks[c] + 1, ranks[c])
    rank = jnp.concatenate(ranks, axis=0)
    return jnp.where(rank < SEL_TOPN, 0.0, SEL_NEG).astype(BF16)


def _cmp_kernel(qT_ref, k_ref, vT_ref, ocT_ref, qaT_ref, s_scr, *, tq, nsub, n_cmp):
    i = pl.program_id(1)
    n_pad = k_ref.shape[2]
    nn = lax.broadcasted_iota(jnp.int32, (n_pad, tq), 0)
    frow = lax.broadcasted_iota(jnp.int32, (CMP_FEAT, tq), 0)

    jr = lax.broadcasted_iota(jnp.int32, (N_SEL, n_pad), 0)
    nc = lax.broadcasted_iota(jnp.int32, (N_SEL, n_pad), 1)
    ov = ((nc * CMP_STRIDE < jr * SEL_BLOCK + SEL_BLOCK) & (nc * CMP_STRIDE + CMP_BLOCK > jr * SEL_BLOCK)
          & (nc < n_cmp))
    ovT = jnp.where(ov, 1.0, 0.0).astype(BF16)
    jj = lax.broadcasted_iota(jnp.int32, (N_SEL, tq), 0)

    for sub in range(nsub):
        qs = slice(sub * tq, (sub + 1) * tq)
        for g in range(N_GROUPS):
            kc = k_ref[0, 0][:, g * LANE:g * LANE + HEAD_DIM + CMP_FEAT]
            for r in range(N_REP):
                h = N_REP * g + r
                feat = jnp.zeros((CMP_FEAT, tq), F32)
                for k, pc in enumerate(_bf16_pieces(SLOPES_NSA[g, r] * LOG2E)):
                    feat = jnp.where(frow == k, pc * CMP_STRIDE, feat)
                qTa = jnp.concatenate([qT_ref[0, h * HEAD_DIM:(h + 1) * HEAD_DIM, qs], feat.astype(BF16)], axis=0)
                s_scr[sub * N_QH + h] = _dot(kc, qTa)

    for sub in range(nsub):
        qs = slice(sub * tq, (sub + 1) * tq)
        t0 = (i * nsub + sub) * tq
        tt = t0 + lax.broadcasted_iota(jnp.int32, (n_pad, tq), 1)
        maskadd = jnp.where((tt >= nn * CMP_STRIDE + CMP_BLOCK - 1) & (nn < n_cmp), 0.0, NEG)
        colvalid = jnp.where(t0 + lax.broadcasted_iota(jnp.int32, (1, tq), 1) >= CMP_BLOCK - 1, 1.0, 0.0)
        blk_t = (t0 + lax.broadcasted_iota(jnp.int32, (N_SEL, tq), 1)) // SEL_BLOCK
        valid = jj <= blk_t
        forced = (jj == 0) | (jj == blk_t) | (jj == blk_t - 1)

        for g in range(N_GROUPS):
            vcT = vT_ref[0, 0][g * LANE:g * LANE + HEAD_DIM, :]
            psum = jnp.zeros((n_pad, tq), F32)
            qTs = []
            for r in range(N_REP):
                h = N_REP * g + r
                qTs.append(qT_ref[0, h * HEAD_DIM:(h + 1) * HEAD_DIM, qs])
                sm = s_scr[sub * N_QH + h] + maskadd
                m = jnp.max(sm, axis=0, keepdims=True)
                e = jnp.exp2(sm - m)
                den = jnp.sum(e, axis=0, keepdims=True)
                p = e * (colvalid / den)
                ocT_ref[0, h * HEAD_DIM:(h + 1) * HEAD_DIM, qs] = _dot(vcT, p.astype(BF16))
                psum = psum + p

            imp = _f32_dot_exact_lhs(ovT, psum)
            score = jnp.where(valid, imp, -jnp.inf)
            score = jnp.where(forced & valid, jnp.inf, score)
            negm = _select_mask(score, tq)
            for r in range(N_REP):
                qaT_ref[0, N_REP * g + r, :, qs] = jnp.concatenate(
                    [qTs[r], negm, _slope_rows(SLOPES_NSA[g, r], tq)], axis=0)


def _cmp(qT, cmp_k, cmp_vT, b, seq, tq=256, nsub=4):
    ts = tq * nsub
    nq = seq // ts
    n_cmp = (seq - CMP_BLOCK) // CMP_STRIDE + 1
    n_pad = cmp_k.shape[2]
    return pl.pallas_call(
        functools.partial(_cmp_kernel, tq=tq, nsub=nsub, n_cmp=n_cmp),
        grid=(b, nq),
        in_specs=[
            pl.BlockSpec((1, QT_ROWS, ts), lambda bi, i: (bi, 0, i)),
            pl.BlockSpec((1, 1, n_pad, 2 * LANE), lambda bi, i: (0, bi, 0, 0)),
            pl.BlockSpec((1, 1, 2 * LANE, n_pad), lambda bi, i: (1, bi, 0, 0)),
        ],
        out_specs=[
            pl.BlockSpec((1, QT_ROWS, ts), lambda bi, i: (bi, 0, i)),
            pl.BlockSpec((1, N_QH, LANE, ts), lambda bi, i: (bi, 0, 0, i)),
        ],
        out_shape=[
            jax.ShapeDtypeStruct((b, QT_ROWS, seq), F32),
            jax.ShapeDtypeStruct((b, N_QH, LANE, seq), BF16),
        ],
        scratch_shapes=[pltpu.VMEM((nsub * N_QH, n_pad, tq), F32)],
        compiler_params=pltpu.CompilerParams(
            dimension_semantics=("arbitrary", "arbitrary"), vmem_limit_bytes=VMEM_LIMIT),
        name="cmp",
    )(qT, cmp_k, cmp_vT)


SLC_AHEAD = 2
SLC_HEADS = 1
SLC_BUFS = SLC_AHEAD + 1

HALF = TK // 2
_TRI = {0: (slice(0, HALF), slice(HALF, TK), slice(HALF, TK)),
        1: (slice(HALF, TK), slice(0, HALF), slice(0, HALF))}


def _score_phase(qT_ref, k_tiles, mask_id, caus, s_buf, mt_buf, heads):
    for g in range(N_GROUPS):
        for r in range(N_REP):
            h = N_REP * g + r
            if h not in heads:
                continue
            qT = qT_ref[0, h]
            if mask_id is None:
                s = _dot(k_tiles[g], qT)
                s_buf[h] = s
                mt_buf[h] = jnp.max(s, axis=0, keepdims=True)
                continue
            fr, hr, hl = _TRI[mask_id]
            s_full = _dot(k_tiles[g][fr], qT) + caus[mask_id, fr, :]
            s_half = _dot(k_tiles[g][hr], qT[:, hl]) + caus[mask_id, hr, hl]
            s_buf[h, fr, :] = s_full
            s_buf[h, hr, hl] = s_half
            m_half = jnp.max(s_half, axis=0, keepdims=True)
            filler = jnp.full((1, HALF), -3e38, F32)
            m_half = jnp.concatenate([filler, m_half] if hl.start else [m_half, filler], axis=1)
            mt_buf[h] = jnp.maximum(jnp.max(s_full, axis=0, keepdims=True), m_half)


def _value_phase(vT_tiles, mask_id, s_buf, mt_buf, m_scr, acc_scr, heads):
    for g in range(N_GROUPS):
        for r in range(N_REP):
            h = N_REP * g + r
            if h not in heads:
                continue
            m_old = m_scr[h]
            m_new = jnp.maximum(m_old, mt_buf[h])
            alpha = jnp.exp2(m_old - m_new)
            if mask_id is None:
                pT = jnp.exp2(s_buf[h] - m_new).astype(BF16)
                upd = _dot(vT_tiles[g], pT)
            else:
                fr, hr, hl = _TRI[mask_id]
                p_full = jnp.exp2(s_buf[h, fr, :] - m_new).astype(BF16)
                p_half = jnp.exp2(s_buf[h, hr, hl] - m_new[:, hl]).astype(BF16)
                u_half = _dot(vT_tiles[g][:, hr], p_half)
                zero = jnp.zeros_like(u_half)
                upd = _dot(vT_tiles[g][:, fr], p_full) + jnp.concatenate(
                    [zero, u_half] if hl.start else [u_half, zero], axis=1)
            acc_scr[h] = alpha * acc_scr[h] + upd
            m_scr[h] = m_new


def _sweep_reset(m_scr, acc_scr):
    m_scr[...] = jnp.full(m_scr.shape, -3e38, F32)
    acc_scr[...] = jnp.zeros(acc_scr.shape, F32)


def _heads_out(o_ref, outs, rows=slice(None)):
    for g in range(N_GROUPS):
        oT = jnp.concatenate(outs[N_REP * g:N_REP * (g + 1)], axis=0)
        o_ref[rows, g * 2 * LANE:(g + 1) * 2 * LANE] = oT.T.astype(o_ref.dtype)


def _slcwin_kernel(qT_ref, ks0, ks1, kw0, kw1, vsT_ref, vwT_ref, gT_ref, ocT_ref,
                   o_ref, caus, s_scr, mt_scr, m_s, acc_s, m_w, acc_w, *, tq, nq):
    tk = TK
    bi = pl.program_id(0)
    i = pl.program_id(1)
    assert NSA_WINDOW // tk == 2 and tq == tk

    @pl.when((bi == 0) & (i == 0))
    def _init():
        kk = lax.broadcasted_iota(jnp.int32, (tk, tq), 0)
        qq = lax.broadcasted_iota(jnp.int32, (tk, tq), 1)
        caus[0] = jnp.where(kk <= qq, 0.0, NEG)
        caus[1] = jnp.where(kk > qq, 0.0, NEG)

    ks = (ks0, ks1)
    kw = (kw0, kw1)

    def run(ii):
        _sweep_reset(m_s, acc_s)
        _sweep_reset(m_w, acc_w)
        stream = []
        for d in (2, 1, 0):
            if ii - d >= 0:
                stream.append((kw, vwT_ref, ii - d, {0: 0, 2: 1}.get(d), m_w, acc_w))
        for j in range(ii + 1):
            stream.append((ks, vsT_ref, j, 0 if j == ii else None, m_s, acc_s))

        def score(p, heads):
            refs, _, idx, mask_id, _, _ = stream[p]
            buf = p % SLC_BUFS
            _score_phase(qT_ref, [refs[g][idx * tk:(idx + 1) * tk, :] for g in range(N_GROUPS)],
                         mask_id, caus, s_scr.at[buf], mt_scr.at[buf], heads)

        def value(p, heads):
            _, vref, idx, mask_id, m_scr, acc_scr = stream[p]
            buf = p % SLC_BUFS
            _value_phase([vref[0, g, idx] for g in range(N_GROUPS)], mask_id,
                         s_scr.at[buf], mt_scr.at[buf], m_scr, acc_scr, heads)

        for p in range(len(stream) + SLC_AHEAD):
            for h0 in range(0, N_QH, SLC_HEADS):
                heads = range(h0, h0 + SLC_HEADS)
                if p < len(stream):
                    score(p, heads)
                if p >= SLC_AHEAD:
                    value(p - SLC_AHEAD, heads)

        gT = gT_ref[0]
        outs = []
        for h in range(N_QH):
            gc = h * N_BRANCH
            f_slc = gT[gc + 1:gc + 2] * (1.0 / acc_s[h, HEAD_DIM:HEAD_DIM + 1, :])
            f_win = gT[gc + 2:gc + 3] * (1.0 / acc_w[h, HEAD_DIM:HEAD_DIM + 1, :])
            outs.append(gT[gc:gc + 1] * ocT_ref[0, h * HEAD_DIM:(h + 1) * HEAD_DIM, :]
                        + f_slc * acc_s[h, 0:HEAD_DIM, :]
                        + f_win * acc_w[h, 0:HEAD_DIM, :])
        _heads_out(o_ref, outs)

    for ii in range(nq):
        pl.when(i == ii)(functools.partial(run, ii))


def _slcwin(qaT, main, vT, gT, ocT, b, seq, tq=TK):
    nq = seq // tq
    n = b * seq

    def col(c):
        return pl.BlockSpec((seq, LANE), lambda bi, i, c=c: (bi, c))

    return pl.pallas_call(
        functools.partial(_slcwin_kernel, tq=tq, nq=nq),
        grid=(b, nq),
        in_specs=[
            pl.BlockSpec((1, N_QH, LANE, tq), lambda bi, i: (bi, 0, 0, i)),
            col(KS_COL), col(KS_COL + 1), col(KW_COL), col(KW_COL + 1),
            pl.BlockSpec((1, N_GROUPS, seq // TK, VT_ROWS, TK), lambda bi, i: (bi, 0, 0, 0, 0)),
            pl.BlockSpec((1, N_GROUPS, seq // TK, VT_ROWS, TK), lambda bi, i: (bi, 1, 0, 0, 0)),
            pl.BlockSpec((1, GT_ROWS, tq), lambda bi, i: (bi, 0, i)),
            pl.BlockSpec((1, QT_ROWS, tq), lambda bi, i: (bi, 0, i)),
        ],
        out_specs=pl.BlockSpec((tq, 4 * LANE), lambda bi, i: (bi * nq + i, 0)),
        out_shape=jax.ShapeDtypeStruct((n, 4 * LANE), BF16),
        scratch_shapes=[
            pltpu.VMEM((2, TK, tq), F32),
            pltpu.VMEM((SLC_BUFS, N_QH, TK, tq), F32),
            pltpu.VMEM((SLC_BUFS, N_QH, 1, tq), F32),
            pltpu.VMEM((N_QH, 1, tq), F32),
            pltpu.VMEM((N_QH, VT_ROWS, tq), F32),
            pltpu.VMEM((N_QH, 1, tq), F32),
            pltpu.VMEM((N_QH, VT_ROWS, tq), F32),
        ],
        compiler_params=pltpu.CompilerParams(
            dimension_semantics=("arbitrary", "arbitrary"), vmem_limit_bytes=VMEM_LIMIT),
        name="slcwin",
    )(qaT, main, main, main, main, vT, vT, gT, ocT)


SWA_AHEAD = 3
SWA_BUFS = SWA_AHEAD + 1

def _swa_kernel(sink_ref, qT_ref, k0, k1, vT_ref, o_ref, tri, mab, s_far, s_ab, s_c, *, tq, nsub):
    bi = pl.program_id(0)
    i = pl.program_id(1)
    w = SWA_WINDOW
    assert tq == 2 * w and w == LANE

    @pl.when((bi == 0) & (i == 0))
    def _init():
        kk = lax.broadcasted_iota(jnp.int32, (w, w), 0)
        qq = lax.broadcasted_iota(jnp.int32, (w, w), 1)
        causal = jnp.where(kk <= qq, 0.0, NEG)
        far = jnp.where(kk > qq, 0.0, NEG)
        tri[0] = far
        tri[1] = causal
        tri[2] = jnp.full((w, w), NEG, F32)
        mab[...] = jnp.concatenate([causal, far], axis=1)

    kk_ref = (k0, k1)
    mts, sinks, outs = {}, {}, {}

    def geometry(sub):
        tile = i * nsub + sub
        return tile, tile * tq

    def score(n):
        sub, h = divmod(n, N_QH)
        g, r = divmod(h, N_REP)
        tile, t0 = geometry(sub)
        far0 = pl.multiple_of(jnp.maximum(t0 - w, 0), w)
        main0 = pl.multiple_of(t0, tq)
        buf = n % SWA_BUFS
        qaT = qT_ref[0, h, :, sub * tq:(sub + 1) * tq]
        sf = _dot(kk_ref[g][pl.ds(far0, w), :], qaT[:, 0:w]) + tri[jnp.where(tile == 0, 2, 0)]
        sab = _dot(kk_ref[g][pl.ds(main0, w), :], qaT) + mab[...]
        sc = _dot(kk_ref[g][pl.ds(main0 + w, w), :], qaT[:, w:tq]) + tri[1]
        s_far[buf] = sf
        s_ab[buf] = sab
        s_c[buf] = sc
        tpos = (t0 + lax.broadcasted_iota(jnp.int32, (1, tq), 1)).astype(F32)
        sink = (sink_ref[g, r] * LOG2E) + float(SLOPES_SWA[g, r] * LOG2E) * tpos
        m_side = jnp.concatenate([jnp.max(sf, axis=0, keepdims=True), jnp.max(sc, axis=0, keepdims=True)], axis=1)
        mts[n] = jnp.maximum(jnp.maximum(jnp.max(sab, axis=0, keepdims=True), m_side), sink)
        sinks[n] = sink

    def value(n):
        sub, h = divmod(n, N_QH)
        g = h // N_REP
        tile, _ = geometry(sub)
        buf = n % SWA_BUFS
        vT_far = vT_ref[0, g, jnp.maximum(tile - 1, 0)][:, w:2 * w]
        vT_main = vT_ref[0, g, tile]
        m = mts.pop(n)
        pf = jnp.exp2(s_far[buf] - m[:, 0:w]).astype(BF16)
        pab = jnp.exp2(s_ab[buf] - m).astype(BF16)
        pc = jnp.exp2(s_c[buf] - m[:, w:tq]).astype(BF16)
        acc = _dot(vT_main[:, 0:w], pab) + jnp.concatenate([_dot(vT_far, pf), _dot(vT_main[:, w:tq], pc)], axis=1)
        den = acc[HEAD_DIM:HEAD_DIM + 1, :] + jnp.exp2(sinks.pop(n) - m)
        outs[n] = acc[0:HEAD_DIM, :] * (1.0 / den)
        if h == N_QH - 1:
            _heads_out(o_ref, [outs.pop(sub * N_QH + hh) for hh in range(N_QH)], rows=slice(sub * tq, (sub + 1) * tq))

    n_items = nsub * N_QH
    for n in range(n_items + SWA_AHEAD):
        if n < n_items:
            score(n)
        if n >= SWA_AHEAD:
            value(n - SWA_AHEAD)


def _swa(sinks, qsT, main, vT, b, seq, tq=TK, nsub=4):
    ts = tq * nsub
    nq = seq // ts
    n = b * seq

    def col(c):
        return pl.BlockSpec((seq, LANE), lambda bi, i, c=c: (bi, c))

    return pl.pallas_call(
        functools.partial(_swa_kernel, tq=tq, nsub=nsub),
        grid=(b, nq),
        in_specs=[
            pl.BlockSpec(memory_space=pltpu.SMEM),
            pl.BlockSpec((1, N_QH, LANE, ts), lambda bi, i: (bi, 0, 0, i)),
            col(KSW_COL), col(KSW_COL + 1),
            pl.BlockSpec((1, N_GROUPS, seq // TK, VT_ROWS, TK), lambda bi, i: (bi, 2, 0, 0, 0)),
        ],
        out_specs=pl.BlockSpec((ts, 4 * LANE), lambda bi, i: (bi * nq + i, 0)),
        out_shape=jax.ShapeDtypeStruct((n, 4 * LANE), BF16),
        scratch_shapes=[
            pltpu.VMEM((3, SWA_WINDOW, SWA_WINDOW), F32),
            pltpu.VMEM((SWA_WINDOW, tq), F32),
            pltpu.VMEM((SWA_BUFS, SWA_WINDOW, SWA_WINDOW), F32),
            pltpu.VMEM((SWA_BUFS, SWA_WINDOW, tq), F32),
            pltpu.VMEM((SWA_BUFS, SWA_WINDOW, SWA_WINDOW), F32),
        ],
        compiler_params=pltpu.CompilerParams(
            dimension_semantics=("arbitrary", "arbitrary"), vmem_limit_bytes=VMEM_LIMIT),
        name="swa",
    )(sinks, qsT, main, main, vT)


def _outmlp_kernel(on_ref, os_ref, x_ref, wo_ref, g2_ref, g3_ref, wu_ref, wd_ref, g4_ref, o_ref, *, ff_chunk, n_sub):
    half = N_HEADS * HEAD_DIM // 2
    tm = x_ref.shape[0]
    subs = [slice(k * tm // n_sub, (k + 1) * tm // n_sub) for k in range(n_sub)]
    mixes = [_dot(on_ref[rs, :], wo_ref[0:half, :]) + _dot(os_ref[rs, :], wo_ref[half:2 * half, :]) for rs in subs]
    for rs, mix in zip(subs, mixes):
        h1 = x_ref[rs, :] + _rms(mix, g2_ref[...])
        m = _rms(h1, g3_ref[...]).astype(BF16)
        acc = jnp.zeros(h1.shape, F32)
        for c in range(D_FF // ff_chunk):
            u = _dot(m, wu_ref[:, c * ff_chunk:(c + 1) * ff_chunk])
            u = jnp.square(jnp.maximum(u, 0.0)).astype(BF16)
            acc = acc + _dot(u, wd_ref[c * ff_chunk:(c + 1) * ff_chunk, :])
        o_ref[rs, :] = h1 + _rms(acc, g4_ref[...])


def _outmlp(onsa, oswa, x2, wo, g2, g3, wu, wd, g4, tm=1024, ff_chunk=1024, n_sub=4):
    n = x2.shape[0]

    def const(shape):
        return pl.BlockSpec(shape, lambda i: (0, 0), pipeline_mode=pl.Buffered(1))

    return pl.pallas_call(
        functools.partial(_outmlp_kernel, ff_chunk=ff_chunk, n_sub=n_sub),
        grid=(n // tm,),
        in_specs=[
            pl.BlockSpec((tm, 4 * LANE), lambda i: (i, 0)),
            pl.BlockSpec((tm, 4 * LANE), lambda i: (i, 0)),
            pl.BlockSpec((tm, D_MODEL), lambda i: (i, 0)),
            const((D_MODEL, D_MODEL)),
            const((1, D_MODEL)),
            const((1, D_MODEL)),
            const((D_MODEL, D_FF)),
            const((D_FF, D_MODEL)),
            const((1, D_MODEL)),
        ],
        out_specs=pl.BlockSpec((tm, D_MODEL), lambda i: (i, 0)),
        out_shape=jax.ShapeDtypeStruct((n, D_MODEL), F32),
        compiler_params=pltpu.CompilerParams(
            dimension_semantics=("arbitrary",), vmem_limit_bytes=VMEM_LIMIT),
        name="outmlp",
    )(onsa, oswa, x2, wo, g2, g3, wu, wd, g4)


def _layout_w_in(w):
    sizes = [512, 128, 128, 128, 128, 128, 128, N_HEADS // 2 * N_BRANCH, 512, 128, 128]
    offs = np.concatenate([[0], np.cumsum(sizes)])
    q_n, kc, vc, ks, vs, kw, vw, gt, q_s, k_s, v_s = [w[:, offs[k]:offs[k + 1]] for k in range(len(sizes))]
    std = [ks, kw, k_s, kc, vc]
    gt = jnp.concatenate([gt, jnp.zeros((w.shape[0], GT_ROWS - gt.shape[1]), w.dtype)], axis=1)
    tr = jnp.concatenate([q_n, q_s, vs, vw, v_s, gt], axis=1).T
    return jnp.concatenate(std, axis=1).astype(BF16), tr.astype(BF16)


def _layout_w1(w1):
    w = w1.reshape(2, CMP_STRIDE, HEAD_DIM, CMP_HIDDEN)
    z = jnp.zeros_like(w)
    top = jnp.concatenate([w, z], axis=-1)
    bot = jnp.concatenate([z, w], axis=-1)
    e = jnp.stack([top, bot], axis=2)
    return e.reshape(2, CMP_STRIDE * 2 * HEAD_DIM, 2 * CMP_HIDDEN).astype(BF16)


def _layout_w2(w2):
    z = jnp.zeros((CMP_HIDDEN, HEAD_DIM), w2.dtype)
    top = jnp.concatenate([w2, z, z, z], axis=1)
    bot = jnp.concatenate([z, z, w2, z], axis=1)
    return jnp.concatenate([top, bot], axis=0).astype(BF16)


def _layout_pe(pe):
    p = pe.reshape(2, CMP_STRIDE, 1, HEAD_DIM)
    return jnp.broadcast_to(p, (2, CMP_STRIDE, 2, HEAD_DIM)).reshape(2, 1, CMP_STRIDE * 2 * HEAD_DIM)


def kernel(x, norm_mix_pre, w_in, cmp_pe_k, cmp_w1_k, cmp_w2_k, cmp_pe_v, cmp_w1_v, cmp_w2_v,
           sinks, w_out, norm_mix_post, norm_mlp_pre, w_up, w_down, norm_mlp_post):
    b, seq, _ = x.shape
    assert seq // SEL_BLOCK == N_SEL and seq <= POS_SPLIT * 256
    depth = w_in.shape[0]
    h = x.reshape(b * seq, D_MODEL)
    for li in range(depth):
        w_p, wT_p = _layout_w_in(w_in[li])
        main, z, qT, qsT, vT, gT = _inproj(h, norm_mix_pre[li][None], w_p, wT_p, b, seq)
        pe2 = jnp.stack([_layout_pe(cmp_pe_k[li]), _layout_pe(cmp_pe_v[li])])
        w1e = jnp.stack([_layout_w1(cmp_w1_k[li]), _layout_w1(cmp_w1_v[li])])
        w2e = jnp.stack([_layout_w2(cmp_w2_k[li]), _layout_w2(cmp_w2_v[li])])
        cmp_kv, cmp_kvT = _compress(z, pe2, w1e, w2e, jnp.swapaxes(w2e, 1, 2))
        ocT, qaT = _cmp(qT, cmp_kv, cmp_kvT, b, seq)
        onsa = _slcwin(qaT, main, vT, gT, ocT, b, seq)
        oswa = _swa(sinks[li], qsT, main, vT, b, seq)
        h = _outmlp(onsa, oswa, h, w_out[li].astype(BF16), norm_mix_post[li][None], norm_mlp_pre[li][None],
                    w_up[li].astype(BF16), w_down[li].astype(BF16), norm_mlp_post[li][None])
    return h.reshape(b, seq, D_MODEL)
```

```python
import functools
import math

import jax
import jax.numpy as jnp
import numpy as np
from jax import lax
from jax.experimental import pallas as pl
from jax.experimental.pallas import tpu as pltpu

F32 = jnp.float32
BF16 = jnp.bfloat16

D_MODEL = 1024
HEAD_DIM = 64
N_HEADS = 16
N_GROUPS = 2
N_REP = 4
N_QH = N_GROUPS * N_REP
CMP_BLOCK = 32
CMP_STRIDE = 16
CMP_HIDDEN = 4 * HEAD_DIM
SEL_BLOCK = 64
SEL_TOPN = 8
N_SEL = 32
NSA_WINDOW = 512
SWA_WINDOW = 128
D_FF = 4 * D_MODEL
NORM_EPS = 1e-6
N_BRANCH = 3

LANE = 128
LOG2E = math.log2(math.e)
NEG = -1e30
SEL_NEG = -(2.0 ** 100)
VMEM_LIMIT = 56 * 1024 * 1024

TK = 256
VT_ROWS = 80
SEL_ROW = HEAD_DIM
POS_ROW = HEAD_DIM + N_SEL
N_PIECE = 3
CMP_FEAT = 16
POS_SPLIT = 256

KS_COL, KW_COL, KSW_COL = 0, 2, 4
MAIN_COLS = 6 * LANE
STD_COLS = 5 * LANE
QT_ROWS = N_QH * HEAD_DIM
VT_ROW0 = 2 * QT_ROWS
GT_ROW = VT_ROW0 + 3 * N_GROUPS * HEAD_DIM
GT_ROWS = 32
T_ROWS = GT_ROW + GT_ROWS


def _slopes():
    s = 2.0 ** (-8.0 * (np.arange(N_HEADS) + 1) / N_HEADS)
    nsa = s[0::2].reshape(N_GROUPS, N_REP)
    swa = s[1::2].reshape(N_GROUPS, N_REP)
    return nsa, swa


SLOPES_NSA, SLOPES_SWA = _slopes()


def _bf16_round(x):
    u = np.float32(x).reshape(1).view(np.uint32)
    u = (u + (((u >> 16) & 1) + 0x7FFF)) & np.uint32(0xFFFF0000)
    return float(u.view(np.float32)[0])


def _bf16_pieces(x, n=N_PIECE):
    out, rem = [], float(np.float32(x))
    for _ in range(n):
        p = _bf16_round(rem)
        out.append(p)
        rem = float(np.float32(rem - p))
    return out


_NT = (((1,), (1,)), ((), ()))


def _dot_nt(a, b, **kw):
    return lax.dot_general(a, b, _NT, preferred_element_type=F32, **kw)


def _dot(a, b, **kw):
    return jnp.dot(a, b, preferred_element_type=F32, **kw)


def _rms(v, g):
    return v * lax.rsqrt(jnp.mean(v * v, axis=-1, keepdims=True) + NORM_EPS) * g


def _slope_rows(slope, tq):
    pieces = _bf16_pieces(slope * LOG2E)
    vals = pieces + [p * POS_SPLIT for p in pieces]
    prow = lax.broadcasted_iota(jnp.int32, (LANE - POS_ROW, tq), 0)
    feat = jnp.zeros((LANE - POS_ROW, tq), F32)
    for k, v in enumerate(vals):
        feat = jnp.where(prow == k, v, feat)
    return feat.astype(BF16)


def _inproj_kernel(x_ref, g_ref, w_ref, wT_ref, main_ref, z_ref, qT_ref, qsT_ref, vT_ref, gT_ref,
                   kc_scr, vc_scr, *, tm, seq):
    qscale = LOG2E * HEAD_DIM ** -0.5
    lane = lax.broadcasted_iota(jnp.int32, (TK, LANE), 1)
    lo = lane < HEAD_DIM
    zmid = jnp.zeros((N_SEL, TK), BF16)
    ones_blk = jnp.where(lax.broadcasted_iota(jnp.int32, (VT_ROWS - HEAD_DIM, TK), 0) == 0, 1.0, 0.0).astype(BF16)
    t0 = (pl.program_id(0) * tm) % seq

    for st in range(tm // TK):
        rs = slice(st * TK, (st + 1) * TK)
        a = _rms(x_ref[rs, :], g_ref[...]).astype(BF16)

        res = _dot(a, w_ref[...])
        pos = t0 + st * TK + lax.broadcasted_iota(jnp.int32, (TK, LANE), 0)
        posf = jnp.where((lane >= POS_ROW) & (lane < POS_ROW + N_PIECE), (pos % POS_SPLIT).astype(F32), 0.0)
        posf = jnp.where((lane >= POS_ROW + N_PIECE) & (lane < POS_ROW + 2 * N_PIECE),
                         (pos // POS_SPLIT).astype(F32), posf)
        onehot = jnp.where(lane == pos // SEL_BLOCK + SEL_ROW, 1.0, 0.0)
        for t in range(3):
            blk = res[:, t * LANE:(t + 1) * LANE]
            extra = posf + onehot if 2 * t == KS_COL else posf
            for g, src in enumerate((blk, pltpu.roll(blk, HEAD_DIM, 1))):
                c = 2 * t + g
                main_ref[rs, c * LANE:(c + 1) * LANE] = (jnp.where(lo, src, 0.0) + extra).astype(BF16)
        kc_scr[...] = res[:, 3 * LANE:4 * LANE]
        vc_scr[...] = res[:, 4 * LANE:STD_COLS]
        zr = TK // CMP_STRIDE
        for c in range(CMP_STRIDE):
            for t, scr in enumerate((kc_scr, vc_scr)):
                z_ref[t, 0, st * zr:(st + 1) * zr, c * LANE:(c + 1) * LANE] = (
                    scr[pl.ds(c, zr, stride=CMP_STRIDE), :].astype(BF16))

        resT = _dot_nt(wT_ref[...], a)
        qT_ref[0, :, rs] = (resT[0:QT_ROWS] * qscale).astype(BF16)
        for h in range(N_QH):
            r0 = QT_ROWS + h * HEAD_DIM
            qsT_ref[0, h, :, rs] = jnp.concatenate(
                [(resT[r0:r0 + HEAD_DIM] * qscale).astype(BF16), zmid,
                 _slope_rows(SLOPES_SWA[h // N_REP, h % N_REP], TK)], axis=0)
        for tg in range(3 * N_GROUPS):
            r0 = VT_ROW0 + tg * HEAD_DIM
            vT_ref[0, tg, st, 0:HEAD_DIM, :] = resT[r0:r0 + HEAD_DIM, :].astype(BF16)
            vT_ref[0, tg, st, HEAD_DIM:VT_ROWS, :] = ones_blk
        gT_ref[0, :, rs] = jax.nn.sigmoid(resT[GT_ROW:T_ROWS])


def _inproj(x2, gain, w_p, wT_p, b, seq, tm=1024):
    n = x2.shape[0]
    nt = seq // tm
    return pl.pallas_call(
        functools.partial(_inproj_kernel, tm=tm, seq=seq),
        grid=(n // tm,),
        in_specs=[
            pl.BlockSpec((tm, D_MODEL), lambda i: (i, 0)),
            pl.BlockSpec((1, D_MODEL), lambda i: (0, 0)),
            pl.BlockSpec((D_MODEL, STD_COLS), lambda i: (0, 0)),
            pl.BlockSpec((T_ROWS, D_MODEL), lambda i: (0, 0)),
        ],
        out_specs=[
            pl.BlockSpec((tm, MAIN_COLS), lambda i: (i, 0)),
            pl.BlockSpec((2, 1, tm // CMP_STRIDE, CMP_STRIDE * LANE), lambda i: (0, i // nt, i % nt, 0)),
            pl.BlockSpec((1, QT_ROWS, tm), lambda i: (i // nt, 0, i % nt)),
            pl.BlockSpec((1, N_QH, LANE, tm), lambda i: (i // nt, 0, 0, i % nt)),
            pl.BlockSpec((1, 3 * N_GROUPS, tm // TK, VT_ROWS, TK), lambda i: (i // nt, 0, i % nt, 0, 0)),
            pl.BlockSpec((1, GT_ROWS, tm), lambda i: (i // nt, 0, i % nt)),
        ],
        out_shape=[
            jax.ShapeDtypeStruct((n, MAIN_COLS), BF16),
            jax.ShapeDtypeStruct((2, b, seq // CMP_STRIDE, CMP_STRIDE * LANE), BF16),
            jax.ShapeDtypeStruct((b, QT_ROWS, seq), BF16),
            jax.ShapeDtypeStruct((b, N_QH, LANE, seq), BF16),
            jax.ShapeDtypeStruct((b, 3 * N_GROUPS, seq // TK, VT_ROWS, TK), BF16),
            jax.ShapeDtypeStruct((b, GT_ROWS, seq), F32),
        ],
        scratch_shapes=[pltpu.VMEM((TK, LANE), F32), pltpu.VMEM((TK, LANE), F32)],
        compiler_params=pltpu.CompilerParams(
            dimension_semantics=("arbitrary",), vmem_limit_bytes=VMEM_LIMIT),
        name="inproj",
    )(x2, gain, w_p, wT_p)


def _compress_kernel(z_ref, pe_ref, w1_ref, w2_ref, w2T_ref, o_ref, oT_ref):
    nb, n, k = z_ref.shape[1:]
    z = z_ref[0].reshape(nb * n, k).astype(F32)
    zt = (z + pe_ref[0, 0]).astype(BF16)
    zb = (z + pe_ref[0, 1]).astype(BF16)
    a = _dot(zt, w1_ref[0, 0])
    bm = _dot(zb, w1_ref[0, 1])
    h = a + pltpu.roll(bm, nb * n - 1, 0)
    hg = jax.nn.gelu(h).astype(BF16)
    o = _dot(hg, w2_ref[0])
    lane = lax.broadcasted_iota(jnp.int32, o.shape, 1) % LANE
    nidx = lax.broadcasted_iota(jnp.int32, o.shape, 0) % n
    o = o + jnp.where((lane >= HEAD_DIM) & (lane < HEAD_DIM + N_PIECE), nidx.astype(F32), 0.0)
    for e in range(nb):
        o_ref[0, e] = o[e * n:(e + 1) * n].astype(o_ref.dtype)
        oT_ref[0, e] = _dot_nt(w2T_ref[0], hg[e * n:(e + 1) * n]).astype(oT_ref.dtype)


def _compress(z, pe2, w1e, w2e, w2eT, nb=4):
    _, b, n, k = z.shape
    nb = math.gcd(b, nb)
    return pl.pallas_call(
        _compress_kernel,
        grid=(2, b // nb),
        in_specs=[
            pl.BlockSpec((1, nb, n, k), lambda t, i: (t, i, 0, 0)),
            pl.BlockSpec((1, 2, 1, k), lambda t, i: (t, 0, 0, 0)),
            pl.BlockSpec((1, 2, k, 2 * CMP_HIDDEN), lambda t, i: (t, 0, 0, 0)),
            pl.BlockSpec((1, 2 * CMP_HIDDEN, 2 * LANE), lambda t, i: (t, 0, 0)),
            pl.BlockSpec((1, 2 * LANE, 2 * CMP_HIDDEN), lambda t, i: (t, 0, 0)),
        ],
        out_specs=[
            pl.BlockSpec((1, nb, n, 2 * LANE), lambda t, i: (t, i, 0, 0)),
            pl.BlockSpec((1, nb, 2 * LANE, n), lambda t, i: (t, i, 0, 0)),
        ],
        out_shape=[
            jax.ShapeDtypeStruct((2, b, n, 2 * LANE), BF16),
            jax.ShapeDtypeStruct((2, b, 2 * LANE, n), BF16),
        ],
        compiler_params=pltpu.CompilerParams(
            dimension_semantics=("arbitrary", "arbitrary"), vmem_limit_bytes=VMEM_LIMIT),
        name="compress",
    )(z, pe2, w1e, w2e, w2eT)


def _f32_dot_exact_lhs(a_bf16, x):
    out = None
    rem = x
    for _ in range(N_PIECE):
        piece = rem.astype(BF16)
        rem = rem - piece.astype(F32)
        d = _dot(a_bf16, piece)
        out = d if out is None else out + d
    return out


def _select_mask(score, tq):
    sub = 8
    n_chunk = N_SEL // sub
    chunks = [score[sub * c:sub * (c + 1)] for c in range(n_chunk)]
    ranks = [jnp.zeros((sub, tq), jnp.int32) for _ in range(n_chunk)]
    jrow = lax.broadcasted_iota(jnp.int32, (sub, tq), 0)
    for k in range(N_SEL):
        sk = score[k:k + 1, :]
        for c in range(n_chunk):
            if k < sub * c:
                ahead = sk >= chunks[c]
            elif k >= sub * (c + 1):
                ahead = sk > chunks[c]
            else:
                ahead = (sk > chunks[c]) | ((sk == chunks[c]) & (jrow + sub * c > k))
            ranks[c] = jnp.where(ahead, ranks[c] + 1, ranks[c])
    rank = jnp.concatenate(ranks, axis=0)
    return jnp.where(rank < SEL_TOPN, 0.0, SEL_NEG).astype(BF16)


def _cmp_kernel(qT_ref, k_ref, vT_ref, ocT_ref, qaT_ref, s_scr, *, tq, nsub, n_cmp):
    i = pl.program_id(1)
    n_pad = k_ref.shape[2]
    nn = lax.broadcasted_iota(jnp.int32, (n_pad, tq), 0)
    frow = lax.broadcasted_iota(jnp.int32, (CMP_FEAT, tq), 0)

    jr = lax.broadcasted_iota(jnp.int32, (N_SEL, n_pad), 0)
    nc = lax.broadcasted_iota(jnp.int32, (N_SEL, n_pad), 1)
    ov = ((nc * CMP_STRIDE < jr * SEL_BLOCK + SEL_BLOCK) & (nc * CMP_STRIDE + CMP_BLOCK > jr * SEL_BLOCK)
          & (nc < n_cmp))
    ovT = jnp.where(ov, 1.0, 0.0).astype(BF16)
    jj = lax.broadcasted_iota(jnp.int32, (N_SEL, tq), 0)

    for sub in range(nsub):
        qs = slice(sub * tq, (sub + 1) * tq)
        for g in range(N_GROUPS):
            kc = k_ref[0, 0][:, g * LANE:g * LANE + HEAD_DIM + CMP_FEAT]
            for r in range(N_REP):
                h = N_REP * g + r
                feat = jnp.zeros((CMP_FEAT, tq), F32)
                for k, pc in enumerate(_bf16_pieces(SLOPES_NSA[g, r] * LOG2E)):
                    feat = jnp.where(frow == k, pc * CMP_STRIDE, feat)
                qTa = jnp.concatenate([qT_ref[0, h * HEAD_DIM:(h + 1) * HEAD_DIM, qs], feat.astype(BF16)], axis=0)
                s_scr[sub * N_QH + h] = _dot(kc, qTa)

    for sub in range(nsub):
        qs = slice(sub * tq, (sub + 1) * tq)
        t0 = (i * nsub + sub) * tq
        tt = t0 + lax.broadcasted_iota(jnp.int32, (n_pad, tq), 1)
        maskadd = jnp.where((tt >= nn * CMP_STRIDE + CMP_BLOCK - 1) & (nn < n_cmp), 0.0, NEG)
        colvalid = jnp.where(t0 + lax.broadcasted_iota(jnp.int32, (1, tq), 1) >= CMP_BLOCK - 1, 1.0, 0.0)
        blk_t = (t0 + lax.broadcasted_iota(jnp.int32, (N_SEL, tq), 1)) // SEL_BLOCK
        valid = jj <= blk_t
        forced = (jj == 0) | (jj == blk_t) | (jj == blk_t - 1)

        for g in range(N_GROUPS):
            vcT = vT_ref[0, 0][g * LANE:g * LANE + HEAD_DIM, :]
            psum = jnp.zeros((n_pad, tq), F32)
            qTs = []
            for r in range(N_REP):
                h = N_REP * g + r
                qTs.append(qT_ref[0, h * HEAD_DIM:(h + 1) * HEAD_DIM, qs])
                sm = s_scr[sub * N_QH + h] + maskadd
                m = jnp.max(sm, axis=0, keepdims=True)
                e = jnp.exp2(sm - m)
                den = jnp.sum(e, axis=0, keepdims=True)
                p = e * (colvalid / den)
                ocT_ref[0, h * HEAD_DIM:(h + 1) * HEAD_DIM, qs] = _dot(vcT, p.astype(BF16))
                psum = psum + p

            imp = _f32_dot_exact_lhs(ovT, psum)
            score = jnp.where(valid, imp, -jnp.inf)
            score = jnp.where(forced & valid, jnp.inf, score)
            negm = _select_mask(score, tq)
            for r in range(N_REP):
                qaT_ref[0, N_REP * g + r, :, qs] = jnp.concatenate(
                    [qTs[r], negm, _slope_rows(SLOPES_NSA[g, r], tq)], axis=0)


def _cmp(qT, cmp_k, cmp_vT, b, seq, tq=256, nsub=4):
    ts = tq * nsub
    nq = seq // ts
    n_cmp = (seq - CMP_BLOCK) // CMP_STRIDE + 1
    n_pad = cmp_k.shape[2]
    return pl.pallas_call(
        functools.partial(_cmp_kernel, tq=tq, nsub=nsub, n_cmp=n_cmp),
        grid=(b, nq),
        in_specs=[
            pl.BlockSpec((1, QT_ROWS, ts), lambda bi, i: (bi, 0, i)),
            pl.BlockSpec((1, 1, n_pad, 2 * LANE), lambda bi, i: (0, bi, 0, 0)),
            pl.BlockSpec((1, 1, 2 * LANE, n_pad), lambda bi, i: (1, bi, 0, 0)),
        ],
        out_specs=[
            pl.BlockSpec((1, QT_ROWS, ts), lambda bi, i: (bi, 0, i)),
            pl.BlockSpec((1, N_QH, LANE, ts), lambda bi, i: (bi, 0, 0, i)),
        ],
        out_shape=[
            jax.ShapeDtypeStruct((b, QT_ROWS, seq), F32),
            jax.ShapeDtypeStruct((b, N_QH, LANE, seq), BF16),
        ],
        scratch_shapes=[pltpu.VMEM((nsub * N_QH, n_pad, tq), F32)],
        compiler_params=pltpu.CompilerParams(
            dimension_semantics=("arbitrary", "arbitrary"), vmem_limit_bytes=VMEM_LIMIT),
        name="cmp",
    )(qT, cmp_k, cmp_vT)


SLC_LAG = 6
SLC_BUFS = SLC_LAG // N_QH + 1

HALF = TK // 2
_TRI = {0: (slice(0, HALF), slice(HALF, TK), slice(HALF, TK)),
        1: (slice(HALF, TK), slice(0, HALF), slice(0, HALF))}


def _score_phase(qT_ref, k_tiles, mask_id, caus, s_buf, mt_buf, heads):
    for g in range(N_GROUPS):
        for r in range(N_REP):
            h = N_REP * g + r
            if h not in heads:
                continue
            qT = qT_ref[0, h]
            if mask_id is None:
                s = _dot(k_tiles[g], qT)
                s_buf[h] = s
                mt_buf[h] = jnp.max(s, axis=0, keepdims=True)
                continue
            fr, hr, hl = _TRI[mask_id]
            s_full = _dot(k_tiles[g][fr], qT) + caus[mask_id, fr, :]
            s_half = _dot(k_tiles[g][hr], qT[:, hl]) + caus[mask_id, hr, hl]
            s_buf[h, fr, :] = s_full
            s_buf[h, hr, hl] = s_half
            m_half = jnp.max(s_half, axis=0, keepdims=True)
            filler = jnp.full((1, HALF), -3e38, F32)
            m_half = jnp.concatenate([filler, m_half] if hl.start else [m_half, filler], axis=1)
            mt_buf[h] = jnp.maximum(jnp.max(s_full, axis=0, keepdims=True), m_half)


def _value_phase(vT_tiles, mask_id, s_buf, mt_buf, m_scr, acc_scr, heads):
    for g in range(N_GROUPS):
        for r in range(N_REP):
            h = N_REP * g + r
            if h not in heads:
                continue
            m_old = m_scr[h]
            m_new = jnp.maximum(m_old, mt_buf[h])
            alpha = jnp.exp2(m_old - m_new)
            if mask_id is None:
                pT = jnp.exp2(s_buf[h] - m_new).astype(BF16)
                upd = _dot(vT_tiles[g], pT)
            else:
                fr, hr, hl = _TRI[mask_id]
                p_full = jnp.exp2(s_buf[h, fr, :] - m_new).astype(BF16)
                p_half = jnp.exp2(s_buf[h, hr, hl] - m_new[:, hl]).astype(BF16)
                u_half = _dot(vT_tiles[g][:, hr], p_half)
                zero = jnp.zeros_like(u_half)
                upd = _dot(vT_tiles[g][:, fr], p_full) + jnp.concatenate(
                    [zero, u_half] if hl.start else [u_half, zero], axis=1)
            acc_scr[h] = alpha * acc_scr[h] + upd
            m_scr[h] = m_new


def _sweep_reset(m_scr, acc_scr):
    m_scr[...] = jnp.full(m_scr.shape, -3e38, F32)
    acc_scr[...] = jnp.zeros(acc_scr.shape, F32)


def _heads_out(o_ref, outs, rows=slice(None)):
    for g in range(N_GROUPS):
        oT = jnp.concatenate(outs[N_REP * g:N_REP * (g + 1)], axis=0)
        o_ref[rows, g * 2 * LANE:(g + 1) * 2 * LANE] = oT.T.astype(o_ref.dtype)


def _slcwin_kernel(qT_ref, ks0, ks1, kw0, kw1, vsT_ref, vwT_ref, gT_ref, ocT_ref,
                   o_ref, caus, s_scr, mt_scr, m_s, acc_s, m_w, acc_w, *, tq, nq):
    tk = TK
    bi = pl.program_id(0)
    i = pl.program_id(1)
    assert NSA_WINDOW // tk == 2 and tq == tk

    @pl.when((bi == 0) & (i == 0))
    def _init():
        kk = lax.broadcasted_iota(jnp.int32, (tk, tq), 0)
        qq = lax.broadcasted_iota(jnp.int32, (tk, tq), 1)
        caus[0] = jnp.where(kk <= qq, 0.0, NEG)
        caus[1] = jnp.where(kk > qq, 0.0, NEG)

    ks = (ks0, ks1)
    kw = (kw0, kw1)

    def run(ii):
        _sweep_reset(m_s, acc_s)
        _sweep_reset(m_w, acc_w)
        stream = []
        for d in (2, 1, 0):
            if ii - d >= 0:
                stream.append((kw, vwT_ref, ii - d, {0: 0, 2: 1}.get(d), m_w, acc_w))
        for j in range(ii + 1):
            stream.append((ks, vsT_ref, j, 0 if j == ii else None, m_s, acc_s))

        def score(p, heads):
            refs, _, idx, mask_id, _, _ = stream[p]
            buf = p % SLC_BUFS
            _score_phase(qT_ref, [refs[g][idx * tk:(idx + 1) * tk, :] for g in range(N_GROUPS)],
                         mask_id, caus, s_scr.at[buf], mt_scr.at[buf], heads)

        def value(p, heads):
            _, vref, idx, mask_id, m_scr, acc_scr = stream[p]
            buf = p % SLC_BUFS
            _value_phase([vref[0, g, idx] for g in range(N_GROUPS)], mask_id,
                         s_scr.at[buf], mt_scr.at[buf], m_scr, acc_scr, heads)

        n_items = len(stream) * N_QH
        for n in range(n_items + SLC_LAG):
            if n < n_items:
                score(n // N_QH, (n % N_QH,))
            if n >= SLC_LAG:
                value((n - SLC_LAG) // N_QH, ((n - SLC_LAG) % N_QH,))

        gT = gT_ref[0]
        outs = []
        for h in range(N_QH):
            gc = h * N_BRANCH
            f_slc = gT[gc + 1:gc + 2] * (1.0 / acc_s[h, HEAD_DIM:HEAD_DIM + 1, :])
            f_win = gT[gc + 2:gc + 3] * (1.0 / acc_w[h, HEAD_DIM:HEAD_DIM + 1, :])
            outs.append(gT[gc:gc + 1] * ocT_ref[0, h * HEAD_DIM:(h + 1) * HEAD_DIM, :]
                        + f_slc * acc_s[h, 0:HEAD_DIM, :]
                        + f_win * acc_w[h, 0:HEAD_DIM, :])
        _heads_out(o_ref, outs)

    for ii in range(nq):
        pl.when(i == ii)(functools.partial(run, ii))


def _slcwin(qaT, main, vT, gT, ocT, b, seq, tq=TK):
    nq = seq // tq
    n = b * seq

    def col(c):
        return pl.BlockSpec((seq, LANE), lambda bi, i, c=c: (bi, c))

    return pl.pallas_call(
        functools.partial(_slcwin_kernel, tq=tq, nq=nq),
        grid=(b, nq),
        in_specs=[
            pl.BlockSpec((1, N_QH, LANE, tq), lambda bi, i: (bi, 0, 0, i)),
            col(KS_COL), col(KS_COL + 1), col(KW_COL), col(KW_COL + 1),
            pl.BlockSpec((1, N_GROUPS, seq // TK, VT_ROWS, TK), lambda bi, i: (bi, 0, 0, 0, 0)),
            pl.BlockSpec((1, N_GROUPS, seq // TK, VT_ROWS, TK), lambda bi, i: (bi, 1, 0, 0, 0)),
            pl.BlockSpec((1, GT_ROWS, tq), lambda bi, i: (bi, 0, i)),
            pl.BlockSpec((1, QT_ROWS, tq), lambda bi, i: (bi, 0, i)),
        ],
        out_specs=pl.BlockSpec((tq, 4 * LANE), lambda bi, i: (bi * nq + i, 0)),
        out_shape=jax.ShapeDtypeStruct((n, 4 * LANE), BF16),
        scratch_shapes=[
            pltpu.VMEM((2, TK, tq), F32),
            pltpu.VMEM((SLC_BUFS, N_QH, TK, tq), F32),
            pltpu.VMEM((SLC_BUFS, N_QH, 1, tq), F32),
            pltpu.VMEM((N_QH, 1, tq), F32),
            pltpu.VMEM((N_QH, VT_ROWS, tq), F32),
            pltpu.VMEM((N_QH, 1, tq), F32),
            pltpu.VMEM((N_QH, VT_ROWS, tq), F32),
        ],
        compiler_params=pltpu.CompilerParams(
            dimension_semantics=("arbitrary", "arbitrary"), vmem_limit_bytes=VMEM_LIMIT),
        name="slcwin",
    )(qaT, main, main, main, main, vT, vT, gT, ocT)


SWA_AHEAD = 3
SWA_BUFS = SWA_AHEAD + 1

def _swa_kernel(sink_ref, qT_ref, k0, k1, vT_ref, o_ref, tri, mab, s_far, s_ab, s_c, *, tq, nsub):
    bi = pl.program_id(0)
    i = pl.program_id(1)
    w = SWA_WINDOW
    assert tq == 2 * w and w == LANE

    @pl.when((bi == 0) & (i == 0))
    def _init():
        kk = lax.broadcasted_iota(jnp.int32, (w, w), 0)
        qq = lax.broadcasted_iota(jnp.int32, (w, w), 1)
        causal = jnp.where(kk <= qq, 0.0, NEG)
        far = jnp.where(kk > qq, 0.0, NEG)
        tri[0] = far
        tri[1] = causal
        tri[2] = jnp.full((w, w), NEG, F32)
        mab[...] = jnp.concatenate([causal, far], axis=1)

    kk_ref = (k0, k1)
    mts, sinks, outs = {}, {}, {}

    def geometry(sub):
        tile = i * nsub + sub
        return tile, tile * tq

    def score(n):
        sub, h = divmod(n, N_QH)
        g, r = divmod(h, N_REP)
        tile, t0 = geometry(sub)
        far0 = pl.multiple_of(jnp.maximum(t0 - w, 0), w)
        main0 = pl.multiple_of(t0, tq)
        buf = n % SWA_BUFS
        qaT = qT_ref[0, h, :, sub * tq:(sub + 1) * tq]
        sf = _dot(kk_ref[g][pl.ds(far0, w), :], qaT[:, 0:w]) + tri[jnp.where(tile == 0, 2, 0)]
        sab = _dot(kk_ref[g][pl.ds(main0, w), :], qaT) + mab[...]
        sc = _dot(kk_ref[g][pl.ds(main0 + w, w), :], qaT[:, w:tq]) + tri[1]
        s_far[buf] = sf
        s_ab[buf] = sab
        s_c[buf] = sc
        tpos = (t0 + lax.broadcasted_iota(jnp.int32, (1, tq), 1)).astype(F32)
        sink = (sink_ref[g, r] * LOG2E) + float(SLOPES_SWA[g, r] * LOG2E) * tpos
        m_side = jnp.concatenate([jnp.max(sf, axis=0, keepdims=True), jnp.max(sc, axis=0, keepdims=True)], axis=1)
        mts[n] = jnp.maximum(jnp.maximum(jnp.max(sab, axis=0, keepdims=True), m_side), sink)
        sinks[n] = sink

    def value(n):
        sub, h = divmod(n, N_QH)
        g = h // N_REP
        tile, _ = geometry(sub)
        buf = n % SWA_BUFS
        vT_far = vT_ref[0, g, jnp.maximum(tile - 1, 0)][:, w:2 * w]
        vT_main = vT_ref[0, g, tile]
        m = mts.pop(n)
        pf = jnp.exp2(s_far[buf] - m[:, 0:w]).astype(BF16)
        pab = jnp.exp2(s_ab[buf] - m).astype(BF16)
        pc = jnp.exp2(s_c[buf] - m[:, w:tq]).astype(BF16)
        acc = _dot(vT_main[:, 0:w], pab) + jnp.concatenate([_dot(vT_far, pf), _dot(vT_main[:, w:tq], pc)], axis=1)
        den = acc[HEAD_DIM:HEAD_DIM + 1, :] + jnp.exp2(sinks.pop(n) - m)
        outs[n] = acc[0:HEAD_DIM, :] * (1.0 / den)
        if h == N_QH - 1:
            _heads_out(o_ref, [outs.pop(sub * N_QH + hh) for hh in range(N_QH)], rows=slice(sub * tq, (sub + 1) * tq))

    n_items = nsub * N_QH
    for n in range(n_items + SWA_AHEAD):
        if n < n_items:
            score(n)
        if n >= SWA_AHEAD:
            value(n - SWA_AHEAD)


def _swa(sinks, qsT, main, vT, b, seq, tq=TK, nsub=4):
    ts = tq * nsub
    nq = seq // ts
    n = b * seq

    def col(c):
        return pl.BlockSpec((seq, LANE), lambda bi, i, c=c: (bi, c))

    return pl.pallas_call(
        functools.partial(_swa_kernel, tq=tq, nsub=nsub),
        grid=(b, nq),
        in_specs=[
            pl.BlockSpec(memory_space=pltpu.SMEM),
            pl.BlockSpec((1, N_QH, LANE, ts), lambda bi, i: (bi, 0, 0, i)),
            col(KSW_COL), col(KSW_COL + 1),
            pl.BlockSpec((1, N_GROUPS, seq // TK, VT_ROWS, TK), lambda bi, i: (bi, 2, 0, 0, 0)),
        ],
        out_specs=pl.BlockSpec((ts, 4 * LANE), lambda bi, i: (bi * nq + i, 0)),
        out_shape=jax.ShapeDtypeStruct((n, 4 * LANE), BF16),
        scratch_shapes=[
            pltpu.VMEM((3, SWA_WINDOW, SWA_WINDOW), F32),
            pltpu.VMEM((SWA_WINDOW, tq), F32),
            pltpu.VMEM((SWA_BUFS, SWA_WINDOW, SWA_WINDOW), F32),
            pltpu.VMEM((SWA_BUFS, SWA_WINDOW, tq), F32),
            pltpu.VMEM((SWA_BUFS, SWA_WINDOW, SWA_WINDOW), F32),
        ],
        compiler_params=pltpu.CompilerParams(
            dimension_semantics=("arbitrary", "arbitrary"), vmem_limit_bytes=VMEM_LIMIT),
        name="swa",
    )(sinks, qsT, main, main, vT)


def _outmlp_kernel(on_ref, os_ref, x_ref, wo_ref, g2_ref, g3_ref, wu_ref, wd_ref, g4_ref, o_ref, *, ff_chunk, n_sub):
    half = N_HEADS * HEAD_DIM // 2
    tm = x_ref.shape[0]
    subs = [slice(k * tm // n_sub, (k + 1) * tm // n_sub) for k in range(n_sub)]
    mixes = [_dot(on_ref[rs, :], wo_ref[0:half, :]) + _dot(os_ref[rs, :], wo_ref[half:2 * half, :]) for rs in subs]
    for rs, mix in zip(subs, mixes):
        h1 = x_ref[rs, :] + _rms(mix, g2_ref[...])
        m = _rms(h1, g3_ref[...]).astype(BF16)
        acc = jnp.zeros(h1.shape, F32)
        for c in range(D_FF // ff_chunk):
            u = _dot(m, wu_ref[:, c * ff_chunk:(c + 1) * ff_chunk])
            u = jnp.square(jnp.maximum(u, 0.0)).astype(BF16)
            acc = acc + _dot(u, wd_ref[c * ff_chunk:(c + 1) * ff_chunk, :])
        o_ref[rs, :] = h1 + _rms(acc, g4_ref[...])


def _outmlp(onsa, oswa, x2, wo, g2, g3, wu, wd, g4, tm=1024, ff_chunk=1024, n_sub=4):
    n = x2.shape[0]

    def const(shape):
        return pl.BlockSpec(shape, lambda i: (0, 0), pipeline_mode=pl.Buffered(1))

    return pl.pallas_call(
        functools.partial(_outmlp_kernel, ff_chunk=ff_chunk, n_sub=n_sub),
        grid=(n // tm,),
        in_specs=[
            pl.BlockSpec((tm, 4 * LANE), lambda i: (i, 0)),
            pl.BlockSpec((tm, 4 * LANE), lambda i: (i, 0)),
            pl.BlockSpec((tm, D_MODEL), lambda i: (i, 0)),
            const((D_MODEL, D_MODEL)),
            const((1, D_MODEL)),
            const((1, D_MODEL)),
            const((D_MODEL, D_FF)),
            const((D_FF, D_MODEL)),
            const((1, D_MODEL)),
        ],
        out_specs=pl.BlockSpec((tm, D_MODEL), lambda i: (i, 0)),
        out_shape=jax.ShapeDtypeStruct((n, D_MODEL), F32),
        compiler_params=pltpu.CompilerParams(
            dimension_semantics=("arbitrary",), vmem_limit_bytes=VMEM_LIMIT),
        name="outmlp",
    )(onsa, oswa, x2, wo, g2, g3, wu, wd, g4)


def _layout_w_in(w):
    sizes = [512, 128, 128, 128, 128, 128, 128, N_HEADS // 2 * N_BRANCH, 512, 128, 128]
    offs = np.concatenate([[0], np.cumsum(sizes)])
    q_n, kc, vc, ks, vs, kw, vw, gt, q_s, k_s, v_s = [w[:, offs[k]:offs[k + 1]] for k in range(len(sizes))]
    std = [ks, kw, k_s, kc, vc]
    gt = jnp.concatenate([gt, jnp.zeros((w.shape[0], GT_ROWS - gt.shape[1]), w.dtype)], axis=1)
    tr = jnp.concatenate([q_n, q_s, vs, vw, v_s, gt], axis=1).T
    return jnp.concatenate(std, axis=1).astype(BF16), tr.astype(BF16)


def _layout_w1(w1):
    w = w1.reshape(2, CMP_STRIDE, HEAD_DIM, CMP_HIDDEN)
    z = jnp.zeros_like(w)
    top = jnp.concatenate([w, z], axis=-1)
    bot = jnp.concatenate([z, w], axis=-1)
    e = jnp.stack([top, bot], axis=2)
    return e.reshape(2, CMP_STRIDE * 2 * HEAD_DIM, 2 * CMP_HIDDEN).astype(BF16)


def _layout_w2(w2):
    z = jnp.zeros((CMP_HIDDEN, HEAD_DIM), w2.dtype)
    top = jnp.concatenate([w2, z, z, z], axis=1)
    bot = jnp.concatenate([z, z, w2, z], axis=1)
    return jnp.concatenate([top, bot], axis=0).astype(BF16)


def _layout_pe(pe):
    p = pe.reshape(2, CMP_STRIDE, 1, HEAD_DIM)
    return jnp.broadcast_to(p, (2, CMP_STRIDE, 2, HEAD_DIM)).reshape(2, 1, CMP_STRIDE * 2 * HEAD_DIM)


def kernel(x, norm_mix_pre, w_in, cmp_pe_k, cmp_w1_k, cmp_w2_k, cmp_pe_v, cmp_w1_v, cmp_w2_v,
           sinks, w_out, norm_mix_post, norm_mlp_pre, w_up, w_down, norm_mlp_post):
    b, seq, _ = x.shape
    assert seq // SEL_BLOCK == N_SEL and seq <= POS_SPLIT * 256
    depth = w_in.shape[0]
    h = x.reshape(b * seq, D_MODEL)
    for li in range(depth):
        w_p, wT_p = _layout_w_in(w_in[li])
        main, z, qT, qsT, vT, gT = _inproj(h, norm_mix_pre[li][None], w_p, wT_p, b, seq)
        pe2 = jnp.stack([_layout_pe(cmp_pe_k[li]), _layout_pe(cmp_pe_v[li])])
        w1e = jnp.stack([_layout_w1(cmp_w1_k[li]), _layout_w1(cmp_w1_v[li])])
        w2e = jnp.stack([_layout_w2(cmp_w2_k[li]), _layout_w2(cmp_w2_v[li])])
        cmp_kv, cmp_kvT = _compress(z, pe2, w1e, w2e, jnp.swapaxes(w2e, 1, 2))
        ocT, qaT = _cmp(qT, cmp_kv, cmp_kvT, b, seq)
        onsa = _slcwin(qaT, main, vT, gT, ocT, b, seq)
        oswa = _swa(sinks[li], qsT, main, vT, b, seq)
        h = _outmlp(onsa, oswa, h, w_out[li].astype(BF16), norm_mix_post[li][None], norm_mlp_pre[li][None],
                    w_up[li].astype(BF16), w_down[li].astype(BF16), norm_mlp_post[li][None])
    return h.reshape(b, seq, D_MODEL)
```

```python
import functools
import math

import jax
import jax.numpy as jnp
import numpy as np
from jax import lax
from jax.experimental import pallas as pl
from jax.experimental.pallas import tpu as pltpu

F32 = jnp.float32
BF16 = jnp.bfloat16

D_MODEL = 1024
HEAD_DIM = 64
N_HEADS = 16
N_GROUPS = 2
N_REP = 4
N_QH = N_GROUPS * N_REP
CMP_BLOCK = 32
CMP_STRIDE = 16
CMP_HIDDEN = 4 * HEAD_DIM
SEL_BLOCK = 64
SEL_TOPN = 8
N_SEL = 32
NSA_WINDOW = 512
SWA_WINDOW = 128
D_FF = 4 * D_MODEL
NORM_EPS = 1e-6
N_BRANCH = 3

LANE = 128
LOG2E = math.log2(math.e)
NEG = -1e30
SEL_NEG = -(2.0 ** 100)
VMEM_LIMIT = 56 * 1024 * 1024

TK = 256
VT_ROWS = 80
SEL_ROW = HEAD_DIM
POS_ROW = HEAD_DIM + N_SEL
N_PIECE = 3
CMP_FEAT = 16
POS_SPLIT = 256

KS_COL, KW_COL, KSW_COL = 0, 2, 4
MAIN_COLS = 6 * LANE
STD_COLS = 5 * LANE
QT_ROWS = N_QH * HEAD_DIM
VT_ROW0 = 2 * QT_ROWS
GT_ROW = VT_ROW0 + 3 * N_GROUPS * HEAD_DIM
GT_ROWS = 32
T_ROWS = GT_ROW + GT_ROWS


def _slopes():
    s = 2.0 ** (-8.0 * (np.arange(N_HEADS) + 1) / N_HEADS)
    nsa = s[0::2].reshape(N_GROUPS, N_REP)
    swa = s[1::2].reshape(N_GROUPS, N_REP)
    return nsa, swa


SLOPES_NSA, SLOPES_SWA = _slopes()


def _bf16_round(x):
    u = np.float32(x).reshape(1).view(np.uint32)
    u = (u + (((u >> 16) & 1) + 0x7FFF)) & np.uint32(0xFFFF0000)
    return float(u.view(np.float32)[0])


def _bf16_pieces(x, n=N_PIECE):
    out, rem = [], float(np.float32(x))
    for _ in range(n):
        p = _bf16_round(rem)
        out.append(p)
        rem = float(np.float32(rem - p))
    return out


_NT = (((1,), (1,)), ((), ()))


def _dot_nt(a, b, **kw):
    return lax.dot_general(a, b, _NT, preferred_element_type=F32, **kw)


def _dot(a, b, **kw):
    return jnp.dot(a, b, preferred_element_type=F32, **kw)


def _rms(v, g):
    return v * lax.rsqrt(jnp.mean(v * v, axis=-1, keepdims=True) + NORM_EPS) * g


def _slope_rows(slope, tq):
    pieces = _bf16_pieces(slope * LOG2E)
    vals = pieces + [p * POS_SPLIT for p in pieces]
    prow = lax.broadcasted_iota(jnp.int32, (LANE - POS_ROW, tq), 0)
    feat = jnp.zeros((LANE - POS_ROW, tq), F32)
    for k, v in enumerate(vals):
        feat = jnp.where(prow == k, v, feat)
    return feat.astype(BF16)


def _inproj_kernel(x_ref, g_ref, w_ref, wT_ref, main_ref, z_ref, qT_ref, qsT_ref, vT_ref, gT_ref,
                   kc_scr, vc_scr, *, tm, seq):
    qscale = LOG2E * HEAD_DIM ** -0.5
    lane = lax.broadcasted_iota(jnp.int32, (TK, LANE), 1)
    lo = lane < HEAD_DIM
    zmid = jnp.zeros((N_SEL, TK), BF16)
    ones_blk = jnp.where(lax.broadcasted_iota(jnp.int32, (VT_ROWS - HEAD_DIM, TK), 0) == 0, 1.0, 0.0).astype(BF16)
    t0 = (pl.program_id(0) * tm) % seq

    for st in range(tm // TK):
        rs = slice(st * TK, (st + 1) * TK)
        a = _rms(x_ref[rs, :], g_ref[...]).astype(BF16)

        res = _dot(a, w_ref[...])
        pos = t0 + st * TK + lax.broadcasted_iota(jnp.int32, (TK, LANE), 0)
        posf = jnp.where((lane >= POS_ROW) & (lane < POS_ROW + N_PIECE), (pos % POS_SPLIT).astype(F32), 0.0)
        posf = jnp.where((lane >= POS_ROW + N_PIECE) & (lane < POS_ROW + 2 * N_PIECE),
                         (pos // POS_SPLIT).astype(F32), posf)
        onehot = jnp.where(lane == pos // SEL_BLOCK + SEL_ROW, 1.0, 0.0)
        for t in range(3):
            blk = res[:, t * LANE:(t + 1) * LANE]
            extra = posf + onehot if 2 * t == KS_COL else posf
            for g, src in enumerate((blk, pltpu.roll(blk, HEAD_DIM, 1))):
                c = 2 * t + g
                main_ref[rs, c * LANE:(c + 1) * LANE] = (jnp.where(lo, src, 0.0) + extra).astype(BF16)
        kc_scr[...] = res[:, 3 * LANE:4 * LANE]
        vc_scr[...] = res[:, 4 * LANE:STD_COLS]
        zr = TK // CMP_STRIDE
        for c in range(CMP_STRIDE):
            for t, scr in enumerate((kc_scr, vc_scr)):
                z_ref[t, 0, st * zr:(st + 1) * zr, c * LANE:(c + 1) * LANE] = (
                    scr[pl.ds(c, zr, stride=CMP_STRIDE), :].astype(BF16))

        resT = _dot_nt(wT_ref[...], a)
        qT_ref[0, :, rs] = (resT[0:QT_ROWS] * qscale).astype(BF16)
        for h in range(N_QH):
            r0 = QT_ROWS + h * HEAD_DIM
            qsT_ref[0, h, :, rs] = jnp.concatenate(
                [(resT[r0:r0 + HEAD_DIM] * qscale).astype(BF16), zmid,
                 _slope_rows(SLOPES_SWA[h // N_REP, h % N_REP], TK)], axis=0)
        for tg in range(3 * N_GROUPS):
            r0 = VT_ROW0 + tg * HEAD_DIM
            vT_ref[0, tg, st, 0:HEAD_DIM, :] = resT[r0:r0 + HEAD_DIM, :].astype(BF16)
            vT_ref[0, tg, st, HEAD_DIM:VT_ROWS, :] = ones_blk
        gT_ref[0, :, rs] = jax.nn.sigmoid(resT[GT_ROW:T_ROWS])


def _inproj(x2, gain, w_p, wT_p, b, seq, tm=1024):
    n = x2.shape[0]
    nt = seq // tm
    return pl.pallas_call(
        functools.partial(_inproj_kernel, tm=tm, seq=seq),
        grid=(n // tm,),
        in_specs=[
            pl.BlockSpec((tm, D_MODEL), lambda i: (i, 0)),
            pl.BlockSpec((1, D_MODEL), lambda i: (0, 0)),
            pl.BlockSpec((D_MODEL, STD_COLS), lambda i: (0, 0)),
            pl.BlockSpec((T_ROWS, D_MODEL), lambda i: (0, 0)),
        ],
        out_specs=[
            pl.BlockSpec((tm, MAIN_COLS), lambda i: (i, 0)),
            pl.BlockSpec((2, 1, tm // CMP_STRIDE, CMP_STRIDE * LANE), lambda i: (0, i // nt, i % nt, 0)),
            pl.BlockSpec((1, QT_ROWS, tm), lambda i: (i // nt, 0, i % nt)),
            pl.BlockSpec((1, N_QH, LANE, tm), lambda i: (i // nt, 0, 0, i % nt)),
            pl.BlockSpec((1, 3 * N_GROUPS, tm // TK, VT_ROWS, TK), lambda i: (i // nt, 0, i % nt, 0, 0)),
            pl.BlockSpec((1, GT_ROWS, tm), lambda i: (i // nt, 0, i % nt)),
        ],
        out_shape=[
            jax.ShapeDtypeStruct((n, MAIN_COLS), BF16),
            jax.ShapeDtypeStruct((2, b, seq // CMP_STRIDE, CMP_STRIDE * LANE), BF16),
            jax.ShapeDtypeStruct((b, QT_ROWS, seq), BF16),
            jax.ShapeDtypeStruct((b, N_QH, LANE, seq), BF16),
            jax.ShapeDtypeStruct((b, 3 * N_GROUPS, seq // TK, VT_ROWS, TK), BF16),
            jax.ShapeDtypeStruct((b, GT_ROWS, seq), F32),
        ],
        scratch_shapes=[pltpu.VMEM((TK, LANE), F32), pltpu.VMEM((TK, LANE), F32)],
        compiler_params=pltpu.CompilerParams(
            dimension_semantics=("arbitrary",), vmem_limit_bytes=VMEM_LIMIT),
        name="inproj",
    )(x2, gain, w_p, wT_p)


def _compress_kernel(z_ref, pe_ref, w1_ref, w2_ref, w2T_ref, o_ref, oT_ref):
    nb, n, k = z_ref.shape[1:]
    z = z_ref[0].reshape(nb * n, k).astype(F32)
    zt = (z + pe_ref[0, 0]).astype(BF16)
    zb = (z + pe_ref[0, 1]).astype(BF16)
    a = _dot(zt, w1_ref[0, 0])
    bm = _dot(zb, w1_ref[0, 1])
    h = a + pltpu.roll(bm, nb * n - 1, 0)
    hg = jax.nn.gelu(h).astype(BF16)
    o = _dot(hg, w2_ref[0])
    lane = lax.broadcasted_iota(jnp.int32, o.shape, 1) % LANE
    nidx = lax.broadcasted_iota(jnp.int32, o.shape, 0) % n
    o = o + jnp.where((lane >= HEAD_DIM) & (lane < HEAD_DIM + N_PIECE), nidx.astype(F32), 0.0)
    for e in range(nb):
        o_ref[0, e] = o[e * n:(e + 1) * n].astype(o_ref.dtype)
        oT_ref[0, e] = _dot_nt(w2T_ref[0], hg[e * n:(e + 1) * n]).astype(oT_ref.dtype)


def _compress(z, pe2, w1e, w2e, w2eT, nb=4):
    _, b, n, k = z.shape
    nb = math.gcd(b, nb)
    return pl.pallas_call(
        _compress_kernel,
        grid=(2, b // nb),
        in_specs=[
            pl.BlockSpec((1, nb, n, k), lambda t, i: (t, i, 0, 0)),
            pl.BlockSpec((1, 2, 1, k), lambda t, i: (t, 0, 0, 0)),
            pl.BlockSpec((1, 2, k, 2 * CMP_HIDDEN), lambda t, i: (t, 0, 0, 0)),
            pl.BlockSpec((1, 2 * CMP_HIDDEN, 2 * LANE), lambda t, i: (t, 0, 0)),
            pl.BlockSpec((1, 2 * LANE, 2 * CMP_HIDDEN), lambda t, i: (t, 0, 0)),
        ],
        out_specs=[
            pl.BlockSpec((1, nb, n, 2 * LANE), lambda t, i: (t, i, 0, 0)),
            pl.BlockSpec((1, nb, 2 * LANE, n), lambda t, i: (t, i, 0, 0)),
        ],
        out_shape=[
            jax.ShapeDtypeStruct((2, b, n, 2 * LANE), BF16),
            jax.ShapeDtypeStruct((2, b, 2 * LANE, n), BF16),
        ],
        compiler_params=pltpu.CompilerParams(
            dimension_semantics=("arbitrary", "arbitrary"), vmem_limit_bytes=VMEM_LIMIT),
        name="compress",
    )(z, pe2, w1e, w2e, w2eT)


def _f32_dot_exact_lhs(a_bf16, x):
    out = None
    rem = x
    for _ in range(N_PIECE):
        piece = rem.astype(BF16)
        rem = rem - piece.astype(F32)
        d = _dot(a_bf16, piece)
        out = d if out is None else out + d
    return out


def _select_mask(score, tq, n_blk):
    sub = 8
    n_chunk = -(-n_blk // sub)
    chunks = [score[sub * c:sub * (c + 1)] for c in range(n_chunk)]
    ranks = [jnp.zeros((sub, tq), jnp.int32) for _ in range(n_chunk)]
    jrow = lax.broadcasted_iota(jnp.int32, (sub, tq), 0)
    for k in range(n_blk):
        sk = score[k:k + 1, :]
        for c in range(n_chunk):
            if k < sub * c:
                ahead = sk >= chunks[c]
            elif k >= sub * (c + 1):
                ahead = sk > chunks[c]
            else:
                ahead = (sk > chunks[c]) | ((sk == chunks[c]) & (jrow + sub * c > k))
            ranks[c] = jnp.where(ahead, ranks[c] + 1, ranks[c])
    negm = [jnp.where(r < SEL_TOPN, 0.0, SEL_NEG) for r in ranks]
    negm += [jnp.full((sub, tq), SEL_NEG, F32)] * (N_SEL // sub - n_chunk)
    return jnp.concatenate(negm, axis=0).astype(BF16)


def _cmp_kernel(qT_ref, k_ref, vT_ref, ocT_ref, qaT_ref, s_scr, *, tq, nsub, n_cmp, nq):
    i = pl.program_id(1)
    n_pad = k_ref.shape[2]
    frow = lax.broadcasted_iota(jnp.int32, (CMP_FEAT, tq), 0)
    jj = lax.broadcasted_iota(jnp.int32, (N_SEL, tq), 0)

    def run(ii):
        tiles = []
        for sub in range(nsub):
            t0 = (ii * nsub + sub) * tq
            n_vis = min(n_pad, -(-((t0 + tq) // CMP_STRIDE - 1) // 16) * 16)
            tiles.append((sub, t0, n_vis, (t0 + tq) // SEL_BLOCK))

        for sub, t0, n_vis, _ in tiles:
            qs = slice(sub * tq, (sub + 1) * tq)
            for g in range(N_GROUPS):
                kc = k_ref[0, 0][0:n_vis, g * LANE:g * LANE + HEAD_DIM + CMP_FEAT]
                for r in range(N_REP):
                    h = N_REP * g + r
                    feat = jnp.zeros((CMP_FEAT, tq), F32)
                    for k, pc in enumerate(_bf16_pieces(SLOPES_NSA[g, r] * LOG2E)):
                        feat = jnp.where(frow == k, pc * CMP_STRIDE, feat)
                    qTa = jnp.concatenate([qT_ref[0, h * HEAD_DIM:(h + 1) * HEAD_DIM, qs], feat.astype(BF16)], axis=0)
                    s_scr[sub * N_QH + h, 0:n_vis] = _dot(kc, qTa)

        for sub, t0, n_vis, n_blk in tiles:
            qs = slice(sub * tq, (sub + 1) * tq)
            nn = lax.broadcasted_iota(jnp.int32, (n_vis, tq), 0)
            tt = t0 + lax.broadcasted_iota(jnp.int32, (n_vis, tq), 1)
            maskadd = jnp.where((tt >= nn * CMP_STRIDE + CMP_BLOCK - 1) & (nn < n_cmp), 0.0, NEG)
            colvalid = jnp.where(t0 + lax.broadcasted_iota(jnp.int32, (1, tq), 1) >= CMP_BLOCK - 1, 1.0, 0.0)
            blk_t = (t0 + lax.broadcasted_iota(jnp.int32, (N_SEL, tq), 1)) // SEL_BLOCK
            valid = jj <= blk_t
            forced = (jj == 0) | (jj == blk_t) | (jj == blk_t - 1)
            jr = lax.broadcasted_iota(jnp.int32, (N_SEL, n_vis), 0)
            nc = lax.broadcasted_iota(jnp.int32, (N_SEL, n_vis), 1)
            ov = ((nc * CMP_STRIDE < jr * SEL_BLOCK + SEL_BLOCK) & (nc * CMP_STRIDE + CMP_BLOCK > jr * SEL_BLOCK)
                  & (nc < n_cmp))
            ovT = jnp.where(ov, 1.0, 0.0).astype(BF16)

            for g in range(N_GROUPS):
                vcT = vT_ref[0, 0][g * LANE:g * LANE + HEAD_DIM, 0:n_vis]
                psum = jnp.zeros((n_vis, tq), F32)
                qTs = []
                for r in range(N_REP):
                    h = N_REP * g + r
                    qTs.append(qT_ref[0, h * HEAD_DIM:(h + 1) * HEAD_DIM, qs])
                    sm = s_scr[sub * N_QH + h, 0:n_vis] + maskadd
                    m = jnp.max(sm, axis=0, keepdims=True)
                    e = jnp.exp2(sm - m)
                    den = jnp.sum(e, axis=0, keepdims=True)
                    p = e * (colvalid / den)
                    ocT_ref[0, h * HEAD_DIM:(h + 1) * HEAD_DIM, qs] = _dot(vcT, p.astype(BF16))
                    psum = psum + p

                imp = _f32_dot_exact_lhs(ovT, psum)
                score = jnp.where(valid, imp, -jnp.inf)
                score = jnp.where(forced & valid, jnp.inf, score)
                negm = _select_mask(score, tq, n_blk)
                for r in range(N_REP):
                    qaT_ref[0, N_REP * g + r, :, qs] = jnp.concatenate(
                        [qTs[r], negm, _slope_rows(SLOPES_NSA[g, r], tq)], axis=0)

    for ii in range(nq):
        pl.when(i == ii)(functools.partial(run, ii))


def _cmp(qT, cmp_k, cmp_vT, b, seq, tq=256, nsub=4):
    ts = tq * nsub
    nq = seq // ts
    n_cmp = (seq - CMP_BLOCK) // CMP_STRIDE + 1
    n_pad = cmp_k.shape[2]
    return pl.pallas_call(
        functools.partial(_cmp_kernel, tq=tq, nsub=nsub, n_cmp=n_cmp, nq=nq),
        grid=(b, nq),
        in_specs=[
            pl.BlockSpec((1, QT_ROWS, ts), lambda bi, i: (bi, 0, i)),
            pl.BlockSpec((1, 1, n_pad, 2 * LANE), lambda bi, i: (0, bi, 0, 0)),
            pl.BlockSpec((1, 1, 2 * LANE, n_pad), lambda bi, i: (1, bi, 0, 0)),
        ],
        out_specs=[
            pl.BlockSpec((1, QT_ROWS, ts), lambda bi, i: (bi, 0, i)),
            pl.BlockSpec((1, N_QH, LANE, ts), lambda bi, i: (bi, 0, 0, i)),
        ],
        out_shape=[
            jax.ShapeDtypeStruct((b, QT_ROWS, seq), F32),
            jax.ShapeDtypeStruct((b, N_QH, LANE, seq), BF16),
        ],
        scratch_shapes=[pltpu.VMEM((nsub * N_QH, n_pad, tq), F32)],
        compiler_params=pltpu.CompilerParams(
            dimension_semantics=("arbitrary", "arbitrary"), vmem_limit_bytes=VMEM_LIMIT),
        name="cmp",
    )(qT, cmp_k, cmp_vT)


SLC_LAG = 6
SLC_BUFS = SLC_LAG // N_QH + 1

HALF = TK // 2
_TRI = {0: (slice(0, HALF), slice(HALF, TK), slice(HALF, TK)),
        1: (slice(HALF, TK), slice(0, HALF), slice(0, HALF))}


def _score_phase(qT_ref, k_tiles, mask_id, caus, s_buf, mt_buf, heads):
    for g in range(N_GROUPS):
        for r in range(N_REP):
            h = N_REP * g + r
            if h not in heads:
                continue
            qT = qT_ref[0, h]
            if mask_id is None:
                s = _dot(k_tiles[g], qT)
                s_buf[h] = s
                mt_buf[h] = jnp.max(s, axis=0, keepdims=True)
                continue
            fr, hr, hl = _TRI[mask_id]
            s_full = _dot(k_tiles[g][fr], qT) + caus[mask_id, fr, :]
            s_half = _dot(k_tiles[g][hr], qT[:, hl]) + caus[mask_id, hr, hl]
            s_buf[h, fr, :] = s_full
            s_buf[h, hr, hl] = s_half
            m_half = jnp.max(s_half, axis=0, keepdims=True)
            filler = jnp.full((1, HALF), -3e38, F32)
            m_half = jnp.concatenate([filler, m_half] if hl.start else [m_half, filler], axis=1)
            mt_buf[h] = jnp.maximum(jnp.max(s_full, axis=0, keepdims=True), m_half)


def _value_phase(vT_tiles, mask_id, s_buf, mt_buf, m_scr, acc_scr, heads):
    for g in range(N_GROUPS):
        for r in range(N_REP):
            h = N_REP * g + r
            if h not in heads:
                continue
            m_old = m_scr[h]
            m_new = jnp.maximum(m_old, mt_buf[h])
            alpha = jnp.exp2(m_old - m_new)
            if mask_id is None:
                pT = jnp.exp2(s_buf[h] - m_new).astype(BF16)
                upd = _dot(vT_tiles[g], pT)
            else:
                fr, hr, hl = _TRI[mask_id]
                p_full = jnp.exp2(s_buf[h, fr, :] - m_new).astype(BF16)
                p_half = jnp.exp2(s_buf[h, hr, hl] - m_new[:, hl]).astype(BF16)
                u_half = _dot(vT_tiles[g][:, hr], p_half)
                zero = jnp.zeros_like(u_half)
                upd = _dot(vT_tiles[g][:, fr], p_full) + jnp.concatenate(
                    [zero, u_half] if hl.start else [u_half, zero], axis=1)
            acc_scr[h] = alpha * acc_scr[h] + upd
            m_scr[h] = m_new


def _sweep_reset(m_scr, acc_scr):
    m_scr[...] = jnp.full(m_scr.shape, -3e38, F32)
    acc_scr[...] = jnp.zeros(acc_scr.shape, F32)


def _heads_out(o_ref, outs, rows=slice(None)):
    for g in range(N_GROUPS):
        oT = jnp.concatenate(outs[N_REP * g:N_REP * (g + 1)], axis=0)
        o_ref[rows, g * 2 * LANE:(g + 1) * 2 * LANE] = oT.T.astype(o_ref.dtype)


def _slcwin_kernel(qT_ref, ks0, ks1, kw0, kw1, vsT_ref, vwT_ref, gT_ref, ocT_ref,
                   o_ref, caus, s_scr, mt_scr, m_s, acc_s, m_w, acc_w, *, tq, nq):
    tk = TK
    bi = pl.program_id(0)
    i = pl.program_id(1)
    assert NSA_WINDOW // tk == 2 and tq == tk

    @pl.when((bi == 0) & (i == 0))
    def _init():
        kk = lax.broadcasted_iota(jnp.int32, (tk, tq), 0)
        qq = lax.broadcasted_iota(jnp.int32, (tk, tq), 1)
        caus[0] = jnp.where(kk <= qq, 0.0, NEG)
        caus[1] = jnp.where(kk > qq, 0.0, NEG)

    ks = (ks0, ks1)
    kw = (kw0, kw1)

    def run(ii):
        _sweep_reset(m_s, acc_s)
        _sweep_reset(m_w, acc_w)
        stream = []
        for d in (2, 1, 0):
            if ii - d >= 0:
                stream.append((kw, vwT_ref, ii - d, {0: 0, 2: 1}.get(d), m_w, acc_w))
        for j in range(ii + 1):
            stream.append((ks, vsT_ref, j, 0 if j == ii else None, m_s, acc_s))

        def score(p, heads):
            refs, _, idx, mask_id, _, _ = stream[p]
            buf = p % SLC_BUFS
            _score_phase(qT_ref, [refs[g][idx * tk:(idx + 1) * tk, :] for g in range(N_GROUPS)],
                         mask_id, caus, s_scr.at[buf], mt_scr.at[buf], heads)

        def value(p, heads):
            _, vref, idx, mask_id, m_scr, acc_scr = stream[p]
            buf = p % SLC_BUFS
            _value_phase([vref[0, g, idx] for g in range(N_GROUPS)], mask_id,
                         s_scr.at[buf], mt_scr.at[buf], m_scr, acc_scr, heads)

        n_items = len(stream) * N_QH
        for n in range(n_items + SLC_LAG):
            if n < n_items:
                score(n // N_QH, (n % N_QH,))
            if n >= SLC_LAG:
                value((n - SLC_LAG) // N_QH, ((n - SLC_LAG) % N_QH,))

        gT = gT_ref[0]
        outs = []
        for h in range(N_QH):
            gc = h * N_BRANCH
            f_slc = gT[gc + 1:gc + 2] * (1.0 / acc_s[h, HEAD_DIM:HEAD_DIM + 1, :])
            f_win = gT[gc + 2:gc + 3] * (1.0 / acc_w[h, HEAD_DIM:HEAD_DIM + 1, :])
            outs.append(gT[gc:gc + 1] * ocT_ref[0, h * HEAD_DIM:(h + 1) * HEAD_DIM, :]
                        + f_slc * acc_s[h, 0:HEAD_DIM, :]
                        + f_win * acc_w[h, 0:HEAD_DIM, :])
        _heads_out(o_ref, outs)

    for ii in range(nq):
        pl.when(i == ii)(functools.partial(run, ii))


def _slcwin(qaT, main, vT, gT, ocT, b, seq, tq=TK):
    nq = seq // tq
    n = b * seq

    def col(c):
        return pl.BlockSpec((seq, LANE), lambda bi, i, c=c: (bi, c))

    return pl.pallas_call(
        functools.partial(_slcwin_kernel, tq=tq, nq=nq),
        grid=(b, nq),
        in_specs=[
            pl.BlockSpec((1, N_QH, LANE, tq), lambda bi, i: (bi, 0, 0, i)),
            col(KS_COL), col(KS_COL + 1), col(KW_COL), col(KW_COL + 1),
            pl.BlockSpec((1, N_GROUPS, seq // TK, VT_ROWS, TK), lambda bi, i: (bi, 0, 0, 0, 0)),
            pl.BlockSpec((1, N_GROUPS, seq // TK, VT_ROWS, TK), lambda bi, i: (bi, 1, 0, 0, 0)),
            pl.BlockSpec((1, GT_ROWS, tq), lambda bi, i: (bi, 0, i)),
            pl.BlockSpec((1, QT_ROWS, tq), lambda bi, i: (bi, 0, i)),
        ],
        out_specs=pl.BlockSpec((tq, 4 * LANE), lambda bi, i: (bi * nq + i, 0)),
        out_shape=jax.ShapeDtypeStruct((n, 4 * LANE), BF16),
        scratch_shapes=[
            pltpu.VMEM((2, TK, tq), F32),
            pltpu.VMEM((SLC_BUFS, N_QH, TK, tq), F32),
            pltpu.VMEM((SLC_BUFS, N_QH, 1, tq), F32),
            pltpu.VMEM((N_QH, 1, tq), F32),
            pltpu.VMEM((N_QH, VT_ROWS, tq), F32),
            pltpu.VMEM((N_QH, 1, tq), F32),
            pltpu.VMEM((N_QH, VT_ROWS, tq), F32),
        ],
        compiler_params=pltpu.CompilerParams(
            dimension_semantics=("arbitrary", "arbitrary"), vmem_limit_bytes=VMEM_LIMIT),
        name="slcwin",
    )(qaT, main, main, main, main, vT, vT, gT, ocT)


SWA_AHEAD = 3
SWA_BUFS = SWA_AHEAD + 1

def _swa_kernel(sink_ref, qT_ref, k0, k1, vT_ref, o_ref, tri, mab, s_far, s_ab, s_c, *, tq, nsub):
    bi = pl.program_id(0)
    i = pl.program_id(1)
    w = SWA_WINDOW
    assert tq == 2 * w and w == LANE

    @pl.when((bi == 0) & (i == 0))
    def _init():
        kk = lax.broadcasted_iota(jnp.int32, (w, w), 0)
        qq = lax.broadcasted_iota(jnp.int32, (w, w), 1)
        causal = jnp.where(kk <= qq, 0.0, NEG)
        far = jnp.where(kk > qq, 0.0, NEG)
        tri[0] = far
        tri[1] = causal
        tri[2] = jnp.full((w, w), NEG, F32)
        mab[...] = jnp.concatenate([causal, far], axis=1)

    kk_ref = (k0, k1)
    mts, sinks, outs = {}, {}, {}

    def geometry(sub):
        tile = i * nsub + sub
        return tile, tile * tq

    def score(n):
        sub, h = divmod(n, N_QH)
        g, r = divmod(h, N_REP)
        tile, t0 = geometry(sub)
        far0 = pl.multiple_of(jnp.maximum(t0 - w, 0), w)
        main0 = pl.multiple_of(t0, tq)
        buf = n % SWA_BUFS
        qaT = qT_ref[0, h, :, sub * tq:(sub + 1) * tq]
        sf = _dot(kk_ref[g][pl.ds(far0, w), :], qaT[:, 0:w]) + tri[jnp.where(tile == 0, 2, 0)]
        sab = _dot(kk_ref[g][pl.ds(main0, w), :], qaT) + mab[...]
        sc = _dot(kk_ref[g][pl.ds(main0 + w, w), :], qaT[:, w:tq]) + tri[1]
        s_far[buf] = sf
        s_ab[buf] = sab
        s_c[buf] = sc
        tpos = (t0 + lax.broadcasted_iota(jnp.int32, (1, tq), 1)).astype(F32)
        sink = (sink_ref[g, r] * LOG2E) + float(SLOPES_SWA[g, r] * LOG2E) * tpos
        m_side = jnp.concatenate([jnp.max(sf, axis=0, keepdims=True), jnp.max(sc, axis=0, keepdims=True)], axis=1)
        mts[n] = jnp.maximum(jnp.maximum(jnp.max(sab, axis=0, keepdims=True), m_side), sink)
        sinks[n] = sink

    def value(n):
        sub, h = divmod(n, N_QH)
        g = h // N_REP
        tile, _ = geometry(sub)
        buf = n % SWA_BUFS
        vT_far = vT_ref[0, g, jnp.maximum(tile - 1, 0)][:, w:2 * w]
        vT_main = vT_ref[0, g, tile]
        m = mts.pop(n)
        pf = jnp.exp2(s_far[buf] - m[:, 0:w]).astype(BF16)
        pab = jnp.exp2(s_ab[buf] - m).astype(BF16)
        pc = jnp.exp2(s_c[buf] - m[:, w:tq]).astype(BF16)
        acc = _dot(vT_main[:, 0:w], pab) + jnp.concatenate([_dot(vT_far, pf), _dot(vT_main[:, w:tq], pc)], axis=1)
        den = acc[HEAD_DIM:HEAD_DIM + 1, :] + jnp.exp2(sinks.pop(n) - m)
        outs[n] = acc[0:HEAD_DIM, :] * (1.0 / den)
        if h == N_QH - 1:
            _heads_out(o_ref, [outs.pop(sub * N_QH + hh) for hh in range(N_QH)], rows=slice(sub * tq, (sub + 1) * tq))

    n_items = nsub * N_QH
    for n in range(n_items + SWA_AHEAD):
        if n < n_items:
            score(n)
        if n >= SWA_AHEAD:
            value(n - SWA_AHEAD)


def _swa(sinks, qsT, main, vT, b, seq, tq=TK, nsub=4):
    ts = tq * nsub
    nq = seq // ts
    n = b * seq

    def col(c):
        return pl.BlockSpec((seq, LANE), lambda bi, i, c=c: (bi, c))

    return pl.pallas_call(
        functools.partial(_swa_kernel, tq=tq, nsub=nsub),
        grid=(b, nq),
        in_specs=[
            pl.BlockSpec(memory_space=pltpu.SMEM),
            pl.BlockSpec((1, N_QH, LANE, ts), lambda bi, i: (bi, 0, 0, i)),
            col(KSW_COL), col(KSW_COL + 1),
            pl.BlockSpec((1, N_GROUPS, seq // TK, VT_ROWS, TK), lambda bi, i: (bi, 2, 0, 0, 0)),
        ],
        out_specs=pl.BlockSpec((ts, 4 * LANE), lambda bi, i: (bi * nq + i, 0)),
        out_shape=jax.ShapeDtypeStruct((n, 4 * LANE), BF16),
        scratch_shapes=[
            pltpu.VMEM((3, SWA_WINDOW, SWA_WINDOW), F32),
            pltpu.VMEM((SWA_WINDOW, tq), F32),
            pltpu.VMEM((SWA_BUFS, SWA_WINDOW, SWA_WINDOW), F32),
            pltpu.VMEM((SWA_BUFS, SWA_WINDOW, tq), F32),
            pltpu.VMEM((SWA_BUFS, SWA_WINDOW, SWA_WINDOW), F32),
        ],
        compiler_params=pltpu.CompilerParams(
            dimension_semantics=("arbitrary", "arbitrary"), vmem_limit_bytes=VMEM_LIMIT),
        name="swa",
    )(sinks, qsT, main, main, vT)


def _outmlp_kernel(on_ref, os_ref, x_ref, wo_ref, g2_ref, g3_ref, wu_ref, wd_ref, g4_ref, o_ref, *, ff_chunk, n_sub):
    half = N_HEADS * HEAD_DIM // 2
    tm = x_ref.shape[0]
    subs = [slice(k * tm // n_sub, (k + 1) * tm // n_sub) for k in range(n_sub)]
    mixes = [_dot(on_ref[rs, :], wo_ref[0:half, :]) + _dot(os_ref[rs, :], wo_ref[half:2 * half, :]) for rs in subs]
    for rs, mix in zip(subs, mixes):
        h1 = x_ref[rs, :] + _rms(mix, g2_ref[...])
        m = _rms(h1, g3_ref[...]).astype(BF16)
        acc = jnp.zeros(h1.shape, F32)
        for c in range(D_FF // ff_chunk):
            u = _dot(m, wu_ref[:, c * ff_chunk:(c + 1) * ff_chunk])
            u = jnp.square(jnp.maximum(u, 0.0)).astype(BF16)
            acc = acc + _dot(u, wd_ref[c * ff_chunk:(c + 1) * ff_chunk, :])
        o_ref[rs, :] = h1 + _rms(acc, g4_ref[...])


def _outmlp(onsa, oswa, x2, wo, g2, g3, wu, wd, g4, tm=1024, ff_chunk=1024, n_sub=4):
    n = x2.shape[0]

    def const(shape):
        return pl.BlockSpec(shape, lambda i: (0, 0), pipeline_mode=pl.Buffered(1))

    return pl.pallas_call(
        functools.partial(_outmlp_kernel, ff_chunk=ff_chunk, n_sub=n_sub),
        grid=(n // tm,),
        in_specs=[
            pl.BlockSpec((tm, 4 * LANE), lambda i: (i, 0)),
            pl.BlockSpec((tm, 4 * LANE), lambda i: (i, 0)),
            pl.BlockSpec((tm, D_MODEL), lambda i: (i, 0)),
            const((D_MODEL, D_MODEL)),
            const((1, D_MODEL)),
            const((1, D_MODEL)),
            const((D_MODEL, D_FF)),
            const((D_FF, D_MODEL)),
            const((1, D_MODEL)),
        ],
        out_specs=pl.BlockSpec((tm, D_MODEL), lambda i: (i, 0)),
        out_shape=jax.ShapeDtypeStruct((n, D_MODEL), F32),
        compiler_params=pltpu.CompilerParams(
            dimension_semantics=("arbitrary",), vmem_limit_bytes=VMEM_LIMIT),
        name="outmlp",
    )(onsa, oswa, x2, wo, g2, g3, wu, wd, g4)


def _layout_w_in(w):
    sizes = [512, 128, 128, 128, 128, 128, 128, N_HEADS // 2 * N_BRANCH, 512, 128, 128]
    offs = np.concatenate([[0], np.cumsum(sizes)])
    q_n, kc, vc, ks, vs, kw, vw, gt, q_s, k_s, v_s = [w[:, offs[k]:offs[k + 1]] for k in range(len(sizes))]
    std = [ks, kw, k_s, kc, vc]
    gt = jnp.concatenate([gt, jnp.zeros((w.shape[0], GT_ROWS - gt.shape[1]), w.dtype)], axis=1)
    tr = jnp.concatenate([q_n, q_s, vs, vw, v_s, gt], axis=1).T
    return jnp.concatenate(std, axis=1).astype(BF16), tr.astype(BF16)


def _layout_w1(w1):
    w = w1.reshape(2, CMP_STRIDE, HEAD_DIM, CMP_HIDDEN)
    z = jnp.zeros_like(w)
    top = jnp.concatenate([w, z], axis=-1)
    bot = jnp.concatenate([z, w], axis=-1)
    e = jnp.stack([top, bot], axis=2)
    return e.reshape(2, CMP_STRIDE * 2 * HEAD_DIM, 2 * CMP_HIDDEN).astype(BF16)


def _layout_w2(w2):
    z = jnp.zeros((CMP_HIDDEN, HEAD_DIM), w2.dtype)
    top = jnp.concatenate([w2, z, z, z], axis=1)
    bot = jnp.concatenate([z, z, w2, z], axis=1)
    return jnp.concatenate([top, bot], axis=0).astype(BF16)


def _layout_pe(pe):
    p = pe.reshape(2, CMP_STRIDE, 1, HEAD_DIM)
    return jnp.broadcast_to(p, (2, CMP_STRIDE, 2, HEAD_DIM)).reshape(2, 1, CMP_STRIDE * 2 * HEAD_DIM)


def kernel(x, norm_mix_pre, w_in, cmp_pe_k, cmp_w1_k, cmp_w2_k, cmp_pe_v, cmp_w1_v, cmp_w2_v,
           sinks, w_out, norm_mix_post, norm_mlp_pre, w_up, w_down, norm_mlp_post):
    b, seq, _ = x.shape
    assert seq // SEL_BLOCK == N_SEL and seq <= POS_SPLIT * 256
    depth = w_in.shape[0]
    h = x.reshape(b * seq, D_MODEL)
    for li in range(depth):
        w_p, wT_p = _layout_w_in(w_in[li])
        main, z, qT, qsT, vT, gT = _inproj(h, norm_mix_pre[li][None], w_p, wT_p, b, seq)
        pe2 = jnp.stack([_layout_pe(cmp_pe_k[li]), _layout_pe(cmp_pe_v[li])])
        w1e = jnp.stack([_layout_w1(cmp_w1_k[li]), _layout_w1(cmp_w1_v[li])])
        w2e = jnp.stack([_layout_w2(cmp_w2_k[li]), _layout_w2(cmp_w2_v[li])])
        cmp_kv, cmp_kvT = _compress(z, pe2, w1e, w2e, jnp.swapaxes(w2e, 1, 2))
        ocT, qaT = _cmp(qT, cmp_kv, cmp_kvT, b, seq)
        onsa = _slcwin(qaT, main, vT, gT, ocT, b, seq)
        oswa = _swa(sinks[li], qsT, main, vT, b, seq)
        h = _outmlp(onsa, oswa, h, w_out[li].astype(BF16), norm_mix_post[li][None], norm_mlp_pre[li][None],
                    w_up[li].astype(BF16), w_down[li].astype(BF16), norm_mlp_post[li][None])
    return h.reshape(b, seq, D_MODEL)
```

```python
import functools
import math

import jax
import jax.numpy as jnp
import numpy as np
from jax import lax
from jax.experimental import pallas as pl
from jax.experimental.pallas import tpu as pltpu

F32 = jnp.float32
BF16 = jnp.bfloat16

D_MODEL = 1024
HEAD_DIM = 64
N_HEADS = 16
N_GROUPS = 2
N_REP = 4
N_QH = N_GROUPS * N_REP
CMP_BLOCK = 32
CMP_STRIDE = 16
CMP_HIDDEN = 4 * HEAD_DIM
SEL_BLOCK = 64
SEL_TOPN = 8
N_SEL = 32
NSA_WINDOW = 512
SWA_WINDOW = 128
D_FF = 4 * D_MODEL
NORM_EPS = 1e-6
N_BRANCH = 3

LANE = 128
LOG2E = math.log2(math.e)
NEG = -1e30
SEL_NEG = -(2.0 ** 100)
VMEM_LIMIT = 56 * 1024 * 1024

TK = 256
VT_ROWS = 80
SEL_ROW = HEAD_DIM
POS_ROW = HEAD_DIM + N_SEL
N_PIECE = 3
CMP_FEAT = 16
POS_SPLIT = 256

KS_COL, KW_COL, KSW_COL = 0, 2, 4
MAIN_COLS = 6 * LANE
STD_COLS = 5 * LANE
QT_ROWS = N_QH * HEAD_DIM
VT_ROW0 = 2 * QT_ROWS
GT_ROW = VT_ROW0 + 3 * N_GROUPS * HEAD_DIM
GT_ROWS = 32
T_ROWS = GT_ROW + GT_ROWS


def _slopes():
    s = 2.0 ** (-8.0 * (np.arange(N_HEADS) + 1) / N_HEADS)
    nsa = s[0::2].reshape(N_GROUPS, N_REP)
    swa = s[1::2].reshape(N_GROUPS, N_REP)
    return nsa, swa


SLOPES_NSA, SLOPES_SWA = _slopes()


def _bf16_round(x):
    u = np.float32(x).reshape(1).view(np.uint32)
    u = (u + (((u >> 16) & 1) + 0x7FFF)) & np.uint32(0xFFFF0000)
    return float(u.view(np.float32)[0])


def _bf16_pieces(x, n=N_PIECE):
    out, rem = [], float(np.float32(x))
    for _ in range(n):
        p = _bf16_round(rem)
        out.append(p)
        rem = float(np.float32(rem - p))
    return out


_NT = (((1,), (1,)), ((), ()))


def _dot_nt(a, b, **kw):
    return lax.dot_general(a, b, _NT, preferred_element_type=F32, **kw)


def _dot(a, b, **kw):
    return jnp.dot(a, b, preferred_element_type=F32, **kw)


def _rms(v, g):
    return v * lax.rsqrt(jnp.mean(v * v, axis=-1, keepdims=True) + NORM_EPS) * g


def _slope_rows(slope, tq):
    pieces = _bf16_pieces(slope * LOG2E)
    vals = pieces + [p * POS_SPLIT for p in pieces]
    prow = lax.broadcasted_iota(jnp.int32, (LANE - POS_ROW, tq), 0)
    feat = jnp.zeros((LANE - POS_ROW, tq), F32)
    for k, v in enumerate(vals):
        feat = jnp.where(prow == k, v, feat)
    return feat.astype(BF16)


def _inproj_kernel(x_ref, g_ref, w_ref, wT_ref, main_ref, z_ref, qT_ref, qsT_ref, vT_ref, gT_ref,
                   kc_scr, vc_scr, *, tm, seq):
    qscale = LOG2E * HEAD_DIM ** -0.5
    lane = lax.broadcasted_iota(jnp.int32, (TK, LANE), 1)
    lo = lane < HEAD_DIM
    zmid = jnp.zeros((N_SEL, TK), BF16)
    ones_blk = jnp.where(lax.broadcasted_iota(jnp.int32, (VT_ROWS - HEAD_DIM, TK), 0) == 0, 1.0, 0.0).astype(BF16)
    t0 = (pl.program_id(0) * tm) % seq

    for st in range(tm // TK):
        rs = slice(st * TK, (st + 1) * TK)
        a = _rms(x_ref[rs, :], g_ref[...]).astype(BF16)

        res = _dot(a, w_ref[...])
        pos = t0 + st * TK + lax.broadcasted_iota(jnp.int32, (TK, LANE), 0)
        posf = jnp.where((lane >= POS_ROW) & (lane < POS_ROW + N_PIECE), (pos % POS_SPLIT).astype(F32), 0.0)
        posf = jnp.where((lane >= POS_ROW + N_PIECE) & (lane < POS_ROW + 2 * N_PIECE),
                         (pos // POS_SPLIT).astype(F32), posf)
        onehot = jnp.where(lane == pos // SEL_BLOCK + SEL_ROW, 1.0, 0.0)
        for t in range(3):
            blk = res[:, t * LANE:(t + 1) * LANE]
            extra = posf + onehot if 2 * t == KS_COL else posf
            for g, src in enumerate((blk, pltpu.roll(blk, HEAD_DIM, 1))):
                c = 2 * t + g
                main_ref[rs, c * LANE:(c + 1) * LANE] = (jnp.where(lo, src, 0.0) + extra).astype(BF16)
        kc_scr[...] = res[:, 3 * LANE:4 * LANE]
        vc_scr[...] = res[:, 4 * LANE:STD_COLS]
        zr = TK // CMP_STRIDE
        for c in range(CMP_STRIDE):
            for t, scr in enumerate((kc_scr, vc_scr)):
                z_ref[t, 0, st * zr:(st + 1) * zr, c * LANE:(c + 1) * LANE] = (
                    scr[pl.ds(c, zr, stride=CMP_STRIDE), :].astype(BF16))

        resT = _dot_nt(wT_ref[...], a)
        qT_ref[0, :, rs] = (resT[0:QT_ROWS] * qscale).astype(BF16)
        for h in range(N_QH):
            r0 = QT_ROWS + h * HEAD_DIM
            qsT_ref[0, h, :, rs] = jnp.concatenate(
                [(resT[r0:r0 + HEAD_DIM] * qscale).astype(BF16), zmid,
                 _slope_rows(SLOPES_SWA[h // N_REP, h % N_REP], TK)], axis=0)
        for tg in range(3 * N_GROUPS):
            r0 = VT_ROW0 + tg * HEAD_DIM
            vT_ref[0, tg, st, 0:HEAD_DIM, :] = resT[r0:r0 + HEAD_DIM, :].astype(BF16)
            vT_ref[0, tg, st, HEAD_DIM:VT_ROWS, :] = ones_blk
        gT_ref[0, :, rs] = jax.nn.sigmoid(resT[GT_ROW:T_ROWS])


def _inproj(x2, gain, w_p, wT_p, b, seq, tm=1024):
    n = x2.shape[0]
    nt = seq // tm
    return pl.pallas_call(
        functools.partial(_inproj_kernel, tm=tm, seq=seq),
        grid=(n // tm,),
        in_specs=[
            pl.BlockSpec((tm, D_MODEL), lambda i: (i, 0)),
            pl.BlockSpec((1, D_MODEL), lambda i: (0, 0)),
            pl.BlockSpec((D_MODEL, STD_COLS), lambda i: (0, 0)),
            pl.BlockSpec((T_ROWS, D_MODEL), lambda i: (0, 0)),
        ],
        out_specs=[
            pl.BlockSpec((tm, MAIN_COLS), lambda i: (i, 0)),
            pl.BlockSpec((2, 1, tm // CMP_STRIDE, CMP_STRIDE * LANE), lambda i: (0, i // nt, i % nt, 0)),
            pl.BlockSpec((1, QT_ROWS, tm), lambda i: (i // nt, 0, i % nt)),
            pl.BlockSpec((1, N_QH, LANE, tm), lambda i: (i // nt, 0, 0, i % nt)),
            pl.BlockSpec((1, 3 * N_GROUPS, tm // TK, VT_ROWS, TK), lambda i: (i // nt, 0, i % nt, 0, 0)),
            pl.BlockSpec((1, GT_ROWS, tm), lambda i: (i // nt, 0, i % nt)),
        ],
        out_shape=[
            jax.ShapeDtypeStruct((n, MAIN_COLS), BF16),
            jax.ShapeDtypeStruct((2, b, seq // CMP_STRIDE, CMP_STRIDE * LANE), BF16),
            jax.ShapeDtypeStruct((b, QT_ROWS, seq), BF16),
            jax.ShapeDtypeStruct((b, N_QH, LANE, seq), BF16),
            jax.ShapeDtypeStruct((b, 3 * N_GROUPS, seq // TK, VT_ROWS, TK), BF16),
            jax.ShapeDtypeStruct((b, GT_ROWS, seq), F32),
        ],
        scratch_shapes=[pltpu.VMEM((TK, LANE), F32), pltpu.VMEM((TK, LANE), F32)],
        compiler_params=pltpu.CompilerParams(
            dimension_semantics=("arbitrary",), vmem_limit_bytes=VMEM_LIMIT),
        name="inproj",
    )(x2, gain, w_p, wT_p)


def _compress_kernel(z_ref, pe_ref, w1_ref, w2_ref, w2T_ref, o_ref, oT_ref):
    nb, n, k = z_ref.shape[1:]
    z = z_ref[0].reshape(nb * n, k).astype(F32)
    zt = (z + pe_ref[0, 0]).astype(BF16)
    zb = (z + pe_ref[0, 1]).astype(BF16)
    a = _dot(zt, w1_ref[0, 0])
    bm = _dot(zb, w1_ref[0, 1])
    h = a + pltpu.roll(bm, nb * n - 1, 0)
    hg = jax.nn.gelu(h).astype(BF16)
    o = _dot(hg, w2_ref[0])
    lane = lax.broadcasted_iota(jnp.int32, o.shape, 1) % LANE
    nidx = lax.broadcasted_iota(jnp.int32, o.shape, 0) % n
    o = o + jnp.where((lane >= HEAD_DIM) & (lane < HEAD_DIM + N_PIECE), nidx.astype(F32), 0.0)
    for e in range(nb):
        o_ref[0, e] = o[e * n:(e + 1) * n].astype(o_ref.dtype)
        oT_ref[0, e] = _dot_nt(w2T_ref[0], hg[e * n:(e + 1) * n]).astype(oT_ref.dtype)


def _compress(z, pe2, w1e, w2e, w2eT, nb=4):
    _, b, n, k = z.shape
    nb = math.gcd(b, nb)
    return pl.pallas_call(
        _compress_kernel,
        grid=(2, b // nb),
        in_specs=[
            pl.BlockSpec((1, nb, n, k), lambda t, i: (t, i, 0, 0)),
            pl.BlockSpec((1, 2, 1, k), lambda t, i: (t, 0, 0, 0)),
            pl.BlockSpec((1, 2, k, 2 * CMP_HIDDEN), lambda t, i: (t, 0, 0, 0)),
            pl.BlockSpec((1, 2 * CMP_HIDDEN, 2 * LANE), lambda t, i: (t, 0, 0)),
            pl.BlockSpec((1, 2 * LANE, 2 * CMP_HIDDEN), lambda t, i: (t, 0, 0)),
        ],
        out_specs=[
            pl.BlockSpec((1, nb, n, 2 * LANE), lambda t, i: (t, i, 0, 0)),
            pl.BlockSpec((1, nb, 2 * LANE, n), lambda t, i: (t, i, 0, 0)),
        ],
        out_shape=[
            jax.ShapeDtypeStruct((2, b, n, 2 * LANE), BF16),
            jax.ShapeDtypeStruct((2, b, 2 * LANE, n), BF16),
        ],
        compiler_params=pltpu.CompilerParams(
            dimension_semantics=("arbitrary", "arbitrary"), vmem_limit_bytes=VMEM_LIMIT),
        name="compress",
    )(z, pe2, w1e, w2e, w2eT)


def _f32_dot_exact_lhs(a_bf16, x):
    out = None
    rem = x
    for _ in range(N_PIECE):
        piece = rem.astype(BF16)
        rem = rem - piece.astype(F32)
        d = _dot(a_bf16, piece)
        out = d if out is None else out + d
    return out


def _select_mask(score, tq, n_blk):
    sub = 8
    n_chunk = -(-n_blk // sub)
    chunks = [score[sub * c:sub * (c + 1)] for c in range(n_chunk)]
    ranks = [jnp.zeros((sub, tq), jnp.int32) for _ in range(n_chunk)]
    jrow = lax.broadcasted_iota(jnp.int32, (sub, tq), 0)
    for k in range(n_blk):
        sk = score[k:k + 1, :]
        for c in range(n_chunk):
            if k < sub * c:
                ahead = sk >= chunks[c]
            elif k >= sub * (c + 1):
                ahead = sk > chunks[c]
            else:
                ahead = (sk > chunks[c]) | ((sk == chunks[c]) & (jrow + sub * c > k))
            ranks[c] = jnp.where(ahead, ranks[c] + 1, ranks[c])
    negm = [jnp.where(r < SEL_TOPN, 0.0, SEL_NEG) for r in ranks]
    negm += [jnp.full((sub, tq), SEL_NEG, F32)] * (N_SEL // sub - n_chunk)
    return jnp.concatenate(negm, axis=0).astype(BF16)


def _cmp_kernel(qT_ref, k_ref, vT_ref, ocT_ref, negm_ref, s_scr, *, tq, nsub, n_cmp, nq):
    i = pl.program_id(1)
    n_pad = k_ref.shape[2]
    frow = lax.broadcasted_iota(jnp.int32, (CMP_FEAT, tq), 0)
    jj = lax.broadcasted_iota(jnp.int32, (N_SEL, tq), 0)

    def run(ii):
        tiles = []
        for sub in range(nsub):
            t0 = (ii * nsub + sub) * tq
            n_vis = min(n_pad, -(-((t0 + tq) // CMP_STRIDE - 1) // 16) * 16)
            tiles.append((sub, t0, n_vis, (t0 + tq) // SEL_BLOCK))

        for sub, t0, n_vis, _ in tiles:
            qs = slice(sub * tq, (sub + 1) * tq)
            for g in range(N_GROUPS):
                kc = k_ref[0, 0][0:n_vis, g * LANE:g * LANE + HEAD_DIM + CMP_FEAT]
                for r in range(N_REP):
                    h = N_REP * g + r
                    feat = jnp.zeros((CMP_FEAT, tq), F32)
                    for k, pc in enumerate(_bf16_pieces(SLOPES_NSA[g, r] * LOG2E)):
                        feat = jnp.where(frow == k, pc * CMP_STRIDE, feat)
                    qTa = jnp.concatenate([qT_ref[0, h * HEAD_DIM:(h + 1) * HEAD_DIM, qs], feat.astype(BF16)], axis=0)
                    s_scr[sub * N_QH + h, 0:n_vis] = _dot(kc, qTa)

        for sub, t0, n_vis, n_blk in tiles:
            qs = slice(sub * tq, (sub + 1) * tq)
            nn = lax.broadcasted_iota(jnp.int32, (n_vis, tq), 0)
            tt = t0 + lax.broadcasted_iota(jnp.int32, (n_vis, tq), 1)
            maskadd = jnp.where((tt >= nn * CMP_STRIDE + CMP_BLOCK - 1) & (nn < n_cmp), 0.0, NEG)
            colvalid = jnp.where(t0 + lax.broadcasted_iota(jnp.int32, (1, tq), 1) >= CMP_BLOCK - 1, 1.0, 0.0)
            blk_t = (t0 + lax.broadcasted_iota(jnp.int32, (N_SEL, tq), 1)) // SEL_BLOCK
            valid = jj <= blk_t
            forced = (jj == 0) | (jj == blk_t) | (jj == blk_t - 1)
            jr = lax.broadcasted_iota(jnp.int32, (N_SEL, n_vis), 0)
            nc = lax.broadcasted_iota(jnp.int32, (N_SEL, n_vis), 1)
            ov = ((nc * CMP_STRIDE < jr * SEL_BLOCK + SEL_BLOCK) & (nc * CMP_STRIDE + CMP_BLOCK > jr * SEL_BLOCK)
                  & (nc < n_cmp))
            ovT = jnp.where(ov, 1.0, 0.0).astype(BF16)

            for g in range(N_GROUPS):
                vcT = vT_ref[0, 0][g * LANE:g * LANE + HEAD_DIM, 0:n_vis]
                psum = jnp.zeros((n_vis, tq), F32)
                for r in range(N_REP):
                    h = N_REP * g + r
                    sm = s_scr[sub * N_QH + h, 0:n_vis] + maskadd
                    m = jnp.max(sm, axis=0, keepdims=True)
                    e = jnp.exp2(sm - m)
                    den = jnp.sum(e, axis=0, keepdims=True)
                    p = e * (colvalid / den)
                    ocT_ref[0, h * HEAD_DIM:(h + 1) * HEAD_DIM, qs] = _dot(vcT, p.astype(BF16)).astype(ocT_ref.dtype)
                    psum = psum + p

                imp = _f32_dot_exact_lhs(ovT, psum)
                score = jnp.where(valid, imp, -jnp.inf)
                score = jnp.where(forced & valid, jnp.inf, score)
                negm_ref[0, g, :, qs] = _select_mask(score, tq, n_blk)

    for ii in range(nq):
        pl.when(i == ii)(functools.partial(run, ii))


def _cmp(qT, cmp_k, cmp_vT, b, seq, tq=256, nsub=4):
    ts = tq * nsub
    nq = seq // ts
    n_cmp = (seq - CMP_BLOCK) // CMP_STRIDE + 1
    n_pad = cmp_k.shape[2]
    return pl.pallas_call(
        functools.partial(_cmp_kernel, tq=tq, nsub=nsub, n_cmp=n_cmp, nq=nq),
        grid=(b, nq),
        in_specs=[
            pl.BlockSpec((1, QT_ROWS, ts), lambda bi, i: (bi, 0, i)),
            pl.BlockSpec((1, 1, n_pad, 2 * LANE), lambda bi, i: (0, bi, 0, 0)),
            pl.BlockSpec((1, 1, 2 * LANE, n_pad), lambda bi, i: (1, bi, 0, 0)),
        ],
        out_specs=[
            pl.BlockSpec((1, QT_ROWS, ts), lambda bi, i: (bi, 0, i)),
            pl.BlockSpec((1, N_GROUPS, N_SEL, ts), lambda bi, i: (bi, 0, 0, i)),
        ],
        out_shape=[
            jax.ShapeDtypeStruct((b, QT_ROWS, seq), BF16),
            jax.ShapeDtypeStruct((b, N_GROUPS, N_SEL, seq), BF16),
        ],
        scratch_shapes=[pltpu.VMEM((nsub * N_QH, n_pad, tq), F32)],
        compiler_params=pltpu.CompilerParams(
            dimension_semantics=("arbitrary", "arbitrary"), vmem_limit_bytes=VMEM_LIMIT),
        name="cmp",
    )(qT, cmp_k, cmp_vT)


SLC_LAG = 6
SLC_BUFS = SLC_LAG // N_QH + 1

HALF = TK // 2
_TRI = {0: (slice(0, HALF), slice(HALF, TK), slice(HALF, TK)),
        1: (slice(HALF, TK), slice(0, HALF), slice(0, HALF))}


def _score_phase(qa_ref, k_tiles, mask_id, caus, s_buf, mt_buf, heads):
    for g in range(N_GROUPS):
        for r in range(N_REP):
            h = N_REP * g + r
            if h not in heads:
                continue
            qT = qa_ref[h]
            if mask_id is None:
                s = _dot(k_tiles[g], qT)
                s_buf[h] = s
                mt_buf[h] = jnp.max(s, axis=0, keepdims=True)
                continue
            fr, hr, hl = _TRI[mask_id]
            s_full = _dot(k_tiles[g][fr], qT) + caus[mask_id, fr, :]
            s_half = _dot(k_tiles[g][hr], qT[:, hl]) + caus[mask_id, hr, hl]
            s_buf[h, fr, :] = s_full
            s_buf[h, hr, hl] = s_half
            m_half = jnp.max(s_half, axis=0, keepdims=True)
            filler = jnp.full((1, HALF), -3e38, F32)
            m_half = jnp.concatenate([filler, m_half] if hl.start else [m_half, filler], axis=1)
            mt_buf[h] = jnp.maximum(jnp.max(s_full, axis=0, keepdims=True), m_half)


def _value_phase(vT_tiles, mask_id, s_buf, mt_buf, m_scr, acc_scr, heads):
    for g in range(N_GROUPS):
        for r in range(N_REP):
            h = N_REP * g + r
            if h not in heads:
                continue
            m_old = m_scr[h]
            m_new = jnp.maximum(m_old, mt_buf[h])
            alpha = jnp.exp2(m_old - m_new)
            if mask_id is None:
                pT = jnp.exp2(s_buf[h] - m_new).astype(BF16)
                upd = _dot(vT_tiles[g], pT)
            else:
                fr, hr, hl = _TRI[mask_id]
                p_full = jnp.exp2(s_buf[h, fr, :] - m_new).astype(BF16)
                p_half = jnp.exp2(s_buf[h, hr, hl] - m_new[:, hl]).astype(BF16)
                u_half = _dot(vT_tiles[g][:, hr], p_half)
                zero = jnp.zeros_like(u_half)
                upd = _dot(vT_tiles[g][:, fr], p_full) + jnp.concatenate(
                    [zero, u_half] if hl.start else [u_half, zero], axis=1)
            acc_scr[h] = alpha * acc_scr[h] + upd
            m_scr[h] = m_new


def _sweep_reset(m_scr, acc_scr):
    m_scr[...] = jnp.full(m_scr.shape, -3e38, F32)
    acc_scr[...] = jnp.zeros(acc_scr.shape, F32)


def _heads_out(o_ref, outs, rows=slice(None)):
    for g in range(N_GROUPS):
        oT = jnp.concatenate(outs[N_REP * g:N_REP * (g + 1)], axis=0)
        o_ref[rows, g * 2 * LANE:(g + 1) * 2 * LANE] = oT.T.astype(o_ref.dtype)


def _slcwin_kernel(qT_ref, negm_ref, ks0, ks1, kw0, kw1, vsT_ref, vwT_ref, gT_ref, ocT_ref,
                   o_ref, caus, qa_scr, s_scr, mt_scr, m_s, acc_s, m_w, acc_w, *, tq, nq):
    tk = TK
    bi = pl.program_id(0)
    i = pl.program_id(1)
    assert NSA_WINDOW // tk == 2 and tq == tk

    @pl.when((bi == 0) & (i == 0))
    def _init():
        kk = lax.broadcasted_iota(jnp.int32, (tk, tq), 0)
        qq = lax.broadcasted_iota(jnp.int32, (tk, tq), 1)
        caus[0] = jnp.where(kk <= qq, 0.0, NEG)
        caus[1] = jnp.where(kk > qq, 0.0, NEG)

    ks = (ks0, ks1)
    kw = (kw0, kw1)

    for h in range(N_QH):
        g, r = divmod(h, N_REP)
        qa_scr[h] = jnp.concatenate([qT_ref[0, h * HEAD_DIM:(h + 1) * HEAD_DIM, :], negm_ref[0, g],
                                     _slope_rows(SLOPES_NSA[g, r], tq)], axis=0)

    def run(ii):
        _sweep_reset(m_s, acc_s)
        _sweep_reset(m_w, acc_w)
        stream = []
        for d in (2, 1, 0):
            if ii - d >= 0:
                stream.append((kw, vwT_ref, ii - d, {0: 0, 2: 1}.get(d), m_w, acc_w))
        for j in range(ii + 1):
            stream.append((ks, vsT_ref, j, 0 if j == ii else None, m_s, acc_s))

        def score(p, heads):
            refs, _, idx, mask_id, _, _ = stream[p]
            buf = p % SLC_BUFS
            _score_phase(qa_scr, [refs[g][idx * tk:(idx + 1) * tk, :] for g in range(N_GROUPS)],
                         mask_id, caus, s_scr.at[buf], mt_scr.at[buf], heads)

        def value(p, heads):
            _, vref, idx, mask_id, m_scr, acc_scr = stream[p]
            buf = p % SLC_BUFS
            _value_phase([vref[0, g, idx] for g in range(N_GROUPS)], mask_id,
                         s_scr.at[buf], mt_scr.at[buf], m_scr, acc_scr, heads)

        n_items = len(stream) * N_QH
        for n in range(n_items + SLC_LAG):
            if n < n_items:
                score(n // N_QH, (n % N_QH,))
            if n >= SLC_LAG:
                value((n - SLC_LAG) // N_QH, ((n - SLC_LAG) % N_QH,))

        gT = gT_ref[0]
        outs = []
        for h in range(N_QH):
            gc = h * N_BRANCH
            f_slc = gT[gc + 1:gc + 2] * (1.0 / acc_s[h, HEAD_DIM:HEAD_DIM + 1, :])
            f_win = gT[gc + 2:gc + 3] * (1.0 / acc_w[h, HEAD_DIM:HEAD_DIM + 1, :])
            outs.append(gT[gc:gc + 1] * ocT_ref[0, h * HEAD_DIM:(h + 1) * HEAD_DIM, :]
                        + f_slc * acc_s[h, 0:HEAD_DIM, :]
                        + f_win * acc_w[h, 0:HEAD_DIM, :])
        _heads_out(o_ref, outs)

    for ii in range(nq):
        pl.when(i == ii)(functools.partial(run, ii))


def _slcwin(qT, negm, main, vT, gT, ocT, b, seq, tq=TK):
    nq = seq // tq
    n = b * seq

    def col(c):
        return pl.BlockSpec((seq, LANE), lambda bi, i, c=c: (bi, c))

    return pl.pallas_call(
        functools.partial(_slcwin_kernel, tq=tq, nq=nq),
        grid=(b, nq),
        in_specs=[
            pl.BlockSpec((1, QT_ROWS, tq), lambda bi, i: (bi, 0, i)),
            pl.BlockSpec((1, N_GROUPS, N_SEL, tq), lambda bi, i: (bi, 0, 0, i)),
            col(KS_COL), col(KS_COL + 1), col(KW_COL), col(KW_COL + 1),
            pl.BlockSpec((1, N_GROUPS, seq // TK, VT_ROWS, TK), lambda bi, i: (bi, 0, 0, 0, 0)),
            pl.BlockSpec((1, N_GROUPS, seq // TK, VT_ROWS, TK), lambda bi, i: (bi, 1, 0, 0, 0)),
            pl.BlockSpec((1, GT_ROWS, tq), lambda bi, i: (bi, 0, i)),
            pl.BlockSpec((1, QT_ROWS, tq), lambda bi, i: (bi, 0, i)),
        ],
        out_specs=pl.BlockSpec((tq, 4 * LANE), lambda bi, i: (bi * nq + i, 0)),
        out_shape=jax.ShapeDtypeStruct((n, 4 * LANE), BF16),
        scratch_shapes=[
            pltpu.VMEM((2, TK, tq), F32),
            pltpu.VMEM((N_QH, LANE, tq), BF16),
            pltpu.VMEM((SLC_BUFS, N_QH, TK, tq), F32),
            pltpu.VMEM((SLC_BUFS, N_QH, 1, tq), F32),
            pltpu.VMEM((N_QH, 1, tq), F32),
            pltpu.VMEM((N_QH, VT_ROWS, tq), F32),
            pltpu.VMEM((N_QH, 1, tq), F32),
            pltpu.VMEM((N_QH, VT_ROWS, tq), F32),
        ],
        compiler_params=pltpu.CompilerParams(
            dimension_semantics=("arbitrary", "arbitrary"), vmem_limit_bytes=VMEM_LIMIT),
        name="slcwin",
    )(qT, negm, main, main, main, main, vT, vT, gT, ocT)


SWA_AHEAD = 3
SWA_BUFS = SWA_AHEAD + 1

def _swa_kernel(sink_ref, qT_ref, k0, k1, vT_ref, o_ref, tri, mab, s_far, s_ab, s_c, *, tq, nsub):
    bi = pl.program_id(0)
    i = pl.program_id(1)
    w = SWA_WINDOW
    assert tq == 2 * w and w == LANE

    @pl.when((bi == 0) & (i == 0))
    def _init():
        kk = lax.broadcasted_iota(jnp.int32, (w, w), 0)
        qq = lax.broadcasted_iota(jnp.int32, (w, w), 1)
        causal = jnp.where(kk <= qq, 0.0, NEG)
        far = jnp.where(kk > qq, 0.0, NEG)
        tri[0] = far
        tri[1] = causal
        tri[2] = jnp.full((w, w), NEG, F32)
        mab[...] = jnp.concatenate([causal, far], axis=1)

    kk_ref = (k0, k1)
    mts, sinks, outs = {}, {}, {}

    def geometry(sub):
        tile = i * nsub + sub
        return tile, tile * tq

    def score(n):
        sub, h = divmod(n, N_QH)
        g, r = divmod(h, N_REP)
        tile, t0 = geometry(sub)
        far0 = pl.multiple_of(jnp.maximum(t0 - w, 0), w)
        main0 = pl.multiple_of(t0, tq)
        buf = n % SWA_BUFS
        qaT = qT_ref[0, h, :, sub * tq:(sub + 1) * tq]
        sf = _dot(kk_ref[g][pl.ds(far0, w), :], qaT[:, 0:w]) + tri[jnp.where(tile == 0, 2, 0)]
        sab = _dot(kk_ref[g][pl.ds(main0, w), :], qaT) + mab[...]
        sc = _dot(kk_ref[g][pl.ds(main0 + w, w), :], qaT[:, w:tq]) + tri[1]
        s_far[buf] = sf
        s_ab[buf] = sab
        s_c[buf] = sc
        tpos = (t0 + lax.broadcasted_iota(jnp.int32, (1, tq), 1)).astype(F32)
        sink = (sink_ref[g, r] * LOG2E) + float(SLOPES_SWA[g, r] * LOG2E) * tpos
        m_side = jnp.concatenate([jnp.max(sf, axis=0, keepdims=True), jnp.max(sc, axis=0, keepdims=True)], axis=1)
        mts[n] = jnp.maximum(jnp.maximum(jnp.max(sab, axis=0, keepdims=True), m_side), sink)
        sinks[n] = sink

    def value(n):
        sub, h = divmod(n, N_QH)
        g = h // N_REP
        tile, _ = geometry(sub)
        buf = n % SWA_BUFS
        vT_far = vT_ref[0, g, jnp.maximum(tile - 1, 0)][:, w:2 * w]
        vT_main = vT_ref[0, g, tile]
        m = mts.pop(n)
        pf = jnp.exp2(s_far[buf] - m[:, 0:w]).astype(BF16)
        pab = jnp.exp2(s_ab[buf] - m).astype(BF16)
        pc = jnp.exp2(s_c[buf] - m[:, w:tq]).astype(BF16)
        acc = _dot(vT_main[:, 0:w], pab) + jnp.concatenate([_dot(vT_far, pf), _dot(vT_main[:, w:tq], pc)], axis=1)
        den = acc[HEAD_DIM:HEAD_DIM + 1, :] + jnp.exp2(sinks.pop(n) - m)
        outs[n] = acc[0:HEAD_DIM, :] * (1.0 / den)
        if h == N_QH - 1:
            _heads_out(o_ref, [outs.pop(sub * N_QH + hh) for hh in range(N_QH)], rows=slice(sub * tq, (sub + 1) * tq))

    n_items = nsub * N_QH
    for n in range(n_items + SWA_AHEAD):
        if n < n_items:
            score(n)
        if n >= SWA_AHEAD:
            value(n - SWA_AHEAD)


def _swa(sinks, qsT, main, vT, b, seq, tq=TK, nsub=4):
    ts = tq * nsub
    nq = seq // ts
    n = b * seq

    def col(c):
        return pl.BlockSpec((seq, LANE), lambda bi, i, c=c: (bi, c))

    return pl.pallas_call(
        functools.partial(_swa_kernel, tq=tq, nsub=nsub),
        grid=(b, nq),
        in_specs=[
            pl.BlockSpec(memory_space=pltpu.SMEM),
            pl.BlockSpec((1, N_QH, LANE, ts), lambda bi, i: (bi, 0, 0, i)),
            col(KSW_COL), col(KSW_COL + 1),
            pl.BlockSpec((1, N_GROUPS, seq // TK, VT_ROWS, TK), lambda bi, i: (bi, 2, 0, 0, 0)),
        ],
        out_specs=pl.BlockSpec((ts, 4 * LANE), lambda bi, i: (bi * nq + i, 0)),
        out_shape=jax.ShapeDtypeStruct((n, 4 * LANE), BF16),
        scratch_shapes=[
            pltpu.VMEM((3, SWA_WINDOW, SWA_WINDOW), F32),
            pltpu.VMEM((SWA_WINDOW, tq), F32),
            pltpu.VMEM((SWA_BUFS, SWA_WINDOW, SWA_WINDOW), F32),
            pltpu.VMEM((SWA_BUFS, SWA_WINDOW, tq), F32),
            pltpu.VMEM((SWA_BUFS, SWA_WINDOW, SWA_WINDOW), F32),
        ],
        compiler_params=pltpu.CompilerParams(
            dimension_semantics=("arbitrary", "arbitrary"), vmem_limit_bytes=VMEM_LIMIT),
        name="swa",
    )(sinks, qsT, main, main, vT)


def _outmlp_kernel(on_ref, os_ref, x_ref, wo_ref, g2_ref, g3_ref, wu_ref, wd_ref, g4_ref, o_ref, *, ff_chunk, n_sub):
    half = N_HEADS * HEAD_DIM // 2
    tm = x_ref.shape[0]
    subs = [slice(k * tm // n_sub, (k + 1) * tm // n_sub) for k in range(n_sub)]
    mixes = [_dot(on_ref[rs, :], wo_ref[0:half, :]) + _dot(os_ref[rs, :], wo_ref[half:2 * half, :]) for rs in subs]
    for rs, mix in zip(subs, mixes):
        h1 = x_ref[rs, :] + _rms(mix, g2_ref[...])
        m = _rms(h1, g3_ref[...]).astype(BF16)
        acc = jnp.zeros(h1.shape, F32)
        for c in range(D_FF // ff_chunk):
            u = _dot(m, wu_ref[:, c * ff_chunk:(c + 1) * ff_chunk])
            u = jnp.square(jnp.maximum(u, 0.0)).astype(BF16)
            acc = acc + _dot(u, wd_ref[c * ff_chunk:(c + 1) * ff_chunk, :])
        o_ref[rs, :] = h1 + _rms(acc, g4_ref[...])


def _outmlp(onsa, oswa, x2, wo, g2, g3, wu, wd, g4, tm=1024, ff_chunk=1024, n_sub=4):
    n = x2.shape[0]

    def const(shape):
        return pl.BlockSpec(shape, lambda i: (0, 0), pipeline_mode=pl.Buffered(1))

    return pl.pallas_call(
        functools.partial(_outmlp_kernel, ff_chunk=ff_chunk, n_sub=n_sub),
        grid=(n // tm,),
        in_specs=[
            pl.BlockSpec((tm, 4 * LANE), lambda i: (i, 0)),
            pl.BlockSpec((tm, 4 * LANE), lambda i: (i, 0)),
            pl.BlockSpec((tm, D_MODEL), lambda i: (i, 0)),
            const((D_MODEL, D_MODEL)),
            const((1, D_MODEL)),
            const((1, D_MODEL)),
            const((D_MODEL, D_FF)),
            const((D_FF, D_MODEL)),
            const((1, D_MODEL)),
        ],
        out_specs=pl.BlockSpec((tm, D_MODEL), lambda i: (i, 0)),
        out_shape=jax.ShapeDtypeStruct((n, D_MODEL), F32),
        compiler_params=pltpu.CompilerParams(
            dimension_semantics=("arbitrary",), vmem_limit_bytes=VMEM_LIMIT),
        name="outmlp",
    )(onsa, oswa, x2, wo, g2, g3, wu, wd, g4)


def _layout_w_in(w):
    sizes = [512, 128, 128, 128, 128, 128, 128, N_HEADS // 2 * N_BRANCH, 512, 128, 128]
    offs = np.concatenate([[0], np.cumsum(sizes)])
    q_n, kc, vc, ks, vs, kw, vw, gt, q_s, k_s, v_s = [w[:, offs[k]:offs[k + 1]] for k in range(len(sizes))]
    std = [ks, kw, k_s, kc, vc]
    gt = jnp.concatenate([gt, jnp.zeros((w.shape[0], GT_ROWS - gt.shape[1]), w.dtype)], axis=1)
    tr = jnp.concatenate([q_n, q_s, vs, vw, v_s, gt], axis=1).T
    return jnp.concatenate(std, axis=1).astype(BF16), tr.astype(BF16)


def _layout_w1(w1):
    w = w1.reshape(2, CMP_STRIDE, HEAD_DIM, CMP_HIDDEN)
    z = jnp.zeros_like(w)
    top = jnp.concatenate([w, z], axis=-1)
    bot = jnp.concatenate([z, w], axis=-1)
    e = jnp.stack([top, bot], axis=2)
    return e.reshape(2, CMP_STRIDE * 2 * HEAD_DIM, 2 * CMP_HIDDEN).astype(BF16)


def _layout_w2(w2):
    z = jnp.zeros((CMP_HIDDEN, HEAD_DIM), w2.dtype)
    top = jnp.concatenate([w2, z, z, z], axis=1)
    bot = jnp.concatenate([z, z, w2, z], axis=1)
    return jnp.concatenate([top, bot], axis=0).astype(BF16)


def _layout_pe(pe):
    p = pe.reshape(2, CMP_STRIDE, 1, HEAD_DIM)
    return jnp.broadcast_to(p, (2, CMP_STRIDE, 2, HEAD_DIM)).reshape(2, 1, CMP_STRIDE * 2 * HEAD_DIM)


def kernel(x, norm_mix_pre, w_in, cmp_pe_k, cmp_w1_k, cmp_w2_k, cmp_pe_v, cmp_w1_v, cmp_w2_v,
           sinks, w_out, norm_mix_post, norm_mlp_pre, w_up, w_down, norm_mlp_post):
    b, seq, _ = x.shape
    assert seq // SEL_BLOCK == N_SEL and seq <= POS_SPLIT * 256
    depth = w_in.shape[0]
    h = x.reshape(b * seq, D_MODEL)
    for li in range(depth):
        w_p, wT_p = _layout_w_in(w_in[li])
        main, z, qT, qsT, vT, gT = _inproj(h, norm_mix_pre[li][None], w_p, wT_p, b, seq)
        pe2 = jnp.stack([_layout_pe(cmp_pe_k[li]), _layout_pe(cmp_pe_v[li])])
        w1e = jnp.stack([_layout_w1(cmp_w1_k[li]), _layout_w1(cmp_w1_v[li])])
        w2e = jnp.stack([_layout_w2(cmp_w2_k[li]), _layout_w2(cmp_w2_v[li])])
        cmp_kv, cmp_kvT = _compress(z, pe2, w1e, w2e, jnp.swapaxes(w2e, 1, 2))
        ocT, negm = _cmp(qT, cmp_kv, cmp_kvT, b, seq)
        onsa = _slcwin(qT, negm, main, vT, gT, ocT, b, seq)
        oswa = _swa(sinks[li], qsT, main, vT, b, seq)
        h = _outmlp(onsa, oswa, h, w_out[li].astype(BF16), norm_mix_post[li][None], norm_mlp_pre[li][None],
                    w_up[li].astype(BF16), w_down[li].astype(BF16), norm_mlp_post[li][None])
    return h.reshape(b, seq, D_MODEL)
```

```python
import functools
import math

import jax
import jax.numpy as jnp
import numpy as np
from jax import lax
from jax.experimental import pallas as pl
from jax.experimental.pallas import tpu as pltpu

F32 = jnp.float32
BF16 = jnp.bfloat16

D_MODEL = 1024
HEAD_DIM = 64
N_HEADS = 16
N_GROUPS = 2
N_REP = 4
N_QH = N_GROUPS * N_REP
CMP_BLOCK = 32
CMP_STRIDE = 16
CMP_HIDDEN = 4 * HEAD_DIM
SEL_BLOCK = 64
SEL_TOPN = 8
N_SEL = 32
NSA_WINDOW = 512
SWA_WINDOW = 128
D_FF = 4 * D_MODEL
NORM_EPS = 1e-6
N_BRANCH = 3

LANE = 128
LOG2E = math.log2(math.e)
NEG = -1e30
SEL_NEG = -(2.0 ** 100)
VMEM_LIMIT = 56 * 1024 * 1024

TK = 256
VT_ROWS = 80
SEL_ROW = HEAD_DIM
POS_ROW = HEAD_DIM + N_SEL
N_PIECE = 3
CMP_FEAT = 16
POS_SPLIT = 256

KS_COL, KW_COL, KSW_COL = 0, 2, 4
MAIN_COLS = 6 * LANE
STD_COLS = 5 * LANE
QT_ROWS = N_QH * HEAD_DIM
VT_ROW0 = 2 * QT_ROWS
GT_ROW = VT_ROW0 + 3 * N_GROUPS * HEAD_DIM
GT_ROWS = 32
T_ROWS = GT_ROW + GT_ROWS


def _slopes():
    s = 2.0 ** (-8.0 * (np.arange(N_HEADS) + 1) / N_HEADS)
    nsa = s[0::2].reshape(N_GROUPS, N_REP)
    swa = s[1::2].reshape(N_GROUPS, N_REP)
    return nsa, swa


SLOPES_NSA, SLOPES_SWA = _slopes()


def _bf16_round(x):
    u = np.float32(x).reshape(1).view(np.uint32)
    u = (u + (((u >> 16) & 1) + 0x7FFF)) & np.uint32(0xFFFF0000)
    return float(u.view(np.float32)[0])


def _bf16_pieces(x, n=N_PIECE):
    out, rem = [], float(np.float32(x))
    for _ in range(n):
        p = _bf16_round(rem)
        out.append(p)
        rem = float(np.float32(rem - p))
    return out


_NT = (((1,), (1,)), ((), ()))


def _dot_nt(a, b, **kw):
    return lax.dot_general(a, b, _NT, preferred_element_type=F32, **kw)


def _dot(a, b, **kw):
    return jnp.dot(a, b, preferred_element_type=F32, **kw)


def _rms(v, g):
    return v * lax.rsqrt(jnp.mean(v * v, axis=-1, keepdims=True) + NORM_EPS) * g


def _slope_rows(slope, tq):
    pieces = _bf16_pieces(slope * LOG2E)
    vals = pieces + [p * POS_SPLIT for p in pieces]
    prow = lax.broadcasted_iota(jnp.int32, (LANE - POS_ROW, tq), 0)
    feat = jnp.zeros((LANE - POS_ROW, tq), F32)
    for k, v in enumerate(vals):
        feat = jnp.where(prow == k, v, feat)
    return feat.astype(BF16)


def _inproj_kernel(x_ref, g_ref, w_ref, wT_ref, main_ref, z_ref, qT_ref, qsT_ref, vT_ref, gT_ref,
                   kc_scr, vc_scr, *, tm, seq):
    qscale = LOG2E * HEAD_DIM ** -0.5
    lane = lax.broadcasted_iota(jnp.int32, (TK, LANE), 1)
    lo = lane < HEAD_DIM
    zmid = jnp.zeros((N_SEL, TK), BF16)
    ones_blk = jnp.where(lax.broadcasted_iota(jnp.int32, (VT_ROWS - HEAD_DIM, TK), 0) == 0, 1.0, 0.0).astype(BF16)
    t0 = (pl.program_id(0) * tm) % seq

    for st in range(tm // TK):
        rs = slice(st * TK, (st + 1) * TK)
        a = _rms(x_ref[rs, :], g_ref[...]).astype(BF16)

        res = _dot(a, w_ref[...])
        pos = t0 + st * TK + lax.broadcasted_iota(jnp.int32, (TK, LANE), 0)
        posf = jnp.where((lane >= POS_ROW) & (lane < POS_ROW + N_PIECE), (pos % POS_SPLIT).astype(F32), 0.0)
        posf = jnp.where((lane >= POS_ROW + N_PIECE) & (lane < POS_ROW + 2 * N_PIECE),
                         (pos // POS_SPLIT).astype(F32), posf)
        onehot = jnp.where(lane == pos // SEL_BLOCK + SEL_ROW, 1.0, 0.0)
        for t in range(3):
            blk = res[:, t * LANE:(t + 1) * LANE]
            extra = posf + onehot if 2 * t == KS_COL else posf
            for g, src in enumerate((blk, pltpu.roll(blk, HEAD_DIM, 1))):
                c = 2 * t + g
                main_ref[rs, c * LANE:(c + 1) * LANE] = (jnp.where(lo, src, 0.0) + extra).astype(BF16)
        kc_scr[...] = res[:, 3 * LANE:4 * LANE]
        vc_scr[...] = res[:, 4 * LANE:STD_COLS]
        zr = TK // CMP_STRIDE
        for c in range(CMP_STRIDE):
            for t, scr in enumerate((kc_scr, vc_scr)):
                z_ref[t, 0, st * zr:(st + 1) * zr, c * LANE:(c + 1) * LANE] = (
                    scr[pl.ds(c, zr, stride=CMP_STRIDE), :].astype(BF16))

        resT = _dot_nt(wT_ref[...], a)
        qT_ref[0, :, rs] = (resT[0:QT_ROWS] * qscale).astype(BF16)
        for h in range(N_QH):
            r0 = QT_ROWS + h * HEAD_DIM
            qsT_ref[0, h, :, rs] = jnp.concatenate(
                [(resT[r0:r0 + HEAD_DIM] * qscale).astype(BF16), zmid,
                 _slope_rows(SLOPES_SWA[h // N_REP, h % N_REP], TK)], axis=0)
        for tg in range(3 * N_GROUPS):
            r0 = VT_ROW0 + tg * HEAD_DIM
            vT_ref[0, tg, st, 0:HEAD_DIM, :] = resT[r0:r0 + HEAD_DIM, :].astype(BF16)
            vT_ref[0, tg, st, HEAD_DIM:VT_ROWS, :] = ones_blk
        gT_ref[0, :, rs] = jax.nn.sigmoid(resT[GT_ROW:T_ROWS])


def _inproj(x2, gain, w_p, wT_p, b, seq, tm=1024):
    n = x2.shape[0]
    nt = seq // tm
    return pl.pallas_call(
        functools.partial(_inproj_kernel, tm=tm, seq=seq),
        grid=(n // tm,),
        in_specs=[
            pl.BlockSpec((tm, D_MODEL), lambda i: (i, 0)),
            pl.BlockSpec((1, D_MODEL), lambda i: (0, 0)),
            pl.BlockSpec((D_MODEL, STD_COLS), lambda i: (0, 0)),
            pl.BlockSpec((T_ROWS, D_MODEL), lambda i: (0, 0)),
        ],
        out_specs=[
            pl.BlockSpec((tm, MAIN_COLS), lambda i: (i, 0)),
            pl.BlockSpec((2, 1, tm // CMP_STRIDE, CMP_STRIDE * LANE), lambda i: (0, i // nt, i % nt, 0)),
            pl.BlockSpec((1, QT_ROWS, tm), lambda i: (i // nt, 0, i % nt)),
            pl.BlockSpec((1, N_QH, LANE, tm), lambda i: (i // nt, 0, 0, i % nt)),
            pl.BlockSpec((1, 3 * N_GROUPS, tm // TK, VT_ROWS, TK), lambda i: (i // nt, 0, i % nt, 0, 0)),
            pl.BlockSpec((1, GT_ROWS, tm), lambda i: (i // nt, 0, i % nt)),
        ],
        out_shape=[
            jax.ShapeDtypeStruct((n, MAIN_COLS), BF16),
            jax.ShapeDtypeStruct((2, b, seq // CMP_STRIDE, CMP_STRIDE * LANE), BF16),
            jax.ShapeDtypeStruct((b, QT_ROWS, seq), BF16),
            jax.ShapeDtypeStruct((b, N_QH, LANE, seq), BF16),
            jax.ShapeDtypeStruct((b, 3 * N_GROUPS, seq // TK, VT_ROWS, TK), BF16),
            jax.ShapeDtypeStruct((b, GT_ROWS, seq), F32),
        ],
        scratch_shapes=[pltpu.VMEM((TK, LANE), F32), pltpu.VMEM((TK, LANE), F32)],
        compiler_params=pltpu.CompilerParams(
            dimension_semantics=("arbitrary",), vmem_limit_bytes=VMEM_LIMIT),
        name="inproj",
    )(x2, gain, w_p, wT_p)


def _compress_kernel(z_ref, pe_ref, w1_ref, w2_ref, w2T_ref, o_ref, oT_ref):
    nb, n, k = z_ref.shape[1:]
    z = z_ref[0].reshape(nb * n, k).astype(F32)
    zt = (z + pe_ref[0, 0]).astype(BF16)
    zb = (z + pe_ref[0, 1]).astype(BF16)
    a = _dot(zt, w1_ref[0, 0])
    bm = _dot(zb, w1_ref[0, 1])
    h = a + pltpu.roll(bm, nb * n - 1, 0)
    hg = jax.nn.gelu(h).astype(BF16)
    o = _dot(hg, w2_ref[0])
    lane = lax.broadcasted_iota(jnp.int32, o.shape, 1) % LANE
    nidx = lax.broadcasted_iota(jnp.int32, o.shape, 0) % n
    o = o + jnp.where((lane >= HEAD_DIM) & (lane < HEAD_DIM + N_PIECE), nidx.astype(F32), 0.0)
    for e in range(nb):
        o_ref[0, e] = o[e * n:(e + 1) * n].astype(o_ref.dtype)
        oT_ref[0, e] = _dot_nt(w2T_ref[0], hg[e * n:(e + 1) * n]).astype(oT_ref.dtype)


def _compress(z, pe2, w1e, w2e, w2eT, nb=4):
    _, b, n, k = z.shape
    nb = math.gcd(b, nb)
    return pl.pallas_call(
        _compress_kernel,
        grid=(2, b // nb),
        in_specs=[
            pl.BlockSpec((1, nb, n, k), lambda t, i: (t, i, 0, 0)),
            pl.BlockSpec((1, 2, 1, k), lambda t, i: (t, 0, 0, 0)),
            pl.BlockSpec((1, 2, k, 2 * CMP_HIDDEN), lambda t, i: (t, 0, 0, 0)),
            pl.BlockSpec((1, 2 * CMP_HIDDEN, 2 * LANE), lambda t, i: (t, 0, 0)),
            pl.BlockSpec((1, 2 * LANE, 2 * CMP_HIDDEN), lambda t, i: (t, 0, 0)),
        ],
        out_specs=[
            pl.BlockSpec((1, nb, n, 2 * LANE), lambda t, i: (t, i, 0, 0)),
            pl.BlockSpec((1, nb, 2 * LANE, n), lambda t, i: (t, i, 0, 0)),
        ],
        out_shape=[
            jax.ShapeDtypeStruct((2, b, n, 2 * LANE), BF16),
            jax.ShapeDtypeStruct((2, b, 2 * LANE, n), BF16),
        ],
        compiler_params=pltpu.CompilerParams(
            dimension_semantics=("arbitrary", "arbitrary"), vmem_limit_bytes=VMEM_LIMIT),
        name="compress",
    )(z, pe2, w1e, w2e, w2eT)


def _f32_dot_exact_lhs(a_bf16, x):
    out = None
    rem = x
    for _ in range(N_PIECE):
        piece = rem.astype(BF16)
        rem = rem - piece.astype(F32)
        d = _dot(a_bf16, piece)
        out = d if out is None else out + d
    return out


def _select_mask(score, tq, n_blk):
    sub = 8
    n_chunk = -(-n_blk // sub)
    chunks = [score[sub * c:sub * (c + 1)] for c in range(n_chunk)]
    ranks = [jnp.zeros((sub, tq), jnp.int32) for _ in range(n_chunk)]
    jrow = lax.broadcasted_iota(jnp.int32, (sub, tq), 0)
    for k in range(n_blk):
        sk = score[k:k + 1, :]
        for c in range(n_chunk):
            if k < sub * c:
                ahead = sk >= chunks[c]
            elif k >= sub * (c + 1):
                ahead = sk > chunks[c]
            else:
                ahead = (sk > chunks[c]) | ((sk == chunks[c]) & (jrow + sub * c > k))
            ranks[c] = jnp.where(ahead, ranks[c] + 1, ranks[c])
    negm = [jnp.where(r < SEL_TOPN, 0.0, SEL_NEG) for r in ranks]
    negm += [jnp.full((sub, tq), SEL_NEG, F32)] * (N_SEL // sub - n_chunk)
    return jnp.concatenate(negm, axis=0).astype(BF16)


def _cmp_kernel(qT_ref, k_ref, vT_ref, ocT_ref, negm_ref, s_scr, *, tq, nsub, n_cmp, nq):
    i = pl.program_id(1)
    n_pad = k_ref.shape[2]
    frow = lax.broadcasted_iota(jnp.int32, (CMP_FEAT, tq), 0)
    jj = lax.broadcasted_iota(jnp.int32, (N_SEL, tq), 0)

    def run(ii):
        tiles = []
        for sub in range(nsub):
            t0 = (ii * nsub + sub) * tq
            n_vis = min(n_pad, -(-((t0 + tq) // CMP_STRIDE - 1) // 16) * 16)
            tiles.append((sub, t0, n_vis, (t0 + tq) // SEL_BLOCK))

        for sub, t0, n_vis, _ in tiles:
            qs = slice(sub * tq, (sub + 1) * tq)
            for g in range(N_GROUPS):
                kc = k_ref[0, 0][0:n_vis, g * LANE:g * LANE + HEAD_DIM + CMP_FEAT]
                for r in range(N_REP):
                    h = N_REP * g + r
                    feat = jnp.zeros((CMP_FEAT, tq), F32)
                    for k, pc in enumerate(_bf16_pieces(SLOPES_NSA[g, r] * LOG2E)):
                        feat = jnp.where(frow == k, pc * CMP_STRIDE, feat)
                    qTa = jnp.concatenate([qT_ref[0, h * HEAD_DIM:(h + 1) * HEAD_DIM, qs], feat.astype(BF16)], axis=0)
                    s_scr[sub * N_QH + h, 0:n_vis] = _dot(kc, qTa)

        for sub, t0, n_vis, n_blk in tiles:
            qs = slice(sub * tq, (sub + 1) * tq)
            nn = lax.broadcasted_iota(jnp.int32, (n_vis, tq), 0)
            tt = t0 + lax.broadcasted_iota(jnp.int32, (n_vis, tq), 1)
            maskadd = jnp.where((tt >= nn * CMP_STRIDE + CMP_BLOCK - 1) & (nn < n_cmp), 0.0, NEG)
            colvalid = jnp.where(t0 + lax.broadcasted_iota(jnp.int32, (1, tq), 1) >= CMP_BLOCK - 1, 1.0, 0.0)
            blk_t = (t0 + lax.broadcasted_iota(jnp.int32, (N_SEL, tq), 1)) // SEL_BLOCK
            valid = jj <= blk_t
            forced = (jj == 0) | (jj == blk_t) | (jj == blk_t - 1)
            jr = lax.broadcasted_iota(jnp.int32, (N_SEL, n_vis), 0)
            nc = lax.broadcasted_iota(jnp.int32, (N_SEL, n_vis), 1)
            ov = ((nc * CMP_STRIDE < jr * SEL_BLOCK + SEL_BLOCK) & (nc * CMP_STRIDE + CMP_BLOCK > jr * SEL_BLOCK)
                  & (nc < n_cmp))
            ovT = jnp.where(ov, 1.0, 0.0).astype(BF16)

            for g in range(N_GROUPS):
                vcT = vT_ref[0, 0][g * LANE:g * LANE + HEAD_DIM, 0:n_vis]
                psum = jnp.zeros((n_vis, tq), F32)
                for r in range(N_REP):
                    h = N_REP * g + r
                    sm = s_scr[sub * N_QH + h, 0:n_vis] + maskadd
                    m = jnp.max(sm, axis=0, keepdims=True)
                    e = jnp.exp2(sm - m)
                    den = jnp.sum(e, axis=0, keepdims=True)
                    p = e * (colvalid / den)
                    ocT_ref[0, h * HEAD_DIM:(h + 1) * HEAD_DIM, qs] = _dot(vcT, p.astype(BF16)).astype(ocT_ref.dtype)
                    psum = psum + p

                imp = _f32_dot_exact_lhs(ovT, psum)
                score = jnp.where(valid, imp, -jnp.inf)
                score = jnp.where(forced & valid, jnp.inf, score)
                negm_ref[0, g, :, qs] = _select_mask(score, tq, n_blk)

    for ii in range(nq):
        pl.when(i == ii)(functools.partial(run, ii))


def _cmp(qT, cmp_k, cmp_vT, b, seq, tq=256, nsub=8):
    ts = tq * nsub
    nq = seq // ts
    n_cmp = (seq - CMP_BLOCK) // CMP_STRIDE + 1
    n_pad = cmp_k.shape[2]
    return pl.pallas_call(
        functools.partial(_cmp_kernel, tq=tq, nsub=nsub, n_cmp=n_cmp, nq=nq),
        grid=(b, nq),
        in_specs=[
            pl.BlockSpec((1, QT_ROWS, ts), lambda bi, i: (bi, 0, i)),
            pl.BlockSpec((1, 1, n_pad, 2 * LANE), lambda bi, i: (0, bi, 0, 0)),
            pl.BlockSpec((1, 1, 2 * LANE, n_pad), lambda bi, i: (1, bi, 0, 0)),
        ],
        out_specs=[
            pl.BlockSpec((1, QT_ROWS, ts), lambda bi, i: (bi, 0, i)),
            pl.BlockSpec((1, N_GROUPS, N_SEL, ts), lambda bi, i: (bi, 0, 0, i)),
        ],
        out_shape=[
            jax.ShapeDtypeStruct((b, QT_ROWS, seq), BF16),
            jax.ShapeDtypeStruct((b, N_GROUPS, N_SEL, seq), BF16),
        ],
        scratch_shapes=[pltpu.VMEM((nsub * N_QH, n_pad, tq), F32)],
        compiler_params=pltpu.CompilerParams(
            dimension_semantics=("arbitrary", "arbitrary"), vmem_limit_bytes=VMEM_LIMIT),
        name="cmp",
    )(qT, cmp_k, cmp_vT)


SLC_LAG = 7
SLC_BUFS = SLC_LAG // N_QH + 1

HALF = TK // 2
_TRI = {0: (slice(0, HALF), slice(HALF, TK), slice(HALF, TK)),
        1: (slice(HALF, TK), slice(0, HALF), slice(0, HALF))}


def _score_phase(qa_ref, k_tiles, mask_id, caus, s_buf, mt_buf, heads):
    for g in range(N_GROUPS):
        for r in range(N_REP):
            h = N_REP * g + r
            if h not in heads:
                continue
            qT = qa_ref[h]
            if mask_id is None:
                s = _dot(k_tiles[g], qT)
                s_buf[h] = s
                mt_buf[h] = jnp.max(s, axis=0, keepdims=True)
                continue
            fr, hr, hl = _TRI[mask_id]
            s_full = _dot(k_tiles[g][fr], qT) + caus[mask_id, fr, :]
            s_half = _dot(k_tiles[g][hr], qT[:, hl]) + caus[mask_id, hr, hl]
            s_buf[h, fr, :] = s_full
            s_buf[h, hr, hl] = s_half
            m_half = jnp.max(s_half, axis=0, keepdims=True)
            filler = jnp.full((1, HALF), -3e38, F32)
            m_half = jnp.concatenate([filler, m_half] if hl.start else [m_half, filler], axis=1)
            mt_buf[h] = jnp.maximum(jnp.max(s_full, axis=0, keepdims=True), m_half)


def _value_phase(vT_tiles, mask_id, s_buf, mt_buf, m_scr, acc_scr, heads):
    for g in range(N_GROUPS):
        for r in range(N_REP):
            h = N_REP * g + r
            if h not in heads:
                continue
            m_old = m_scr[h]
            m_new = jnp.maximum(m_old, mt_buf[h])
            alpha = jnp.exp2(m_old - m_new)
            if mask_id is None:
                pT = jnp.exp2(s_buf[h] - m_new).astype(BF16)
                upd = _dot(vT_tiles[g], pT)
            else:
                fr, hr, hl = _TRI[mask_id]
                p_full = jnp.exp2(s_buf[h, fr, :] - m_new).astype(BF16)
                p_half = jnp.exp2(s_buf[h, hr, hl] - m_new[:, hl]).astype(BF16)
                u_half = _dot(vT_tiles[g][:, hr], p_half)
                zero = jnp.zeros_like(u_half)
                upd = _dot(vT_tiles[g][:, fr], p_full) + jnp.concatenate(
                    [zero, u_half] if hl.start else [u_half, zero], axis=1)
            acc_scr[h] = alpha * acc_scr[h] + upd
            m_scr[h] = m_new


def _sweep_reset(m_scr, acc_scr):
    m_scr[...] = jnp.full(m_scr.shape, -3e38, F32)
    acc_scr[...] = jnp.zeros(acc_scr.shape, F32)


def _heads_out(o_ref, outs, rows=slice(None)):
    for g in range(N_GROUPS):
        oT = jnp.concatenate(outs[N_REP * g:N_REP * (g + 1)], axis=0)
        o_ref[rows, g * 2 * LANE:(g + 1) * 2 * LANE] = oT.T.astype(o_ref.dtype)


def _slcwin_kernel(qT_ref, negm_ref, ks0, ks1, kw0, kw1, vsT_ref, vwT_ref, gT_ref, ocT_ref,
                   o_ref, caus, qa_scr, s_scr, mt_scr, m_s, acc_s, m_w, acc_w, *, tq, nq):
    tk = TK
    bi = pl.program_id(0)
    i = pl.program_id(1)
    assert NSA_WINDOW // tk == 2 and tq == tk

    @pl.when((bi == 0) & (i == 0))
    def _init():
        kk = lax.broadcasted_iota(jnp.int32, (tk, tq), 0)
        qq = lax.broadcasted_iota(jnp.int32, (tk, tq), 1)
        caus[0] = jnp.where(kk <= qq, 0.0, NEG)
        caus[1] = jnp.where(kk > qq, 0.0, NEG)

    ks = (ks0, ks1)
    kw = (kw0, kw1)

    for h in range(N_QH):
        g, r = divmod(h, N_REP)
        qa_scr[h] = jnp.concatenate([qT_ref[0, h * HEAD_DIM:(h + 1) * HEAD_DIM, :], negm_ref[0, g],
                                     _slope_rows(SLOPES_NSA[g, r], tq)], axis=0)

    def run(ii):
        _sweep_reset(m_s, acc_s)
        _sweep_reset(m_w, acc_w)
        stream = []
        for d in (2, 1, 0):
            if ii - d >= 0:
                stream.append((kw, vwT_ref, ii - d, {0: 0, 2: 1}.get(d), m_w, acc_w))
        for j in range(ii + 1):
            stream.append((ks, vsT_ref, j, 0 if j == ii else None, m_s, acc_s))

        def score(p, heads):
            refs, _, idx, mask_id, _, _ = stream[p]
            buf = p % SLC_BUFS
            _score_phase(qa_scr, [refs[g][idx * tk:(idx + 1) * tk, :] for g in range(N_GROUPS)],
                         mask_id, caus, s_scr.at[buf], mt_scr.at[buf], heads)

        def value(p, heads):
            _, vref, idx, mask_id, m_scr, acc_scr = stream[p]
            buf = p % SLC_BUFS
            _value_phase([vref[0, g, idx] for g in range(N_GROUPS)], mask_id,
                         s_scr.at[buf], mt_scr.at[buf], m_scr, acc_scr, heads)

        n_items = len(stream) * N_QH
        for n in range(n_items + SLC_LAG):
            if n < n_items:
                score(n // N_QH, (n % N_QH,))
            if n >= SLC_LAG:
                value((n - SLC_LAG) // N_QH, ((n - SLC_LAG) % N_QH,))

        gT = gT_ref[0]
        outs = []
        for h in range(N_QH):
            gc = h * N_BRANCH
            f_slc = gT[gc + 1:gc + 2] * (1.0 / acc_s[h, HEAD_DIM:HEAD_DIM + 1, :])
            f_win = gT[gc + 2:gc + 3] * (1.0 / acc_w[h, HEAD_DIM:HEAD_DIM + 1, :])
            outs.append(gT[gc:gc + 1] * ocT_ref[0, h * HEAD_DIM:(h + 1) * HEAD_DIM, :]
                        + f_slc * acc_s[h, 0:HEAD_DIM, :]
                        + f_win * acc_w[h, 0:HEAD_DIM, :])
        _heads_out(o_ref, outs)

    for ii in range(nq):
        pl.when(i == ii)(functools.partial(run, ii))


def _slcwin(qT, negm, main, vT, gT, ocT, b, seq, tq=TK):
    nq = seq // tq
    n = b * seq

    def col(c):
        return pl.BlockSpec((seq, LANE), lambda bi, i, c=c: (bi, c))

    return pl.pallas_call(
        functools.partial(_slcwin_kernel, tq=tq, nq=nq),
        grid=(b, nq),
        in_specs=[
            pl.BlockSpec((1, QT_ROWS, tq), lambda bi, i: (bi, 0, i)),
            pl.BlockSpec((1, N_GROUPS, N_SEL, tq), lambda bi, i: (bi, 0, 0, i)),
            col(KS_COL), col(KS_COL + 1), col(KW_COL), col(KW_COL + 1),
            pl.BlockSpec((1, N_GROUPS, seq // TK, VT_ROWS, TK), lambda bi, i: (bi, 0, 0, 0, 0)),
            pl.BlockSpec((1, N_GROUPS, seq // TK, VT_ROWS, TK), lambda bi, i: (bi, 1, 0, 0, 0)),
            pl.BlockSpec((1, GT_ROWS, tq), lambda bi, i: (bi, 0, i)),
            pl.BlockSpec((1, QT_ROWS, tq), lambda bi, i: (bi, 0, i)),
        ],
        out_specs=pl.BlockSpec((tq, 4 * LANE), lambda bi, i: (bi * nq + i, 0)),
        out_shape=jax.ShapeDtypeStruct((n, 4 * LANE), BF16),
        scratch_shapes=[
            pltpu.VMEM((2, TK, tq), F32),
            pltpu.VMEM((N_QH, LANE, tq), BF16),
            pltpu.VMEM((SLC_BUFS, N_QH, TK, tq), F32),
            pltpu.VMEM((SLC_BUFS, N_QH, 1, tq), F32),
            pltpu.VMEM((N_QH, 1, tq), F32),
            pltpu.VMEM((N_QH, VT_ROWS, tq), F32),
            pltpu.VMEM((N_QH, 1, tq), F32),
            pltpu.VMEM((N_QH, VT_ROWS, tq), F32),
        ],
        compiler_params=pltpu.CompilerParams(
            dimension_semantics=("arbitrary", "arbitrary"), vmem_limit_bytes=VMEM_LIMIT),
        name="slcwin",
    )(qT, negm, main, main, main, main, vT, vT, gT, ocT)


SWA_AHEAD = 3
SWA_BUFS = SWA_AHEAD + 1

def _swa_kernel(sink_ref, qT_ref, k0, k1, vT_ref, o_ref, tri, mab, s_far, s_ab, s_c, *, tq, nsub):
    bi = pl.program_id(0)
    i = pl.program_id(1)
    w = SWA_WINDOW
    assert tq == 2 * w and w == LANE

    @pl.when((bi == 0) & (i == 0))
    def _init():
        kk = lax.broadcasted_iota(jnp.int32, (w, w), 0)
        qq = lax.broadcasted_iota(jnp.int32, (w, w), 1)
        causal = jnp.where(kk <= qq, 0.0, NEG)
        far = jnp.where(kk > qq, 0.0, NEG)
        tri[0] = far
        tri[1] = causal
        tri[2] = jnp.full((w, w), NEG, F32)
        mab[...] = jnp.concatenate([causal, far], axis=1)

    kk_ref = (k0, k1)
    mts, sinks, outs = {}, {}, {}

    def geometry(sub):
        tile = i * nsub + sub
        return tile, tile * tq

    def score(n):
        sub, h = divmod(n, N_QH)
        g, r = divmod(h, N_REP)
        tile, t0 = geometry(sub)
        far0 = pl.multiple_of(jnp.maximum(t0 - w, 0), w)
        main0 = pl.multiple_of(t0, tq)
        buf = n % SWA_BUFS
        qaT = qT_ref[0, h, :, sub * tq:(sub + 1) * tq]
        sf = _dot(kk_ref[g][pl.ds(far0, w), :], qaT[:, 0:w]) + tri[jnp.where(tile == 0, 2, 0)]
        sab = _dot(kk_ref[g][pl.ds(main0, w), :], qaT) + mab[...]
        sc = _dot(kk_ref[g][pl.ds(main0 + w, w), :], qaT[:, w:tq]) + tri[1]
        s_far[buf] = sf
        s_ab[buf] = sab
        s_c[buf] = sc
        tpos = (t0 + lax.broadcasted_iota(jnp.int32, (1, tq), 1)).astype(F32)
        sink = (sink_ref[g, r] * LOG2E) + float(SLOPES_SWA[g, r] * LOG2E) * tpos
        m_side = jnp.concatenate([jnp.max(sf, axis=0, keepdims=True), jnp.max(sc, axis=0, keepdims=True)], axis=1)
        mts[n] = jnp.maximum(jnp.maximum(jnp.max(sab, axis=0, keepdims=True), m_side), sink)
        sinks[n] = sink

    def value(n):
        sub, h = divmod(n, N_QH)
        g = h // N_REP
        tile, _ = geometry(sub)
        buf = n % SWA_BUFS
        vT_far = vT_ref[0, g, jnp.maximum(tile - 1, 0)][:, w:2 * w]
        vT_main = vT_ref[0, g, tile]
        m = mts.pop(n)
        pf = jnp.exp2(s_far[buf] - m[:, 0:w]).astype(BF16)
        pab = jnp.exp2(s_ab[buf] - m).astype(BF16)
        pc = jnp.exp2(s_c[buf] - m[:, w:tq]).astype(BF16)
        acc = _dot(vT_main[:, 0:w], pab) + jnp.concatenate([_dot(vT_far, pf), _dot(vT_main[:, w:tq], pc)], axis=1)
        den = acc[HEAD_DIM:HEAD_DIM + 1, :] + jnp.exp2(sinks.pop(n) - m)
        outs[n] = acc[0:HEAD_DIM, :] * (1.0 / den)
        if h == N_QH - 1:
            _heads_out(o_ref, [outs.pop(sub * N_QH + hh) for hh in range(N_QH)], rows=slice(sub * tq, (sub + 1) * tq))

    n_items = nsub * N_QH
    for n in range(n_items + SWA_AHEAD):
        if n < n_items:
            score(n)
        if n >= SWA_AHEAD:
            value(n - SWA_AHEAD)


def _swa(sinks, qsT, main, vT, b, seq, tq=TK, nsub=4):
    ts = tq * nsub
    nq = seq // ts
    n = b * seq

    def col(c):
        return pl.BlockSpec((seq, LANE), lambda bi, i, c=c: (bi, c))

    return pl.pallas_call(
        functools.partial(_swa_kernel, tq=tq, nsub=nsub),
        grid=(b, nq),
        in_specs=[
            pl.BlockSpec(memory_space=pltpu.SMEM),
            pl.BlockSpec((1, N_QH, LANE, ts), lambda bi, i: (bi, 0, 0, i)),
            col(KSW_COL), col(KSW_COL + 1),
            pl.BlockSpec((1, N_GROUPS, seq // TK, VT_ROWS, TK), lambda bi, i: (bi, 2, 0, 0, 0)),
        ],
        out_specs=pl.BlockSpec((ts, 4 * LANE), lambda bi, i: (bi * nq + i, 0)),
        out_shape=jax.ShapeDtypeStruct((n, 4 * LANE), BF16),
        scratch_shapes=[
            pltpu.VMEM((3, SWA_WINDOW, SWA_WINDOW), F32),
            pltpu.VMEM((SWA_WINDOW, tq), F32),
            pltpu.VMEM((SWA_BUFS, SWA_WINDOW, SWA_WINDOW), F32),
            pltpu.VMEM((SWA_BUFS, SWA_WINDOW, tq), F32),
            pltpu.VMEM((SWA_BUFS, SWA_WINDOW, SWA_WINDOW), F32),
        ],
        compiler_params=pltpu.CompilerParams(
            dimension_semantics=("arbitrary", "arbitrary"), vmem_limit_bytes=VMEM_LIMIT),
        name="swa",
    )(sinks, qsT, main, main, vT)


def _outmlp_kernel(on_ref, os_ref, x_ref, wo_ref, g2_ref, g3_ref, wu_ref, wd_ref, g4_ref, o_ref, *, ff_chunk, n_sub):
    half = N_HEADS * HEAD_DIM // 2
    tm = x_ref.shape[0]
    subs = [slice(k * tm // n_sub, (k + 1) * tm // n_sub) for k in range(n_sub)]
    mixes = [_dot(on_ref[rs, :], wo_ref[0:half, :]) + _dot(os_ref[rs, :], wo_ref[half:2 * half, :]) for rs in subs]
    for rs, mix in zip(subs, mixes):
        h1 = x_ref[rs, :] + _rms(mix, g2_ref[...])
        m = _rms(h1, g3_ref[...]).astype(BF16)
        acc = jnp.zeros(h1.shape, F32)
        for c in range(D_FF // ff_chunk):
            u = _dot(m, wu_ref[:, c * ff_chunk:(c + 1) * ff_chunk])
            u = jnp.square(jnp.maximum(u, 0.0)).astype(BF16)
            acc = acc + _dot(u, wd_ref[c * ff_chunk:(c + 1) * ff_chunk, :])
        o_ref[rs, :] = h1 + _rms(acc, g4_ref[...])


def _outmlp(onsa, oswa, x2, wo, g2, g3, wu, wd, g4, tm=1024, ff_chunk=1024, n_sub=4):
    n = x2.shape[0]

    def const(shape):
        return pl.BlockSpec(shape, lambda i: (0, 0), pipeline_mode=pl.Buffered(1))

    return pl.pallas_call(
        functools.partial(_outmlp_kernel, ff_chunk=ff_chunk, n_sub=n_sub),
        grid=(n // tm,),
        in_specs=[
            pl.BlockSpec((tm, 4 * LANE), lambda i: (i, 0)),
            pl.BlockSpec((tm, 4 * LANE), lambda i: (i, 0)),
            pl.BlockSpec((tm, D_MODEL), lambda i: (i, 0)),
            const((D_MODEL, D_MODEL)),
            const((1, D_MODEL)),
            const((1, D_MODEL)),
            const((D_MODEL, D_FF)),
            const((D_FF, D_MODEL)),
            const((1, D_MODEL)),
        ],
        out_specs=pl.BlockSpec((tm, D_MODEL), lambda i: (i, 0)),
        out_shape=jax.ShapeDtypeStruct((n, D_MODEL), F32),
        compiler_params=pltpu.CompilerParams(
            dimension_semantics=("arbitrary",), vmem_limit_bytes=VMEM_LIMIT),
        name="outmlp",
    )(onsa, oswa, x2, wo, g2, g3, wu, wd, g4)


def _layout_w_in(w):
    sizes = [512, 128, 128, 128, 128, 128, 128, N_HEADS // 2 * N_BRANCH, 512, 128, 128]
    offs = np.concatenate([[0], np.cumsum(sizes)])
    q_n, kc, vc, ks, vs, kw, vw, gt, q_s, k_s, v_s = [w[:, offs[k]:offs[k + 1]] for k in range(len(sizes))]
    std = [ks, kw, k_s, kc, vc]
    gt = jnp.concatenate([gt, jnp.zeros((w.shape[0], GT_ROWS - gt.shape[1]), w.dtype)], axis=1)
    tr = jnp.concatenate([q_n, q_s, vs, vw, v_s, gt], axis=1).T
    return jnp.concatenate(std, axis=1).astype(BF16), tr.astype(BF16)


def _layout_w1(w1):
    w = w1.reshape(2, CMP_STRIDE, HEAD_DIM, CMP_HIDDEN)
    z = jnp.zeros_like(w)
    top = jnp.concatenate([w, z], axis=-1)
    bot = jnp.concatenate([z, w], axis=-1)
    e = jnp.stack([top, bot], axis=2)
    return e.reshape(2, CMP_STRIDE * 2 * HEAD_DIM, 2 * CMP_HIDDEN).astype(BF16)


def _layout_w2(w2):
    z = jnp.zeros((CMP_HIDDEN, HEAD_DIM), w2.dtype)
    top = jnp.concatenate([w2, z, z, z], axis=1)
    bot = jnp.concatenate([z, z, w2, z], axis=1)
    return jnp.concatenate([top, bot], axis=0).astype(BF16)


def _layout_pe(pe):
    p = pe.reshape(2, CMP_STRIDE, 1, HEAD_DIM)
    return jnp.broadcast_to(p, (2, CMP_STRIDE, 2, HEAD_DIM)).reshape(2, 1, CMP_STRIDE * 2 * HEAD_DIM)


def kernel(x, norm_mix_pre, w_in, cmp_pe_k, cmp_w1_k, cmp_w2_k, cmp_pe_v, cmp_w1_v, cmp_w2_v,
           sinks, w_out, norm_mix_post, norm_mlp_pre, w_up, w_down, norm_mlp_post):
    b, seq, _ = x.shape
    assert seq // SEL_BLOCK == N_SEL and seq <= POS_SPLIT * 256
    depth = w_in.shape[0]
    h = x.reshape(b * seq, D_MODEL)
    for li in range(depth):
        w_p, wT_p = _layout_w_in(w_in[li])
        main, z, qT, qsT, vT, gT = _inproj(h, norm_mix_pre[li][None], w_p, wT_p, b, seq)
        pe2 = jnp.stack([_layout_pe(cmp_pe_k[li]), _layout_pe(cmp_pe_v[li])])
        w1e = jnp.stack([_layout_w1(cmp_w1_k[li]), _layout_w1(cmp_w1_v[li])])
        w2e = jnp.stack([_layout_w2(cmp_w2_k[li]), _layout_w2(cmp_w2_v[li])])
        cmp_kv, cmp_kvT = _compress(z, pe2, w1e, w2e, jnp.swapaxes(w2e, 1, 2))
        ocT, negm = _cmp(qT, cmp_kv, cmp_kvT, b, seq)
        onsa = _slcwin(qT, negm, main, vT, gT, ocT, b, seq)
        oswa = _swa(sinks[li], qsT, main, vT, b, seq)
        h = _outmlp(onsa, oswa, h, w_out[li].astype(BF16), norm_mix_post[li][None], norm_mlp_pre[li][None],
                    w_up[li].astype(BF16), w_down[li].astype(BF16), norm_mlp_post[li][None])
    return h.reshape(b, seq, D_MODEL)
```

```python
import functools
import math

import jax
import jax.numpy as jnp
import numpy as np
from jax import lax
from jax.experimental import pallas as pl
from jax.experimental.pallas import tpu as pltpu

F32 = jnp.float32
BF16 = jnp.bfloat16

D_MODEL = 1024
HEAD_DIM = 64
N_HEADS = 16
N_GROUPS = 2
N_REP = 4
N_QH = N_GROUPS * N_REP
CMP_BLOCK = 32
CMP_STRIDE = 16
CMP_HIDDEN = 4 * HEAD_DIM
SEL_BLOCK = 64
SEL_TOPN = 8
N_SEL = 32
NSA_WINDOW = 512
SWA_WINDOW = 128
D_FF = 4 * D_MODEL
NORM_EPS = 1e-6
N_BRANCH = 3

LANE = 128
LOG2E = math.log2(math.e)
NEG = -1e30
SEL_NEG = -(2.0 ** 100)
VMEM_LIMIT = 56 * 1024 * 1024

TK = 256
VT_ROWS = 80
SEL_ROW = HEAD_DIM
POS_ROW = HEAD_DIM + N_SEL
N_PIECE = 3
CMP_FEAT = 16
POS_SPLIT = 256

KS_COL, KW_COL, KSW_COL = 0, 2, 4
MAIN_COLS = 6 * LANE
STD_COLS = 5 * LANE
QT_ROWS = N_QH * HEAD_DIM
VT_ROW0 = 2 * QT_ROWS
GT_ROW = VT_ROW0 + 3 * N_GROUPS * HEAD_DIM
GT_ROWS = 32
T_ROWS = GT_ROW + GT_ROWS


def _slopes():
    s = 2.0 ** (-8.0 * (np.arange(N_HEADS) + 1) / N_HEADS)
    nsa = s[0::2].reshape(N_GROUPS, N_REP)
    swa = s[1::2].reshape(N_GROUPS, N_REP)
    return nsa, swa


SLOPES_NSA, SLOPES_SWA = _slopes()


def _bf16_round(x):
    u = np.float32(x).reshape(1).view(np.uint32)
    u = (u + (((u >> 16) & 1) + 0x7FFF)) & np.uint32(0xFFFF0000)
    return float(u.view(np.float32)[0])


def _bf16_pieces(x, n=N_PIECE):
    out, rem = [], float(np.float32(x))
    for _ in range(n):
        p = _bf16_round(rem)
        out.append(p)
        rem = float(np.float32(rem - p))
    return out


_NT = (((1,), (1,)), ((), ()))


def _dot_nt(a, b, **kw):
    return lax.dot_general(a, b, _NT, preferred_element_type=F32, **kw)


def _dot(a, b, **kw):
    return jnp.dot(a, b, preferred_element_type=F32, **kw)


def _rms(v, g):
    return v * lax.rsqrt(jnp.mean(v * v, axis=-1, keepdims=True) + NORM_EPS) * g


def _slope_rows(slope, tq):
    pieces = _bf16_pieces(slope * LOG2E)
    vals = pieces + [p * POS_SPLIT for p in pieces]
    prow = lax.broadcasted_iota(jnp.int32, (LANE - POS_ROW, tq), 0)
    feat = jnp.zeros((LANE - POS_ROW, tq), F32)
    for k, v in enumerate(vals):
        feat = jnp.where(prow == k, v, feat)
    return feat.astype(BF16)


def _inproj_kernel(x_ref, g_ref, w_ref, wT_ref, main_ref, z_ref, qT_ref, vT_ref, gT_ref,
                   kc_scr, vc_scr, *, tm, seq):
    qscale = LOG2E * HEAD_DIM ** -0.5
    lane = lax.broadcasted_iota(jnp.int32, (TK, LANE), 1)
    lo = lane < HEAD_DIM
    ones_blk = jnp.where(lax.broadcasted_iota(jnp.int32, (VT_ROWS - HEAD_DIM, TK), 0) == 0, 1.0, 0.0).astype(BF16)
    t0 = (pl.program_id(0) * tm) % seq

    for st in range(tm // TK):
        rs = slice(st * TK, (st + 1) * TK)
        a = _rms(x_ref[rs, :], g_ref[...]).astype(BF16)

        res = _dot(a, w_ref[...])
        pos = t0 + st * TK + lax.broadcasted_iota(jnp.int32, (TK, LANE), 0)
        posf = jnp.where((lane >= POS_ROW) & (lane < POS_ROW + N_PIECE), (pos % POS_SPLIT).astype(F32), 0.0)
        posf = jnp.where((lane >= POS_ROW + N_PIECE) & (lane < POS_ROW + 2 * N_PIECE),
                         (pos // POS_SPLIT).astype(F32), posf)
        onehot = jnp.where(lane == pos // SEL_BLOCK + SEL_ROW, 1.0, 0.0)
        for t in range(3):
            blk = res[:, t * LANE:(t + 1) * LANE]
            extra = posf + onehot if 2 * t == KS_COL else posf
            for g, src in enumerate((blk, pltpu.roll(blk, HEAD_DIM, 1))):
                c = 2 * t + g
                main_ref[rs, c * LANE:(c + 1) * LANE] = (jnp.where(lo, src, 0.0) + extra).astype(BF16)
        kc_scr[...] = res[:, 3 * LANE:4 * LANE]
        vc_scr[...] = res[:, 4 * LANE:STD_COLS]
        zr = TK // CMP_STRIDE
        for c in range(CMP_STRIDE):
            for t, scr in enumerate((kc_scr, vc_scr)):
                z_ref[t, 0, st * zr:(st + 1) * zr, c * LANE:(c + 1) * LANE] = (
                    scr[pl.ds(c, zr, stride=CMP_STRIDE), :].astype(BF16))

        resT = _dot_nt(wT_ref[...], a)
        qT_ref[0, :, rs] = (resT[0:VT_ROW0] * qscale).astype(BF16)
        for tg in range(3 * N_GROUPS):
            r0 = VT_ROW0 + tg * HEAD_DIM
            vT_ref[0, tg, st, 0:HEAD_DIM, :] = resT[r0:r0 + HEAD_DIM, :].astype(BF16)
            vT_ref[0, tg, st, HEAD_DIM:VT_ROWS, :] = ones_blk
        gT_ref[0, :, rs] = jax.nn.sigmoid(resT[GT_ROW:T_ROWS])


def _inproj(x2, gain, w_p, wT_p, b, seq, tm=1024):
    n = x2.shape[0]
    nt = seq // tm
    return pl.pallas_call(
        functools.partial(_inproj_kernel, tm=tm, seq=seq),
        grid=(n // tm,),
        in_specs=[
            pl.BlockSpec((tm, D_MODEL), lambda i: (i, 0)),
            pl.BlockSpec((1, D_MODEL), lambda i: (0, 0)),
            pl.BlockSpec((D_MODEL, STD_COLS), lambda i: (0, 0)),
            pl.BlockSpec((T_ROWS, D_MODEL), lambda i: (0, 0)),
        ],
        out_specs=[
            pl.BlockSpec((tm, MAIN_COLS), lambda i: (i, 0)),
            pl.BlockSpec((2, 1, tm // CMP_STRIDE, CMP_STRIDE * LANE), lambda i: (0, i // nt, i % nt, 0)),
            pl.BlockSpec((1, VT_ROW0, tm), lambda i: (i // nt, 0, i % nt)),
            pl.BlockSpec((1, 3 * N_GROUPS, tm // TK, VT_ROWS, TK), lambda i: (i // nt, 0, i % nt, 0, 0)),
            pl.BlockSpec((1, GT_ROWS, tm), lambda i: (i // nt, 0, i % nt)),
        ],
        out_shape=[
            jax.ShapeDtypeStruct((n, MAIN_COLS), BF16),
            jax.ShapeDtypeStruct((2, b, seq // CMP_STRIDE, CMP_STRIDE * LANE), BF16),
            jax.ShapeDtypeStruct((b, VT_ROW0, seq), BF16),
            jax.ShapeDtypeStruct((b, 3 * N_GROUPS, seq // TK, VT_ROWS, TK), BF16),
            jax.ShapeDtypeStruct((b, GT_ROWS, seq), F32),
        ],
        scratch_shapes=[pltpu.VMEM((TK, LANE), F32), pltpu.VMEM((TK, LANE), F32)],
        compiler_params=pltpu.CompilerParams(
            dimension_semantics=("arbitrary",), vmem_limit_bytes=VMEM_LIMIT),
        name="inproj",
    )(x2, gain, w_p, wT_p)


def _compress_kernel(z_ref, pe_ref, w1_ref, w2_ref, w2T_ref, o_ref, oT_ref):
    nb, n, k = z_ref.shape[1:]
    z = z_ref[0].reshape(nb * n, k).astype(F32)
    zt = (z + pe_ref[0, 0]).astype(BF16)
    zb = (z + pe_ref[0, 1]).astype(BF16)
    a = _dot(zt, w1_ref[0, 0])
    bm = _dot(zb, w1_ref[0, 1])
    h = a + pltpu.roll(bm, nb * n - 1, 0)
    hg = jax.nn.gelu(h).astype(BF16)
    o = _dot(hg, w2_ref[0])
    lane = lax.broadcasted_iota(jnp.int32, o.shape, 1) % LANE
    nidx = lax.broadcasted_iota(jnp.int32, o.shape, 0) % n
    o = o + jnp.where((lane >= HEAD_DIM) & (lane < HEAD_DIM + N_PIECE), nidx.astype(F32), 0.0)
    for e in range(nb):
        o_ref[0, e] = o[e * n:(e + 1) * n].astype(o_ref.dtype)
        oT_ref[0, e] = _dot_nt(w2T_ref[0], hg[e * n:(e + 1) * n]).astype(oT_ref.dtype)


def _compress(z, pe2, w1e, w2e, w2eT, nb=4):
    _, b, n, k = z.shape
    nb = math.gcd(b, nb)
    return pl.pallas_call(
        _compress_kernel,
        grid=(2, b // nb),
        in_specs=[
            pl.BlockSpec((1, nb, n, k), lambda t, i: (t, i, 0, 0)),
            pl.BlockSpec((1, 2, 1, k), lambda t, i: (t, 0, 0, 0)),
            pl.BlockSpec((1, 2, k, 2 * CMP_HIDDEN), lambda t, i: (t, 0, 0, 0)),
            pl.BlockSpec((1, 2 * CMP_HIDDEN, 2 * LANE), lambda t, i: (t, 0, 0)),
            pl.BlockSpec((1, 2 * LANE, 2 * CMP_HIDDEN), lambda t, i: (t, 0, 0)),
        ],
        out_specs=[
            pl.BlockSpec((1, nb, n, 2 * LANE), lambda t, i: (t, i, 0, 0)),
            pl.BlockSpec((1, nb, 2 * LANE, n), lambda t, i: (t, i, 0, 0)),
        ],
        out_shape=[
            jax.ShapeDtypeStruct((2, b, n, 2 * LANE), BF16),
            jax.ShapeDtypeStruct((2, b, 2 * LANE, n), BF16),
        ],
        compiler_params=pltpu.CompilerParams(
            dimension_semantics=("arbitrary", "arbitrary"), vmem_limit_bytes=VMEM_LIMIT),
        name="compress",
    )(z, pe2, w1e, w2e, w2eT)


def _f32_dot_exact_lhs(a_bf16, x):
    out = None
    rem = x
    for _ in range(N_PIECE):
        piece = rem.astype(BF16)
        rem = rem - piece.astype(F32)
        d = _dot(a_bf16, piece)
        out = d if out is None else out + d
    return out


def _select_mask(score, tq, n_blk):
    sub = 8
    n_chunk = -(-n_blk // sub)
    chunks = [score[sub * c:sub * (c + 1)] for c in range(n_chunk)]
    ranks = [jnp.zeros((sub, tq), jnp.int32) for _ in range(n_chunk)]
    jrow = lax.broadcasted_iota(jnp.int32, (sub, tq), 0)
    for k in range(n_blk):
        sk = score[k:k + 1, :]
        for c in range(n_chunk):
            if k < sub * c:
                ahead = sk >= chunks[c]
            elif k >= sub * (c + 1):
                ahead = sk > chunks[c]
            else:
                ahead = (sk > chunks[c]) | ((sk == chunks[c]) & (jrow + sub * c > k))
            ranks[c] = jnp.where(ahead, ranks[c] + 1, ranks[c])
    negm = [jnp.where(r < SEL_TOPN, 0.0, SEL_NEG) for r in ranks]
    negm += [jnp.full((sub, tq), SEL_NEG, F32)] * (N_SEL // sub - n_chunk)
    return jnp.concatenate(negm, axis=0).astype(BF16)


def _cmp_kernel(qT_ref, k_ref, vT_ref, ocT_ref, negm_ref, s_scr, *, tq, nsub, n_cmp, nq):
    i = pl.program_id(1)
    n_pad = k_ref.shape[2]
    frow = lax.broadcasted_iota(jnp.int32, (CMP_FEAT, tq), 0)
    jj = lax.broadcasted_iota(jnp.int32, (N_SEL, tq), 0)

    def run(ii):
        tiles = []
        for sub in range(nsub):
            t0 = (ii * nsub + sub) * tq
            n_vis = min(n_pad, -(-((t0 + tq) // CMP_STRIDE - 1) // 16) * 16)
            tiles.append((sub, t0, n_vis, (t0 + tq) // SEL_BLOCK))

        for sub, t0, n_vis, _ in tiles:
            qs = slice(sub * tq, (sub + 1) * tq)
            for g in range(N_GROUPS):
                kc = k_ref[0, 0][0:n_vis, g * LANE:g * LANE + HEAD_DIM + CMP_FEAT]
                for r in range(N_REP):
                    h = N_REP * g + r
                    feat = jnp.zeros((CMP_FEAT, tq), F32)
                    for k, pc in enumerate(_bf16_pieces(SLOPES_NSA[g, r] * LOG2E)):
                        feat = jnp.where(frow == k, pc * CMP_STRIDE, feat)
                    qTa = jnp.concatenate([qT_ref[0, h * HEAD_DIM:(h + 1) * HEAD_DIM, qs], feat.astype(BF16)], axis=0)
                    s_scr[sub * N_QH + h, 0:n_vis] = _dot(kc, qTa)

        for sub, t0, n_vis, n_blk in tiles:
            qs = slice(sub * tq, (sub + 1) * tq)
            nn = lax.broadcasted_iota(jnp.int32, (n_vis, tq), 0)
            tt = t0 + lax.broadcasted_iota(jnp.int32, (n_vis, tq), 1)
            maskadd = jnp.where((tt >= nn * CMP_STRIDE + CMP_BLOCK - 1) & (nn < n_cmp), 0.0, NEG)
            colvalid = jnp.where(t0 + lax.broadcasted_iota(jnp.int32, (1, tq), 1) >= CMP_BLOCK - 1, 1.0, 0.0)
            blk_t = (t0 + lax.broadcasted_iota(jnp.int32, (N_SEL, tq), 1)) // SEL_BLOCK
            valid = jj <= blk_t
            forced = (jj == 0) | (jj == blk_t) | (jj == blk_t - 1)
            jr = lax.broadcasted_iota(jnp.int32, (N_SEL, n_vis), 0)
            nc = lax.broadcasted_iota(jnp.int32, (N_SEL, n_vis), 1)
            ov = ((nc * CMP_STRIDE < jr * SEL_BLOCK + SEL_BLOCK) & (nc * CMP_STRIDE + CMP_BLOCK > jr * SEL_BLOCK)
                  & (nc < n_cmp))
            ovT = jnp.where(ov, 1.0, 0.0).astype(BF16)

            for g in range(N_GROUPS):
                vcT = vT_ref[0, 0][g * LANE:g * LANE + HEAD_DIM, 0:n_vis]
                psum = jnp.zeros((n_vis, tq), F32)
                for r in range(N_REP):
                    h = N_REP * g + r
                    sm = s_scr[sub * N_QH + h, 0:n_vis] + maskadd
                    m = jnp.max(sm, axis=0, keepdims=True)
                    e = jnp.exp2(sm - m)
                    den = jnp.sum(e, axis=0, keepdims=True)
                    p = e * (colvalid / den)
                    ocT_ref[0, h * HEAD_DIM:(h + 1) * HEAD_DIM, qs] = _dot(vcT, p.astype(BF16)).astype(ocT_ref.dtype)
                    psum = psum + p

                imp = _f32_dot_exact_lhs(ovT, psum)
                score = jnp.where(valid, imp, -jnp.inf)
                score = jnp.where(forced & valid, jnp.inf, score)
                negm_ref[0, g, :, qs] = _select_mask(score, tq, n_blk)

    for ii in range(nq):
        pl.when(i == ii)(functools.partial(run, ii))


def _cmp(qT, cmp_k, cmp_vT, b, seq, tq=256, nsub=8):
    ts = tq * nsub
    nq = seq // ts
    n_cmp = (seq - CMP_BLOCK) // CMP_STRIDE + 1
    n_pad = cmp_k.shape[2]
    return pl.pallas_call(
        functools.partial(_cmp_kernel, tq=tq, nsub=nsub, n_cmp=n_cmp, nq=nq),
        grid=(b, nq),
        in_specs=[
            pl.BlockSpec((1, QT_ROWS, ts), lambda bi, i: (bi, 0, i)),
            pl.BlockSpec((1, 1, n_pad, 2 * LANE), lambda bi, i: (0, bi, 0, 0)),
            pl.BlockSpec((1, 1, 2 * LANE, n_pad), lambda bi, i: (1, bi, 0, 0)),
        ],
        out_specs=[
            pl.BlockSpec((1, QT_ROWS, ts), lambda bi, i: (bi, 0, i)),
            pl.BlockSpec((1, N_GROUPS, N_SEL, ts), lambda bi, i: (bi, 0, 0, i)),
        ],
        out_shape=[
            jax.ShapeDtypeStruct((b, QT_ROWS, seq), BF16),
            jax.ShapeDtypeStruct((b, N_GROUPS, N_SEL, seq), BF16),
        ],
        scratch_shapes=[pltpu.VMEM((nsub * N_QH, n_pad, tq), F32)],
        compiler_params=pltpu.CompilerParams(
            dimension_semantics=("arbitrary", "arbitrary"), vmem_limit_bytes=VMEM_LIMIT),
        name="cmp",
    )(qT, cmp_k, cmp_vT)


SLC_LAG = 6
SLC_BUFS = SLC_LAG // N_QH + 1

HALF = TK // 2
_TRI = {0: (slice(0, HALF), slice(HALF, TK), slice(HALF, TK)),
        1: (slice(HALF, TK), slice(0, HALF), slice(0, HALF))}


def _score_phase(qa_ref, k_tiles, mask_id, caus, s_buf, mt_buf, heads):
    for g in range(N_GROUPS):
        for r in range(N_REP):
            h = N_REP * g + r
            if h not in heads:
                continue
            qT = qa_ref[h]
            if mask_id is None:
                s = _dot(k_tiles[g], qT)
                s_buf[h] = s
                mt_buf[h] = jnp.max(s, axis=0, keepdims=True)
                continue
            fr, hr, hl = _TRI[mask_id]
            s_full = _dot(k_tiles[g][fr], qT) + caus[mask_id, fr, :]
            s_half = _dot(k_tiles[g][hr], qT[:, hl]) + caus[mask_id, hr, hl]
            s_buf[h, fr, :] = s_full
            s_buf[h, hr, hl] = s_half
            m_half = jnp.max(s_half, axis=0, keepdims=True)
            filler = jnp.full((1, HALF), -3e38, F32)
            m_half = jnp.concatenate([filler, m_half] if hl.start else [m_half, filler], axis=1)
            mt_buf[h] = jnp.maximum(jnp.max(s_full, axis=0, keepdims=True), m_half)


def _value_phase(vT_tiles, mask_id, s_buf, mt_buf, m_scr, acc_scr, heads):
    for g in range(N_GROUPS):
        for r in range(N_REP):
            h = N_REP * g + r
            if h not in heads:
                continue
            m_old = m_scr[h]
            m_new = jnp.maximum(m_old, mt_buf[h])
            alpha = jnp.exp2(m_old - m_new)
            if mask_id is None:
                pT = jnp.exp2(s_buf[h] - m_new).astype(BF16)
                upd = _dot(vT_tiles[g], pT)
            else:
                fr, hr, hl = _TRI[mask_id]
                p_full = jnp.exp2(s_buf[h, fr, :] - m_new).astype(BF16)
                p_half = jnp.exp2(s_buf[h, hr, hl] - m_new[:, hl]).astype(BF16)
                u_half = _dot(vT_tiles[g][:, hr], p_half)
                zero = jnp.zeros_like(u_half)
                upd = _dot(vT_tiles[g][:, fr], p_full) + jnp.concatenate(
                    [zero, u_half] if hl.start else [u_half, zero], axis=1)
            acc_scr[h] = alpha * acc_scr[h] + upd
            m_scr[h] = m_new


def _sweep_reset(m_scr, acc_scr):
    m_scr[...] = jnp.full(m_scr.shape, -3e38, F32)
    acc_scr[...] = jnp.zeros(acc_scr.shape, F32)


def _heads_out(o_ref, outs, rows=slice(None)):
    for g in range(N_GROUPS):
        oT = jnp.concatenate(outs[N_REP * g:N_REP * (g + 1)], axis=0)
        o_ref[rows, g * 2 * LANE:(g + 1) * 2 * LANE] = oT.T.astype(o_ref.dtype)


def _slcwin_kernel(qT_ref, negm_ref, ks0, ks1, kw0, kw1, vsT_ref, vwT_ref, gT_ref, ocT_ref,
                   o_ref, caus, qa_scr, s_scr, mt_scr, m_s, acc_s, m_w, acc_w, *, tq, nq):
    tk = TK
    bi = pl.program_id(0)
    i = pl.program_id(1)
    assert NSA_WINDOW // tk == 2 and tq == tk

    @pl.when((bi == 0) & (i == 0))
    def _init():
        kk = lax.broadcasted_iota(jnp.int32, (tk, tq), 0)
        qq = lax.broadcasted_iota(jnp.int32, (tk, tq), 1)
        caus[0] = jnp.where(kk <= qq, 0.0, NEG)
        caus[1] = jnp.where(kk > qq, 0.0, NEG)

    ks = (ks0, ks1)
    kw = (kw0, kw1)

    for h in range(N_QH):
        g, r = divmod(h, N_REP)
        qa_scr[h] = jnp.concatenate([qT_ref[0, h * HEAD_DIM:(h + 1) * HEAD_DIM, :], negm_ref[0, g],
                                     _slope_rows(SLOPES_NSA[g, r], tq)], axis=0)

    def run(ii):
        _sweep_reset(m_s, acc_s)
        _sweep_reset(m_w, acc_w)
        stream = []
        for d in (2, 1, 0):
            if ii - d >= 0:
                stream.append((kw, vwT_ref, ii - d, {0: 0, 2: 1}.get(d), m_w, acc_w))
        for j in range(ii + 1):
            stream.append((ks, vsT_ref, j, 0 if j == ii else None, m_s, acc_s))

        def score(p, heads):
            refs, _, idx, mask_id, _, _ = stream[p]
            buf = p % SLC_BUFS
            _score_phase(qa_scr, [refs[g][idx * tk:(idx + 1) * tk, :] for g in range(N_GROUPS)],
                         mask_id, caus, s_scr.at[buf], mt_scr.at[buf], heads)

        def value(p, heads):
            _, vref, idx, mask_id, m_scr, acc_scr = stream[p]
            buf = p % SLC_BUFS
            _value_phase([vref[0, g, idx] for g in range(N_GROUPS)], mask_id,
                         s_scr.at[buf], mt_scr.at[buf], m_scr, acc_scr, heads)

        n_items = len(stream) * N_QH
        for n in range(n_items + SLC_LAG):
            if n < n_items:
                score(n // N_QH, (n % N_QH,))
            if n >= SLC_LAG:
                value((n - SLC_LAG) // N_QH, ((n - SLC_LAG) % N_QH,))

        gT = gT_ref[0]
        outs = []
        for h in range(N_QH):
            gc = h * N_BRANCH
            f_slc = gT[gc + 1:gc + 2] * (1.0 / acc_s[h, HEAD_DIM:HEAD_DIM + 1, :])
            f_win = gT[gc + 2:gc + 3] * (1.0 / acc_w[h, HEAD_DIM:HEAD_DIM + 1, :])
            outs.append(gT[gc:gc + 1] * ocT_ref[0, h * HEAD_DIM:(h + 1) * HEAD_DIM, :]
                        + f_slc * acc_s[h, 0:HEAD_DIM, :]
                        + f_win * acc_w[h, 0:HEAD_DIM, :])
        _heads_out(o_ref, outs)

    for ii in range(nq):
        pl.when(i == ii)(functools.partial(run, ii))


def _slcwin(qT, negm, main, vT, gT, ocT, b, seq, tq=TK):
    nq = seq // tq
    n = b * seq

    def col(c):
        return pl.BlockSpec((seq, LANE), lambda bi, i, c=c: (bi, c))

    return pl.pallas_call(
        functools.partial(_slcwin_kernel, tq=tq, nq=nq),
        grid=(b, nq),
        in_specs=[
            pl.BlockSpec((1, QT_ROWS, tq), lambda bi, i: (bi, 0, i)),
            pl.BlockSpec((1, N_GROUPS, N_SEL, tq), lambda bi, i: (bi, 0, 0, i)),
            col(KS_COL), col(KS_COL + 1), col(KW_COL), col(KW_COL + 1),
            pl.BlockSpec((1, N_GROUPS, seq // TK, VT_ROWS, TK), lambda bi, i: (bi, 0, 0, 0, 0)),
            pl.BlockSpec((1, N_GROUPS, seq // TK, VT_ROWS, TK), lambda bi, i: (bi, 1, 0, 0, 0)),
            pl.BlockSpec((1, GT_ROWS, tq), lambda bi, i: (bi, 0, i)),
            pl.BlockSpec((1, QT_ROWS, tq), lambda bi, i: (bi, 0, i)),
        ],
        out_specs=pl.BlockSpec((tq, 4 * LANE), lambda bi, i: (bi * nq + i, 0)),
        out_shape=jax.ShapeDtypeStruct((n, 4 * LANE), BF16),
        scratch_shapes=[
            pltpu.VMEM((2, TK, tq), F32),
            pltpu.VMEM((N_QH, LANE, tq), BF16),
            pltpu.VMEM((SLC_BUFS, N_QH, TK, tq), F32),
            pltpu.VMEM((SLC_BUFS, N_QH, 1, tq), F32),
            pltpu.VMEM((N_QH, 1, tq), F32),
            pltpu.VMEM((N_QH, VT_ROWS, tq), F32),
            pltpu.VMEM((N_QH, 1, tq), F32),
            pltpu.VMEM((N_QH, VT_ROWS, tq), F32),
        ],
        compiler_params=pltpu.CompilerParams(
            dimension_semantics=("arbitrary", "arbitrary"), vmem_limit_bytes=VMEM_LIMIT),
        name="slcwin",
    )(qT, negm, main, main, main, main, vT, vT, gT, ocT)


SWA_AHEAD = 3
SWA_BUFS = SWA_AHEAD + 1

def _swa_kernel(sink_ref, qT_ref, k0, k1, vT_ref, o_ref, tri, mab, s_far, s_ab, s_c, *, tq, nsub):
    bi = pl.program_id(0)
    i = pl.program_id(1)
    w = SWA_WINDOW
    assert tq == 2 * w and w == LANE

    @pl.when((bi == 0) & (i == 0))
    def _init():
        kk = lax.broadcasted_iota(jnp.int32, (w, w), 0)
        qq = lax.broadcasted_iota(jnp.int32, (w, w), 1)
        causal = jnp.where(kk <= qq, 0.0, NEG)
        far = jnp.where(kk > qq, 0.0, NEG)
        tri[0] = far
        tri[1] = causal
        tri[2] = jnp.full((w, w), NEG, F32)
        mab[...] = jnp.concatenate([causal, far], axis=1)

    kk_ref = (k0, k1)
    zmid = jnp.zeros((N_SEL, tq), BF16)
    mts, sinks, outs = {}, {}, {}

    def geometry(sub):
        tile = i * nsub + sub
        return tile, tile * tq

    def score(n):
        sub, h = divmod(n, N_QH)
        g, r = divmod(h, N_REP)
        tile, t0 = geometry(sub)
        far0 = pl.multiple_of(jnp.maximum(t0 - w, 0), w)
        main0 = pl.multiple_of(t0, tq)
        buf = n % SWA_BUFS
        qaT = jnp.concatenate([qT_ref[0, h * HEAD_DIM:(h + 1) * HEAD_DIM, sub * tq:(sub + 1) * tq], zmid,
                               _slope_rows(SLOPES_SWA[g, r], tq)], axis=0)
        sf = _dot(kk_ref[g][pl.ds(far0, w), :], qaT[:, 0:w]) + tri[jnp.where(tile == 0, 2, 0)]
        sab = _dot(kk_ref[g][pl.ds(main0, w), :], qaT) + mab[...]
        sc = _dot(kk_ref[g][pl.ds(main0 + w, w), :], qaT[:, w:tq]) + tri[1]
        s_far[buf] = sf
        s_ab[buf] = sab
        s_c[buf] = sc
        tpos = (t0 + lax.broadcasted_iota(jnp.int32, (1, tq), 1)).astype(F32)
        sink = (sink_ref[g, r] * LOG2E) + float(SLOPES_SWA[g, r] * LOG2E) * tpos
        m_side = jnp.concatenate([jnp.max(sf, axis=0, keepdims=True), jnp.max(sc, axis=0, keepdims=True)], axis=1)
        mts[n] = jnp.maximum(jnp.maximum(jnp.max(sab, axis=0, keepdims=True), m_side), sink)
        sinks[n] = sink

    def value(n):
        sub, h = divmod(n, N_QH)
        g = h // N_REP
        tile, _ = geometry(sub)
        buf = n % SWA_BUFS
        vT_far = vT_ref[0, g, jnp.maximum(tile - 1, 0)][:, w:2 * w]
        vT_main = vT_ref[0, g, tile]
        m = mts.pop(n)
        pf = jnp.exp2(s_far[buf] - m[:, 0:w]).astype(BF16)
        pab = jnp.exp2(s_ab[buf] - m).astype(BF16)
        pc = jnp.exp2(s_c[buf] - m[:, w:tq]).astype(BF16)
        acc = _dot(vT_main[:, 0:w], pab) + jnp.concatenate([_dot(vT_far, pf), _dot(vT_main[:, w:tq], pc)], axis=1)
        den = acc[HEAD_DIM:HEAD_DIM + 1, :] + jnp.exp2(sinks.pop(n) - m)
        outs[n] = acc[0:HEAD_DIM, :] * (1.0 / den)
        if h == N_QH - 1:
            _heads_out(o_ref, [outs.pop(sub * N_QH + hh) for hh in range(N_QH)], rows=slice(sub * tq, (sub + 1) * tq))

    n_items = nsub * N_QH
    for n in range(n_items + SWA_AHEAD):
        if n < n_items:
            score(n)
        if n >= SWA_AHEAD:
            value(n - SWA_AHEAD)


def _swa(sinks, qT, main, vT, b, seq, tq=TK, nsub=4):
    ts = tq * nsub
    nq = seq // ts
    n = b * seq

    def col(c):
        return pl.BlockSpec((seq, LANE), lambda bi, i, c=c: (bi, c))

    return pl.pallas_call(
        functools.partial(_swa_kernel, tq=tq, nsub=nsub),
        grid=(b, nq),
        in_specs=[
            pl.BlockSpec(memory_space=pltpu.SMEM),
            pl.BlockSpec((1, QT_ROWS, ts), lambda bi, i: (bi, 1, i)),
            col(KSW_COL), col(KSW_COL + 1),
            pl.BlockSpec((1, N_GROUPS, seq // TK, VT_ROWS, TK), lambda bi, i: (bi, 2, 0, 0, 0)),
        ],
        out_specs=pl.BlockSpec((ts, 4 * LANE), lambda bi, i: (bi * nq + i, 0)),
        out_shape=jax.ShapeDtypeStruct((n, 4 * LANE), BF16),
        scratch_shapes=[
            pltpu.VMEM((3, SWA_WINDOW, SWA_WINDOW), F32),
            pltpu.VMEM((SWA_WINDOW, tq), F32),
            pltpu.VMEM((SWA_BUFS, SWA_WINDOW, SWA_WINDOW), F32),
            pltpu.VMEM((SWA_BUFS, SWA_WINDOW, tq), F32),
            pltpu.VMEM((SWA_BUFS, SWA_WINDOW, SWA_WINDOW), F32),
        ],
        compiler_params=pltpu.CompilerParams(
            dimension_semantics=("arbitrary", "arbitrary"), vmem_limit_bytes=VMEM_LIMIT),
        name="swa",
    )(sinks, qT, main, main, vT)


def _outmlp_kernel(on_ref, os_ref, x_ref, wo_ref, g2_ref, g3_ref, wu_ref, wd_ref, g4_ref, o_ref, *, ff_chunk, n_sub):
    half = N_HEADS * HEAD_DIM // 2
    tm = x_ref.shape[0]
    subs = [slice(k * tm // n_sub, (k + 1) * tm // n_sub) for k in range(n_sub)]
    mixes = [_dot(on_ref[rs, :], wo_ref[0:half, :]) + _dot(os_ref[rs, :], wo_ref[half:2 * half, :]) for rs in subs]
    for rs, mix in zip(subs, mixes):
        h1 = x_ref[rs, :] + _rms(mix, g2_ref[...])
        m = _rms(h1, g3_ref[...]).astype(BF16)
        acc = jnp.zeros(h1.shape, F32)
        for c in range(D_FF // ff_chunk):
            u = _dot(m, wu_ref[:, c * ff_chunk:(c + 1) * ff_chunk])
            u = jnp.square(jnp.maximum(u, 0.0)).astype(BF16)
            acc = acc + _dot(u, wd_ref[c * ff_chunk:(c + 1) * ff_chunk, :])
        o_ref[rs, :] = h1 + _rms(acc, g4_ref[...])


def _outmlp(onsa, oswa, x2, wo, g2, g3, wu, wd, g4, tm=1024, ff_chunk=1024, n_sub=4):
    n = x2.shape[0]

    def const(shape):
        return pl.BlockSpec(shape, lambda i: (0, 0), pipeline_mode=pl.Buffered(1))

    return pl.pallas_call(
        functools.partial(_outmlp_kernel, ff_chunk=ff_chunk, n_sub=n_sub),
        grid=(n // tm,),
        in_specs=[
            pl.BlockSpec((tm, 4 * LANE), lambda i: (i, 0)),
            pl.BlockSpec((tm, 4 * LANE), lambda i: (i, 0)),
            pl.BlockSpec((tm, D_MODEL), lambda i: (i, 0)),
            const((D_MODEL, D_MODEL)),
            const((1, D_MODEL)),
            const((1, D_MODEL)),
            const((D_MODEL, D_FF)),
            const((D_FF, D_MODEL)),
            const((1, D_MODEL)),
        ],
        out_specs=pl.BlockSpec((tm, D_MODEL), lambda i: (i, 0)),
        out_shape=jax.ShapeDtypeStruct((n, D_MODEL), F32),
        compiler_params=pltpu.CompilerParams(
            dimension_semantics=("arbitrary",), vmem_limit_bytes=VMEM_LIMIT),
        name="outmlp",
    )(onsa, oswa, x2, wo, g2, g3, wu, wd, g4)


def _layout_w_in(w):
    sizes = [512, 128, 128, 128, 128, 128, 128, N_HEADS // 2 * N_BRANCH, 512, 128, 128]
    offs = np.concatenate([[0], np.cumsum(sizes)])
    q_n, kc, vc, ks, vs, kw, vw, gt, q_s, k_s, v_s = [w[:, offs[k]:offs[k + 1]] for k in range(len(sizes))]
    std = [ks, kw, k_s, kc, vc]
    gt = jnp.concatenate([gt, jnp.zeros((w.shape[0], GT_ROWS - gt.shape[1]), w.dtype)], axis=1)
    tr = jnp.concatenate([q_n, q_s, vs, vw, v_s, gt], axis=1).T
    return jnp.concatenate(std, axis=1).astype(BF16), tr.astype(BF16)


def _layout_w1(w1):
    w = w1.reshape(2, CMP_STRIDE, HEAD_DIM, CMP_HIDDEN)
    z = jnp.zeros_like(w)
    top = jnp.concatenate([w, z], axis=-1)
    bot = jnp.concatenate([z, w], axis=-1)
    e = jnp.stack([top, bot], axis=2)
    return e.reshape(2, CMP_STRIDE * 2 * HEAD_DIM, 2 * CMP_HIDDEN).astype(BF16)


def _layout_w2(w2):
    z = jnp.zeros((CMP_HIDDEN, HEAD_DIM), w2.dtype)
    top = jnp.concatenate([w2, z, z, z], axis=1)
    bot = jnp.concatenate([z, z, w2, z], axis=1)
    return jnp.concatenate([top, bot], axis=0).astype(BF16)


def _layout_pe(pe):
    p = pe.reshape(2, CMP_STRIDE, 1, HEAD_DIM)
    return jnp.broadcast_to(p, (2, CMP_STRIDE, 2, HEAD_DIM)).reshape(2, 1, CMP_STRIDE * 2 * HEAD_DIM)


def kernel(x, norm_mix_pre, w_in, cmp_pe_k, cmp_w1_k, cmp_w2_k, cmp_pe_v, cmp_w1_v, cmp_w2_v,
           sinks, w_out, norm_mix_post, norm_mlp_pre, w_up, w_down, norm_mlp_post):
    b, seq, _ = x.shape
    assert seq // SEL_BLOCK == N_SEL and seq <= POS_SPLIT * 256
    depth = w_in.shape[0]
    h = x.reshape(b * seq, D_MODEL)
    for li in range(depth):
        w_p, wT_p = _layout_w_in(w_in[li])
        main, z, qT, vT, gT = _inproj(h, norm_mix_pre[li][None], w_p, wT_p, b, seq)
        pe2 = jnp.stack([_layout_pe(cmp_pe_k[li]), _layout_pe(cmp_pe_v[li])])
        w1e = jnp.stack([_layout_w1(cmp_w1_k[li]), _layout_w1(cmp_w1_v[li])])
        w2e = jnp.stack([_layout_w2(cmp_w2_k[li]), _layout_w2(cmp_w2_v[li])])
        cmp_kv, cmp_kvT = _compress(z, pe2, w1e, w2e, jnp.swapaxes(w2e, 1, 2))
        ocT, negm = _cmp(qT, cmp_kv, cmp_kvT, b, seq)
        onsa = _slcwin(qT, negm, main, vT, gT, ocT, b, seq)
        oswa = _swa(sinks[li], qT, main, vT, b, seq)
        h = _outmlp(onsa, oswa, h, w_out[li].astype(BF16), norm_mix_post[li][None], norm_mlp_pre[li][None],
                    w_up[li].astype(BF16), w_down[li].astype(BF16), norm_mlp_post[li][None])
    return h.reshape(b, seq, D_MODEL)
```

```python
import functools
import math

import jax
import jax.numpy as jnp
import numpy as np
from jax import lax
from jax.experimental import pallas as pl
from jax.experimental.pallas import tpu as pltpu

F32 = jnp.float32
BF16 = jnp.bfloat16

D_MODEL = 1024
HEAD_DIM = 64
N_HEADS = 16
N_GROUPS = 2
N_REP = 4
N_QH = N_GROUPS * N_REP
CMP_BLOCK = 32
CMP_STRIDE = 16
CMP_HIDDEN = 4 * HEAD_DIM
SEL_BLOCK = 64
SEL_TOPN = 8
N_SEL = 32
NSA_WINDOW = 512
SWA_WINDOW = 128
D_FF = 4 * D_MODEL
NORM_EPS = 1e-6
N_BRANCH = 3

LANE = 128
LOG2E = math.log2(math.e)
NEG = -1e30
SEL_NEG = -(2.0 ** 100)
VMEM_LIMIT = 56 * 1024 * 1024

TK = 256
VT_ROWS = 80
SEL_ROW = HEAD_DIM
POS_ROW = HEAD_DIM + N_SEL
N_PIECE = 3
CMP_FEAT = 16
POS_SPLIT = 256

KS_COL, KW_COL, KSW_COL = 0, 2, 4
MAIN_COLS = 6 * LANE
STD_COLS = 5 * LANE
QT_ROWS = N_QH * HEAD_DIM
VT_ROW0 = 2 * QT_ROWS
GT_ROW = VT_ROW0 + 3 * N_GROUPS * HEAD_DIM
GT_ROWS = 32
T_ROWS = GT_ROW + GT_ROWS


def _slopes():
    s = 2.0 ** (-8.0 * (np.arange(N_HEADS) + 1) / N_HEADS)
    nsa = s[0::2].reshape(N_GROUPS, N_REP)
    swa = s[1::2].reshape(N_GROUPS, N_REP)
    return nsa, swa


SLOPES_NSA, SLOPES_SWA = _slopes()


def _bf16_round(x):
    u = np.float32(x).reshape(1).view(np.uint32)
    u = (u + (((u >> 16) & 1) + 0x7FFF)) & np.uint32(0xFFFF0000)
    return float(u.view(np.float32)[0])


def _bf16_pieces(x, n=N_PIECE):
    out, rem = [], float(np.float32(x))
    for _ in range(n):
        p = _bf16_round(rem)
        out.append(p)
        rem = float(np.float32(rem - p))
    return out


_NT = (((1,), (1,)), ((), ()))


def _dot_nt(a, b, **kw):
    return lax.dot_general(a, b, _NT, preferred_element_type=F32, **kw)


def _dot(a, b, **kw):
    return jnp.dot(a, b, preferred_element_type=F32, **kw)


def _rms(v, g):
    return v * lax.rsqrt(jnp.mean(v * v, axis=-1, keepdims=True) + NORM_EPS) * g


def _slope_rows(slope, tq):
    pieces = _bf16_pieces(slope * LOG2E)
    vals = pieces + [p * POS_SPLIT for p in pieces]
    prow = lax.broadcasted_iota(jnp.int32, (LANE - POS_ROW, tq), 0)
    feat = jnp.zeros((LANE - POS_ROW, tq), F32)
    for k, v in enumerate(vals):
        feat = jnp.where(prow == k, v, feat)
    return feat.astype(BF16)


def _inproj_kernel(x_ref, g_ref, w_ref, wT_ref, main_ref, z_ref, qT_ref, vT_ref, gT_ref,
                   kc_scr, vc_scr, *, tm, seq):
    qscale = LOG2E * HEAD_DIM ** -0.5
    lane = lax.broadcasted_iota(jnp.int32, (TK, LANE), 1)
    lo = lane < HEAD_DIM
    ones_blk = jnp.where(lax.broadcasted_iota(jnp.int32, (VT_ROWS - HEAD_DIM, TK), 0) == 0, 1.0, 0.0).astype(BF16)
    t0 = (pl.program_id(0) * tm) % seq

    for st in range(tm // TK):
        rs = slice(st * TK, (st + 1) * TK)
        a = _rms(x_ref[rs, :], g_ref[...]).astype(BF16)

        res = _dot(a, w_ref[...])
        pos = t0 + st * TK + lax.broadcasted_iota(jnp.int32, (TK, LANE), 0)
        posf = jnp.where((lane >= POS_ROW) & (lane < POS_ROW + N_PIECE), (pos % POS_SPLIT).astype(F32), 0.0)
        posf = jnp.where((lane >= POS_ROW + N_PIECE) & (lane < POS_ROW + 2 * N_PIECE),
                         (pos // POS_SPLIT).astype(F32), posf)
        onehot = jnp.where(lane == pos // SEL_BLOCK + SEL_ROW, 1.0, 0.0)
        for t in range(3):
            blk = res[:, t * LANE:(t + 1) * LANE]
            extra = posf + onehot if 2 * t == KS_COL else posf
            for g, src in enumerate((blk, pltpu.roll(blk, HEAD_DIM, 1))):
                c = 2 * t + g
                main_ref[rs, c * LANE:(c + 1) * LANE] = (jnp.where(lo, src, 0.0) + extra).astype(BF16)
        kc_scr[...] = res[:, 3 * LANE:4 * LANE]
        vc_scr[...] = res[:, 4 * LANE:STD_COLS]
        zr = TK // CMP_STRIDE
        for c in range(CMP_STRIDE):
            for t, scr in enumerate((kc_scr, vc_scr)):
                z_ref[t, 0, st * zr:(st + 1) * zr, c * LANE:(c + 1) * LANE] = (
                    scr[pl.ds(c, zr, stride=CMP_STRIDE), :].astype(BF16))

        resT = _dot_nt(wT_ref[...], a)
        qT_ref[0, :, rs] = (resT[0:VT_ROW0] * qscale).astype(BF16)
        for tg in range(3 * N_GROUPS):
            r0 = VT_ROW0 + tg * HEAD_DIM
            vT_ref[0, tg, st, 0:HEAD_DIM, :] = resT[r0:r0 + HEAD_DIM, :].astype(BF16)
            vT_ref[0, tg, st, HEAD_DIM:VT_ROWS, :] = ones_blk
        gT_ref[0, :, rs] = jax.nn.sigmoid(resT[GT_ROW:T_ROWS])


def _inproj(x2, gain, w_p, wT_p, b, seq, tm=1024):
    n = x2.shape[0]
    nt = seq // tm
    return pl.pallas_call(
        functools.partial(_inproj_kernel, tm=tm, seq=seq),
        grid=(n // tm,),
        in_specs=[
            pl.BlockSpec((tm, D_MODEL), lambda i: (i, 0)),
            pl.BlockSpec((1, D_MODEL), lambda i: (0, 0)),
            pl.BlockSpec((D_MODEL, STD_COLS), lambda i: (0, 0)),
            pl.BlockSpec((T_ROWS, D_MODEL), lambda i: (0, 0)),
        ],
        out_specs=[
            pl.BlockSpec((tm, MAIN_COLS), lambda i: (i, 0)),
            pl.BlockSpec((2, 1, tm // CMP_STRIDE, CMP_STRIDE * LANE), lambda i: (0, i // nt, i % nt, 0)),
            pl.BlockSpec((1, VT_ROW0, tm), lambda i: (i // nt, 0, i % nt)),
            pl.BlockSpec((1, 3 * N_GROUPS, tm // TK, VT_ROWS, TK), lambda i: (i // nt, 0, i % nt, 0, 0)),
            pl.BlockSpec((1, GT_ROWS, tm), lambda i: (i // nt, 0, i % nt)),
        ],
        out_shape=[
            jax.ShapeDtypeStruct((n, MAIN_COLS), BF16),
            jax.ShapeDtypeStruct((2, b, seq // CMP_STRIDE, CMP_STRIDE * LANE), BF16),
            jax.ShapeDtypeStruct((b, VT_ROW0, seq), BF16),
            jax.ShapeDtypeStruct((b, 3 * N_GROUPS, seq // TK, VT_ROWS, TK), BF16),
            jax.ShapeDtypeStruct((b, GT_ROWS, seq), F32),
        ],
        scratch_shapes=[pltpu.VMEM((TK, LANE), F32), pltpu.VMEM((TK, LANE), F32)],
        compiler_params=pltpu.CompilerParams(
            dimension_semantics=("arbitrary",), vmem_limit_bytes=VMEM_LIMIT),
        name="inproj",
    )(x2, gain, w_p, wT_p)


def _compress_kernel(z_ref, pe_ref, w1_ref, w2_ref, w2T_ref, o_ref, oT_ref):
    nb, n, k = z_ref.shape[1:]
    z = z_ref[0].reshape(nb * n, k).astype(F32)
    zt = (z + pe_ref[0, 0]).astype(BF16)
    zb = (z + pe_ref[0, 1]).astype(BF16)
    a = _dot(zt, w1_ref[0, 0])
    bm = _dot(zb, w1_ref[0, 1])
    h = a + pltpu.roll(bm, nb * n - 1, 0)
    hg = jax.nn.gelu(h).astype(BF16)
    o = _dot(hg, w2_ref[0])
    lane = lax.broadcasted_iota(jnp.int32, o.shape, 1) % LANE
    nidx = lax.broadcasted_iota(jnp.int32, o.shape, 0) % n
    o = o + jnp.where((lane >= HEAD_DIM) & (lane < HEAD_DIM + N_PIECE), nidx.astype(F32), 0.0)
    for e in range(nb):
        o_ref[0, e] = o[e * n:(e + 1) * n].astype(o_ref.dtype)
        oT_ref[0, e] = _dot_nt(w2T_ref[0], hg[e * n:(e + 1) * n]).astype(oT_ref.dtype)


def _compress(z, pe2, w1e, w2e, w2eT, nb=4):
    _, b, n, k = z.shape
    nb = math.gcd(b, nb)
    return pl.pallas_call(
        _compress_kernel,
        grid=(2, b // nb),
        in_specs=[
            pl.BlockSpec((1, nb, n, k), lambda t, i: (t, i, 0, 0)),
            pl.BlockSpec((1, 2, 1, k), lambda t, i: (t, 0, 0, 0)),
            pl.BlockSpec((1, 2, k, 2 * CMP_HIDDEN), lambda t, i: (t, 0, 0, 0)),
            pl.BlockSpec((1, 2 * CMP_HIDDEN, 2 * LANE), lambda t, i: (t, 0, 0)),
            pl.BlockSpec((1, 2 * LANE, 2 * CMP_HIDDEN), lambda t, i: (t, 0, 0)),
        ],
        out_specs=[
            pl.BlockSpec((1, nb, n, 2 * LANE), lambda t, i: (t, i, 0, 0)),
            pl.BlockSpec((1, nb, 2 * LANE, n), lambda t, i: (t, i, 0, 0)),
        ],
        out_shape=[
            jax.ShapeDtypeStruct((2, b, n, 2 * LANE), BF16),
            jax.ShapeDtypeStruct((2, b, 2 * LANE, n), BF16),
        ],
        compiler_params=pltpu.CompilerParams(
            dimension_semantics=("arbitrary", "arbitrary"), vmem_limit_bytes=VMEM_LIMIT),
        name="compress",
    )(z, pe2, w1e, w2e, w2eT)


def _f32_dot_exact_lhs(a_bf16, x):
    out = None
    rem = x
    for _ in range(N_PIECE):
        piece = rem.astype(BF16)
        rem = rem - piece.astype(F32)
        d = _dot(a_bf16, piece)
        out = d if out is None else out + d
    return out


def _select_mask(score, tq, n_blk):
    sub = 8
    n_chunk = -(-n_blk // sub)
    chunks = [score[sub * c:sub * (c + 1)] for c in range(n_chunk)]
    ranks = [jnp.zeros((sub, tq), jnp.int32) for _ in range(n_chunk)]
    jrow = lax.broadcasted_iota(jnp.int32, (sub, tq), 0)
    for k in range(n_blk):
        sk = score[k:k + 1, :]
        for c in range(n_chunk):
            if k < sub * c:
                ahead = sk >= chunks[c]
            elif k >= sub * (c + 1):
                ahead = sk > chunks[c]
            else:
                ahead = (sk > chunks[c]) | ((sk == chunks[c]) & (jrow + sub * c > k))
            ranks[c] = jnp.where(ahead, ranks[c] + 1, ranks[c])
    negm = [jnp.where(r < SEL_TOPN, 0.0, SEL_NEG) for r in ranks]
    negm += [jnp.full((sub, tq), SEL_NEG, F32)] * (N_SEL // sub - n_chunk)
    return jnp.concatenate(negm, axis=0).astype(BF16)


def _cmp_kernel(qT_ref, k_ref, vT_ref, ocT_ref, negm_ref, s_scr, *, tq, nsub, n_cmp, nq):
    i = pl.program_id(1)
    n_pad = k_ref.shape[2]
    frow = lax.broadcasted_iota(jnp.int32, (CMP_FEAT, tq), 0)
    jj = lax.broadcasted_iota(jnp.int32, (N_SEL, tq), 0)

    def run(ii):
        tiles = []
        for sub in range(nsub):
            t0 = (ii * nsub + sub) * tq
            n_vis = min(n_pad, -(-((t0 + tq) // CMP_STRIDE - 1) // 16) * 16)
            tiles.append((sub, t0, n_vis, (t0 + tq) // SEL_BLOCK))

        for sub, t0, n_vis, _ in tiles:
            qs = slice(sub * tq, (sub + 1) * tq)
            for g in range(N_GROUPS):
                kc = k_ref[0, 0][0:n_vis, g * LANE:g * LANE + HEAD_DIM + CMP_FEAT]
                for r in range(N_REP):
                    h = N_REP * g + r
                    feat = jnp.zeros((CMP_FEAT, tq), F32)
                    for k, pc in enumerate(_bf16_pieces(SLOPES_NSA[g, r] * LOG2E)):
                        feat = jnp.where(frow == k, pc * CMP_STRIDE, feat)
                    qTa = jnp.concatenate([qT_ref[0, h * HEAD_DIM:(h + 1) * HEAD_DIM, qs], feat.astype(BF16)], axis=0)
                    s_scr[sub * N_QH + h, 0:n_vis] = _dot(kc, qTa)

        for sub, t0, n_vis, n_blk in tiles:
            qs = slice(sub * tq, (sub + 1) * tq)
            nn = lax.broadcasted_iota(jnp.int32, (n_vis, tq), 0)
            tt = t0 + lax.broadcasted_iota(jnp.int32, (n_vis, tq), 1)
            maskadd = jnp.where((tt >= nn * CMP_STRIDE + CMP_BLOCK - 1) & (nn < n_cmp), 0.0, NEG)
            colvalid = jnp.where(t0 + lax.broadcasted_iota(jnp.int32, (1, tq), 1) >= CMP_BLOCK - 1, 1.0, 0.0)
            blk_t = (t0 + lax.broadcasted_iota(jnp.int32, (N_SEL, tq), 1)) // SEL_BLOCK
            valid = jj <= blk_t
            forced = (jj == 0) | (jj == blk_t) | (jj == blk_t - 1)
            jr = lax.broadcasted_iota(jnp.int32, (N_SEL, n_vis), 0)
            nc = lax.broadcasted_iota(jnp.int32, (N_SEL, n_vis), 1)
            ov = ((nc * CMP_STRIDE < jr * SEL_BLOCK + SEL_BLOCK) & (nc * CMP_STRIDE + CMP_BLOCK > jr * SEL_BLOCK)
                  & (nc < n_cmp))
            ovT = jnp.where(ov, 1.0, 0.0).astype(BF16)

            for g in range(N_GROUPS):
                vcT = vT_ref[0, 0][g * LANE:g * LANE + HEAD_DIM, 0:n_vis]
                psum = jnp.zeros((n_vis, tq), F32)
                for r in range(N_REP):
                    h = N_REP * g + r
                    sm = s_scr[sub * N_QH + h, 0:n_vis] + maskadd
                    m = jnp.max(sm, axis=0, keepdims=True)
                    e = jnp.exp2(sm - m)
                    den = jnp.sum(e, axis=0, keepdims=True)
                    p = e * (colvalid / den)
                    ocT_ref[0, h * HEAD_DIM:(h + 1) * HEAD_DIM, qs] = _dot(vcT, p.astype(BF16)).astype(ocT_ref.dtype)
                    psum = psum + p

                imp = _f32_dot_exact_lhs(ovT, psum)
                score = jnp.where(valid, imp, -jnp.inf)
                score = jnp.where(forced & valid, jnp.inf, score)
                negm_ref[0, g, :, qs] = _select_mask(score, tq, n_blk)

    for ii in range(nq):
        pl.when(i == ii)(functools.partial(run, ii))


def _cmp(qT, cmp_k, cmp_vT, b, seq, tq=256, nsub=8):
    ts = tq * nsub
    nq = seq // ts
    n_cmp = (seq - CMP_BLOCK) // CMP_STRIDE + 1
    n_pad = cmp_k.shape[2]
    return pl.pallas_call(
        functools.partial(_cmp_kernel, tq=tq, nsub=nsub, n_cmp=n_cmp, nq=nq),
        grid=(b, nq),
        in_specs=[
            pl.BlockSpec((1, QT_ROWS, ts), lambda bi, i: (bi, 0, i)),
            pl.BlockSpec((1, 1, n_pad, 2 * LANE), lambda bi, i: (0, bi, 0, 0)),
            pl.BlockSpec((1, 1, 2 * LANE, n_pad), lambda bi, i: (1, bi, 0, 0)),
        ],
        out_specs=[
            pl.BlockSpec((1, QT_ROWS, ts), lambda bi, i: (bi, 0, i)),
            pl.BlockSpec((1, N_GROUPS, N_SEL, ts), lambda bi, i: (bi, 0, 0, i)),
        ],
        out_shape=[
            jax.ShapeDtypeStruct((b, QT_ROWS, seq), BF16),
            jax.ShapeDtypeStruct((b, N_GROUPS, N_SEL, seq), BF16),
        ],
        scratch_shapes=[pltpu.VMEM((nsub * N_QH, n_pad, tq), F32)],
        compiler_params=pltpu.CompilerParams(
            dimension_semantics=("arbitrary", "arbitrary"), vmem_limit_bytes=VMEM_LIMIT),
        name="cmp",
    )(qT, cmp_k, cmp_vT)


SLC_LAG = 6
SLC_BUFS = SLC_LAG // N_QH + 1

HALF = TK // 2
_TRI = {0: (slice(0, HALF), slice(HALF, TK), slice(HALF, TK)),
        1: (slice(HALF, TK), slice(0, HALF), slice(0, HALF))}


def _score_phase(qa_ref, k_tiles, mask_id, caus, s_buf, mt_buf, heads):
    for g in range(N_GROUPS):
        for r in range(N_REP):
            h = N_REP * g + r
            if h not in heads:
                continue
            qT = qa_ref[h]
            if mask_id is None:
                s = _dot(k_tiles[g], qT)
                s_buf[h] = s
                mt_buf[h] = jnp.max(s, axis=0, keepdims=True)
                continue
            fr, hr, hl = _TRI[mask_id]
            s_full = _dot(k_tiles[g][fr], qT) + caus[mask_id, fr, :]
            s_half = _dot(k_tiles[g][hr], qT[:, hl]) + caus[mask_id, hr, hl]
            s_buf[h, fr, :] = s_full
            s_buf[h, hr, hl] = s_half
            m_half = jnp.max(s_half, axis=0, keepdims=True)
            filler = jnp.full((1, HALF), -3e38, F32)
            m_half = jnp.concatenate([filler, m_half] if hl.start else [m_half, filler], axis=1)
            mt_buf[h] = jnp.maximum(jnp.max(s_full, axis=0, keepdims=True), m_half)


def _value_phase(vT_tiles, mask_id, s_buf, mt_buf, m_scr, acc_scr, heads):
    for g in range(N_GROUPS):
        for r in range(N_REP):
            h = N_REP * g + r
            if h not in heads:
                continue
            m_old = m_scr[h]
            m_new = jnp.maximum(m_old, mt_buf[h])
            alpha = jnp.exp2(m_old - m_new)
            if mask_id is None:
                pT = jnp.exp2(s_buf[h] - m_new).astype(BF16)
                upd = _dot(vT_tiles[g], pT)
            else:
                fr, hr, hl = _TRI[mask_id]
                p_full = jnp.exp2(s_buf[h, fr, :] - m_new).astype(BF16)
                p_half = jnp.exp2(s_buf[h, hr, hl] - m_new[:, hl]).astype(BF16)
                u_half = _dot(vT_tiles[g][:, hr], p_half)
                zero = jnp.zeros_like(u_half)
                upd = _dot(vT_tiles[g][:, fr], p_full) + jnp.concatenate(
                    [zero, u_half] if hl.start else [u_half, zero], axis=1)
            acc_scr[h] = alpha * acc_scr[h] + upd
            m_scr[h] = m_new


def _sweep_reset(m_scr, acc_scr):
    m_scr[...] = jnp.full(m_scr.shape, -3e38, F32)
    acc_scr[...] = jnp.zeros(acc_scr.shape, F32)


def _heads_out(o_ref, outs, rows=slice(None)):
    for g in range(N_GROUPS):
        oT = jnp.concatenate(outs[N_REP * g:N_REP * (g + 1)], axis=0)
        o_ref[rows, g * 2 * LANE:(g + 1) * 2 * LANE] = oT.T.astype(o_ref.dtype)


def _slcwin_kernel(qT_ref, negm_ref, ks0, ks1, kw0, kw1, vsT_ref, vwT_ref, gT_ref, ocT_ref,
                   o_ref, caus, qa_scr, s_scr, mt_scr, m_s, acc_s, m_w, acc_w, *, tq, nq):
    tk = TK
    bi = pl.program_id(0)
    i = pl.program_id(1)
    assert NSA_WINDOW // tk == 2 and tq == tk

    @pl.when((bi == 0) & (i == 0))
    def _init():
        kk = lax.broadcasted_iota(jnp.int32, (tk, tq), 0)
        qq = lax.broadcasted_iota(jnp.int32, (tk, tq), 1)
        caus[0] = jnp.where(kk <= qq, 0.0, NEG)
        caus[1] = jnp.where(kk > qq, 0.0, NEG)

    ks = (ks0, ks1)
    kw = (kw0, kw1)

    for h in range(N_QH):
        g, r = divmod(h, N_REP)
        qa_scr[h] = jnp.concatenate([qT_ref[0, h * HEAD_DIM:(h + 1) * HEAD_DIM, :], negm_ref[0, g],
                                     _slope_rows(SLOPES_NSA[g, r], tq)], axis=0)

    def run(ii):
        _sweep_reset(m_s, acc_s)
        _sweep_reset(m_w, acc_w)
        stream = []
        for d in (2, 1, 0):
            if ii - d >= 0:
                stream.append((kw, vwT_ref, ii - d, {0: 0, 2: 1}.get(d), m_w, acc_w))
        for j in range(ii + 1):
            stream.append((ks, vsT_ref, j, 0 if j == ii else None, m_s, acc_s))

        def score(p, heads):
            refs, _, idx, mask_id, _, _ = stream[p]
            buf = p % SLC_BUFS
            _score_phase(qa_scr, [refs[g][idx * tk:(idx + 1) * tk, :] for g in range(N_GROUPS)],
                         mask_id, caus, s_scr.at[buf], mt_scr.at[buf], heads)

        def value(p, heads):
            _, vref, idx, mask_id, m_scr, acc_scr = stream[p]
            buf = p % SLC_BUFS
            _value_phase([vref[0, g, idx] for g in range(N_GROUPS)], mask_id,
                         s_scr.at[buf], mt_scr.at[buf], m_scr, acc_scr, heads)

        n_items = len(stream) * N_QH
        for n in range(n_items + SLC_LAG):
            if n < n_items:
                score(n // N_QH, (n % N_QH,))
            if n >= SLC_LAG:
                value((n - SLC_LAG) // N_QH, ((n - SLC_LAG) % N_QH,))

        gT = gT_ref[0]
        outs = []
        for h in range(N_QH):
            gc = h * N_BRANCH
            f_slc = gT[gc + 1:gc + 2] * (1.0 / acc_s[h, HEAD_DIM:HEAD_DIM + 1, :])
            f_win = gT[gc + 2:gc + 3] * (1.0 / acc_w[h, HEAD_DIM:HEAD_DIM + 1, :])
            outs.append(gT[gc:gc + 1] * ocT_ref[0, h * HEAD_DIM:(h + 1) * HEAD_DIM, :]
                        + f_slc * acc_s[h, 0:HEAD_DIM, :]
                        + f_win * acc_w[h, 0:HEAD_DIM, :])
        _heads_out(o_ref, outs)

    for ii in range(nq):
        pl.when(i == ii)(functools.partial(run, ii))


def _slcwin(qT, negm, main, vT, gT, ocT, b, seq, tq=TK):
    nq = seq // tq
    n = b * seq

    def col(c):
        return pl.BlockSpec((seq, LANE), lambda bi, i, c=c: (bi, c))

    return pl.pallas_call(
        functools.partial(_slcwin_kernel, tq=tq, nq=nq),
        grid=(b, nq),
        in_specs=[
            pl.BlockSpec((1, QT_ROWS, tq), lambda bi, i: (bi, 0, i)),
            pl.BlockSpec((1, N_GROUPS, N_SEL, tq), lambda bi, i: (bi, 0, 0, i)),
            col(KS_COL), col(KS_COL + 1), col(KW_COL), col(KW_COL + 1),
            pl.BlockSpec((1, N_GROUPS, seq // TK, VT_ROWS, TK), lambda bi, i: (bi, 0, 0, 0, 0)),
            pl.BlockSpec((1, N_GROUPS, seq // TK, VT_ROWS, TK), lambda bi, i: (bi, 1, 0, 0, 0)),
            pl.BlockSpec((1, GT_ROWS, tq), lambda bi, i: (bi, 0, i)),
            pl.BlockSpec((1, QT_ROWS, tq), lambda bi, i: (bi, 0, i)),
        ],
        out_specs=pl.BlockSpec((tq, 4 * LANE), lambda bi, i: (bi * nq + i, 0)),
        out_shape=jax.ShapeDtypeStruct((n, 4 * LANE), BF16),
        scratch_shapes=[
            pltpu.VMEM((2, TK, tq), F32),
            pltpu.VMEM((N_QH, LANE, tq), BF16),
            pltpu.VMEM((SLC_BUFS, N_QH, TK, tq), F32),
            pltpu.VMEM((SLC_BUFS, N_QH, 1, tq), F32),
            pltpu.VMEM((N_QH, 1, tq), F32),
            pltpu.VMEM((N_QH, VT_ROWS, tq), F32),
            pltpu.VMEM((N_QH, 1, tq), F32),
            pltpu.VMEM((N_QH, VT_ROWS, tq), F32),
        ],
        compiler_params=pltpu.CompilerParams(
            dimension_semantics=("arbitrary", "arbitrary"), vmem_limit_bytes=VMEM_LIMIT),
        name="slcwin",
    )(qT, negm, main, main, main, main, vT, vT, gT, ocT)


SWA_AHEAD = 3
SWA_BUFS = SWA_AHEAD + 1

def _swa_kernel(sink_ref, qT_ref, k0, k1, vT_ref, o_ref, tri, mab, s_far, s_ab, s_c, *, tq, nsub):
    bi = pl.program_id(0)
    i = pl.program_id(1)
    w = SWA_WINDOW
    assert tq == 2 * w and w == LANE

    @pl.when((bi == 0) & (i == 0))
    def _init():
        kk = lax.broadcasted_iota(jnp.int32, (w, w), 0)
        qq = lax.broadcasted_iota(jnp.int32, (w, w), 1)
        causal = jnp.where(kk <= qq, 0.0, NEG)
        far = jnp.where(kk > qq, 0.0, NEG)
        tri[0] = far
        tri[1] = causal
        tri[2] = jnp.full((w, w), NEG, F32)
        mab[...] = jnp.concatenate([causal, far], axis=1)

    kk_ref = (k0, k1)
    zmid = jnp.zeros((N_SEL, tq), BF16)
    mts, sinks, outs = {}, {}, {}

    def geometry(sub):
        tile = i * nsub + sub
        return tile, tile * tq

    def score(n):
        sub, h = divmod(n, N_QH)
        g, r = divmod(h, N_REP)
        tile, t0 = geometry(sub)
        far0 = pl.multiple_of(jnp.maximum(t0 - w, 0), w)
        main0 = pl.multiple_of(t0, tq)
        buf = n % SWA_BUFS
        qaT = jnp.concatenate([qT_ref[0, h * HEAD_DIM:(h + 1) * HEAD_DIM, sub * tq:(sub + 1) * tq], zmid,
                               _slope_rows(SLOPES_SWA[g, r], tq)], axis=0)
        sf = _dot(kk_ref[g][pl.ds(far0, w), :], qaT[:, 0:w]) + tri[jnp.where(tile == 0, 2, 0)]
        sab = _dot(kk_ref[g][pl.ds(main0, w), :], qaT) + mab[...]
        sc = _dot(kk_ref[g][pl.ds(main0 + w, w), :], qaT[:, w:tq]) + tri[1]
        s_far[buf] = sf
        s_ab[buf] = sab
        s_c[buf] = sc
        tpos = (t0 + lax.broadcasted_iota(jnp.int32, (1, tq), 1)).astype(F32)
        sink = (sink_ref[g, r] * LOG2E) + float(SLOPES_SWA[g, r] * LOG2E) * tpos
        m_side = jnp.concatenate([jnp.max(sf, axis=0, keepdims=True), jnp.max(sc, axis=0, keepdims=True)], axis=1)
        mts[n] = jnp.maximum(jnp.maximum(jnp.max(sab, axis=0, keepdims=True), m_side), sink)
        sinks[n] = sink

    def value(n):
        sub, h = divmod(n, N_QH)
        g = h // N_REP
        tile, _ = geometry(sub)
        buf = n % SWA_BUFS
        vT_far = vT_ref[0, g, jnp.maximum(tile - 1, 0)][:, w:2 * w]
        vT_main = vT_ref[0, g, tile]
        m = mts.pop(n)
        pf = jnp.exp2(s_far[buf] - m[:, 0:w]).astype(BF16)
        pab = jnp.exp2(s_ab[buf] - m).astype(BF16)
        pc = jnp.exp2(s_c[buf] - m[:, w:tq]).astype(BF16)
        acc = _dot(vT_main[:, 0:w], pab) + jnp.concatenate([_dot(vT_far, pf), _dot(vT_main[:, w:tq], pc)], axis=1)
        den = acc[HEAD_DIM:HEAD_DIM + 1, :] + jnp.exp2(sinks.pop(n) - m)
        outs[n] = acc[0:HEAD_DIM, :] * (1.0 / den)
        if h == N_QH - 1:
            _heads_out(o_ref, [outs.pop(sub * N_QH + hh) for hh in range(N_QH)], rows=slice(sub * tq, (sub + 1) * tq))

    n_items = nsub * N_QH
    for n in range(n_items + SWA_AHEAD):
        if n < n_items:
            score(n)
        if n >= SWA_AHEAD:
            value(n - SWA_AHEAD)


def _swa(sinks, qT, main, vT, b, seq, tq=TK, nsub=8):
    ts = tq * nsub
    nq = seq // ts
    n = b * seq

    def col(c):
        return pl.BlockSpec((seq, LANE), lambda bi, i, c=c: (bi, c))

    return pl.pallas_call(
        functools.partial(_swa_kernel, tq=tq, nsub=nsub),
        grid=(b, nq),
        in_specs=[
            pl.BlockSpec(memory_space=pltpu.SMEM),
            pl.BlockSpec((1, QT_ROWS, ts), lambda bi, i: (bi, 1, i)),
            col(KSW_COL), col(KSW_COL + 1),
            pl.BlockSpec((1, N_GROUPS, seq // TK, VT_ROWS, TK), lambda bi, i: (bi, 2, 0, 0, 0)),
        ],
        out_specs=pl.BlockSpec((ts, 4 * LANE), lambda bi, i: (bi * nq + i, 0)),
        out_shape=jax.ShapeDtypeStruct((n, 4 * LANE), BF16),
        scratch_shapes=[
            pltpu.VMEM((3, SWA_WINDOW, SWA_WINDOW), F32),
            pltpu.VMEM((SWA_WINDOW, tq), F32),
            pltpu.VMEM((SWA_BUFS, SWA_WINDOW, SWA_WINDOW), F32),
            pltpu.VMEM((SWA_BUFS, SWA_WINDOW, tq), F32),
            pltpu.VMEM((SWA_BUFS, SWA_WINDOW, SWA_WINDOW), F32),
        ],
        compiler_params=pltpu.CompilerParams(
            dimension_semantics=("arbitrary", "arbitrary"), vmem_limit_bytes=VMEM_LIMIT),
        name="swa",
    )(sinks, qT, main, main, vT)


def _outmlp_kernel(on_ref, os_ref, x_ref, wo_ref, g2_ref, g3_ref, wu_ref, wd_ref, g4_ref, o_ref, *, ff_chunk, n_sub):
    half = N_HEADS * HEAD_DIM // 2
    tm = x_ref.shape[0]
    subs = [slice(k * tm // n_sub, (k + 1) * tm // n_sub) for k in range(n_sub)]
    mixes = [_dot(on_ref[rs, :], wo_ref[0:half, :]) + _dot(os_ref[rs, :], wo_ref[half:2 * half, :]) for rs in subs]
    for rs, mix in zip(subs, mixes):
        h1 = x_ref[rs, :] + _rms(mix, g2_ref[...])
        m = _rms(h1, g3_ref[...]).astype(BF16)
        acc = jnp.zeros(h1.shape, F32)
        for c in range(D_FF // ff_chunk):
            u = _dot(m, wu_ref[:, c * ff_chunk:(c + 1) * ff_chunk])
            u = jnp.square(jnp.maximum(u, 0.0)).astype(BF16)
            acc = acc + _dot(u, wd_ref[c * ff_chunk:(c + 1) * ff_chunk, :])
        o_ref[rs, :] = h1 + _rms(acc, g4_ref[...])


def _outmlp(onsa, oswa, x2, wo, g2, g3, wu, wd, g4, tm=1024, ff_chunk=1024, n_sub=4):
    n = x2.shape[0]

    def const(shape):
        return pl.BlockSpec(shape, lambda i: (0, 0), pipeline_mode=pl.Buffered(1))

    return pl.pallas_call(
        functools.partial(_outmlp_kernel, ff_chunk=ff_chunk, n_sub=n_sub),
        grid=(n // tm,),
        in_specs=[
            pl.BlockSpec((tm, 4 * LANE), lambda i: (i, 0)),
            pl.BlockSpec((tm, 4 * LANE), lambda i: (i, 0)),
            pl.BlockSpec((tm, D_MODEL), lambda i: (i, 0)),
            const((D_MODEL, D_MODEL)),
            const((1, D_MODEL)),
            const((1, D_MODEL)),
            const((D_MODEL, D_FF)),
            const((D_FF, D_MODEL)),
            const((1, D_MODEL)),
        ],
        out_specs=pl.BlockSpec((tm, D_MODEL), lambda i: (i, 0)),
        out_shape=jax.ShapeDtypeStruct((n, D_MODEL), F32),
        compiler_params=pltpu.CompilerParams(
            dimension_semantics=("arbitrary",), vmem_limit_bytes=VMEM_LIMIT),
        name="outmlp",
    )(onsa, oswa, x2, wo, g2, g3, wu, wd, g4)


def _layout_w_in(w):
    sizes = [512, 128, 128, 128, 128, 128, 128, N_HEADS // 2 * N_BRANCH, 512, 128, 128]
    offs = np.concatenate([[0], np.cumsum(sizes)])
    q_n, kc, vc, ks, vs, kw, vw, gt, q_s, k_s, v_s = [w[:, offs[k]:offs[k + 1]] for k in range(len(sizes))]
    std = [ks, kw, k_s, kc, vc]
    gt = jnp.concatenate([gt, jnp.zeros((w.shape[0], GT_ROWS - gt.shape[1]), w.dtype)], axis=1)
    tr = jnp.concatenate([q_n, q_s, vs, vw, v_s, gt], axis=1).T
    return jnp.concatenate(std, axis=1).astype(BF16), tr.astype(BF16)


def _layout_w1(w1):
    w = w1.reshape(2, CMP_STRIDE, HEAD_DIM, CMP_HIDDEN)
    z = jnp.zeros_like(w)
    top = jnp.concatenate([w, z], axis=-1)
    bot = jnp.concatenate([z, w], axis=-1)
    e = jnp.stack([top, bot], axis=2)
    return e.reshape(2, CMP_STRIDE * 2 * HEAD_DIM, 2 * CMP_HIDDEN).astype(BF16)


def _layout_w2(w2):
    z = jnp.zeros((CMP_HIDDEN, HEAD_DIM), w2.dtype)
    top = jnp.concatenate([w2, z, z, z], axis=1)
    bot = jnp.concatenate([z, z, w2, z], axis=1)
    return jnp.concatenate([top, bot], axis=0).astype(BF16)


def _layout_pe(pe):
    p = pe.reshape(2, CMP_STRIDE, 1, HEAD_DIM)
    return jnp.broadcast_to(p, (2, CMP_STRIDE, 2, HEAD_DIM)).reshape(2, 1, CMP_STRIDE * 2 * HEAD_DIM)


def kernel(x, norm_mix_pre, w_in, cmp_pe_k, cmp_w1_k, cmp_w2_k, cmp_pe_v, cmp_w1_v, cmp_w2_v,
           sinks, w_out, norm_mix_post, norm_mlp_pre, w_up, w_down, norm_mlp_post):
    b, seq, _ = x.shape
    assert seq // SEL_BLOCK == N_SEL and seq <= POS_SPLIT * 256
    depth = w_in.shape[0]
    h = x.reshape(b * seq, D_MODEL)
    for li in range(depth):
        w_p, wT_p = _layout_w_in(w_in[li])
        main, z, qT, vT, gT = _inproj(h, norm_mix_pre[li][None], w_p, wT_p, b, seq)
        pe2 = jnp.stack([_layout_pe(cmp_pe_k[li]), _layout_pe(cmp_pe_v[li])])
        w1e = jnp.stack([_layout_w1(cmp_w1_k[li]), _layout_w1(cmp_w1_v[li])])
        w2e = jnp.stack([_layout_w2(cmp_w2_k[li]), _layout_w2(cmp_w2_v[li])])
        cmp_kv, cmp_kvT = _compress(z, pe2, w1e, w2e, jnp.swapaxes(w2e, 1, 2))
        ocT, negm = _cmp(qT, cmp_kv, cmp_kvT, b, seq)
        onsa = _slcwin(qT, negm, main, vT, gT, ocT, b, seq)
        oswa = _swa(sinks[li], qT, main, vT, b, seq)
        h = _outmlp(onsa, oswa, h, w_out[li].astype(BF16), norm_mix_post[li][None], norm_mlp_pre[li][None],
                    w_up[li].astype(BF16), w_down[li].astype(BF16), norm_mlp_post[li][None])
    return h.reshape(b, seq, D_MODEL)
```

```python
import functools
import math

import jax
import jax.numpy as jnp
import numpy as np
from jax import lax
from jax.experimental import pallas as pl
from jax.experimental.pallas import tpu as pltpu

F32 = jnp.float32
BF16 = jnp.bfloat16

D_MODEL = 1024
HEAD_DIM = 64
N_HEADS = 16
N_GROUPS = 2
N_REP = 4
N_QH = N_GROUPS * N_REP
CMP_BLOCK = 32
CMP_STRIDE = 16
CMP_HIDDEN = 4 * HEAD_DIM
SEL_BLOCK = 64
SEL_TOPN = 8
N_SEL = 32
NSA_WINDOW = 512
SWA_WINDOW = 128
D_FF = 4 * D_MODEL
NORM_EPS = 1e-6
N_BRANCH = 3

LANE = 128
LOG2E = math.log2(math.e)
NEG = -1e30
SEL_NEG = -(2.0 ** 100)
VMEM_LIMIT = 56 * 1024 * 1024

TK = 256
VT_ROWS = 80
SEL_ROW = HEAD_DIM
POS_ROW = HEAD_DIM + N_SEL
N_PIECE = 3
CMP_FEAT = 16
POS_SPLIT = 256

KS_COL, KW_COL, KSW_COL = 0, 2, 4
MAIN_COLS = 6 * LANE
STD_COLS = 5 * LANE
QT_ROWS = N_QH * HEAD_DIM
VT_ROW0 = 2 * QT_ROWS
GT_ROW = VT_ROW0 + 3 * N_GROUPS * HEAD_DIM
GT_ROWS = 32
T_ROWS = GT_ROW + GT_ROWS


def _slopes():
    s = 2.0 ** (-8.0 * (np.arange(N_HEADS) + 1) / N_HEADS)
    nsa = s[0::2].reshape(N_GROUPS, N_REP)
    swa = s[1::2].reshape(N_GROUPS, N_REP)
    return nsa, swa


SLOPES_NSA, SLOPES_SWA = _slopes()


def _bf16_round(x):
    u = np.float32(x).reshape(1).view(np.uint32)
    u = (u + (((u >> 16) & 1) + 0x7FFF)) & np.uint32(0xFFFF0000)
    return float(u.view(np.float32)[0])


def _bf16_pieces(x, n=N_PIECE):
    out, rem = [], float(np.float32(x))
    for _ in range(n):
        p = _bf16_round(rem)
        out.append(p)
        rem = float(np.float32(rem - p))
    return out


_NT = (((1,), (1,)), ((), ()))


def _dot_nt(a, b, **kw):
    return lax.dot_general(a, b, _NT, preferred_element_type=F32, **kw)


def _dot(a, b, **kw):
    return jnp.dot(a, b, preferred_element_type=F32, **kw)


def _rms(v, g):
    return v * lax.rsqrt(jnp.mean(v * v, axis=-1, keepdims=True) + NORM_EPS) * g


def _slope_rows(slope, tq):
    pieces = _bf16_pieces(slope * LOG2E)
    vals = pieces + [p * POS_SPLIT for p in pieces]
    prow = lax.broadcasted_iota(jnp.int32, (LANE - POS_ROW, tq), 0)
    feat = jnp.zeros((LANE - POS_ROW, tq), F32)
    for k, v in enumerate(vals):
        feat = jnp.where(prow == k, v, feat)
    return feat.astype(BF16)


def _inproj_kernel(x_ref, g_ref, w_ref, wT_ref, main_ref, z_ref, qT_ref, vT_ref, gT_ref,
                   kc_scr, vc_scr, *, tm, seq):
    qscale = LOG2E * HEAD_DIM ** -0.5
    lane = lax.broadcasted_iota(jnp.int32, (TK, LANE), 1)
    lo = lane < HEAD_DIM
    ones_blk = jnp.where(lax.broadcasted_iota(jnp.int32, (VT_ROWS - HEAD_DIM, TK), 0) == 0, 1.0, 0.0).astype(BF16)
    t0 = (pl.program_id(0) * tm) % seq

    for st in range(tm // TK):
        rs = slice(st * TK, (st + 1) * TK)
        a = _rms(x_ref[rs, :], g_ref[...]).astype(BF16)

        res = _dot(a, w_ref[...])
        pos = t0 + st * TK + lax.broadcasted_iota(jnp.int32, (TK, LANE), 0)
        posf = jnp.where((lane >= POS_ROW) & (lane < POS_ROW + N_PIECE), (pos % POS_SPLIT).astype(F32), 0.0)
        posf = jnp.where((lane >= POS_ROW + N_PIECE) & (lane < POS_ROW + 2 * N_PIECE),
                         (pos // POS_SPLIT).astype(F32), posf)
        onehot = jnp.where(lane == pos // SEL_BLOCK + SEL_ROW, 1.0, 0.0)
        for t in range(3):
            blk = res[:, t * LANE:(t + 1) * LANE]
            extra = posf + onehot if 2 * t == KS_COL else posf
            for g, src in enumerate((blk, pltpu.roll(blk, HEAD_DIM, 1))):
                c = 2 * t + g
                main_ref[rs, c * LANE:(c + 1) * LANE] = (jnp.where(lo, src, 0.0) + extra).astype(BF16)
        kc_scr[...] = res[:, 3 * LANE:4 * LANE]
        vc_scr[...] = res[:, 4 * LANE:STD_COLS]
        zr = TK // CMP_STRIDE
        for c in range(CMP_STRIDE):
            for t, scr in enumerate((kc_scr, vc_scr)):
                z_ref[t, 0, st * zr:(st + 1) * zr, c * LANE:(c + 1) * LANE] = (
                    scr[pl.ds(c, zr, stride=CMP_STRIDE), :].astype(BF16))

        resT = _dot_nt(wT_ref[...], a)
        qT_ref[0, :, rs] = (resT[0:VT_ROW0] * qscale).astype(BF16)
        for tg in range(3 * N_GROUPS):
            r0 = VT_ROW0 + tg * HEAD_DIM
            vT_ref[0, tg, st, 0:HEAD_DIM, :] = resT[r0:r0 + HEAD_DIM, :].astype(BF16)
            vT_ref[0, tg, st, HEAD_DIM:VT_ROWS, :] = ones_blk
        gT_ref[0, :, rs] = jax.nn.sigmoid(resT[GT_ROW:T_ROWS])


def _inproj(x2, gain, w_p, wT_p, b, seq, tm=2048):
    n = x2.shape[0]
    nt = seq // tm
    return pl.pallas_call(
        functools.partial(_inproj_kernel, tm=tm, seq=seq),
        grid=(n // tm,),
        in_specs=[
            pl.BlockSpec((tm, D_MODEL), lambda i: (i, 0)),
            pl.BlockSpec((1, D_MODEL), lambda i: (0, 0)),
            pl.BlockSpec((D_MODEL, STD_COLS), lambda i: (0, 0)),
            pl.BlockSpec((T_ROWS, D_MODEL), lambda i: (0, 0)),
        ],
        out_specs=[
            pl.BlockSpec((tm, MAIN_COLS), lambda i: (i, 0)),
            pl.BlockSpec((2, 1, tm // CMP_STRIDE, CMP_STRIDE * LANE), lambda i: (0, i // nt, i % nt, 0)),
            pl.BlockSpec((1, VT_ROW0, tm), lambda i: (i // nt, 0, i % nt)),
            pl.BlockSpec((1, 3 * N_GROUPS, tm // TK, VT_ROWS, TK), lambda i: (i // nt, 0, i % nt, 0, 0)),
            pl.BlockSpec((1, GT_ROWS, tm), lambda i: (i // nt, 0, i % nt)),
        ],
        out_shape=[
            jax.ShapeDtypeStruct((n, MAIN_COLS), BF16),
            jax.ShapeDtypeStruct((2, b, seq // CMP_STRIDE, CMP_STRIDE * LANE), BF16),
            jax.ShapeDtypeStruct((b, VT_ROW0, seq), BF16),
            jax.ShapeDtypeStruct((b, 3 * N_GROUPS, seq // TK, VT_ROWS, TK), BF16),
            jax.ShapeDtypeStruct((b, GT_ROWS, seq), F32),
        ],
        scratch_shapes=[pltpu.VMEM((TK, LANE), F32), pltpu.VMEM((TK, LANE), F32)],
        compiler_params=pltpu.CompilerParams(
            dimension_semantics=("arbitrary",), vmem_limit_bytes=VMEM_LIMIT),
        name="inproj",
    )(x2, gain, w_p, wT_p)


def _compress_kernel(z_ref, pe_ref, w1_ref, w2_ref, w2T_ref, o_ref, oT_ref):
    nb, n, k = z_ref.shape[1:]
    z = z_ref[0].reshape(nb * n, k).astype(F32)
    zt = (z + pe_ref[0, 0]).astype(BF16)
    zb = (z + pe_ref[0, 1]).astype(BF16)
    a = _dot(zt, w1_ref[0, 0])
    bm = _dot(zb, w1_ref[0, 1])
    h = a + pltpu.roll(bm, nb * n - 1, 0)
    hg = jax.nn.gelu(h).astype(BF16)
    o = _dot(hg, w2_ref[0])
    lane = lax.broadcasted_iota(jnp.int32, o.shape, 1) % LANE
    nidx = lax.broadcasted_iota(jnp.int32, o.shape, 0) % n
    o = o + jnp.where((lane >= HEAD_DIM) & (lane < HEAD_DIM + N_PIECE), nidx.astype(F32), 0.0)
    for e in range(nb):
        o_ref[0, e] = o[e * n:(e + 1) * n].astype(o_ref.dtype)
        oT_ref[0, e] = _dot_nt(w2T_ref[0], hg[e * n:(e + 1) * n]).astype(oT_ref.dtype)


def _compress(z, pe2, w1e, w2e, w2eT, nb=4):
    _, b, n, k = z.shape
    nb = math.gcd(b, nb)
    return pl.pallas_call(
        _compress_kernel,
        grid=(2, b // nb),
        in_specs=[
            pl.BlockSpec((1, nb, n, k), lambda t, i: (t, i, 0, 0)),
            pl.BlockSpec((1, 2, 1, k), lambda t, i: (t, 0, 0, 0)),
            pl.BlockSpec((1, 2, k, 2 * CMP_HIDDEN), lambda t, i: (t, 0, 0, 0)),
            pl.BlockSpec((1, 2 * CMP_HIDDEN, 2 * LANE), lambda t, i: (t, 0, 0)),
            pl.BlockSpec((1, 2 * LANE, 2 * CMP_HIDDEN), lambda t, i: (t, 0, 0)),
        ],
        out_specs=[
            pl.BlockSpec((1, nb, n, 2 * LANE), lambda t, i: (t, i, 0, 0)),
            pl.BlockSpec((1, nb, 2 * LANE, n), lambda t, i: (t, i, 0, 0)),
        ],
        out_shape=[
            jax.ShapeDtypeStruct((2, b, n, 2 * LANE), BF16),
            jax.ShapeDtypeStruct((2, b, 2 * LANE, n), BF16),
        ],
        compiler_params=pltpu.CompilerParams(
            dimension_semantics=("arbitrary", "arbitrary"), vmem_limit_bytes=VMEM_LIMIT),
        name="compress",
    )(z, pe2, w1e, w2e, w2eT)


def _f32_dot_exact_lhs(a_bf16, x):
    out = None
    rem = x
    for _ in range(N_PIECE):
        piece = rem.astype(BF16)
        rem = rem - piece.astype(F32)
        d = _dot(a_bf16, piece)
        out = d if out is None else out + d
    return out


def _select_mask(score, tq, n_blk):
    sub = 8
    n_chunk = -(-n_blk // sub)
    chunks = [score[sub * c:sub * (c + 1)] for c in range(n_chunk)]
    ranks = [jnp.zeros((sub, tq), jnp.int32) for _ in range(n_chunk)]
    jrow = lax.broadcasted_iota(jnp.int32, (sub, tq), 0)
    for k in range(n_blk):
        sk = score[k:k + 1, :]
        for c in range(n_chunk):
            if k < sub * c:
                ahead = sk >= chunks[c]
            elif k >= sub * (c + 1):
                ahead = sk > chunks[c]
            else:
                ahead = (sk > chunks[c]) | ((sk == chunks[c]) & (jrow + sub * c > k))
            ranks[c] = jnp.where(ahead, ranks[c] + 1, ranks[c])
    negm = [jnp.where(r < SEL_TOPN, 0.0, SEL_NEG) for r in ranks]
    negm += [jnp.full((sub, tq), SEL_NEG, F32)] * (N_SEL // sub - n_chunk)
    return jnp.concatenate(negm, axis=0).astype(BF16)


def _cmp_kernel(qT_ref, k_ref, vT_ref, ocT_ref, negm_ref, s_scr, *, tq, nsub, n_cmp, nq):
    i = pl.program_id(1)
    n_pad = k_ref.shape[2]
    frow = lax.broadcasted_iota(jnp.int32, (CMP_FEAT, tq), 0)
    jj = lax.broadcasted_iota(jnp.int32, (N_SEL, tq), 0)

    def run(ii):
        tiles = []
        for sub in range(nsub):
            t0 = (ii * nsub + sub) * tq
            n_vis = min(n_pad, -(-((t0 + tq) // CMP_STRIDE - 1) // 16) * 16)
            tiles.append((sub, t0, n_vis, (t0 + tq) // SEL_BLOCK))

        for sub, t0, n_vis, _ in tiles:
            qs = slice(sub * tq, (sub + 1) * tq)
            for g in range(N_GROUPS):
                kc = k_ref[0, 0][0:n_vis, g * LANE:g * LANE + HEAD_DIM + CMP_FEAT]
                for r in range(N_REP):
                    h = N_REP * g + r
                    feat = jnp.zeros((CMP_FEAT, tq), F32)
                    for k, pc in enumerate(_bf16_pieces(SLOPES_NSA[g, r] * LOG2E)):
                        feat = jnp.where(frow == k, pc * CMP_STRIDE, feat)
                    qTa = jnp.concatenate([qT_ref[0, h * HEAD_DIM:(h + 1) * HEAD_DIM, qs], feat.astype(BF16)], axis=0)
                    s_scr[sub * N_QH + h, 0:n_vis] = _dot(kc, qTa)

        for sub, t0, n_vis, n_blk in tiles:
            qs = slice(sub * tq, (sub + 1) * tq)
            nn = lax.broadcasted_iota(jnp.int32, (n_vis, tq), 0)
            tt = t0 + lax.broadcasted_iota(jnp.int32, (n_vis, tq), 1)
            maskadd = jnp.where((tt >= nn * CMP_STRIDE + CMP_BLOCK - 1) & (nn < n_cmp), 0.0, NEG)
            colvalid = jnp.where(t0 + lax.broadcasted_iota(jnp.int32, (1, tq), 1) >= CMP_BLOCK - 1, 1.0, 0.0)
            blk_t = (t0 + lax.broadcasted_iota(jnp.int32, (N_SEL, tq), 1)) // SEL_BLOCK
            valid = jj <= blk_t
            forced = (jj == 0) | (jj == blk_t) | (jj == blk_t - 1)
            jr = lax.broadcasted_iota(jnp.int32, (N_SEL, n_vis), 0)
            nc = lax.broadcasted_iota(jnp.int32, (N_SEL, n_vis), 1)
            ov = ((nc * CMP_STRIDE < jr * SEL_BLOCK + SEL_BLOCK) & (nc * CMP_STRIDE + CMP_BLOCK > jr * SEL_BLOCK)
                  & (nc < n_cmp))
            ovT = jnp.where(ov, 1.0, 0.0).astype(BF16)

            for g in range(N_GROUPS):
                vcT = vT_ref[0, 0][g * LANE:g * LANE + HEAD_DIM, 0:n_vis]
                psum = jnp.zeros((n_vis, tq), F32)
                for r in range(N_REP):
                    h = N_REP * g + r
                    sm = s_scr[sub * N_QH + h, 0:n_vis] + maskadd
                    m = jnp.max(sm, axis=0, keepdims=True)
                    e = jnp.exp2(sm - m)
                    den = jnp.sum(e, axis=0, keepdims=True)
                    p = e * (colvalid / den)
                    ocT_ref[0, h * HEAD_DIM:(h + 1) * HEAD_DIM, qs] = _dot(vcT, p.astype(BF16)).astype(ocT_ref.dtype)
                    psum = psum + p

                imp = _f32_dot_exact_lhs(ovT, psum)
                score = jnp.where(valid, imp, -jnp.inf)
                score = jnp.where(forced & valid, jnp.inf, score)
                negm_ref[0, g, :, qs] = _select_mask(score, tq, n_blk)

    for ii in range(nq):
        pl.when(i == ii)(functools.partial(run, ii))


def _cmp(qT, cmp_k, cmp_vT, b, seq, tq=256, nsub=8):
    ts = tq * nsub
    nq = seq // ts
    n_cmp = (seq - CMP_BLOCK) // CMP_STRIDE + 1
    n_pad = cmp_k.shape[2]
    return pl.pallas_call(
        functools.partial(_cmp_kernel, tq=tq, nsub=nsub, n_cmp=n_cmp, nq=nq),
        grid=(b, nq),
        in_specs=[
            pl.BlockSpec((1, QT_ROWS, ts), lambda bi, i: (bi, 0, i)),
            pl.BlockSpec((1, 1, n_pad, 2 * LANE), lambda bi, i: (0, bi, 0, 0)),
            pl.BlockSpec((1, 1, 2 * LANE, n_pad), lambda bi, i: (1, bi, 0, 0)),
        ],
        out_specs=[
            pl.BlockSpec((1, QT_ROWS, ts), lambda bi, i: (bi, 0, i)),
            pl.BlockSpec((1, N_GROUPS, N_SEL, ts), lambda bi, i: (bi, 0, 0, i)),
        ],
        out_shape=[
            jax.ShapeDtypeStruct((b, QT_ROWS, seq), BF16),
            jax.ShapeDtypeStruct((b, N_GROUPS, N_SEL, seq), BF16),
        ],
        scratch_shapes=[pltpu.VMEM((nsub * N_QH, n_pad, tq), F32)],
        compiler_params=pltpu.CompilerParams(
            dimension_semantics=("arbitrary", "arbitrary"), vmem_limit_bytes=VMEM_LIMIT),
        name="cmp",
    )(qT, cmp_k, cmp_vT)


SLC_LAG = 6
SLC_BUFS = SLC_LAG // N_QH + 1

HALF = TK // 2
_TRI = {0: (slice(0, HALF), slice(HALF, TK), slice(HALF, TK)),
        1: (slice(HALF, TK), slice(0, HALF), slice(0, HALF))}


def _score_phase(qa_ref, k_tiles, mask_id, caus, s_buf, mt_buf, heads):
    for g in range(N_GROUPS):
        for r in range(N_REP):
            h = N_REP * g + r
            if h not in heads:
                continue
            qT = qa_ref[h]
            if mask_id is None:
                s = _dot(k_tiles[g], qT)
                s_buf[h] = s
                mt_buf[h] = jnp.max(s, axis=0, keepdims=True)
                continue
            fr, hr, hl = _TRI[mask_id]
            s_full = _dot(k_tiles[g][fr], qT) + caus[mask_id, fr, :]
            s_half = _dot(k_tiles[g][hr], qT[:, hl]) + caus[mask_id, hr, hl]
            s_buf[h, fr, :] = s_full
            s_buf[h, hr, hl] = s_half
            m_half = jnp.max(s_half, axis=0, keepdims=True)
            filler = jnp.full((1, HALF), -3e38, F32)
            m_half = jnp.concatenate([filler, m_half] if hl.start else [m_half, filler], axis=1)
            mt_buf[h] = jnp.maximum(jnp.max(s_full, axis=0, keepdims=True), m_half)


def _value_phase(vT_tiles, mask_id, s_buf, mt_buf, m_scr, acc_scr, heads):
    for g in range(N_GROUPS):
        for r in range(N_REP):
            h = N_REP * g + r
            if h not in heads:
                continue
            m_old = m_scr[h]
            m_new = jnp.maximum(m_old, mt_buf[h])
            alpha = jnp.exp2(m_old - m_new)
            if mask_id is None:
                pT = jnp.exp2(s_buf[h] - m_new).astype(BF16)
                upd = _dot(vT_tiles[g], pT)
            else:
                fr, hr, hl = _TRI[mask_id]
                p_full = jnp.exp2(s_buf[h, fr, :] - m_new).astype(BF16)
                p_half = jnp.exp2(s_buf[h, hr, hl] - m_new[:, hl]).astype(BF16)
                u_half = _dot(vT_tiles[g][:, hr], p_half)
                zero = jnp.zeros_like(u_half)
                upd = _dot(vT_tiles[g][:, fr], p_full) + jnp.concatenate(
                    [zero, u_half] if hl.start else [u_half, zero], axis=1)
            acc_scr[h] = alpha * acc_scr[h] + upd
            m_scr[h] = m_new


def _sweep_reset(m_scr, acc_scr):
    m_scr[...] = jnp.full(m_scr.shape, -3e38, F32)
    acc_scr[...] = jnp.zeros(acc_scr.shape, F32)


def _heads_out(o_ref, outs, rows=slice(None)):
    for g in range(N_GROUPS):
        oT = jnp.concatenate(outs[N_REP * g:N_REP * (g + 1)], axis=0)
        o_ref[rows, g * 2 * LANE:(g + 1) * 2 * LANE] = oT.T.astype(o_ref.dtype)


def _slcwin_kernel(qT_ref, negm_ref, ks0, ks1, kw0, kw1, vsT_ref, vwT_ref, gT_ref, ocT_ref,
                   o_ref, caus, qa_scr, s_scr, mt_scr, m_s, acc_s, m_w, acc_w, *, tq, nq):
    tk = TK
    bi = pl.program_id(0)
    i = pl.program_id(1)
    assert NSA_WINDOW // tk == 2 and tq == tk

    @pl.when((bi == 0) & (i == 0))
    def _init():
        kk = lax.broadcasted_iota(jnp.int32, (tk, tq), 0)
        qq = lax.broadcasted_iota(jnp.int32, (tk, tq), 1)
        caus[0] = jnp.where(kk <= qq, 0.0, NEG)
        caus[1] = jnp.where(kk > qq, 0.0, NEG)

    ks = (ks0, ks1)
    kw = (kw0, kw1)

    for h in range(N_QH):
        g, r = divmod(h, N_REP)
        qa_scr[h] = jnp.concatenate([qT_ref[0, h * HEAD_DIM:(h + 1) * HEAD_DIM, :], negm_ref[0, g],
                                     _slope_rows(SLOPES_NSA[g, r], tq)], axis=0)

    def run(ii):
        _sweep_reset(m_s, acc_s)
        _sweep_reset(m_w, acc_w)
        stream = []
        for d in (2, 1, 0):
            if ii - d >= 0:
                stream.append((kw, vwT_ref, ii - d, {0: 0, 2: 1}.get(d), m_w, acc_w))
        for j in range(ii + 1):
            stream.append((ks, vsT_ref, j, 0 if j == ii else None, m_s, acc_s))

        def score(p, heads):
            refs, _, idx, mask_id, _, _ = stream[p]
            buf = p % SLC_BUFS
            _score_phase(qa_scr, [refs[g][idx * tk:(idx + 1) * tk, :] for g in range(N_GROUPS)],
                         mask_id, caus, s_scr.at[buf], mt_scr.at[buf], heads)

        def value(p, heads):
            _, vref, idx, mask_id, m_scr, acc_scr = stream[p]
            buf = p % SLC_BUFS
            _value_phase([vref[0, g, idx] for g in range(N_GROUPS)], mask_id,
                         s_scr.at[buf], mt_scr.at[buf], m_scr, acc_scr, heads)

        n_items = len(stream) * N_QH
        for n in range(n_items + SLC_LAG):
            if n < n_items:
                score(n // N_QH, (n % N_QH,))
            if n >= SLC_LAG:
                value((n - SLC_LAG) // N_QH, ((n - SLC_LAG) % N_QH,))

        gT = gT_ref[0]
        outs = []
        for h in range(N_QH):
            gc = h * N_BRANCH
            f_slc = gT[gc + 1:gc + 2] * (1.0 / acc_s[h, HEAD_DIM:HEAD_DIM + 1, :])
            f_win = gT[gc + 2:gc + 3] * (1.0 / acc_w[h, HEAD_DIM:HEAD_DIM + 1, :])
            outs.append(gT[gc:gc + 1] * ocT_ref[0, h * HEAD_DIM:(h + 1) * HEAD_DIM, :]
                        + f_slc * acc_s[h, 0:HEAD_DIM, :]
                        + f_win * acc_w[h, 0:HEAD_DIM, :])
        _heads_out(o_ref, outs)

    for ii in range(nq):
        pl.when(i == ii)(functools.partial(run, ii))


def _slcwin(qT, negm, main, vT, gT, ocT, b, seq, tq=TK):
    nq = seq // tq
    n = b * seq

    def col(c):
        return pl.BlockSpec((seq, LANE), lambda bi, i, c=c: (bi, c))

    return pl.pallas_call(
        functools.partial(_slcwin_kernel, tq=tq, nq=nq),
        grid=(b, nq),
        in_specs=[
            pl.BlockSpec((1, QT_ROWS, tq), lambda bi, i: (bi, 0, i)),
            pl.BlockSpec((1, N_GROUPS, N_SEL, tq), lambda bi, i: (bi, 0, 0, i)),
            col(KS_COL), col(KS_COL + 1), col(KW_COL), col(KW_COL + 1),
            pl.BlockSpec((1, N_GROUPS, seq // TK, VT_ROWS, TK), lambda bi, i: (bi, 0, 0, 0, 0)),
            pl.BlockSpec((1, N_GROUPS, seq // TK, VT_ROWS, TK), lambda bi, i: (bi, 1, 0, 0, 0)),
            pl.BlockSpec((1, GT_ROWS, tq), lambda bi, i: (bi, 0, i)),
            pl.BlockSpec((1, QT_ROWS, tq), lambda bi, i: (bi, 0, i)),
        ],
        out_specs=pl.BlockSpec((tq, 4 * LANE), lambda bi, i: (bi * nq + i, 0)),
        out_shape=jax.ShapeDtypeStruct((n, 4 * LANE), BF16),
        scratch_shapes=[
            pltpu.VMEM((2, TK, tq), F32),
            pltpu.VMEM((N_QH, LANE, tq), BF16),
            pltpu.VMEM((SLC_BUFS, N_QH, TK, tq), F32),
            pltpu.VMEM((SLC_BUFS, N_QH, 1, tq), F32),
            pltpu.VMEM((N_QH, 1, tq), F32),
            pltpu.VMEM((N_QH, VT_ROWS, tq), F32),
            pltpu.VMEM((N_QH, 1, tq), F32),
            pltpu.VMEM((N_QH, VT_ROWS, tq), F32),
        ],
        compiler_params=pltpu.CompilerParams(
            dimension_semantics=("arbitrary", "arbitrary"), vmem_limit_bytes=VMEM_LIMIT),
        name="slcwin",
    )(qT, negm, main, main, main, main, vT, vT, gT, ocT)


SWA_AHEAD = 3
SWA_BUFS = SWA_AHEAD + 1

def _swa_kernel(sink_ref, qT_ref, k0, k1, vT_ref, o_ref, tri, mab, s_far, s_ab, s_c, *, tq, nsub):
    bi = pl.program_id(0)
    i = pl.program_id(1)
    w = SWA_WINDOW
    assert tq == 2 * w and w == LANE

    @pl.when((bi == 0) & (i == 0))
    def _init():
        kk = lax.broadcasted_iota(jnp.int32, (w, w), 0)
        qq = lax.broadcasted_iota(jnp.int32, (w, w), 1)
        causal = jnp.where(kk <= qq, 0.0, NEG)
        far = jnp.where(kk > qq, 0.0, NEG)
        tri[0] = far
        tri[1] = causal
        tri[2] = jnp.full((w, w), NEG, F32)
        mab[...] = jnp.concatenate([causal, far], axis=1)

    kk_ref = (k0, k1)
    zmid = jnp.zeros((N_SEL, tq), BF16)
    mts, sinks, outs = {}, {}, {}

    def geometry(sub):
        tile = i * nsub + sub
        return tile, tile * tq

    def score(n):
        sub, h = divmod(n, N_QH)
        g, r = divmod(h, N_REP)
        tile, t0 = geometry(sub)
        far0 = pl.multiple_of(jnp.maximum(t0 - w, 0), w)
        main0 = pl.multiple_of(t0, tq)
        buf = n % SWA_BUFS
        qaT = jnp.concatenate([qT_ref[0, h * HEAD_DIM:(h + 1) * HEAD_DIM, sub * tq:(sub + 1) * tq], zmid,
                               _slope_rows(SLOPES_SWA[g, r], tq)], axis=0)
        sf = _dot(kk_ref[g][pl.ds(far0, w), :], qaT[:, 0:w]) + tri[jnp.where(tile == 0, 2, 0)]
        sab = _dot(kk_ref[g][pl.ds(main0, w), :], qaT) + mab[...]
        sc = _dot(kk_ref[g][pl.ds(main0 + w, w), :], qaT[:, w:tq]) + tri[1]
        s_far[buf] = sf
        s_ab[buf] = sab
        s_c[buf] = sc
        tpos = (t0 + lax.broadcasted_iota(jnp.int32, (1, tq), 1)).astype(F32)
        sink = (sink_ref[g, r] * LOG2E) + float(SLOPES_SWA[g, r] * LOG2E) * tpos
        m_side = jnp.concatenate([jnp.max(sf, axis=0, keepdims=True), jnp.max(sc, axis=0, keepdims=True)], axis=1)
        mts[n] = jnp.maximum(jnp.maximum(jnp.max(sab, axis=0, keepdims=True), m_side), sink)
        sinks[n] = sink

    def value(n):
        sub, h = divmod(n, N_QH)
        g = h // N_REP
        tile, _ = geometry(sub)
        buf = n % SWA_BUFS
        vT_far = vT_ref[0, g, jnp.maximum(tile - 1, 0)][:, w:2 * w]
        vT_main = vT_ref[0, g, tile]
        m = mts.pop(n)
        pf = jnp.exp2(s_far[buf] - m[:, 0:w]).astype(BF16)
        pab = jnp.exp2(s_ab[buf] - m).astype(BF16)
        pc = jnp.exp2(s_c[buf] - m[:, w:tq]).astype(BF16)
        acc = _dot(vT_main[:, 0:w], pab) + jnp.concatenate([_dot(vT_far, pf), _dot(vT_main[:, w:tq], pc)], axis=1)
        den = acc[HEAD_DIM:HEAD_DIM + 1, :] + jnp.exp2(sinks.pop(n) - m)
        outs[n] = acc[0:HEAD_DIM, :] * (1.0 / den)
        if h == N_QH - 1:
            _heads_out(o_ref, [outs.pop(sub * N_QH + hh) for hh in range(N_QH)], rows=slice(sub * tq, (sub + 1) * tq))

    n_items = nsub * N_QH
    for n in range(n_items + SWA_AHEAD):
        if n < n_items:
            score(n)
        if n >= SWA_AHEAD:
            value(n - SWA_AHEAD)


def _swa(sinks, qT, main, vT, b, seq, tq=TK, nsub=8):
    ts = tq * nsub
    nq = seq // ts
    n = b * seq

    def col(c):
        return pl.BlockSpec((seq, LANE), lambda bi, i, c=c: (bi, c))

    return pl.pallas_call(
        functools.partial(_swa_kernel, tq=tq, nsub=nsub),
        grid=(b, nq),
        in_specs=[
            pl.BlockSpec(memory_space=pltpu.SMEM),
            pl.BlockSpec((1, QT_ROWS, ts), lambda bi, i: (bi, 1, i)),
            col(KSW_COL), col(KSW_COL + 1),
            pl.BlockSpec((1, N_GROUPS, seq // TK, VT_ROWS, TK), lambda bi, i: (bi, 2, 0, 0, 0)),
        ],
        out_specs=pl.BlockSpec((ts, 4 * LANE), lambda bi, i: (bi * nq + i, 0)),
        out_shape=jax.ShapeDtypeStruct((n, 4 * LANE), BF16),
        scratch_shapes=[
            pltpu.VMEM((3, SWA_WINDOW, SWA_WINDOW), F32),
            pltpu.VMEM((SWA_WINDOW, tq), F32),
            pltpu.VMEM((SWA_BUFS, SWA_WINDOW, SWA_WINDOW), F32),
            pltpu.VMEM((SWA_BUFS, SWA_WINDOW, tq), F32),
            pltpu.VMEM((SWA_BUFS, SWA_WINDOW, SWA_WINDOW), F32),
        ],
        compiler_params=pltpu.CompilerParams(
            dimension_semantics=("arbitrary", "arbitrary"), vmem_limit_bytes=VMEM_LIMIT),
        name="swa",
    )(sinks, qT, main, main, vT)


def _outmlp_kernel(on_ref, os_ref, x_ref, wo_ref, g2_ref, g3_ref, wu_ref, wd_ref, g4_ref, o_ref, *, ff_chunk, n_sub):
    half = N_HEADS * HEAD_DIM // 2
    tm = x_ref.shape[0]
    subs = [slice(k * tm // n_sub, (k + 1) * tm // n_sub) for k in range(n_sub)]
    mixes = [_dot(on_ref[rs, :], wo_ref[0:half, :]) + _dot(os_ref[rs, :], wo_ref[half:2 * half, :]) for rs in subs]
    for rs, mix in zip(subs, mixes):
        h1 = x_ref[rs, :] + _rms(mix, g2_ref[...])
        m = _rms(h1, g3_ref[...]).astype(BF16)
        acc = jnp.zeros(h1.shape, F32)
        for c in range(D_FF // ff_chunk):
            u = _dot(m, wu_ref[:, c * ff_chunk:(c + 1) * ff_chunk])
            u = jnp.square(jnp.maximum(u, 0.0)).astype(BF16)
            acc = acc + _dot(u, wd_ref[c * ff_chunk:(c + 1) * ff_chunk, :])
        o_ref[rs, :] = h1 + _rms(acc, g4_ref[...])


def _outmlp(onsa, oswa, x2, wo, g2, g3, wu, wd, g4, tm=1024, ff_chunk=1024, n_sub=4):
    n = x2.shape[0]

    def const(shape):
        return pl.BlockSpec(shape, lambda i: (0, 0), pipeline_mode=pl.Buffered(1))

    return pl.pallas_call(
        functools.partial(_outmlp_kernel, ff_chunk=ff_chunk, n_sub=n_sub),
        grid=(n // tm,),
        in_specs=[
            pl.BlockSpec((tm, 4 * LANE), lambda i: (i, 0)),
            pl.BlockSpec((tm, 4 * LANE), lambda i: (i, 0)),
            pl.BlockSpec((tm, D_MODEL), lambda i: (i, 0)),
            const((D_MODEL, D_MODEL)),
            const((1, D_MODEL)),
            const((1, D_MODEL)),
            const((D_MODEL, D_FF)),
            const((D_FF, D_MODEL)),
            const((1, D_MODEL)),
        ],
        out_specs=pl.BlockSpec((tm, D_MODEL), lambda i: (i, 0)),
        out_shape=jax.ShapeDtypeStruct((n, D_MODEL), F32),
        compiler_params=pltpu.CompilerParams(
            dimension_semantics=("arbitrary",), vmem_limit_bytes=VMEM_LIMIT),
        name="outmlp",
    )(onsa, oswa, x2, wo, g2, g3, wu, wd, g4)


def _layout_w_in(w):
    sizes = [512, 128, 128, 128, 128, 128, 128, N_HEADS // 2 * N_BRANCH, 512, 128, 128]
    offs = np.concatenate([[0], np.cumsum(sizes)])
    q_n, kc, vc, ks, vs, kw, vw, gt, q_s, k_s, v_s = [w[:, offs[k]:offs[k + 1]] for k in range(len(sizes))]
    std = [ks, kw, k_s, kc, vc]
    gt = jnp.concatenate([gt, jnp.zeros((w.shape[0], GT_ROWS - gt.shape[1]), w.dtype)], axis=1)
    tr = jnp.concatenate([q_n, q_s, vs, vw, v_s, gt], axis=1).T
    return jnp.concatenate(std, axis=1).astype(BF16), tr.astype(BF16)


def _layout_w1(w1):
    w = w1.reshape(2, CMP_STRIDE, HEAD_DIM, CMP_HIDDEN)
    z = jnp.zeros_like(w)
    top = jnp.concatenate([w, z], axis=-1)
    bot = jnp.concatenate([z, w], axis=-1)
    e = jnp.stack([top, bot], axis=2)
    return e.reshape(2, CMP_STRIDE * 2 * HEAD_DIM, 2 * CMP_HIDDEN).astype(BF16)


def _layout_w2(w2):
    z = jnp.zeros((CMP_HIDDEN, HEAD_DIM), w2.dtype)
    top = jnp.concatenate([w2, z, z, z], axis=1)
    bot = jnp.concatenate([z, z, w2, z], axis=1)
    return jnp.concatenate([top, bot], axis=0).astype(BF16)


def _layout_pe(pe):
    p = pe.reshape(2, CMP_STRIDE, 1, HEAD_DIM)
    return jnp.broadcast_to(p, (2, CMP_STRIDE, 2, HEAD_DIM)).reshape(2, 1, CMP_STRIDE * 2 * HEAD_DIM)


def kernel(x, norm_mix_pre, w_in, cmp_pe_k, cmp_w1_k, cmp_w2_k, cmp_pe_v, cmp_w1_v, cmp_w2_v,
           sinks, w_out, norm_mix_post, norm_mlp_pre, w_up, w_down, norm_mlp_post):
    b, seq, _ = x.shape
    assert seq // SEL_BLOCK == N_SEL and seq <= POS_SPLIT * 256
    depth = w_in.shape[0]
    h = x.reshape(b * seq, D_MODEL)
    for li in range(depth):
        w_p, wT_p = _layout_w_in(w_in[li])
        main, z, qT, vT, gT = _inproj(h, norm_mix_pre[li][None], w_p, wT_p, b, seq)
        pe2 = jnp.stack([_layout_pe(cmp_pe_k[li]), _layout_pe(cmp_pe_v[li])])
        w1e = jnp.stack([_layout_w1(cmp_w1_k[li]), _layout_w1(cmp_w1_v[li])])
        w2e = jnp.stack([_layout_w2(cmp_w2_k[li]), _layout_w2(cmp_w2_v[li])])
        cmp_kv, cmp_kvT = _compress(z, pe2, w1e, w2e, jnp.swapaxes(w2e, 1, 2))
        ocT, negm = _cmp(qT, cmp_kv, cmp_kvT, b, seq)
        onsa = _slcwin(qT, negm, main, vT, gT, ocT, b, seq)
        oswa = _swa(sinks[li], qT, main, vT, b, seq)
        h = _outmlp(onsa, oswa, h, w_out[li].astype(BF16), norm_mix_post[li][None], norm_mlp_pre[li][None],
                    w_up[li].astype(BF16), w_down[li].astype(BF16), norm_mlp_post[li][None])
    return h.reshape(b, seq, D_MODEL)
```
